```python
import math
import jax, jax.numpy as jnp
from jax import lax
import numpy as np

D_MODEL = 1024
BATCH = 2
SEQ = 8192
DEPTH = 1

PLE_DIM = 256
EPS = 1e-6
GLA_HEADS = 4
GLA_DK = D_MODEL // 2 // GLA_HEADS
GLA_DV = D_MODEL // GLA_HEADS
GLA_RANK = 16
GLA_TAU = 16.0
GLA_CHUNK = 64
GLA_QK_W = GLA_HEADS * GLA_DK
GLA_V_W = GLA_HEADS * GLA_DV
ATT_GROUPS = ((128, 1), (512, 4), (2048, 16))
ATT_HEADS_PER_GROUP = 4
ATT_HEAD_DIM = 128
N_ATT_GROUPS = len(ATT_GROUPS)
N_ATT_HEADS = N_ATT_GROUPS * ATT_HEADS_PER_GROUP
ATT_W = ATT_HEADS_PER_GROUP * ATT_HEAD_DIM
ATT_BLOCK = 128
REL_BUCKETS = 32
REL_MAX_DIST = 2048
D_FF = 4 * D_MODEL
NEG_INF = -1e30

SPLIT_SIZES = (GLA_QK_W, GLA_QK_W, GLA_V_W, GLA_V_W, GLA_RANK,
               N_ATT_GROUPS * 3 * ATT_W, 2 * D_MODEL)
IN_COLS = sum(SPLIT_SIZES)

kernel_name = "hybrid_gla_dilated_attn_gated_block"


def rmsnorm(x, g):
    xf = x.astype(jnp.float32)
    y = xf * lax.rsqrt(jnp.mean(xf * xf, axis=-1, keepdims=True) + EPS)
    return (y * g.astype(jnp.float32)).astype(x.dtype)


def _split_cols(h, sizes):
    offsets = np.cumsum(np.array(sizes))[:-1].tolist()
    return jnp.split(h, offsets, axis=-1)


def _t5_causal_bucket(n):
    max_exact = REL_BUCKETS // 2
    nf = np.maximum(n, 1).astype(np.float32)
    large = max_exact + (np.log(nf / max_exact) / np.log(REL_MAX_DIST / max_exact)
                         * (REL_BUCKETS - max_exact)).astype(np.int32)
    large = np.minimum(large, REL_BUCKETS - 1)
    return np.where(n < max_exact, n, large).astype(np.int32)


def _group_bias(rel_bias, g, dil):
    qi = np.arange(ATT_BLOCK)[:, None]
    kj = np.arange(2 * ATT_BLOCK)[None, :]
    dist = np.maximum(qi + ATT_BLOCK - kj, 0) * dil
    bucket = _t5_causal_bucket(dist)
    tab = rel_bias[:, g * ATT_HEADS_PER_GROUP:(g + 1) * ATT_HEADS_PER_GROUP]
    return jnp.transpose(tab[bucket], (2, 0, 1)).astype(jnp.float32)


def gla_mixer(q, k, v, log_a):
    B, S, H, dk = q.shape
    dv = v.shape[-1]
    C = GLA_CHUNK
    N = S // C

    def chunk(t):
        return t.astype(jnp.float32).reshape(B, N, C, H, t.shape[-1]).transpose(0, 3, 1, 2, 4)

    q, k, v, g = chunk(q), chunk(k), chunk(v), chunk(log_a)
    b = jnp.cumsum(g, axis=3)
    b_last = b[:, :, :, -1:, :]
    q_dec = q * (dk ** -0.5) * jnp.exp(b)
    k_in = k * jnp.exp(-b)
    k_out = k * jnp.exp(b_last - b)
    causal = np.tril(np.ones((C, C), dtype=bool))
    attn = jnp.where(causal, jnp.einsum('bhncd,bhnsd->bhncs', q_dec, k_in), 0.0)
    o_intra = jnp.einsum('bhncs,bhnsv->bhncv', attn, v)
    upd = jnp.einsum('bhncd,bhncv->nbhdv', k_out, v)
    decay = jnp.exp(b_last[:, :, :, 0, :]).transpose(2, 0, 1, 3)

    def step(state, inp):
        dec, u = inp
        return dec[..., None] * state + u, state

    _, states = lax.scan(step, jnp.zeros((B, H, dk, dv), jnp.float32), (decay, upd))
    o_inter = jnp.einsum('bhncd,nbhdv->bhncv', q_dec, states)
    return (o_intra + o_inter).transpose(0, 2, 3, 1, 4).reshape(B, S, H, dv)


def dilated_attention(q, k, v, bias, dil, win_steps):
    B, S, H, hd = q.shape
    BLK = ATT_BLOCK
    L = S // dil
    nb = -(-L // BLK)
    Lp = nb * BLK
    Z = B * dil

    def sub(t):
        return t.reshape(B, L, dil, H, hd).transpose(0, 2, 3, 1, 4).reshape(Z, H, L, hd)

    qs = jnp.pad(sub(q), ((0, 0), (0, 0), (0, Lp - L), (0, 0))).reshape(Z, H, nb, BLK, hd)

    def kv_blocks(t):
        t = jnp.pad(sub(t), ((0, 0), (0, 0), (BLK, Lp - L), (0, 0))).reshape(Z, H, nb + 1, BLK, hd)
        return jnp.concatenate([t[:, :, :-1], t[:, :, 1:]], axis=3)

    kb, vb = kv_blocks(k), kv_blocks(v)
    qi = np.arange(BLK)[:, None]
    kj = np.arange(2 * BLK)[None, :]
    delta = qi + BLK - kj
    band = (delta >= 0) & (delta <= win_steps)
    valid = band[None] & ((np.arange(nb)[:, None, None] > 0) | (kj >= BLK)[None])
    logits = jnp.einsum('zhnqd,zhnkd->zhnqk', qs, kb).astype(jnp.float32) * (hd ** -0.5)
    logits = jnp.where(valid, logits + bias[:, None], NEG_INF)
    m = jnp.max(logits, axis=-1, keepdims=True)
    pexp = jnp.exp(logits - m)
    s = jnp.sum(pexp, axis=-1, keepdims=True)
    o = jnp.einsum('zhnqk,zhnkd->zhnqd', pexp, vb.astype(jnp.float32)) / s
    lse = (m + jnp.log(s))[..., 0]
    o = o.reshape(B, dil, H, Lp, hd)[:, :, :, :L].transpose(0, 3, 1, 2, 4).reshape(B, S, H, hd)
    lse = lse.reshape(B, dil, H, Lp)[..., :L].transpose(0, 3, 1, 2).reshape(B, S, H)
    return o, lse


def setup_inputs(seed: int = 0) -> dict:
    key = jax.random.key(seed)
    ks = jax.random.split(key, 20)

    def nrm(k, shape, scale):
        return jax.random.normal(k, shape, jnp.float32) * scale

    return {
        "x": nrm(ks[0], (BATCH, SEQ, D_MODEL), 1.0),
        "p": nrm(ks[1], (DEPTH, BATCH, SEQ, PLE_DIM), 1.0),
        "ln1": 1.0 + nrm(ks[2], (DEPTH, D_MODEL), 0.02),
        "w_in": nrm(ks[3], (DEPTH, D_MODEL, IN_COLS), D_MODEL ** -0.5),
        "w_a2": nrm(ks[4], (DEPTH, GLA_RANK, GLA_QK_W), GLA_RANK ** -0.5),
        "b_a": nrm(ks[5], (DEPTH, GLA_QK_W), 0.1),
        "gla_gn": 1.0 + nrm(ks[6], (DEPTH, GLA_V_W), 0.02),
        "w_o_gla": nrm(ks[7], (DEPTH, GLA_V_W, D_MODEL), GLA_V_W ** -0.5),
        "w_o_attn": nrm(ks[8], (DEPTH, ATT_W, D_MODEL), ATT_W ** -0.5),
        "w_out": nrm(ks[9], (DEPTH, D_MODEL, D_MODEL), D_MODEL ** -0.5),
        "ln2": 1.0 + nrm(ks[10], (DEPTH, D_MODEL), 0.02),
        "w_mlp1": nrm(ks[11], (DEPTH, D_MODEL, D_FF), D_MODEL ** -0.5),
        "w_mlp2": nrm(ks[12], (DEPTH, D_FF, D_MODEL), D_FF ** -0.5),
        "ln3": 1.0 + nrm(ks[13], (DEPTH, D_MODEL), 0.02),
        "w_pp": nrm(ks[14], (DEPTH, PLE_DIM, D_MODEL), PLE_DIM ** -0.5),
        "w_pg": nrm(ks[15], (DEPTH, D_MODEL, D_MODEL), D_MODEL ** -0.5),
        "rel_bias": nrm(ks[16], (REL_BUCKETS, N_ATT_HEADS), 0.5),
        "ln_f": 1.0 + nrm(ks[17], (D_MODEL,), 0.02),
    }


def reference(x, p, ln1, w_in, w_a2, b_a, gla_gn, w_o_gla, w_o_attn, w_out,
              ln2, w_mlp1, w_mlp2, ln3, w_pp, w_pg, rel_bias, ln_f):
    B, S, _ = x.shape
    for i in range(DEPTH):
        h = rmsnorm(x, ln1[i])
        hq, hk, hv, hr, ha, hatt, hgate = _split_cols(h @ w_in[i], SPLIT_SIZES)

        log_a = jax.nn.log_sigmoid((ha @ w_a2[i] + b_a[i]).astype(jnp.float32)) / GLA_TAU
        o = gla_mixer(hq.reshape(B, S, GLA_HEADS, GLA_DK),
                      hk.reshape(B, S, GLA_HEADS, GLA_DK),
                      hv.reshape(B, S, GLA_HEADS, GLA_DV),
                      log_a.reshape(B, S, GLA_HEADS, GLA_DK))
        o = o * lax.rsqrt(jnp.mean(o * o, axis=-1, keepdims=True) + EPS)
        o = o.reshape(B, S, GLA_V_W) * gla_gn[i].astype(jnp.float32) * jax.nn.silu(hr.astype(jnp.float32))
        y_gla = o.astype(x.dtype) @ w_o_gla[i]

        hatt = hatt.reshape(B, S, N_ATT_GROUPS, 3, ATT_HEADS_PER_GROUP, ATT_HEAD_DIM)
        outs, lses = [], []
        for g, (win, dil) in enumerate(ATT_GROUPS):
            o_g, lse_g = dilated_attention(hatt[:, :, g, 0], hatt[:, :, g, 1], hatt[:, :, g, 2],
                                           _group_bias(rel_bias, g, dil), dil, win // dil)
            outs.append(o_g)
            lses.append(lse_g)
        wts = jax.nn.softmax(jnp.stack(lses, axis=0), axis=0)
        o_att = jnp.sum(wts[..., None] * jnp.stack(outs, axis=0), axis=0).reshape(B, S, ATT_W)
        y_att = o_att.astype(x.dtype) @ w_o_attn[i]

        g_gla, g_att = jnp.split(hgate, 2, axis=-1)
        mix = (jax.nn.sigmoid(g_gla) * y_gla + jax.nn.sigmoid(g_att) * y_att) @ w_out[i]
        x = x + mix

        h2 = rmsnorm(x, ln2[i])
        x = x + jnp.square(jax.nn.relu(h2 @ w_mlp1[i])) @ w_mlp2[i]

        h3 = rmsnorm(x, ln3[i])
        x = x + jax.nn.sigmoid(h3 @ w_pg[i]) * (p[i] @ w_pp[i])
    return rmsnorm(x, ln_f)
```

```python
import functools

import numpy as np
import jax
import jax.numpy as jnp
from jax import lax
from jax.experimental import pallas as pl
from jax.experimental.pallas import tpu as pltpu

F32 = jnp.float32
BF16 = jnp.bfloat16

D_MODEL = 1024
PLE_DIM = 256
EPS = 1e-6
GLA_HEADS = 4
GLA_DK = 128
GLA_DV = 256
GLA_RANK = 16
GLA_TAU = 16.0
GLA_CHUNK = 64
GLA_QK_W = GLA_HEADS * GLA_DK
GLA_V_W = GLA_HEADS * GLA_DV
ATT_GROUPS = ((128, 1), (512, 4), (2048, 16))
ATT_HEADS_PER_GROUP = 4
ATT_HEAD_DIM = 128
N_ATT_GROUPS = len(ATT_GROUPS)
ATT_W = ATT_HEADS_PER_GROUP * ATT_HEAD_DIM
ATT_BLOCK = 128
REL_BUCKETS = 32
REL_MAX_DIST = 2048
D_FF = 4 * D_MODEL
NEG_INF = -1e30

LANES = 128

COL_Q = 0
COL_K = COL_Q + GLA_QK_W
COL_V = COL_K + GLA_QK_W
COL_R = COL_V + GLA_V_W
COL_GATE = COL_R + GLA_V_W
COL_ATT = COL_GATE + 2 * D_MODEL
Y_COLS = COL_ATT + N_ATT_GROUPS * 3 * ATT_W

VMEM_LIMIT = 56 * 1024 * 1024


def _params(semantics):
    return pltpu.CompilerParams(dimension_semantics=semantics, vmem_limit_bytes=VMEM_LIMIT)


def _rms(x, g):
    return x * lax.rsqrt(jnp.mean(x * x, axis=-1, keepdims=True) + EPS) * g


def _dot(a, b):
    return jnp.dot(a, b, preferred_element_type=F32)


def _dot_nt(a, b):
    return lax.dot_general(a, b, (((1,), (1,)), ((), ())), preferred_element_type=F32)


def _dot_tn(a, b):
    return lax.dot_general(a, b, (((0,), (0,)), ((), ())), preferred_element_type=F32)


def _inproj_kernel(x_ref, ln_ref, w_ref, wa1_ref, wa2_ref, ba_ref, y_ref, la_ref, h_scr):
    @pl.when(pl.program_id(1) == 0)
    def _():
        h = _rms(x_ref[...], ln_ref[...]).astype(BF16)
        h_scr[...] = h
        ha = _dot(h, wa1_ref[...])
        z = jnp.dot(ha, wa2_ref[...], preferred_element_type=F32,
                    precision=lax.Precision.HIGHEST) + ba_ref[...]
        log_sig = jnp.minimum(z, 0.0) - jnp.log(1.0 + jnp.exp(-jnp.abs(z)))
        la_ref[...] = log_sig * (1.0 / GLA_TAU)

    y_ref[...] = _dot(h_scr[...], w_ref[...]).astype(BF16)


def _inproj(x2, ln1, w_main, wa1, wa2, ba, tm=1024, tn=512):
    T = x2.shape[0]
    return pl.pallas_call(
        _inproj_kernel,
        grid=(T // tm, Y_COLS // tn),
        in_specs=[
            pl.BlockSpec((tm, D_MODEL), lambda i, j: (i, 0)),
            pl.BlockSpec((1, D_MODEL), lambda i, j: (0, 0)),
            pl.BlockSpec((D_MODEL, tn), lambda i, j: (0, j)),
            pl.BlockSpec((D_MODEL, LANES), lambda i, j: (0, 0)),
            pl.BlockSpec((LANES, GLA_QK_W), lambda i, j: (0, 0)),
            pl.BlockSpec((1, GLA_QK_W), lambda i, j: (0, 0)),
        ],
        out_specs=[
            pl.BlockSpec((tm, tn), lambda i, j: (i, j)),
            pl.BlockSpec((tm, GLA_QK_W), lambda i, j: (i, 0)),
        ],
        out_shape=[
            jax.ShapeDtypeStruct((T, Y_COLS), BF16),
            jax.ShapeDtypeStruct((T, GLA_QK_W), F32),
        ],
        scratch_shapes=[pltpu.VMEM((tm, D_MODEL), BF16)],
        compiler_params=_params(("parallel", "arbitrary")),
        name="inproj",
    )(x2, ln1, w_main, wa1, wa2, ba)


def _split3(g):
    hi = g.astype(BF16)
    r1 = g - hi.astype(F32)
    mid = r1.astype(BF16)
    lo = (r1 - mid.astype(F32)).astype(BF16)
    return hi, mid, lo


def _gla_kernel(q_ref, k_ref, v_ref, r_ref, la_ref, gn_ref, o_ref, st_ref, *, nchunk):
    C = GLA_CHUNK

    @pl.when(pl.program_id(2) == 0)
    def _():
        st_ref[...] = jnp.zeros_like(st_ref)

    row = lax.broadcasted_iota(jnp.int32, (C, C), 0)
    col = lax.broadcasted_iota(jnp.int32, (C, C), 1)
    causal = row >= col
    tri = causal.astype(BF16)
    scale = GLA_DK ** -0.5
    gn = gn_ref[...]

    for c in range(nchunk):
        sl = pl.ds(c * C, C)
        g = la_ref[0, sl, :]
        hi, mid, lo = _split3(g)
        b = _dot(tri, hi) + _dot(tri, mid) + _dot(tri, lo)
        b_last = b[C - 1:C, :]
        q = q_ref[0, sl, :].astype(F32)
        k = k_ref[0, sl, :].astype(F32)
        v = v_ref[0, sl, :]
        q_dec = (q * scale * jnp.exp(b)).astype(BF16)
        k_in = (k * jnp.exp(-b)).astype(BF16)
        k_out = (k * jnp.exp(b_last - b)).astype(BF16)
        attn = jnp.where(causal, _dot_nt(q_dec, k_in), 0.0).astype(BF16)
        st = st_ref[...]
        o = _dot(attn, v) + _dot_nt(q_dec, st.astype(BF16))
        st_ref[...] = st * jnp.exp(b_last) + _dot_tn(v, k_out)
        r = r_ref[0, sl, :].astype(F32)
        o = _rms(o, gn) * (r * jax.nn.sigmoid(r))
        o_ref[0, sl, :] = o.astype(BF16)


def _gla(y3, la3, gn, tb=512):
    B, S, _ = y3.shape
    nq = COL_K // GLA_DK
    nv = COL_V // GLA_DV
    nr = COL_R // GLA_DV
    return pl.pallas_call(
        functools.partial(_gla_kernel, nchunk=tb // GLA_CHUNK),
        grid=(B, GLA_HEADS, S // tb),
        in_specs=[
            pl.BlockSpec((1, tb, GLA_DK), lambda b, h, t: (b, t, h)),
            pl.BlockSpec((1, tb, GLA_DK), lambda b, h, t: (b, t, nq + h)),
            pl.BlockSpec((1, tb, GLA_DV), lambda b, h, t: (b, t, nv + h)),
            pl.BlockSpec((1, tb, GLA_DV), lambda b, h, t: (b, t, nr + h)),
            pl.BlockSpec((1, tb, GLA_DK), lambda b, h, t: (b, t, h)),
            pl.BlockSpec((1, GLA_DV), lambda b, h, t: (0, h)),
        ],
        out_specs=pl.BlockSpec((1, tb, GLA_DV), lambda b, h, t: (b, t, h)),
        out_shape=jax.ShapeDtypeStruct((B, S, GLA_V_W), BF16),
        scratch_shapes=[pltpu.VMEM((GLA_DV, GLA_DK), F32)],
        compiler_params=_params(("parallel", "parallel", "arbitrary")),
        name="gla",
    )(y3, y3, y3, y3, la3, gn)


def _att_kernel(q_ref, kc_ref, kp_ref, vc_ref, vp_ref, bc_ref, bp_ref, o_ref, lse_ref, *, nblk):
    BLK = ATT_BLOCK
    hd = ATT_HEAD_DIM
    scale = hd ** -0.5
    first = pl.program_id(2) == 0
    lane_head = lax.broadcasted_iota(jnp.int32, (BLK, LANES), 1) // (LANES // ATT_HEADS_PER_GROUP)

    for j in range(nblk):
        rows = pl.ds(j * BLK, BLK)
        lse_tile = jnp.zeros((BLK, LANES), F32)
        for h in range(ATT_HEADS_PER_GROUP):
            cols = pl.ds(h * hd, hd)
            q = q_ref[0, rows, cols]
            kc = kc_ref[0, rows, cols]
            vc = vc_ref[0, rows, cols]
            if j == 0:
                kp = kp_ref[0, :, cols]
                vp = vp_ref[0, :, cols]
            else:
                prev = pl.ds((j - 1) * BLK, BLK)
                kp = kc_ref[0, prev, cols]
                vp = vc_ref[0, prev, cols]
            s_c = _dot_nt(q, kc) * scale + bc_ref[h]
            s_p = _dot_nt(q, kp) * scale + bp_ref[h]
            if j == 0:
                s_p = jnp.where(first, NEG_INF, s_p)
            m = jnp.maximum(jnp.max(s_c, axis=-1, keepdims=True), jnp.max(s_p, axis=-1, keepdims=True))
            p_c = jnp.exp(s_c - m)
            p_p = jnp.exp(s_p - m)
            s = jnp.sum(p_c, axis=-1, keepdims=True) + jnp.sum(p_p, axis=-1, keepdims=True)
            o = (_dot(p_c.astype(BF16), vc) + _dot(p_p.astype(BF16), vp)) / s
            o_ref[0, rows, cols] = o.astype(BF16)
            lse_tile = jnp.where(lane_head == h, m + jnp.log(s), lse_tile)
        lse_ref[0, rows, :] = lse_tile


def _attention_group(y3, bias_c, bias_p, g, dil, tq=512):
    B, S, _ = y3.shape
    L = S // dil
    yv = y3.reshape(B, L, dil * Y_COLS)
    per_tok = Y_COLS // ATT_W
    base = COL_ATT // ATT_W + 3 * g
    ratio = tq // ATT_BLOCK
    cur = lambda c: pl.BlockSpec((1, tq, ATT_W), lambda b, r, n: (b, n, r * per_tok + base + c))
    prv = lambda c: pl.BlockSpec((1, ATT_BLOCK, ATT_W),
                                 lambda b, r, n: (b, jnp.maximum(n * ratio - 1, 0), r * per_tok + base + c))
    bias_spec = pl.BlockSpec((ATT_HEADS_PER_GROUP, ATT_BLOCK, ATT_BLOCK), lambda b, r, n: (0, 0, 0))
    o, lse = pl.pallas_call(
        functools.partial(_att_kernel, nblk=ratio),
        grid=(B, dil, L // tq),
        in_specs=[cur(0), cur(1), prv(1), cur(2), prv(2), bias_spec, bias_spec],
        out_specs=[
            pl.BlockSpec((1, tq, ATT_W), lambda b, r, n: (b, n, r)),
            pl.BlockSpec((1, tq, LANES), lambda b, r, n: (b, n, r)),
        ],
        out_shape=[
            jax.ShapeDtypeStruct((B, L, dil * ATT_W), BF16),
            jax.ShapeDtypeStruct((B, L, dil * LANES), F32),
        ],
        compiler_params=_params(("parallel", "parallel", "arbitrary")),
        name=f"att_g{g}",
    )(yv, yv, yv, yv, yv, bias_c, bias_p)
    return o.reshape(B * S, ATT_W), lse.reshape(B * S, LANES)


def _t5_causal_bucket(n):
    max_exact = REL_BUCKETS // 2
    nf = np.maximum(n, 1).astype(np.float32)
    large = max_exact + (np.log(nf / max_exact) / np.log(REL_MAX_DIST / max_exact)
                         * (REL_BUCKETS - max_exact)).astype(np.int32)
    large = np.minimum(large, REL_BUCKETS - 1)
    return np.where(n < max_exact, n, large).astype(np.int32)


def _group_bias(rel_bias, g, dil, win_steps):
    qi = np.arange(ATT_BLOCK)[:, None]
    kj = np.arange(2 * ATT_BLOCK)[None, :]
    delta = qi + ATT_BLOCK - kj
    band = (delta >= 0) & (delta <= win_steps)
    bucket = _t5_causal_bucket(np.maximum(delta, 0) * dil)
    tab = rel_bias[:, g * ATT_HEADS_PER_GROUP:(g + 1) * ATT_HEADS_PER_GROUP]
    bias = jnp.transpose(tab[bucket], (2, 0, 1)).astype(F32)
    bias = jnp.where(band[None], bias, NEG_INF)
    return bias[:, :, ATT_BLOCK:], bias[:, :, :ATT_BLOCK]


def _mix_kernel(x_ref, og_ref, o0_ref, o1_ref, o2_ref, l0_ref, l1_ref, l2_ref, ga_ref, gb_ref,
                wog_ref, woa_ref, wout_ref, x1_ref):
    hd = ATT_HEAD_DIM
    lw = LANES // ATT_HEADS_PER_GROUP
    y_gla = _dot(og_ref[...], wog_ref[...])
    heads = []
    for h in range(ATT_HEADS_PER_GROUP):
        ls = [l_ref[:, h * lw:h * lw + 1] for l_ref in (l0_ref, l1_ref, l2_ref)]
        mx = jnp.maximum(jnp.maximum(ls[0], ls[1]), ls[2])
        es = [jnp.exp(l - mx) for l in ls]
        inv = 1.0 / (es[0] + es[1] + es[2])
        acc = None
        for e, o_ref in zip(es, (o0_ref, o1_ref, o2_ref)):
            term = (e * inv) * o_ref[:, h * hd:(h + 1) * hd].astype(F32)
            acc = term if acc is None else acc + term
        heads.append(acc.astype(BF16))
    o_att = jnp.concatenate(heads, axis=-1)
    y_att = _dot(o_att, woa_ref[...])
    mix = (jax.nn.sigmoid(ga_ref[...].astype(F32)) * y_gla
           + jax.nn.sigmoid(gb_ref[...].astype(F32)) * y_att)
    x1_ref[...] = x_ref[...] + _dot(mix.astype(BF16), wout_ref[...])


def _mix(x2, y2, o_gla, o_att, lse, wog, woa, wout, tm=512):
    T = x2.shape[0]
    gcol = COL_GATE // D_MODEL
    row = lambda w: pl.BlockSpec((tm, w), lambda i: (i, 0))
    full = lambda a: pl.BlockSpec(a.shape, lambda i: (0, 0))
    return pl.pallas_call(
        _mix_kernel,
        grid=(T // tm,),
        in_specs=[row(D_MODEL), row(GLA_V_W), row(ATT_W), row(ATT_W), row(ATT_W),
                  row(LANES), row(LANES), row(LANES),
                  pl.BlockSpec((tm, D_MODEL), lambda i: (i, gcol)),
                  pl.BlockSpec((tm, D_MODEL), lambda i: (i, gcol + 1)),
                  full(wog), full(woa), full(wout)],
        out_specs=row(D_MODEL),
        out_shape=jax.ShapeDtypeStruct((T, D_MODEL), F32),
        compiler_params=_params(("parallel",)),
        name="mix",
    )(x2, o_gla, *o_att, *lse, y2, y2, wog, woa, wout)


def _mlp_kernel(x_ref, p_ref, ln2_ref, w1_ref, w2_ref, ln3_ref, wpg_ref, wpp_ref, lnf_ref,
                out_ref, h_scr, acc_scr):
    f = pl.program_id(1)

    @pl.when(f == 0)
    def _():
        h_scr[...] = _rms(x_ref[...], ln2_ref[...]).astype(BF16)
        acc_scr[...] = jnp.zeros_like(acc_scr)

    u = jnp.maximum(_dot(h_scr[...], w1_ref[...]), 0.0)
    acc_scr[...] += _dot((u * u).astype(BF16), w2_ref[...])

    @pl.when(f == pl.num_programs(1) - 1)
    def _():
        x2 = x_ref[...] + acc_scr[...]
        h3 = _rms(x2, ln3_ref[...]).astype(BF16)
        gate = jax.nn.sigmoid(_dot(h3, wpg_ref[...]))
        x3 = x2 + gate * _dot(p_ref[...].astype(BF16), wpp_ref[...])
        out_ref[...] = _rms(x3, lnf_ref[...])


def _mlp(x1, p2, ln2, w1, w2, ln3, wpg, wpp, lnf, tm=1024, tf=1024):
    T = x1.shape[0]
    vec = pl.BlockSpec((1, D_MODEL), lambda i, f: (0, 0))
    return pl.pallas_call(
        _mlp_kernel,
        grid=(T // tm, D_FF // tf),
        in_specs=[
            pl.BlockSpec((tm, D_MODEL), lambda i, f: (i, 0)),
            pl.BlockSpec((tm, PLE_DIM), lambda i, f: (i, 0)),
            vec,
            pl.BlockSpec((D_MODEL, tf), lambda i, f: (0, f)),
            pl.BlockSpec((tf, D_MODEL), lambda i, f: (f, 0)),
            vec,
            pl.BlockSpec((D_MODEL, D_MODEL), lambda i, f: (0, 0)),
            pl.BlockSpec((PLE_DIM, D_MODEL), lambda i, f: (0, 0)),
            vec,
        ],
        out_specs=pl.BlockSpec((tm, D_MODEL), lambda i, f: (i, 0)),
        out_shape=jax.ShapeDtypeStruct((T, D_MODEL), F32),
        scratch_shapes=[pltpu.VMEM((tm, D_MODEL), BF16), pltpu.VMEM((tm, D_MODEL), F32)],
        compiler_params=_params(("parallel", "arbitrary")),
        name="mlp",
    )(x1, p2, ln2, w1, w2, ln3, wpg, wpp, lnf)


def _layer(x2, p2, B, S, ln1, w_in, w_a2, b_a, gla_gn, w_o_gla, w_o_attn, w_out,
           ln2, w_mlp1, w_mlp2, ln3, w_pp, w_pg, rel_bias, ln_out):
    sec = np.cumsum((0, GLA_QK_W, GLA_QK_W, GLA_V_W, GLA_V_W, GLA_RANK, N_ATT_GROUPS * 3 * ATT_W))
    w_main = jnp.concatenate(
        [w_in[:, :sec[4]], w_in[:, sec[6]:], w_in[:, sec[5]:sec[6]]], axis=1).astype(BF16)
    wa1 = jnp.pad(w_in[:, sec[4]:sec[5]], ((0, 0), (0, LANES - GLA_RANK))).astype(BF16)
    wa2 = jnp.pad(w_a2, ((0, LANES - GLA_RANK), (0, 0)))

    y2, la = _inproj(x2, ln1[None], w_main, wa1, wa2, b_a[None])
    y3 = y2.reshape(B, S, Y_COLS)
    o_gla = _gla(y3, la.reshape(B, S, GLA_QK_W), gla_gn[None]).reshape(B * S, GLA_V_W)

    o_att, lse = [], []
    for g, (win, dil) in enumerate(ATT_GROUPS):
        bias_c, bias_p = _group_bias(rel_bias, g, dil, win // dil)
        o_g, lse_g = _attention_group(y3, bias_c, bias_p, g, dil)
        o_att.append(o_g)
        lse.append(lse_g)

    x1 = _mix(x2, y2, o_gla, o_att, lse,
              w_o_gla.astype(BF16), w_o_attn.astype(BF16), w_out.astype(BF16))
    return _mlp(x1, p2, ln2[None], w_mlp1.astype(BF16), w_mlp2.astype(BF16), ln3[None],
                w_pg.astype(BF16), w_pp.astype(BF16), ln_out[None])


def kernel(x, p, ln1, w_in, w_a2, b_a, gla_gn, w_o_gla, w_o_attn, w_out, ln2, w_mlp1, w_mlp2,
           ln3, w_pp, w_pg, rel_bias, ln_f):
    B, S, D = x.shape
    depth = p.shape[0]
    assert depth == 1, "the final norm is fused into the single layer's last kernel"
    x2 = x.reshape(B * S, D)
    out = _layer(x2, p[0].reshape(B * S, PLE_DIM), B, S, ln1[0], w_in[0], w_a2[0], b_a[0],
                 gla_gn[0], w_o_gla[0], w_o_attn[0], w_out[0], ln2[0], w_mlp1[0], w_mlp2[0],
                 ln3[0], w_pp[0], w_pg[0], rel_bias, ln_f)
    return out.reshape(B, S, D)
```

```python
import functools

import numpy as np
import jax
import jax.numpy as jnp
from jax import lax
from jax.experimental import pallas as pl
from jax.experimental.pallas import tpu as pltpu

F32 = jnp.float32
BF16 = jnp.bfloat16

D_MODEL = 1024
PLE_DIM = 256
EPS = 1e-6
GLA_HEADS = 4
GLA_DK = 128
GLA_DV = 256
GLA_RANK = 16
GLA_TAU = 16.0
GLA_CHUNK = 64
GLA_QK_W = GLA_HEADS * GLA_DK
GLA_V_W = GLA_HEADS * GLA_DV
ATT_GROUPS = ((128, 1), (512, 4), (2048, 16))
ATT_DILS = tuple(d for _, d in ATT_GROUPS)
ATT_HEADS_PER_GROUP = 4
ATT_HEAD_DIM = 128
N_ATT_GROUPS = len(ATT_GROUPS)
ATT_W = ATT_HEADS_PER_GROUP * ATT_HEAD_DIM
ATT_BLOCK = 128
REL_BUCKETS = 32
REL_MAX_DIST = 2048
D_FF = 4 * D_MODEL
NEG_INF = -1e30

LANES = 128

COL_Q = 0
COL_K = COL_Q + GLA_QK_W
COL_V = COL_K + GLA_QK_W
COL_R = COL_V + GLA_V_W
COL_GATE = COL_R + GLA_V_W
COL_ATT = COL_GATE + 2 * D_MODEL
ATT_GROUP_W = 3 * ATT_W
Y_COLS = COL_ATT + N_ATT_GROUPS * ATT_GROUP_W

TILE = max(ATT_DILS) * ATT_BLOCK

VMEM_LIMIT = 56 * 1024 * 1024


def _params(semantics):
    return pltpu.CompilerParams(dimension_semantics=semantics, vmem_limit_bytes=VMEM_LIMIT)


def _rms(x, g):
    return x * lax.rsqrt(jnp.mean(x * x, axis=-1, keepdims=True) + EPS) * g


def _dot(a, b):
    return jnp.dot(a, b, preferred_element_type=F32)


def _dot_nt(a, b):
    return lax.dot_general(a, b, (((1,), (1,)), ((), ())), preferred_element_type=F32)


def _dot_tn(a, b):
    return lax.dot_general(a, b, (((0,), (0,)), ((), ())), preferred_element_type=F32)


def _inproj_kernel(*refs, n_natural, n_group):
    n_slab = D_MODEL // LANES
    x_slabs = refs[:n_slab]
    ln_ref, w_ref, wa1_ref, wa2_ref, ba_ref, y_ref, la_ref, h_scr, inv_scr = refs[n_slab:]
    j = pl.program_id(1)

    @pl.when(j == 0)
    def _():
        for d, dil in enumerate(ATT_DILS):
            n_r = TILE // dil
            chunk = min(n_r, 256)
            for r in range(dil):
                for c in range(n_r // chunk):
                    dst = pl.ds(r * n_r + c * chunk, chunk)
                    if dil == 1:
                        src = dst
                        parts = [xr[src, :] for xr in x_slabs]
                        sq = parts[0] * parts[0]
                        for part in parts[1:]:
                            sq = sq + part * part
                        ms = jnp.sum(sq, axis=-1, keepdims=True) * (1.0 / D_MODEL)
                        inv = jnp.broadcast_to(lax.rsqrt(ms + EPS), (chunk, LANES))
                        inv_scr[src, :] = inv
                    else:
                        src = pl.ds(r + dil * c * chunk, chunk, stride=dil)
                        parts = [xr[src, :] for xr in x_slabs]
                        inv = inv_scr[src, :]
                    for s, part in enumerate(parts):
                        cols = pl.ds(s * LANES, LANES)
                        h_scr[d, dst, cols] = (part * inv * ln_ref[:, cols]).astype(BF16)
        ha = _dot(h_scr[0], wa1_ref[...])
        z = jnp.dot(ha, wa2_ref[...], preferred_element_type=F32,
                    precision=lax.Precision.HIGHEST) + ba_ref[...]
        log_sig = jnp.minimum(z, 0.0) - jnp.log(1.0 + jnp.exp(-jnp.abs(z)))
        la_ref[...] = log_sig * (1.0 / GLA_TAU)

    d = (j >= n_natural).astype(jnp.int32) + (j >= n_natural + n_group).astype(jnp.int32)
    y_ref[...] = _dot(h_scr[d], w_ref[...]).astype(BF16)


def _inproj(x2, ln1, w_main, wa1, wa2, ba, tn=512):
    T = x2.shape[0]
    n_natural = (COL_ATT + ATT_GROUP_W) // tn
    n_group = ATT_GROUP_W // tn
    n_slab = D_MODEL // LANES
    return pl.pallas_call(
        functools.partial(_inproj_kernel, n_natural=n_natural, n_group=n_group),
        grid=(T // TILE, Y_COLS // tn),
        in_specs=[pl.BlockSpec((TILE, LANES), functools.partial(lambda i, j, s: (i, s), s=s))
                  for s in range(n_slab)] + [
            pl.BlockSpec((1, D_MODEL), lambda i, j: (0, 0)),
            pl.BlockSpec((D_MODEL, tn), lambda i, j: (0, j)),
            pl.BlockSpec((D_MODEL, LANES), lambda i, j: (0, 0)),
            pl.BlockSpec((LANES, GLA_QK_W), lambda i, j: (0, 0)),
            pl.BlockSpec((1, GLA_QK_W), lambda i, j: (0, 0)),
        ],
        out_specs=[
            pl.BlockSpec((TILE, tn), lambda i, j: (i, j)),
            pl.BlockSpec((TILE, GLA_QK_W), lambda i, j: (i, 0)),
        ],
        out_shape=[
            jax.ShapeDtypeStruct((T, Y_COLS), BF16),
            jax.ShapeDtypeStruct((T, GLA_QK_W), F32),
        ],
        scratch_shapes=[pltpu.VMEM((N_ATT_GROUPS, TILE, D_MODEL), BF16),
                        pltpu.VMEM((TILE, LANES), F32)],
        compiler_params=_params(("parallel", "arbitrary")),
        name="inproj",
    )(*([x2] * n_slab), ln1, w_main, wa1, wa2, ba)


def _split3(g):
    hi = g.astype(BF16)
    r1 = g - hi.astype(F32)
    mid = r1.astype(BF16)
    lo = (r1 - mid.astype(F32)).astype(BF16)
    return hi, mid, lo


def _gla_kernel(q_ref, k_ref, v_ref, r_ref, la_ref, gn_ref, o_ref, st_ref, *, nchunk):
    C = GLA_CHUNK

    @pl.when(pl.program_id(2) == 0)
    def _():
        st_ref[...] = jnp.zeros_like(st_ref)

    row = lax.broadcasted_iota(jnp.int32, (C, C), 0)
    col = lax.broadcasted_iota(jnp.int32, (C, C), 1)
    causal = row >= col
    tri = causal.astype(BF16)
    scale = GLA_DK ** -0.5
    gn = gn_ref[...]

    for c in range(nchunk):
        sl = pl.ds(c * C, C)
        g = la_ref[0, sl, :]
        hi, mid, lo = _split3(g)
        b = _dot(tri, hi) + _dot(tri, mid) + _dot(tri, lo)
        b_last = b[C - 1:C, :]
        q = q_ref[0, sl, :].astype(F32)
        k = k_ref[0, sl, :].astype(F32)
        v = v_ref[0, sl, :]
        q_dec = (q * scale * jnp.exp(b)).astype(BF16)
        k_in = (k * jnp.exp(-b)).astype(BF16)
        k_out = (k * jnp.exp(b_last - b)).astype(BF16)
        attn = jnp.where(causal, _dot_nt(q_dec, k_in), 0.0).astype(BF16)
        st = st_ref[...]
        o = _dot(attn, v) + _dot_nt(q_dec, st.astype(BF16))
        st_ref[...] = st * jnp.exp(b_last) + _dot_tn(v, k_out)
        r = r_ref[0, sl, :].astype(F32)
        o = _rms(o, gn) * (r * jax.nn.sigmoid(r))
        o_ref[0, sl, :] = o.astype(BF16)


def _gla(y3, la3, gn, tb=512):
    B, S, _ = y3.shape
    nq = COL_K // GLA_DK
    nv = COL_V // GLA_DV
    nr = COL_R // GLA_DV
    return pl.pallas_call(
        functools.partial(_gla_kernel, nchunk=tb // GLA_CHUNK),
        grid=(B, GLA_HEADS, S // tb),
        in_specs=[
            pl.BlockSpec((1, tb, GLA_DK), lambda b, h, t: (b, t, h)),
            pl.BlockSpec((1, tb, GLA_DK), lambda b, h, t: (b, t, nq + h)),
            pl.BlockSpec((1, tb, GLA_DV), lambda b, h, t: (b, t, nv + h)),
            pl.BlockSpec((1, tb, GLA_DV), lambda b, h, t: (b, t, nr + h)),
            pl.BlockSpec((1, tb, GLA_DK), lambda b, h, t: (b, t, h)),
            pl.BlockSpec((1, GLA_DV), lambda b, h, t: (0, h)),
        ],
        out_specs=pl.BlockSpec((1, tb, GLA_DV), lambda b, h, t: (b, t, h)),
        out_shape=jax.ShapeDtypeStruct((B, S, GLA_V_W), BF16),
        scratch_shapes=[pltpu.VMEM((GLA_DV, GLA_DK), F32)],
        compiler_params=_params(("parallel", "parallel", "arbitrary")),
        name="gla",
    )(y3, y3, y3, y3, la3, gn)


def _t5_causal_bucket(n):
    max_exact = REL_BUCKETS // 2
    nf = np.maximum(n, 1).astype(np.float32)
    large = max_exact + (np.log(nf / max_exact) / np.log(REL_MAX_DIST / max_exact)
                         * (REL_BUCKETS - max_exact)).astype(np.int32)
    large = np.minimum(large, REL_BUCKETS - 1)
    return np.where(n < max_exact, n, large).astype(np.int32)


def _bucket_table():
    qi = np.arange(ATT_BLOCK)[:, None]
    kj = np.arange(2 * ATT_BLOCK)[None, :]
    delta = qi + ATT_BLOCK - kj
    out = []
    for win, dil in ATT_GROUPS:
        band = (delta >= 0) & (delta <= win // dil)
        bucket = np.where(band, _t5_causal_bucket(np.maximum(delta, 0) * dil), -1)
        out.append(np.concatenate([bucket[:, ATT_BLOCK:], bucket[:, :ATT_BLOCK]], axis=1))
    return np.stack(out).astype(np.int32)


def _att_units(units, bias, first, acc_scr, m_scr, l_scr, merge):
    BLK = ATT_BLOCK
    scale = ATT_HEAD_DIM ** -0.5
    bias_c = bias[:, :BLK]
    bias_p = bias[:, BLK:]
    s_c = [_dot_nt(u["q"], u["kc"]) * scale + bias_c for u in units]
    s_p = [_dot_nt(u["q"], u["kp"]) * scale + bias_p for u in units]
    s_p = [jnp.where(first, NEG_INF, sp) if u["kp_from_prev_tile"] else sp for u, sp in zip(units, s_p)]
    m_new = [jnp.broadcast_to(jnp.maximum(jnp.max(sc, axis=-1, keepdims=True),
                                          jnp.max(sp, axis=-1, keepdims=True)), (BLK, BLK))
             for sc, sp in zip(s_c, s_p)]
    if merge:
        m_old = [m_scr[u["rows"], :] for u in units]
        m_new = [jnp.maximum(mo, mn) for mo, mn in zip(m_old, m_new)]
        alpha = [jnp.exp(mo - mn) for mo, mn in zip(m_old, m_new)]
    p_c = [jnp.exp(sc - mn) for sc, mn in zip(s_c, m_new)]
    p_p = [jnp.exp(sp - mn) for sp, mn in zip(s_p, m_new)]
    l_new = [jnp.broadcast_to(jnp.sum(pc, axis=-1, keepdims=True) + jnp.sum(pp, axis=-1, keepdims=True),
                              (BLK, BLK)) for pc, pp in zip(p_c, p_p)]
    acc = [_dot(pc.astype(BF16), u["vc"]) + _dot(pp.astype(BF16), u["vp"])
           for u, pc, pp in zip(units, p_c, p_p)]
    for i, u in enumerate(units):
        rows = u["rows"]
        if merge:
            acc_scr[rows, :] = alpha[i] * acc_scr[rows, :] + acc[i]
            l_scr[rows, :] = alpha[i] * l_scr[rows, :] + l_new[i]
        else:
            acc_scr[rows, :] = acc[i]
            l_scr[rows, :] = l_new[i]
        m_scr[rows, :] = m_new[i]


def _att_kernel(tab_ref, bucket_ref,
                q0, k0, v0, kp0, vp0, q1, k1, v1, kp1, vp1, q2, k2, v2, kp2, vp2,
                o_ref, bias_scr, acc_scr, m_scr, l_scr, *, tiles_per_batch, unroll):
    BLK = ATT_BLOCK
    ti = pl.program_id(0)
    h = pl.program_id(1)

    @pl.when((ti == 0) & (h == 0))
    def _():
        for g in range(N_ATT_GROUPS):
            bucket = bucket_ref[g]
            for hh in range(ATT_HEADS_PER_GROUP):
                bias = jnp.full(bucket.shape, NEG_INF, F32)
                for b in range(REL_BUCKETS):
                    bias = jnp.where(bucket == b, tab_ref[b, g * ATT_HEADS_PER_GROUP + hh], bias)
                bias_scr[g, hh] = bias

    first = (ti % tiles_per_batch) == 0
    groups = ((q0, k0, v0, kp0, vp0), (q1, k1, v1, kp1, vp1), (q2, k2, v2, kp2, vp2))
    for g, (q, k, v, kp, vp) in enumerate(groups):
        dil = ATT_DILS[g]
        n_r = TILE // dil
        kp_rows = kp.shape[0]
        units = []
        for r in range(dil):
            for j in range(n_r // BLK):
                cur = pl.ds(r * n_r + j * BLK, BLK)
                if j > 0:
                    prev = pl.ds(r * n_r + (j - 1) * BLK, BLK)
                    k_prev, v_prev = k[prev, :], v[prev, :]
                else:
                    prev = pl.ds((r * n_r + n_r - BLK) % kp_rows, BLK)
                    k_prev, v_prev = kp[prev, :], vp[prev, :]
                if dil == 1:
                    rows = pl.ds(j * BLK, BLK)
                else:
                    rows = pl.ds(dil * j * BLK + r, BLK, stride=dil)
                units.append(dict(q=q[cur, :], kc=k[cur, :], vc=v[cur, :], kp=k_prev, vp=v_prev,
                                  kp_from_prev_tile=(j == 0), rows=rows))
        bias = bias_scr[g, h]
        for i in range(0, len(units), unroll):
            _att_units(units[i:i + unroll], bias, first, acc_scr, m_scr, l_scr, merge=(g > 0))

    for c in range(TILE // 256):
        rows = pl.ds(c * 256, 256)
        o_ref[rows, :] = (acc_scr[rows, :] / l_scr[rows, :]).astype(BF16)


def _attention(y2, rel_bias, S, unroll=2):
    T = y2.shape[0]
    tiles_per_batch = S // TILE
    hd = ATT_HEAD_DIM

    def col(g, c):
        return (COL_ATT + g * ATT_GROUP_W + c * ATT_W) // hd

    def cur(g, c):
        return pl.BlockSpec((TILE, hd), lambda t, h: (t, col(g, c) + h))

    def prev_tile(g, c):
        return pl.BlockSpec((TILE, hd), lambda t, h: (jnp.maximum(t - 1, 0), col(g, c) + h))

    def prev_block(g, c):
        nb = TILE // ATT_BLOCK
        return pl.BlockSpec((ATT_BLOCK, hd), lambda t, h: (jnp.maximum(t * nb - 1, 0), col(g, c) + h))

    in_specs = [pl.BlockSpec(memory_space=pltpu.SMEM),
                pl.BlockSpec((N_ATT_GROUPS, ATT_BLOCK, 2 * ATT_BLOCK), lambda t, h: (0, 0, 0))]
    for g in range(N_ATT_GROUPS):
        prv = prev_block if ATT_DILS[g] == 1 else prev_tile
        in_specs += [cur(g, 0), cur(g, 1), cur(g, 2), prv(g, 1), prv(g, 2)]
    return pl.pallas_call(
        functools.partial(_att_kernel, tiles_per_batch=tiles_per_batch, unroll=unroll),
        grid=(T // TILE, ATT_HEADS_PER_GROUP),
        in_specs=in_specs,
        out_specs=pl.BlockSpec((TILE, hd), lambda t, h: (t, h)),
        out_shape=jax.ShapeDtypeStruct((T, ATT_W), BF16),
        scratch_shapes=[
            pltpu.VMEM((N_ATT_GROUPS, ATT_HEADS_PER_GROUP, ATT_BLOCK, 2 * ATT_BLOCK), F32),
            pltpu.VMEM((TILE, hd), F32),
            pltpu.VMEM((TILE, LANES), F32),
            pltpu.VMEM((TILE, LANES), F32),
        ],
        compiler_params=_params(("arbitrary", "arbitrary")),
        name="attention",
    )(rel_bias, jnp.asarray(_bucket_table()), *([y2] * (5 * N_ATT_GROUPS)))


def _mix_kernel(x_ref, og_ref, oa_ref, ga_ref, gb_ref, wog_ref, woa_ref, wout_ref, x1_ref):
    y_gla = _dot(og_ref[...], wog_ref[...])
    y_att = _dot(oa_ref[...], woa_ref[...])
    mix = (jax.nn.sigmoid(ga_ref[...].astype(F32)) * y_gla
           + jax.nn.sigmoid(gb_ref[...].astype(F32)) * y_att)
    x1_ref[...] = x_ref[...] + _dot(mix.astype(BF16), wout_ref[...])


def _mix(x2, y2, o_gla, o_att, wog, woa, wout, tm=512):
    T = x2.shape[0]
    gcol = COL_GATE // D_MODEL
    row = lambda w: pl.BlockSpec((tm, w), lambda i: (i, 0))
    full = lambda a: pl.BlockSpec(a.shape, lambda i: (0, 0))
    return pl.pallas_call(
        _mix_kernel,
        grid=(T // tm,),
        in_specs=[row(D_MODEL), row(GLA_V_W), row(ATT_W),
                  pl.BlockSpec((tm, D_MODEL), lambda i: (i, gcol)),
                  pl.BlockSpec((tm, D_MODEL), lambda i: (i, gcol + 1)),
                  full(wog), full(woa), full(wout)],
        out_specs=row(D_MODEL),
        out_shape=jax.ShapeDtypeStruct((T, D_MODEL), F32),
        compiler_params=_params(("parallel",)),
        name="mix",
    )(x2, o_gla, o_att, y2, y2, wog, woa, wout)


def _mlp_kernel(x_ref, p_ref, ln2_ref, w1_ref, w2_ref, ln3_ref, wpg_ref, wpp_ref, lnf_ref,
                out_ref, h_scr, acc_scr):
    f = pl.program_id(1)

    @pl.when(f == 0)
    def _():
        h_scr[...] = _rms(x_ref[...], ln2_ref[...]).astype(BF16)
        acc_scr[...] = jnp.zeros_like(acc_scr)

    u = jnp.maximum(_dot(h_scr[...], w1_ref[...]), 0.0)
    acc_scr[...] += _dot((u * u).astype(BF16), w2_ref[...])

    @pl.when(f == pl.num_programs(1) - 1)
    def _():
        x2 = x_ref[...] + acc_scr[...]
        h3 = _rms(x2, ln3_ref[...]).astype(BF16)
        gate = jax.nn.sigmoid(_dot(h3, wpg_ref[...]))
        x3 = x2 + gate * _dot(p_ref[...].astype(BF16), wpp_ref[...])
        out_ref[...] = _rms(x3, lnf_ref[...])


def _mlp(x1, p2, ln2, w1, w2, ln3, wpg, wpp, lnf, tm=1024, tf=1024):
    T = x1.shape[0]
    vec = pl.BlockSpec((1, D_MODEL), lambda i, f: (0, 0))
    return pl.pallas_call(
        _mlp_kernel,
        grid=(T // tm, D_FF // tf),
        in_specs=[
            pl.BlockSpec((tm, D_MODEL), lambda i, f: (i, 0)),
            pl.BlockSpec((tm, PLE_DIM), lambda i, f: (i, 0)),
            vec,
            pl.BlockSpec((D_MODEL, tf), lambda i, f: (0, f)),
            pl.BlockSpec((tf, D_MODEL), lambda i, f: (f, 0)),
            vec,
            pl.BlockSpec((D_MODEL, D_MODEL), lambda i, f: (0, 0)),
            pl.BlockSpec((PLE_DIM, D_MODEL), lambda i, f: (0, 0)),
            vec,
        ],
        out_specs=pl.BlockSpec((tm, D_MODEL), lambda i, f: (i, 0)),
        out_shape=jax.ShapeDtypeStruct((T, D_MODEL), F32),
        scratch_shapes=[pltpu.VMEM((tm, D_MODEL), BF16), pltpu.VMEM((tm, D_MODEL), F32)],
        compiler_params=_params(("parallel", "arbitrary")),
        name="mlp",
    )(x1, p2, ln2, w1, w2, ln3, wpg, wpp, lnf)


def _layer(x2, p2, B, S, ln1, w_in, w_a2, b_a, gla_gn, w_o_gla, w_o_attn, w_out,
           ln2, w_mlp1, w_mlp2, ln3, w_pp, w_pg, rel_bias, ln_out):
    sec = np.cumsum((0, GLA_QK_W, GLA_QK_W, GLA_V_W, GLA_V_W, GLA_RANK, N_ATT_GROUPS * ATT_GROUP_W))
    w_main = jnp.concatenate(
        [w_in[:, :sec[4]], w_in[:, sec[6]:], w_in[:, sec[5]:sec[6]]], axis=1).astype(BF16)
    wa1 = jnp.pad(w_in[:, sec[4]:sec[5]], ((0, 0), (0, LANES - GLA_RANK))).astype(BF16)
    wa2 = jnp.pad(w_a2, ((0, LANES - GLA_RANK), (0, 0)))

    y2, la = _inproj(x2, ln1[None], w_main, wa1, wa2, b_a[None])
    o_gla = _gla(y2.reshape(B, S, Y_COLS), la.reshape(B, S, GLA_QK_W), gla_gn[None])
    o_att = _attention(y2, rel_bias, S)
    x1 = _mix(x2, y2, o_gla.reshape(B * S, GLA_V_W), o_att,
              w_o_gla.astype(BF16), w_o_attn.astype(BF16), w_out.astype(BF16))
    return _mlp(x1, p2, ln2[None], w_mlp1.astype(BF16), w_mlp2.astype(BF16), ln3[None],
                w_pg.astype(BF16), w_pp.astype(BF16), ln_out[None])


def kernel(x, p, ln1, w_in, w_a2, b_a, gla_gn, w_o_gla, w_o_attn, w_out, ln2, w_mlp1, w_mlp2,
           ln3, w_pp, w_pg, rel_bias, ln_f):
    B, S, D = x.shape
    assert p.shape[0] == 1, "the final norm is fused into the single layer's last kernel"
    assert S % TILE == 0
    x2 = x.reshape(B * S, D)
    out = _layer(x2, p[0].reshape(B * S, PLE_DIM), B, S, ln1[0], w_in[0], w_a2[0], b_a[0],
                 gla_gn[0], w_o_gla[0], w_o_attn[0], w_out[0], ln2[0], w_mlp1[0], w_mlp2[0],
                 ln3[0], w_pp[0], w_pg[0], rel_bias, ln_f)
    return out.reshape(B, S, D)
```

```python
import functools

import numpy as np
import jax
import jax.numpy as jnp
from jax import lax
from jax.experimental import pallas as pl
from jax.experimental.pallas import tpu as pltpu

F32 = jnp.float32
BF16 = jnp.bfloat16

D_MODEL = 1024
PLE_DIM = 256
EPS = 1e-6
GLA_HEADS = 4
GLA_DK = 128
GLA_DV = 256
GLA_RANK = 16
GLA_TAU = 16.0
GLA_CHUNK = 64
GLA_QK_W = GLA_HEADS * GLA_DK
GLA_V_W = GLA_HEADS * GLA_DV
ATT_GROUPS = ((128, 1), (512, 4), (2048, 16))
ATT_DILS = tuple(d for _, d in ATT_GROUPS)
ATT_HEADS_PER_GROUP = 4
ATT_HEAD_DIM = 128
N_ATT_GROUPS = len(ATT_GROUPS)
ATT_W = ATT_HEADS_PER_GROUP * ATT_HEAD_DIM
ATT_BLOCK = 128
REL_BUCKETS = 32
REL_MAX_DIST = 2048
D_FF = 4 * D_MODEL
NEG_INF = -1e30

LANES = 128

COL_Q = 0
COL_K = COL_Q + GLA_QK_W
COL_V = COL_K + GLA_QK_W
COL_R = COL_V + GLA_V_W
COL_GATE = COL_R + GLA_V_W
COL_ATT = COL_GATE + 2 * D_MODEL
ATT_GROUP_W = 3 * ATT_W
Y_COLS = COL_ATT + N_ATT_GROUPS * ATT_GROUP_W

TILE = max(ATT_DILS) * ATT_BLOCK

VMEM_LIMIT = 56 * 1024 * 1024


def _params(semantics):
    return pltpu.CompilerParams(dimension_semantics=semantics, vmem_limit_bytes=VMEM_LIMIT)


def _rms(x, g):
    return x * lax.rsqrt(jnp.mean(x * x, axis=-1, keepdims=True) + EPS) * g


def _dot(a, b):
    return jnp.dot(a, b, preferred_element_type=F32)


def _dot_nt(a, b):
    return lax.dot_general(a, b, (((1,), (1,)), ((), ())), preferred_element_type=F32)


def _dot_tn(a, b):
    return lax.dot_general(a, b, (((0,), (0,)), ((), ())), preferred_element_type=F32)


def _inproj_kernel(*refs, n_natural, n_group):
    n_slab = D_MODEL // LANES
    x_slabs = refs[:n_slab]
    ln_ref, w_ref, wa1_ref, wa2_ref, ba_ref, y_ref, la_ref, h_scr, inv_scr = refs[n_slab:]
    j = pl.program_id(1)

    @pl.when(j == 0)
    def _():
        for d, dil in enumerate(ATT_DILS):
            n_r = TILE // dil
            chunk = min(n_r, 256)
            for r in range(dil):
                for c in range(n_r // chunk):
                    dst = pl.ds(r * n_r + c * chunk, chunk)
                    if dil == 1:
                        src = dst
                        parts = [xr[src, :] for xr in x_slabs]
                        sq = parts[0] * parts[0]
                        for part in parts[1:]:
                            sq = sq + part * part
                        ms = jnp.sum(sq, axis=-1, keepdims=True) * (1.0 / D_MODEL)
                        inv = jnp.broadcast_to(lax.rsqrt(ms + EPS), (chunk, LANES))
                        inv_scr[src, :] = inv
                    else:
                        src = pl.ds(r + dil * c * chunk, chunk, stride=dil)
                        parts = [xr[src, :] for xr in x_slabs]
                        inv = inv_scr[src, :]
                    for s, part in enumerate(parts):
                        cols = pl.ds(s * LANES, LANES)
                        h_scr[d, dst, cols] = (part * inv * ln_ref[:, cols]).astype(BF16)
        ha = _dot(h_scr[0], wa1_ref[...])
        z = jnp.dot(ha, wa2_ref[...], preferred_element_type=F32,
                    precision=lax.Precision.HIGHEST) + ba_ref[...]
        log_sig = jnp.minimum(z, 0.0) - jnp.log(1.0 + jnp.exp(-jnp.abs(z)))
        la_ref[...] = log_sig * (1.0 / GLA_TAU)

    d = (j >= n_natural).astype(jnp.int32) + (j >= n_natural + n_group).astype(jnp.int32)
    y_ref[...] = _dot(h_scr[d], w_ref[...]).astype(BF16)


def _inproj(x2, ln1, w_main, wa1, wa2, ba, tn=512):
    T = x2.shape[0]
    n_natural = (COL_ATT + ATT_GROUP_W) // tn
    n_group = ATT_GROUP_W // tn
    n_slab = D_MODEL // LANES
    return pl.pallas_call(
        functools.partial(_inproj_kernel, n_natural=n_natural, n_group=n_group),
        grid=(T // TILE, Y_COLS // tn),
        in_specs=[pl.BlockSpec((TILE, LANES), functools.partial(lambda i, j, s: (i, s), s=s))
                  for s in range(n_slab)] + [
            pl.BlockSpec((1, D_MODEL), lambda i, j: (0, 0)),
            pl.BlockSpec((D_MODEL, tn), lambda i, j: (0, j)),
            pl.BlockSpec((D_MODEL, LANES), lambda i, j: (0, 0)),
            pl.BlockSpec((LANES, GLA_QK_W), lambda i, j: (0, 0)),
            pl.BlockSpec((1, GLA_QK_W), lambda i, j: (0, 0)),
        ],
        out_specs=[
            pl.BlockSpec((TILE, tn), lambda i, j: (i, j)),
            pl.BlockSpec((TILE, GLA_QK_W), lambda i, j: (i, 0)),
        ],
        out_shape=[
            jax.ShapeDtypeStruct((T, Y_COLS), BF16),
            jax.ShapeDtypeStruct((T, GLA_QK_W), F32),
        ],
        scratch_shapes=[pltpu.VMEM((N_ATT_GROUPS, TILE, D_MODEL), BF16),
                        pltpu.VMEM((TILE, LANES), F32)],
        compiler_params=_params(("parallel", "arbitrary")),
        name="inproj",
    )(*([x2] * n_slab), ln1, w_main, wa1, wa2, ba)


def _split3(g):
    hi = g.astype(BF16)
    r1 = g - hi.astype(F32)
    mid = r1.astype(BF16)
    lo = (r1 - mid.astype(F32)).astype(BF16)
    return hi, mid, lo


def _gla_kernel(q_ref, k_ref, v_ref, r_ref, la_ref, gn_ref, o_ref, st_ref, *, nchunk):
    C = GLA_CHUNK

    @pl.when(pl.program_id(2) == 0)
    def _():
        st_ref[...] = jnp.zeros_like(st_ref)

    row = lax.broadcasted_iota(jnp.int32, (C, C), 0)
    col = lax.broadcasted_iota(jnp.int32, (C, C), 1)
    causal = row >= col
    tri = causal.astype(BF16)
    gn = gn_ref[...]

    for c in range(nchunk):
        sl = pl.ds(c * C, C)
        g = la_ref[0, sl, :]
        hi, mid, lo = _split3(g)
        b = _dot(tri, hi) + _dot(tri, mid) + _dot(tri, lo)
        b_last = b[C - 1:C, :]
        q = q_ref[0, sl, :].astype(F32)
        k = k_ref[0, sl, :].astype(F32)
        v = v_ref[0, sl, :]
        q_dec = (q * jnp.exp(b)).astype(BF16)
        k_in = (k * jnp.exp(-b)).astype(BF16)
        k_out = (k * jnp.exp(b_last - b)).astype(BF16)
        attn = jnp.where(causal, _dot_nt(q_dec, k_in), 0.0).astype(BF16)
        st = st_ref[...]
        o = _dot(attn, v) + _dot_nt(q_dec, st.astype(BF16))
        st_ref[...] = st * jnp.exp(b_last) + _dot_tn(v, k_out)
        r = r_ref[0, sl, :].astype(F32)
        o = _rms(o, gn) * (r * jax.nn.sigmoid(r))
        o_ref[0, sl, :] = o.astype(BF16)


def _gla(y3, la3, gn, tb=512):
    B, S, _ = y3.shape
    nq = COL_K // GLA_DK
    nv = COL_V // GLA_DV
    nr = COL_R // GLA_DV
    return pl.pallas_call(
        functools.partial(_gla_kernel, nchunk=tb // GLA_CHUNK),
        grid=(B, GLA_HEADS, S // tb),
        in_specs=[
            pl.BlockSpec((1, tb, GLA_DK), lambda b, h, t: (b, t, h)),
            pl.BlockSpec((1, tb, GLA_DK), lambda b, h, t: (b, t, nq + h)),
            pl.BlockSpec((1, tb, GLA_DV), lambda b, h, t: (b, t, nv + h)),
            pl.BlockSpec((1, tb, GLA_DV), lambda b, h, t: (b, t, nr + h)),
            pl.BlockSpec((1, tb, GLA_DK), lambda b, h, t: (b, t, h)),
            pl.BlockSpec((1, GLA_DV), lambda b, h, t: (0, h)),
        ],
        out_specs=pl.BlockSpec((1, tb, GLA_DV), lambda b, h, t: (b, t, h)),
        out_shape=jax.ShapeDtypeStruct((B, S, GLA_V_W), BF16),
        scratch_shapes=[pltpu.VMEM((GLA_DV, GLA_DK), F32)],
        compiler_params=_params(("parallel", "parallel", "arbitrary")),
        name="gla",
    )(y3, y3, y3, y3, la3, gn)


def _t5_causal_bucket(n):
    max_exact = REL_BUCKETS // 2
    nf = np.maximum(n, 1).astype(np.float32)
    large = max_exact + (np.log(nf / max_exact) / np.log(REL_MAX_DIST / max_exact)
                         * (REL_BUCKETS - max_exact)).astype(np.int32)
    large = np.minimum(large, REL_BUCKETS - 1)
    return np.where(n < max_exact, n, large).astype(np.int32)


def _bucket_table():
    qi = np.arange(ATT_BLOCK)[:, None]
    kj = np.arange(2 * ATT_BLOCK)[None, :]
    delta = qi + ATT_BLOCK - kj
    out = []
    for win, dil in ATT_GROUPS:
        band = (delta >= 0) & (delta <= win // dil)
        bucket = np.where(band, _t5_causal_bucket(np.maximum(delta, 0) * dil), -1)
        out.append(bucket)
    return np.stack(out).astype(np.int32)


def _att_unit(q, k, v, bias):
    BLK = ATT_BLOCK
    s = _dot_nt(q, k) + bias
    m = jnp.max(jnp.maximum(s[:, :BLK], s[:, BLK:]), axis=-1, keepdims=True)
    p = jnp.exp(s - m).astype(BF16)
    v_ext = jnp.concatenate([v, jnp.ones_like(v)], axis=1)
    pv = _dot(p, v_ext)
    denom = pv[:, BLK:]
    return pv[:, :BLK] / denom, m + jnp.log(denom)


def _att_kernel(tab_ref, bucket_ref,
                q0, k0, v0, kp0, vp0, q1, k1, v1, kp1, vp1, q2, k2, v2, kp2, vp2,
                o_ref, bias_scr, o_scr, lse_scr, *, tiles_per_batch):
    BLK = ATT_BLOCK
    ti = pl.program_id(0)
    h = pl.program_id(1)

    @pl.when((ti == 0) & (h == 0))
    def _():
        for g in range(N_ATT_GROUPS):
            bucket = bucket_ref[g]
            for hh in range(ATT_HEADS_PER_GROUP):
                bias = jnp.full(bucket.shape, NEG_INF, F32)
                for b in range(REL_BUCKETS):
                    bias = jnp.where(bucket == b, tab_ref[b, g * ATT_HEADS_PER_GROUP + hh], bias)
                bias_scr[g, hh] = bias

    lane = lax.broadcasted_iota(jnp.int32, (BLK, 2 * BLK), 1)
    no_prev = jnp.logical_and((ti % tiles_per_batch) == 0, lane < BLK)
    groups = ((q0, k0, v0, kp0, vp0), (q1, k1, v1, kp1, vp1), (q2, k2, v2, kp2, vp2))
    for g, (q, k, v, kp, vp) in enumerate(groups):
        dil = ATT_DILS[g]
        n_r = TILE // dil
        kp_rows = kp.shape[0]
        bias = bias_scr[g, h]
        bias_first = jnp.where(no_prev, NEG_INF, bias)
        for r in range(dil):
            for j in range(n_r // BLK):
                cur = pl.ds(r * n_r + j * BLK, BLK)
                if j > 0:
                    both = pl.ds(r * n_r + (j - 1) * BLK, 2 * BLK)
                    k_all, v_all, b = k[both, :], v[both, :], bias
                else:
                    prev = pl.ds((r * n_r + n_r - BLK) % kp_rows, BLK)
                    k_all = jnp.concatenate([kp[prev, :], k[cur, :]], axis=0)
                    v_all = jnp.concatenate([vp[prev, :], v[cur, :]], axis=0)
                    b = bias_first
                if dil == 1:
                    rows = pl.ds(j * BLK, BLK)
                else:
                    rows = pl.ds(dil * j * BLK + r, BLK, stride=dil)
                o_scr[g, rows, :], lse_scr[g, rows, :] = _att_unit(q[cur, :], k_all, v_all, b)

    for c in range(TILE // 256):
        rows = pl.ds(c * 256, 256)
        lse = [lse_scr[g, rows, :] for g in range(N_ATT_GROUPS)]
        top = functools.reduce(jnp.maximum, lse)
        w = [jnp.exp(x - top) for x in lse]
        num = functools.reduce(lambda a, b: a + b, [w[g] * o_scr[g, rows, :] for g in range(N_ATT_GROUPS)])
        den = functools.reduce(lambda a, b: a + b, w)
        o_ref[rows, :] = (num / den).astype(BF16)


def _attention(y2, rel_bias, S):
    T = y2.shape[0]
    tiles_per_batch = S // TILE
    hd = ATT_HEAD_DIM

    def col(g, c):
        return (COL_ATT + g * ATT_GROUP_W + c * ATT_W) // hd

    def cur(g, c):
        return pl.BlockSpec((TILE, hd), lambda t, h: (t, col(g, c) + h))

    def prev_tile(g, c):
        return pl.BlockSpec((TILE, hd), lambda t, h: (jnp.maximum(t - 1, 0), col(g, c) + h))

    def prev_block(g, c):
        nb = TILE // ATT_BLOCK
        return pl.BlockSpec((ATT_BLOCK, hd), lambda t, h: (jnp.maximum(t * nb - 1, 0), col(g, c) + h))

    in_specs = [pl.BlockSpec(memory_space=pltpu.SMEM),
                pl.BlockSpec((N_ATT_GROUPS, ATT_BLOCK, 2 * ATT_BLOCK), lambda t, h: (0, 0, 0))]
    for g in range(N_ATT_GROUPS):
        prv = prev_block if ATT_DILS[g] == 1 else prev_tile
        in_specs += [cur(g, 0), cur(g, 1), cur(g, 2), prv(g, 1), prv(g, 2)]
    return pl.pallas_call(
        functools.partial(_att_kernel, tiles_per_batch=tiles_per_batch),
        grid=(T // TILE, ATT_HEADS_PER_GROUP),
        in_specs=in_specs,
        out_specs=pl.BlockSpec((TILE, hd), lambda t, h: (t, h)),
        out_shape=jax.ShapeDtypeStruct((T, ATT_W), BF16),
        scratch_shapes=[
            pltpu.VMEM((N_ATT_GROUPS, ATT_HEADS_PER_GROUP, ATT_BLOCK, 2 * ATT_BLOCK), F32),
            pltpu.VMEM((N_ATT_GROUPS, TILE, hd), F32),
            pltpu.VMEM((N_ATT_GROUPS, TILE, LANES), F32),
        ],
        compiler_params=_params(("arbitrary", "arbitrary")),
        name="attention",
    )(rel_bias, jnp.asarray(_bucket_table()), *([y2] * (5 * N_ATT_GROUPS)))


def _mix_kernel(x_ref, og_ref, oa_ref, ga_ref, gb_ref, wog_ref, woa_ref, wout_ref, x1_ref):
    y_gla = _dot(og_ref[...], wog_ref[...])
    y_att = _dot(oa_ref[...], woa_ref[...])
    mix = (jax.nn.sigmoid(ga_ref[...].astype(F32)) * y_gla
           + jax.nn.sigmoid(gb_ref[...].astype(F32)) * y_att)
    x1_ref[...] = x_ref[...] + _dot(mix.astype(BF16), wout_ref[...])


def _mix(x2, y2, o_gla, o_att, wog, woa, wout, tm=512):
    T = x2.shape[0]
    gcol = COL_GATE // D_MODEL
    row = lambda w: pl.BlockSpec((tm, w), lambda i: (i, 0))
    full = lambda a: pl.BlockSpec(a.shape, lambda i: (0, 0))
    return pl.pallas_call(
        _mix_kernel,
        grid=(T // tm,),
        in_specs=[row(D_MODEL), row(GLA_V_W), row(ATT_W),
                  pl.BlockSpec((tm, D_MODEL), lambda i: (i, gcol)),
                  pl.BlockSpec((tm, D_MODEL), lambda i: (i, gcol + 1)),
                  full(wog), full(woa), full(wout)],
        out_specs=row(D_MODEL),
        out_shape=jax.ShapeDtypeStruct((T, D_MODEL), F32),
        compiler_params=_params(("parallel",)),
        name="mix",
    )(x2, o_gla, o_att, y2, y2, wog, woa, wout)


def _mlp_kernel(x_ref, p_ref, ln2_ref, w1_ref, w2_ref, ln3_ref, wpg_ref, wpp_ref, lnf_ref,
                out_ref, h_scr, acc_scr):
    f = pl.program_id(1)

    @pl.when(f == 0)
    def _():
        h_scr[...] = _rms(x_ref[...], ln2_ref[...]).astype(BF16)
        acc_scr[...] = jnp.zeros_like(acc_scr)

    u = jnp.maximum(_dot(h_scr[...], w1_ref[...]), 0.0)
    acc_scr[...] += _dot((u * u).astype(BF16), w2_ref[...])

    @pl.when(f == pl.num_programs(1) - 1)
    def _():
        x2 = x_ref[...] + acc_scr[...]
        h3 = _rms(x2, ln3_ref[...]).astype(BF16)
        gate = jax.nn.sigmoid(_dot(h3, wpg_ref[...]))
        x3 = x2 + gate * _dot(p_ref[...].astype(BF16), wpp_ref[...])
        out_ref[...] = _rms(x3, lnf_ref[...])


def _mlp(x1, p2, ln2, w1, w2, ln3, wpg, wpp, lnf, tm=1024, tf=1024):
    T = x1.shape[0]
    vec = pl.BlockSpec((1, D_MODEL), lambda i, f: (0, 0))
    return pl.pallas_call(
        _mlp_kernel,
        grid=(T // tm, D_FF // tf),
        in_specs=[
            pl.BlockSpec((tm, D_MODEL), lambda i, f: (i, 0)),
            pl.BlockSpec((tm, PLE_DIM), lambda i, f: (i, 0)),
            vec,
            pl.BlockSpec((D_MODEL, tf), lambda i, f: (0, f)),
            pl.BlockSpec((tf, D_MODEL), lambda i, f: (f, 0)),
            vec,
            pl.BlockSpec((D_MODEL, D_MODEL), lambda i, f: (0, 0)),
            pl.BlockSpec((PLE_DIM, D_MODEL), lambda i, f: (0, 0)),
            vec,
        ],
        out_specs=pl.BlockSpec((tm, D_MODEL), lambda i, f: (i, 0)),
        out_shape=jax.ShapeDtypeStruct((T, D_MODEL), F32),
        scratch_shapes=[pltpu.VMEM((tm, D_MODEL), BF16), pltpu.VMEM((tm, D_MODEL), F32)],
        compiler_params=_params(("parallel", "arbitrary")),
        name="mlp",
    )(x1, p2, ln2, w1, w2, ln3, wpg, wpp, lnf)


def _layer(x2, p2, B, S, ln1, w_in, w_a2, b_a, gla_gn, w_o_gla, w_o_attn, w_out,
           ln2, w_mlp1, w_mlp2, ln3, w_pp, w_pg, rel_bias, ln_out):
    sec = np.cumsum((0, GLA_QK_W, GLA_QK_W, GLA_V_W, GLA_V_W, GLA_RANK, N_ATT_GROUPS * ATT_GROUP_W))
    col_scale = np.ones((1, Y_COLS), np.float32)
    col_scale[:, COL_Q:COL_K] = GLA_DK ** -0.5
    for g in range(N_ATT_GROUPS):
        col_scale[:, COL_ATT + g * ATT_GROUP_W:COL_ATT + g * ATT_GROUP_W + ATT_W] = ATT_HEAD_DIM ** -0.5
    w_main = (jnp.concatenate([w_in[:, :sec[4]], w_in[:, sec[6]:], w_in[:, sec[5]:sec[6]]], axis=1)
              * col_scale).astype(BF16)
    wa1 = jnp.pad(w_in[:, sec[4]:sec[5]], ((0, 0), (0, LANES - GLA_RANK))).astype(BF16)
    wa2 = jnp.pad(w_a2, ((0, LANES - GLA_RANK), (0, 0)))

    y2, la = _inproj(x2, ln1[None], w_main, wa1, wa2, b_a[None])
    o_gla = _gla(y2.reshape(B, S, Y_COLS), la.reshape(B, S, GLA_QK_W), gla_gn[None])
    o_att = _attention(y2, rel_bias, S)
    x1 = _mix(x2, y2, o_gla.reshape(B * S, GLA_V_W), o_att,
              w_o_gla.astype(BF16), w_o_attn.astype(BF16), w_out.astype(BF16))
    return _mlp(x1, p2, ln2[None], w_mlp1.astype(BF16), w_mlp2.astype(BF16), ln3[None],
                w_pg.astype(BF16), w_pp.astype(BF16), ln_out[None])


def kernel(x, p, ln1, w_in, w_a2, b_a, gla_gn, w_o_gla, w_o_attn, w_out, ln2, w_mlp1, w_mlp2,
           ln3, w_pp, w_pg, rel_bias, ln_f):
    B, S, D = x.shape
    assert p.shape[0] == 1, "the final norm is fused into the single layer's last kernel"
    assert S % TILE == 0
    x2 = x.reshape(B * S, D)
    out = _layer(x2, p[0].reshape(B * S, PLE_DIM), B, S, ln1[0], w_in[0], w_a2[0], b_a[0],
                 gla_gn[0], w_o_gla[0], w_o_attn[0], w_out[0], ln2[0], w_mlp1[0], w_mlp2[0],
                 ln3[0], w_pp[0], w_pg[0], rel_bias, ln_f)
    return out.reshape(B, S, D)
```

```python
import functools

import numpy as np
import jax
import jax.numpy as jnp
from jax import lax
from jax.experimental import pallas as pl
from jax.experimental.pallas import tpu as pltpu

F32 = jnp.float32
BF16 = jnp.bfloat16

D_MODEL = 1024
PLE_DIM = 256
EPS = 1e-6
GLA_HEADS = 4
GLA_DK = 128
GLA_DV = 256
GLA_RANK = 16
GLA_TAU = 16.0
GLA_CHUNK = 64
GLA_QK_W = GLA_HEADS * GLA_DK
GLA_V_W = GLA_HEADS * GLA_DV
ATT_GROUPS = ((128, 1), (512, 4), (2048, 16))
ATT_DILS = tuple(d for _, d in ATT_GROUPS)
ATT_HEADS_PER_GROUP = 4
ATT_HEAD_DIM = 128
N_ATT_GROUPS = len(ATT_GROUPS)
ATT_W = ATT_HEADS_PER_GROUP * ATT_HEAD_DIM
ATT_BLOCK = 128
REL_BUCKETS = 32
REL_MAX_DIST = 2048
D_FF = 4 * D_MODEL
NEG_INF = -1e30

LANES = 128

COL_Q = 0
COL_K = COL_Q + GLA_QK_W
COL_V = COL_K + GLA_QK_W
COL_R = COL_V + GLA_V_W
COL_GATE = COL_R + GLA_V_W
COL_ATT = COL_GATE + 2 * D_MODEL
ATT_GROUP_W = 3 * ATT_W
Y_COLS = COL_ATT + N_ATT_GROUPS * ATT_GROUP_W

TILE = max(ATT_DILS) * ATT_BLOCK

VMEM_LIMIT = 56 * 1024 * 1024


def _params(semantics):
    return pltpu.CompilerParams(dimension_semantics=semantics, vmem_limit_bytes=VMEM_LIMIT)


def _rms(x, g):
    return x * lax.rsqrt(jnp.mean(x * x, axis=-1, keepdims=True) + EPS) * g


def _dot(a, b):
    return jnp.dot(a, b, preferred_element_type=F32)


def _dot_nt(a, b):
    return lax.dot_general(a, b, (((1,), (1,)), ((), ())), preferred_element_type=F32)


def _dot_tn(a, b):
    return lax.dot_general(a, b, (((0,), (0,)), ((), ())), preferred_element_type=F32)


def _inproj_kernel(*refs, n_natural, n_group):
    n_slab = D_MODEL // LANES
    x_slabs = refs[:n_slab]
    ln_ref, w_ref, wa1_ref, wa2_ref, ba_ref, y_ref, la_ref, h_scr, inv_scr, tmp_scr = refs[n_slab:]
    j = pl.program_id(1)

    @pl.when(j == 0)
    def _():
        chunk = 256
        for c in range(TILE // chunk):
            rows = pl.ds(c * chunk, chunk)
            parts = [xr[rows, :] for xr in x_slabs]
            sq = parts[0] * parts[0]
            for part in parts[1:]:
                sq = sq + part * part
            ms = jnp.sum(sq, axis=-1, keepdims=True) * (1.0 / D_MODEL)
            inv = jnp.broadcast_to(lax.rsqrt(ms + EPS), (chunk, LANES))
            inv_scr[rows, :] = inv
            for s, part in enumerate(parts):
                cols = pl.ds(s * LANES, LANES)
                h_scr[0, rows, cols] = (part * inv * ln_ref[:, cols]).astype(BF16)
        assert ATT_DILS == (1, 4, 16)
        n4, n16 = TILE // 4, TILE // 16
        for s, xr in enumerate(x_slabs):
            cols = pl.ds(s * LANES, LANES)
            ln_s = ln_ref[:, cols]
            for a in range(4):
                for c in range(n4 // chunk):
                    src = pl.ds(a + 4 * c * chunk, chunk, stride=4)
                    dst = pl.ds(a * n4 + c * chunk, chunk)
                    hn = xr[src, :] * inv_scr[src, :] * ln_s
                    tmp_scr[dst, :] = hn
                    h_scr[1, dst, cols] = hn.astype(BF16)
            for a in range(4):
                for b in range(4):
                    src = pl.ds(a * n4 + b, n16, stride=4)
                    h_scr[2, pl.ds((a + 4 * b) * n16, n16), cols] = tmp_scr[src, :].astype(BF16)
        ha = _dot(h_scr[0], wa1_ref[...])
        z = jnp.dot(ha, wa2_ref[...], preferred_element_type=F32,
                    precision=lax.Precision.HIGHEST) + ba_ref[...]
        log_sig = jnp.minimum(z, 0.0) - jnp.log(1.0 + jnp.exp(-jnp.abs(z)))
        la_ref[...] = log_sig * (1.0 / GLA_TAU)

    d = (j >= n_natural).astype(jnp.int32) + (j >= n_natural + n_group).astype(jnp.int32)
    y_ref[...] = _dot(h_scr[d], w_ref[...]).astype(BF16)


def _inproj(x2, ln1, w_main, wa1, wa2, ba, tn=512):
    T = x2.shape[0]
    n_natural = (COL_ATT + ATT_GROUP_W) // tn
    n_group = ATT_GROUP_W // tn
    n_slab = D_MODEL // LANES
    return pl.pallas_call(
        functools.partial(_inproj_kernel, n_natural=n_natural, n_group=n_group),
        grid=(T // TILE, Y_COLS // tn),
        in_specs=[pl.BlockSpec((TILE, LANES), functools.partial(lambda i, j, s: (i, s), s=s))
                  for s in range(n_slab)] + [
            pl.BlockSpec((1, D_MODEL), lambda i, j: (0, 0)),
            pl.BlockSpec((D_MODEL, tn), lambda i, j: (0, j)),
            pl.BlockSpec((D_MODEL, LANES), lambda i, j: (0, 0)),
            pl.BlockSpec((LANES, GLA_QK_W), lambda i, j: (0, 0)),
            pl.BlockSpec((1, GLA_QK_W), lambda i, j: (0, 0)),
        ],
        out_specs=[
            pl.BlockSpec((TILE, tn), lambda i, j: (i, j)),
            pl.BlockSpec((TILE, GLA_QK_W), lambda i, j: (i, 0)),
        ],
        out_shape=[
            jax.ShapeDtypeStruct((T, Y_COLS), BF16),
            jax.ShapeDtypeStruct((T, GLA_QK_W), F32),
        ],
        scratch_shapes=[pltpu.VMEM((N_ATT_GROUPS, TILE, D_MODEL), BF16),
                        pltpu.VMEM((TILE, LANES), F32),
                        pltpu.VMEM((TILE, LANES), F32)],
        compiler_params=_params(("parallel", "arbitrary")),
        name="inproj",
    )(*([x2] * n_slab), ln1, w_main, wa1, wa2, ba)


def _split3(g):
    hi = g.astype(BF16)
    r1 = g - hi.astype(F32)
    mid = r1.astype(BF16)
    lo = (r1 - mid.astype(F32)).astype(BF16)
    return hi, mid, lo


def _gla_kernel(q_ref, k_ref, v_ref, r_ref, la_ref, gn_ref, o_ref, st_ref, *, nchunk):
    C = GLA_CHUNK

    @pl.when(pl.program_id(1) == 0)
    def _():
        st_ref[...] = jnp.zeros_like(st_ref)

    row = lax.broadcasted_iota(jnp.int32, (C, C), 0)
    col = lax.broadcasted_iota(jnp.int32, (C, C), 1)
    causal = row >= col
    tri = causal.astype(BF16)

    H = range(GLA_HEADS)
    ck = [slice(h * GLA_DK, (h + 1) * GLA_DK) for h in H]
    cv = [slice(h * GLA_DV, (h + 1) * GLA_DV) for h in H]
    for c in range(nchunk):
        sl = pl.ds(c * C, C)
        hi, mid, lo = _split3(la_ref[0, sl, :])
        b = _dot(tri, hi) + _dot(tri, mid) + _dot(tri, lo)
        decay = jnp.exp(b[C - 1:C, :])
        q_dec = (q_ref[0, sl, :].astype(F32) * jnp.exp(b)).astype(BF16)
        k_in = k_ref[0, sl, :].astype(F32) * jnp.exp(-b)
        k_out = (k_in * decay).astype(BF16)
        k_in = k_in.astype(BF16)
        v = [v_ref[0, sl, cv[h]] for h in H]
        attn = [_dot_nt(q_dec[:, ck[h]], k_in[:, ck[h]]) for h in H]
        st = [st_ref[h] for h in H]
        o_inter = [_dot_nt(q_dec[:, ck[h]], st[h].astype(BF16)) for h in H]
        upd = [_dot_tn(v[h], k_out[:, ck[h]]) for h in H]
        o = [_dot(jnp.where(causal, attn[h], 0.0).astype(BF16), v[h]) + o_inter[h] for h in H]
        for h in H:
            st_ref[h] = st[h] * decay[:, ck[h]] + upd[h]
        for h in H:
            r = r_ref[0, sl, cv[h]].astype(F32)
            o_ref[0, sl, cv[h]] = (_rms(o[h], gn_ref[:, cv[h]]) * (r * jax.nn.sigmoid(r))).astype(BF16)


def _gla(y3, la3, gn, tb=512):
    B, S, _ = y3.shape
    return pl.pallas_call(
        functools.partial(_gla_kernel, nchunk=tb // GLA_CHUNK),
        grid=(B, S // tb),
        in_specs=[
            pl.BlockSpec((1, tb, GLA_QK_W), lambda b, t: (b, t, COL_Q // GLA_QK_W)),
            pl.BlockSpec((1, tb, GLA_QK_W), lambda b, t: (b, t, COL_K // GLA_QK_W)),
            pl.BlockSpec((1, tb, GLA_V_W), lambda b, t: (b, t, COL_V // GLA_V_W)),
            pl.BlockSpec((1, tb, GLA_V_W), lambda b, t: (b, t, COL_R // GLA_V_W)),
            pl.BlockSpec((1, tb, GLA_QK_W), lambda b, t: (b, t, 0)),
            pl.BlockSpec((1, GLA_V_W), lambda b, t: (0, 0)),
        ],
        out_specs=pl.BlockSpec((1, tb, GLA_V_W), lambda b, t: (b, t, 0)),
        out_shape=jax.ShapeDtypeStruct((B, S, GLA_V_W), BF16),
        scratch_shapes=[pltpu.VMEM((GLA_HEADS, GLA_DV, GLA_DK), F32)],
        compiler_params=_params(("parallel", "arbitrary")),
        name="gla",
    )(y3, y3, y3, y3, la3, gn)


def _t5_causal_bucket(n):
    max_exact = REL_BUCKETS // 2
    nf = np.maximum(n, 1).astype(np.float32)
    large = max_exact + (np.log(nf / max_exact) / np.log(REL_MAX_DIST / max_exact)
                         * (REL_BUCKETS - max_exact)).astype(np.int32)
    large = np.minimum(large, REL_BUCKETS - 1)
    return np.where(n < max_exact, n, large).astype(np.int32)


def _bucket_table():
    qi = np.arange(ATT_BLOCK)[:, None]
    kj = np.arange(2 * ATT_BLOCK)[None, :]
    delta = qi + ATT_BLOCK - kj
    out = []
    for win, dil in ATT_GROUPS:
        band = (delta >= 0) & (delta <= win // dil)
        bucket = np.where(band, _t5_causal_bucket(np.maximum(delta, 0) * dil), -1)
        out.append(bucket)
    return np.stack(out).astype(np.int32)


def _att_unit(q, k, v, bias):
    BLK = ATT_BLOCK
    s = _dot_nt(q, k) + bias
    m = jnp.max(jnp.maximum(s[:, :BLK], s[:, BLK:]), axis=-1, keepdims=True)
    p = jnp.exp(s - m).astype(BF16)
    v_ext = jnp.concatenate([v, jnp.ones_like(v)], axis=1)
    pv = _dot(p, v_ext)
    denom = pv[:, BLK:]
    return pv[:, :BLK] / denom, m + jnp.log(denom)


def _att_kernel(tab_ref, bucket_ref,
                q0, k0, v0, kp0, vp0, q1, k1, v1, kp1, vp1, q2, k2, v2, kp2, vp2,
                o_ref, bias_scr, o_scr, lse_scr, *, tiles_per_batch):
    BLK = ATT_BLOCK
    ti = pl.program_id(0)
    h = pl.program_id(1)

    @pl.when((ti == 0) & (h == 0))
    def _():
        for g in range(N_ATT_GROUPS):
            bucket = bucket_ref[g]
            for hh in range(ATT_HEADS_PER_GROUP):
                bias = jnp.full(bucket.shape, NEG_INF, F32)
                for b in range(REL_BUCKETS):
                    bias = jnp.where(bucket == b, tab_ref[b, g * ATT_HEADS_PER_GROUP + hh], bias)
                bias_scr[g, hh] = bias

    lane = lax.broadcasted_iota(jnp.int32, (BLK, 2 * BLK), 1)
    no_prev = jnp.logical_and((ti % tiles_per_batch) == 0, lane < BLK)
    groups = ((q0, k0, v0, kp0, vp0), (q1, k1, v1, kp1, vp1), (q2, k2, v2, kp2, vp2))
    for g, (q, k, v, kp, vp) in enumerate(groups):
        dil = ATT_DILS[g]
        n_r = TILE // dil
        kp_rows = kp.shape[0]
        bias = bias_scr[g, h]
        bias_first = jnp.where(no_prev, NEG_INF, bias)
        for r in range(dil):
            for j in range(n_r // BLK):
                cur = pl.ds(r * n_r + j * BLK, BLK)
                if j > 0:
                    both = pl.ds(r * n_r + (j - 1) * BLK, 2 * BLK)
                    k_all, v_all, b = k[both, :], v[both, :], bias
                else:
                    prev = pl.ds((r * n_r + n_r - BLK) % kp_rows, BLK)
                    k_all = jnp.concatenate([kp[prev, :], k[cur, :]], axis=0)
                    v_all = jnp.concatenate([vp[prev, :], v[cur, :]], axis=0)
                    b = bias_first
                if dil == 1:
                    rows = pl.ds(j * BLK, BLK)
                else:
                    rows = pl.ds(dil * j * BLK + r, BLK, stride=dil)
                o_scr[g, rows, :], lse_scr[g, rows, :] = _att_unit(q[cur, :], k_all, v_all, b)

    for c in range(TILE // 256):
        rows = pl.ds(c * 256, 256)
        lse = [lse_scr[g, rows, :] for g in range(N_ATT_GROUPS)]
        top = functools.reduce(jnp.maximum, lse)
        w = [jnp.exp(x - top) for x in lse]
        num = functools.reduce(lambda a, b: a + b, [w[g] * o_scr[g, rows, :] for g in range(N_ATT_GROUPS)])
        den = functools.reduce(lambda a, b: a + b, w)
        o_ref[rows, :] = (num / den).astype(BF16)


def _attention(y2, rel_bias, S):
    T = y2.shape[0]
    tiles_per_batch = S // TILE
    hd = ATT_HEAD_DIM

    def col(g, c):
        return (COL_ATT + g * ATT_GROUP_W + c * ATT_W) // hd

    def cur(g, c):
        return pl.BlockSpec((TILE, hd), lambda t, h: (t, col(g, c) + h))

    def prev_tile(g, c):
        return pl.BlockSpec((TILE, hd), lambda t, h: (jnp.maximum(t - 1, 0), col(g, c) + h))

    def prev_block(g, c):
        nb = TILE // ATT_BLOCK
        return pl.BlockSpec((ATT_BLOCK, hd), lambda t, h: (jnp.maximum(t * nb - 1, 0), col(g, c) + h))

    in_specs = [pl.BlockSpec(memory_space=pltpu.SMEM),
                pl.BlockSpec((N_ATT_GROUPS, ATT_BLOCK, 2 * ATT_BLOCK), lambda t, h: (0, 0, 0))]
    for g in range(N_ATT_GROUPS):
        prv = prev_block if ATT_DILS[g] == 1 else prev_tile
        in_specs += [cur(g, 0), cur(g, 1), cur(g, 2), prv(g, 1), prv(g, 2)]
    return pl.pallas_call(
        functools.partial(_att_kernel, tiles_per_batch=tiles_per_batch),
        grid=(T // TILE, ATT_HEADS_PER_GROUP),
        in_specs=in_specs,
        out_specs=pl.BlockSpec((TILE, hd), lambda t, h: (t, h)),
        out_shape=jax.ShapeDtypeStruct((T, ATT_W), BF16),
        scratch_shapes=[
            pltpu.VMEM((N_ATT_GROUPS, ATT_HEADS_PER_GROUP, ATT_BLOCK, 2 * ATT_BLOCK), F32),
            pltpu.VMEM((N_ATT_GROUPS, TILE, hd), F32),
            pltpu.VMEM((N_ATT_GROUPS, TILE, LANES), F32),
        ],
        compiler_params=_params(("arbitrary", "arbitrary")),
        name="attention",
    )(rel_bias, jnp.asarray(_bucket_table()), *([y2] * (5 * N_ATT_GROUPS)))


def _mix_kernel(x_ref, og_ref, oa_ref, ga_ref, gb_ref, wog_ref, woa_ref, wout_ref, x1_ref):
    y_gla = _dot(og_ref[...], wog_ref[...])
    y_att = _dot(oa_ref[...], woa_ref[...])
    mix = (jax.nn.sigmoid(ga_ref[...].astype(F32)) * y_gla
           + jax.nn.sigmoid(gb_ref[...].astype(F32)) * y_att)
    x1_ref[...] = x_ref[...] + _dot(mix.astype(BF16), wout_ref[...])


def _mix(x2, y2, o_gla, o_att, wog, woa, wout, tm=512):
    T = x2.shape[0]
    gcol = COL_GATE // D_MODEL
    row = lambda w: pl.BlockSpec((tm, w), lambda i: (i, 0))
    full = lambda a: pl.BlockSpec(a.shape, lambda i: (0, 0))
    return pl.pallas_call(
        _mix_kernel,
        grid=(T // tm,),
        in_specs=[row(D_MODEL), row(GLA_V_W), row(ATT_W),
                  pl.BlockSpec((tm, D_MODEL), lambda i: (i, gcol)),
                  pl.BlockSpec((tm, D_MODEL), lambda i: (i, gcol + 1)),
                  full(wog), full(woa), full(wout)],
        out_specs=row(D_MODEL),
        out_shape=jax.ShapeDtypeStruct((T, D_MODEL), F32),
        compiler_params=_params(("parallel",)),
        name="mix",
    )(x2, o_gla, o_att, y2, y2, wog, woa, wout)


def _mlp_kernel(x_ref, p_ref, ln2_ref, w1_ref, w2_ref, ln3_ref, wpg_ref, wpp_ref, lnf_ref,
                out_ref, h_scr, acc_scr):
    f = pl.program_id(1)

    @pl.when(f == 0)
    def _():
        h_scr[...] = _rms(x_ref[...], ln2_ref[...]).astype(BF16)
        acc_scr[...] = jnp.zeros_like(acc_scr)

    u = jnp.maximum(_dot(h_scr[...], w1_ref[...]), 0.0)
    acc_scr[...] += _dot((u * u).astype(BF16), w2_ref[...])

    @pl.when(f == pl.num_programs(1) - 1)
    def _():
        x2 = x_ref[...] + acc_scr[...]
        h3 = _rms(x2, ln3_ref[...]).astype(BF16)
        gate = jax.nn.sigmoid(_dot(h3, wpg_ref[...]))
        x3 = x2 + gate * _dot(p_ref[...].astype(BF16), wpp_ref[...])
        out_ref[...] = _rms(x3, lnf_ref[...])


def _mlp(x1, p2, ln2, w1, w2, ln3, wpg, wpp, lnf, tm=1024, tf=1024):
    T = x1.shape[0]
    vec = pl.BlockSpec((1, D_MODEL), lambda i, f: (0, 0))
    return pl.pallas_call(
        _mlp_kernel,
        grid=(T // tm, D_FF // tf),
        in_specs=[
            pl.BlockSpec((tm, D_MODEL), lambda i, f: (i, 0)),
            pl.BlockSpec((tm, PLE_DIM), lambda i, f: (i, 0)),
            vec,
            pl.BlockSpec((D_MODEL, tf), lambda i, f: (0, f)),
            pl.BlockSpec((tf, D_MODEL), lambda i, f: (f, 0)),
            vec,
            pl.BlockSpec((D_MODEL, D_MODEL), lambda i, f: (0, 0)),
            pl.BlockSpec((PLE_DIM, D_MODEL), lambda i, f: (0, 0)),
            vec,
        ],
        out_specs=pl.BlockSpec((tm, D_MODEL), lambda i, f: (i, 0)),
        out_shape=jax.ShapeDtypeStruct((T, D_MODEL), F32),
        scratch_shapes=[pltpu.VMEM((tm, D_MODEL), BF16), pltpu.VMEM((tm, D_MODEL), F32)],
        compiler_params=_params(("parallel", "arbitrary")),
        name="mlp",
    )(x1, p2, ln2, w1, w2, ln3, wpg, wpp, lnf)


def _layer(x2, p2, B, S, ln1, w_in, w_a2, b_a, gla_gn, w_o_gla, w_o_attn, w_out,
           ln2, w_mlp1, w_mlp2, ln3, w_pp, w_pg, rel_bias, ln_out):
    sec = np.cumsum((0, GLA_QK_W, GLA_QK_W, GLA_V_W, GLA_V_W, GLA_RANK, N_ATT_GROUPS * ATT_GROUP_W))
    col_scale = np.ones((1, Y_COLS), np.float32)
    col_scale[:, COL_Q:COL_K] = GLA_DK ** -0.5
    for g in range(N_ATT_GROUPS):
        col_scale[:, COL_ATT + g * ATT_GROUP_W:COL_ATT + g * ATT_GROUP_W + ATT_W] = ATT_HEAD_DIM ** -0.5
    w_main = (jnp.concatenate([w_in[:, :sec[4]], w_in[:, sec[6]:], w_in[:, sec[5]:sec[6]]], axis=1)
              * col_scale).astype(BF16)
    wa1 = jnp.pad(w_in[:, sec[4]:sec[5]], ((0, 0), (0, LANES - GLA_RANK))).astype(BF16)
    wa2 = jnp.pad(w_a2, ((0, LANES - GLA_RANK), (0, 0)))

    y2, la = _inproj(x2, ln1[None], w_main, wa1, wa2, b_a[None])
    o_gla = _gla(y2.reshape(B, S, Y_COLS), la.reshape(B, S, GLA_QK_W), gla_gn[None])
    o_att = _attention(y2, rel_bias, S)
    x1 = _mix(x2, y2, o_gla.reshape(B * S, GLA_V_W), o_att,
              w_o_gla.astype(BF16), w_o_attn.astype(BF16), w_out.astype(BF16))
    return _mlp(x1, p2, ln2[None], w_mlp1.astype(BF16), w_mlp2.astype(BF16), ln3[None],
                w_pg.astype(BF16), w_pp.astype(BF16), ln_out[None])


def kernel(x, p, ln1, w_in, w_a2, b_a, gla_gn, w_o_gla, w_o_attn, w_out, ln2, w_mlp1, w_mlp2,
           ln3, w_pp, w_pg, rel_bias, ln_f):
    B, S, D = x.shape
    assert p.shape[0] == 1, "the final norm is fused into the single layer's last kernel"
    assert S % TILE == 0
    x2 = x.reshape(B * S, D)
    out = _layer(x2, p[0].reshape(B * S, PLE_DIM), B, S, ln1[0], w_in[0], w_a2[0], b_a[0],
                 gla_gn[0], w_o_gla[0], w_o_attn[0], w_out[0], ln2[0], w_mlp1[0], w_mlp2[0],
                 ln3[0], w_pp[0], w_pg[0], rel_bias, ln_f)
    return out.reshape(B, S, D)
```

```python
import functools

import numpy as np
import jax
import jax.numpy as jnp
from jax import lax
from jax.experimental import pallas as pl
from jax.experimental.pallas import tpu as pltpu

F32 = jnp.float32
BF16 = jnp.bfloat16

D_MODEL = 1024
PLE_DIM = 256
EPS = 1e-6
GLA_HEADS = 4
GLA_DK = 128
GLA_DV = 256
GLA_RANK = 16
GLA_TAU = 16.0
GLA_CHUNK = 64
GLA_QK_W = GLA_HEADS * GLA_DK
GLA_V_W = GLA_HEADS * GLA_DV
ATT_GROUPS = ((128, 1), (512, 4), (2048, 16))
ATT_DILS = tuple(d for _, d in ATT_GROUPS)
ATT_HEADS_PER_GROUP = 4
ATT_HEAD_DIM = 128
N_ATT_GROUPS = len(ATT_GROUPS)
ATT_W = ATT_HEADS_PER_GROUP * ATT_HEAD_DIM
ATT_BLOCK = 128
REL_BUCKETS = 32
REL_MAX_DIST = 2048
D_FF = 4 * D_MODEL
NEG_INF = -1e30

LANES = 128

COL_Q = 0
COL_K = COL_Q + GLA_QK_W
COL_V = COL_K + GLA_QK_W
COL_R = COL_V + GLA_V_W
COL_GATE = COL_R + GLA_V_W
COL_ATT = COL_GATE + 2 * D_MODEL
ATT_GROUP_W = 3 * ATT_W
Y_COLS = COL_ATT + N_ATT_GROUPS * ATT_GROUP_W

TILE = max(ATT_DILS) * ATT_BLOCK

VMEM_LIMIT = 56 * 1024 * 1024


def _params(semantics):
    return pltpu.CompilerParams(dimension_semantics=semantics, vmem_limit_bytes=VMEM_LIMIT)


def _rms(x, g):
    return x * lax.rsqrt(jnp.mean(x * x, axis=-1, keepdims=True) + EPS) * g


def _dot(a, b):
    return jnp.dot(a, b, preferred_element_type=F32)


def _dot_nt(a, b):
    return lax.dot_general(a, b, (((1,), (1,)), ((), ())), preferred_element_type=F32)


def _dot_tn(a, b):
    return lax.dot_general(a, b, (((0,), (0,)), ((), ())), preferred_element_type=F32)


W_GATE_COL = COL_R + GLA_V_W
W_ATT_COL = W_GATE_COL + GLA_RANK
W_MIXGATE_COL = W_ATT_COL + N_ATT_GROUPS * ATT_GROUP_W


def _wprep_kernel(a_ref, b_ref, scale_ref, o_ref, *, n_aligned, tn):
    t = pl.program_id(0)

    @pl.when(t < n_aligned)
    def _():
        o_ref[...] = (a_ref[...] * scale_ref[...]).astype(BF16)

    @pl.when(t >= n_aligned)
    def _():
        w = jnp.concatenate([a_ref[...], b_ref[...]], axis=1)[:, GLA_RANK:GLA_RANK + tn]
        o_ref[...] = (w * scale_ref[...]).astype(BF16)


def _wprep(w_in, tn=512, tk=D_MODEL):
    n_aligned = W_GATE_COL // tn
    n_att = (W_MIXGATE_COL - W_ATT_COL) // tn

    def out_col(t):
        return jnp.where(t < n_aligned, t,
                         jnp.where(t < n_aligned + n_att, t - n_aligned + COL_ATT // tn,
                                   t - n_aligned - n_att + COL_GATE // tn))

    col_scale = np.ones((1, Y_COLS), np.float32)
    col_scale[:, COL_Q:COL_K] = GLA_DK ** -0.5
    for g in range(N_ATT_GROUPS):
        col_scale[:, COL_ATT + g * ATT_GROUP_W:COL_ATT + g * ATT_GROUP_W + ATT_W] = ATT_HEAD_DIM ** -0.5
    return pl.pallas_call(
        functools.partial(_wprep_kernel, n_aligned=n_aligned, tn=tn),
        grid=(Y_COLS // tn, D_MODEL // tk),
        in_specs=[
            pl.BlockSpec((tk, tn), lambda t, k: (k, t)),
            pl.BlockSpec((tk, LANES), lambda t, k: (k, (t + 1) * (tn // LANES))),
            pl.BlockSpec((1, tn), lambda t, k: (0, out_col(t))),
        ],
        out_specs=pl.BlockSpec((tk, tn), lambda t, k: (k, out_col(t))),
        out_shape=jax.ShapeDtypeStruct((D_MODEL, Y_COLS), BF16),
        compiler_params=_params(("parallel", "parallel")),
        name="wprep",
    )(w_in, w_in, jnp.asarray(col_scale))


def _inproj_kernel(*refs, n_natural, n_group):
    n_slab = D_MODEL // LANES
    x_slabs = refs[:n_slab]
    ln_ref, w_ref, wa1_ref, wa2_ref, ba_ref, y_ref, la_ref, h_scr, inv_scr, tmp_scr = refs[n_slab:]
    j = pl.program_id(1)

    @pl.when(j == 0)
    def _():
        chunk = 256
        for c in range(TILE // chunk):
            rows = pl.ds(c * chunk, chunk)
            parts = [xr[rows, :] for xr in x_slabs]
            sq = parts[0] * parts[0]
            for part in parts[1:]:
                sq = sq + part * part
            ms = jnp.sum(sq, axis=-1, keepdims=True) * (1.0 / D_MODEL)
            inv = jnp.broadcast_to(lax.rsqrt(ms + EPS), (chunk, LANES))
            inv_scr[rows, :] = inv
            for s, part in enumerate(parts):
                cols = pl.ds(s * LANES, LANES)
                h_scr[0, rows, cols] = (part * inv * ln_ref[:, cols]).astype(BF16)
        assert ATT_DILS == (1, 4, 16)
        n4, n16 = TILE // 4, TILE // 16
        for s, xr in enumerate(x_slabs):
            cols = pl.ds(s * LANES, LANES)
            ln_s = ln_ref[:, cols]
            for a in range(4):
                for c in range(n4 // chunk):
                    src = pl.ds(a + 4 * c * chunk, chunk, stride=4)
                    dst = pl.ds(a * n4 + c * chunk, chunk)
                    hn = xr[src, :] * inv_scr[src, :] * ln_s
                    tmp_scr[dst, :] = hn
                    h_scr[1, dst, cols] = hn.astype(BF16)
            for a in range(4):
                for b in range(4):
                    src = pl.ds(a * n4 + b, n16, stride=4)
                    h_scr[2, pl.ds((a + 4 * b) * n16, n16), cols] = tmp_scr[src, :].astype(BF16)
        lane = lax.broadcasted_iota(jnp.int32, wa1_ref.shape, 1)
        wa1 = jnp.where(lane < GLA_RANK, wa1_ref[...], 0.0).astype(BF16)
        ha = _dot(h_scr[0], wa1)
        a_hi, a_mid, _ = _split3(ha)
        w_hi, w_mid, _ = _split3(wa2_ref[...])
        z = _dot(a_hi, w_hi) + (_dot(a_mid, w_hi) + _dot(a_hi, w_mid)) + ba_ref[...]
        log_sig = jnp.minimum(z, 0.0) - jnp.log(1.0 + jnp.exp(-jnp.abs(z)))
        la_ref[...] = log_sig * (1.0 / GLA_TAU)

    d = (j >= n_natural).astype(jnp.int32) + (j >= n_natural + n_group).astype(jnp.int32)
    y_ref[...] = _dot(h_scr[d], w_ref[...]).astype(BF16)


def _inproj(x2, ln1, w_main, w_in, wa2, ba, tn=512):
    T = x2.shape[0]
    n_natural = (COL_ATT + ATT_GROUP_W) // tn
    n_group = ATT_GROUP_W // tn
    n_slab = D_MODEL // LANES
    return pl.pallas_call(
        functools.partial(_inproj_kernel, n_natural=n_natural, n_group=n_group),
        grid=(T // TILE, Y_COLS // tn),
        in_specs=[pl.BlockSpec((TILE, LANES), functools.partial(lambda i, j, s: (i, s), s=s))
                  for s in range(n_slab)] + [
            pl.BlockSpec((1, D_MODEL), lambda i, j: (0, 0)),
            pl.BlockSpec((D_MODEL, tn), lambda i, j: (0, j)),
            pl.BlockSpec((D_MODEL, LANES), lambda i, j: (0, W_GATE_COL // LANES)),
            pl.BlockSpec((LANES, GLA_QK_W), lambda i, j: (0, 0)),
            pl.BlockSpec((1, GLA_QK_W), lambda i, j: (0, 0)),
        ],
        out_specs=[
            pl.BlockSpec((TILE, tn), lambda i, j: (i, j)),
            pl.BlockSpec((TILE, GLA_QK_W), lambda i, j: (i, 0)),
        ],
        out_shape=[
            jax.ShapeDtypeStruct((T, Y_COLS), BF16),
            jax.ShapeDtypeStruct((T, GLA_QK_W), F32),
        ],
        scratch_shapes=[pltpu.VMEM((N_ATT_GROUPS, TILE, D_MODEL), BF16),
                        pltpu.VMEM((TILE, LANES), F32),
                        pltpu.VMEM((TILE, LANES), F32)],
        compiler_params=_params(("parallel", "arbitrary")),
        name="inproj",
    )(*([x2] * n_slab), ln1, w_main, w_in, wa2, ba)


def _split3(g):
    hi = g.astype(BF16)
    r1 = g - hi.astype(F32)
    mid = r1.astype(BF16)
    lo = (r1 - mid.astype(F32)).astype(BF16)
    return hi, mid, lo


def _gla_kernel(q_ref, k_ref, v_ref, r_ref, la_ref, gn_ref, o_ref, st_ref, *, nchunk):
    C = GLA_CHUNK

    @pl.when(pl.program_id(1) == 0)
    def _():
        st_ref[...] = jnp.zeros_like(st_ref)

    row = lax.broadcasted_iota(jnp.int32, (C, C), 0)
    col = lax.broadcasted_iota(jnp.int32, (C, C), 1)
    causal = row >= col
    tri = causal.astype(BF16)

    H = range(GLA_HEADS)
    ck = [slice(h * GLA_DK, (h + 1) * GLA_DK) for h in H]
    cv = [slice(h * GLA_DV, (h + 1) * GLA_DV) for h in H]
    for c in range(nchunk):
        sl = pl.ds(c * C, C)
        hi, mid, lo = _split3(la_ref[0, sl, :])
        b = _dot(tri, hi) + _dot(tri, mid) + _dot(tri, lo)
        decay = jnp.exp(b[C - 1:C, :])
        q_dec = (q_ref[0, sl, :].astype(F32) * jnp.exp(b)).astype(BF16)
        k_in = k_ref[0, sl, :].astype(F32) * jnp.exp(-b)
        k_out = (k_in * decay).astype(BF16)
        k_in = k_in.astype(BF16)
        v = [v_ref[0, sl, cv[h]] for h in H]
        attn = [_dot_nt(q_dec[:, ck[h]], k_in[:, ck[h]]) for h in H]
        st = [st_ref[h] for h in H]
        o_inter = [_dot_nt(q_dec[:, ck[h]], st[h].astype(BF16)) for h in H]
        upd = [_dot_tn(v[h], k_out[:, ck[h]]) for h in H]
        o = [_dot(jnp.where(causal, attn[h], 0.0).astype(BF16), v[h]) + o_inter[h] for h in H]
        for h in H:
            st_ref[h] = st[h] * decay[:, ck[h]] + upd[h]
        for h in H:
            r = r_ref[0, sl, cv[h]].astype(F32)
            o_ref[0, sl, cv[h]] = (_rms(o[h], gn_ref[:, cv[h]]) * (r * jax.nn.sigmoid(r))).astype(BF16)


def _gla(y3, la3, gn, tb=512):
    B, S, _ = y3.shape
    return pl.pallas_call(
        functools.partial(_gla_kernel, nchunk=tb // GLA_CHUNK),
        grid=(B, S // tb),
        in_specs=[
            pl.BlockSpec((1, tb, GLA_QK_W), lambda b, t: (b, t, COL_Q // GLA_QK_W)),
            pl.BlockSpec((1, tb, GLA_QK_W), lambda b, t: (b, t, COL_K // GLA_QK_W)),
            pl.BlockSpec((1, tb, GLA_V_W), lambda b, t: (b, t, COL_V // GLA_V_W)),
            pl.BlockSpec((1, tb, GLA_V_W), lambda b, t: (b, t, COL_R // GLA_V_W)),
            pl.BlockSpec((1, tb, GLA_QK_W), lambda b, t: (b, t, 0)),
            pl.BlockSpec((1, GLA_V_W), lambda b, t: (0, 0)),
        ],
        out_specs=pl.BlockSpec((1, tb, GLA_V_W), lambda b, t: (b, t, 0)),
        out_shape=jax.ShapeDtypeStruct((B, S, GLA_V_W), BF16),
        scratch_shapes=[pltpu.VMEM((GLA_HEADS, GLA_DV, GLA_DK), F32)],
        compiler_params=_params(("parallel", "arbitrary")),
        name="gla",
    )(y3, y3, y3, y3, la3, gn)


def _t5_causal_bucket(n):
    max_exact = REL_BUCKETS // 2
    nf = np.maximum(n, 1).astype(np.float32)
    large = max_exact + (np.log(nf / max_exact) / np.log(REL_MAX_DIST / max_exact)
                         * (REL_BUCKETS - max_exact)).astype(np.int32)
    large = np.minimum(large, REL_BUCKETS - 1)
    return np.where(n < max_exact, n, large).astype(np.int32)


def _bucket_table():
    qi = np.arange(ATT_BLOCK)[:, None]
    kj = np.arange(2 * ATT_BLOCK)[None, :]
    delta = qi + ATT_BLOCK - kj
    out = []
    for win, dil in ATT_GROUPS:
        band = (delta >= 0) & (delta <= win // dil)
        bucket = np.where(band, _t5_causal_bucket(np.maximum(delta, 0) * dil), -1)
        out.append(bucket)
    return np.stack(out).astype(np.int32)


def _att_unit(q, k, v, bias):
    BLK = ATT_BLOCK
    s = _dot_nt(q, k) + bias
    m = jnp.max(jnp.maximum(s[:, :BLK], s[:, BLK:]), axis=-1, keepdims=True)
    p = jnp.exp(s - m).astype(BF16)
    v_ext = jnp.concatenate([v, jnp.ones_like(v)], axis=1)
    pv = _dot(p, v_ext)
    denom = pv[:, BLK:]
    return pv[:, :BLK] / denom, m + jnp.log(denom)


def _att_kernel(tab_ref, bucket_ref,
                q0, k0, v0, kp0, vp0, q1, k1, v1, kp1, vp1, q2, k2, v2, kp2, vp2,
                o_ref, bias_scr, o_scr, lse_scr, *, tiles_per_batch):
    BLK = ATT_BLOCK
    ti = pl.program_id(0)
    h = pl.program_id(1)

    @pl.when((ti == 0) & (h == 0))
    def _():
        for g in range(N_ATT_GROUPS):
            bucket = bucket_ref[g]
            for hh in range(ATT_HEADS_PER_GROUP):
                bias = jnp.full(bucket.shape, NEG_INF, F32)
                for b in range(REL_BUCKETS):
                    bias = jnp.where(bucket == b, tab_ref[b, g * ATT_HEADS_PER_GROUP + hh], bias)
                bias_scr[g, hh] = bias

    lane = lax.broadcasted_iota(jnp.int32, (BLK, 2 * BLK), 1)
    no_prev = jnp.logical_and((ti % tiles_per_batch) == 0, lane < BLK)
    groups = ((q0, k0, v0, kp0, vp0), (q1, k1, v1, kp1, vp1), (q2, k2, v2, kp2, vp2))
    for g, (q, k, v, kp, vp) in enumerate(groups):
        dil = ATT_DILS[g]
        n_r = TILE // dil
        kp_rows = kp.shape[0]
        bias = bias_scr[g, h]
        bias_first = jnp.where(no_prev, NEG_INF, bias)
        for r in range(dil):
            for j in range(n_r // BLK):
                cur = pl.ds(r * n_r + j * BLK, BLK)
                if j > 0:
                    both = pl.ds(r * n_r + (j - 1) * BLK, 2 * BLK)
                    k_all, v_all, b = k[both, :], v[both, :], bias
                else:
                    prev = pl.ds((r * n_r + n_r - BLK) % kp_rows, BLK)
                    k_all = jnp.concatenate([kp[prev, :], k[cur, :]], axis=0)
                    v_all = jnp.concatenate([vp[prev, :], v[cur, :]], axis=0)
                    b = bias_first
                if dil == 1:
                    rows = pl.ds(j * BLK, BLK)
                else:
                    rows = pl.ds(dil * j * BLK + r, BLK, stride=dil)
                o_scr[g, rows, :], lse_scr[g, rows, :] = _att_unit(q[cur, :], k_all, v_all, b)

    for c in range(TILE // 256):
        rows = pl.ds(c * 256, 256)
        lse = [lse_scr[g, rows, :] for g in range(N_ATT_GROUPS)]
        top = functools.reduce(jnp.maximum, lse)
        w = [jnp.exp(x - top) for x in lse]
        num = functools.reduce(lambda a, b: a + b, [w[g] * o_scr[g, rows, :] for g in range(N_ATT_GROUPS)])
        den = functools.reduce(lambda a, b: a + b, w)
        o_ref[rows, :] = (num / den).astype(BF16)


def _attention(y2, rel_bias, S):
    T = y2.shape[0]
    tiles_per_batch = S // TILE
    hd = ATT_HEAD_DIM

    def col(g, c):
        return (COL_ATT + g * ATT_GROUP_W + c * ATT_W) // hd

    def cur(g, c):
        return pl.BlockSpec((TILE, hd), lambda t, h: (t, col(g, c) + h))

    def prev_tile(g, c):
        return pl.BlockSpec((TILE, hd), lambda t, h: (jnp.maximum(t - 1, 0), col(g, c) + h))

    def prev_block(g, c):
        nb = TILE // ATT_BLOCK
        return pl.BlockSpec((ATT_BLOCK, hd), lambda t, h: (jnp.maximum(t * nb - 1, 0), col(g, c) + h))

    in_specs = [pl.BlockSpec(memory_space=pltpu.SMEM),
                pl.BlockSpec((N_ATT_GROUPS, ATT_BLOCK, 2 * ATT_BLOCK), lambda t, h: (0, 0, 0))]
    for g in range(N_ATT_GROUPS):
        prv = prev_block if ATT_DILS[g] == 1 else prev_tile
        in_specs += [cur(g, 0), cur(g, 1), cur(g, 2), prv(g, 1), prv(g, 2)]
    return pl.pallas_call(
        functools.partial(_att_kernel, tiles_per_batch=tiles_per_batch),
        grid=(T // TILE, ATT_HEADS_PER_GROUP),
        in_specs=in_specs,
        out_specs=pl.BlockSpec((TILE, hd), lambda t, h: (t, h)),
        out_shape=jax.ShapeDtypeStruct((T, ATT_W), BF16),
        scratch_shapes=[
            pltpu.VMEM((N_ATT_GROUPS, ATT_HEADS_PER_GROUP, ATT_BLOCK, 2 * ATT_BLOCK), F32),
            pltpu.VMEM((N_ATT_GROUPS, TILE, hd), F32),
            pltpu.VMEM((N_ATT_GROUPS, TILE, LANES), F32),
        ],
        compiler_params=_params(("arbitrary", "arbitrary")),
        name="attention",
    )(rel_bias, jnp.asarray(_bucket_table()), *([y2] * (5 * N_ATT_GROUPS)))


def _mix_kernel(x_ref, og_ref, oa_ref, ga_ref, gb_ref, wog_ref, woa_ref, wout_ref, x1_ref):
    y_gla = _dot(og_ref[...], wog_ref[...])
    y_att = _dot(oa_ref[...], woa_ref[...])
    mix = (jax.nn.sigmoid(ga_ref[...].astype(F32)) * y_gla
           + jax.nn.sigmoid(gb_ref[...].astype(F32)) * y_att)
    x1_ref[...] = x_ref[...] + _dot(mix.astype(BF16), wout_ref[...])


def _mix(x2, y2, o_gla, o_att, wog, woa, wout, tm=512):
    T = x2.shape[0]
    gcol = COL_GATE // D_MODEL
    row = lambda w: pl.BlockSpec((tm, w), lambda i: (i, 0))
    full = lambda a: pl.BlockSpec(a.shape, lambda i: (0, 0))
    return pl.pallas_call(
        _mix_kernel,
        grid=(T // tm,),
        in_specs=[row(D_MODEL), row(GLA_V_W), row(ATT_W),
                  pl.BlockSpec((tm, D_MODEL), lambda i: (i, gcol)),
                  pl.BlockSpec((tm, D_MODEL), lambda i: (i, gcol + 1)),
                  full(wog), full(woa), full(wout)],
        out_specs=row(D_MODEL),
        out_shape=jax.ShapeDtypeStruct((T, D_MODEL), F32),
        compiler_params=_params(("parallel",)),
        name="mix",
    )(x2, o_gla, o_att, y2, y2, wog, woa, wout)


def _mlp_kernel(x_ref, p_ref, ln2_ref, w1_ref, w2_ref, ln3_ref, wpg_ref, wpp_ref, lnf_ref,
                out_ref, h_scr, acc_scr):
    f = pl.program_id(1)

    @pl.when(f == 0)
    def _():
        h_scr[...] = _rms(x_ref[...], ln2_ref[...]).astype(BF16)
        acc_scr[...] = jnp.zeros_like(acc_scr)

    u = jnp.maximum(_dot(h_scr[...], w1_ref[...]), 0.0)
    acc_scr[...] += _dot((u * u).astype(BF16), w2_ref[...])

    @pl.when(f == pl.num_programs(1) - 1)
    def _():
        x2 = x_ref[...] + acc_scr[...]
        h3 = _rms(x2, ln3_ref[...]).astype(BF16)
        gate = jax.nn.sigmoid(_dot(h3, wpg_ref[...]))
        x3 = x2 + gate * _dot(p_ref[...].astype(BF16), wpp_ref[...])
        out_ref[...] = _rms(x3, lnf_ref[...])


def _mlp(x1, p2, ln2, w1, w2, ln3, wpg, wpp, lnf, tm=1024, tf=1024):
    T = x1.shape[0]
    vec = pl.BlockSpec((1, D_MODEL), lambda i, f: (0, 0))
    return pl.pallas_call(
        _mlp_kernel,
        grid=(T // tm, D_FF // tf),
        in_specs=[
            pl.BlockSpec((tm, D_MODEL), lambda i, f: (i, 0)),
            pl.BlockSpec((tm, PLE_DIM), lambda i, f: (i, 0)),
            vec,
            pl.BlockSpec((D_MODEL, tf), lambda i, f: (0, f)),
            pl.BlockSpec((tf, D_MODEL), lambda i, f: (f, 0)),
            vec,
            pl.BlockSpec((D_MODEL, D_MODEL), lambda i, f: (0, 0)),
            pl.BlockSpec((PLE_DIM, D_MODEL), lambda i, f: (0, 0)),
            vec,
        ],
        out_specs=pl.BlockSpec((tm, D_MODEL), lambda i, f: (i, 0)),
        out_shape=jax.ShapeDtypeStruct((T, D_MODEL), F32),
        scratch_shapes=[pltpu.VMEM((tm, D_MODEL), BF16), pltpu.VMEM((tm, D_MODEL), F32)],
        compiler_params=_params(("parallel", "arbitrary")),
        name="mlp",
    )(x1, p2, ln2, w1, w2, ln3, wpg, wpp, lnf)


def _layer(x2, p2, B, S, ln1, w_in, w_a2, b_a, gla_gn, w_o_gla, w_o_attn, w_out,
           ln2, w_mlp1, w_mlp2, ln3, w_pp, w_pg, rel_bias, ln_out):
    wa2 = jnp.pad(w_a2, ((0, LANES - GLA_RANK), (0, 0)))

    y2, la = _inproj(x2, ln1[None], _wprep(w_in), w_in, wa2, b_a[None])
    o_gla = _gla(y2.reshape(B, S, Y_COLS), la.reshape(B, S, GLA_QK_W), gla_gn[None])
    o_att = _attention(y2, rel_bias, S)
    x1 = _mix(x2, y2, o_gla.reshape(B * S, GLA_V_W), o_att,
              w_o_gla.astype(BF16), w_o_attn.astype(BF16), w_out.astype(BF16))
    return _mlp(x1, p2, ln2[None], w_mlp1.astype(BF16), w_mlp2.astype(BF16), ln3[None],
                w_pg.astype(BF16), w_pp.astype(BF16), ln_out[None])


def kernel(x, p, ln1, w_in, w_a2, b_a, gla_gn, w_o_gla, w_o_attn, w_out, ln2, w_mlp1, w_mlp2,
           ln3, w_pp, w_pg, rel_bias, ln_f):
    B, S, D = x.shape
    assert p.shape[0] == 1, "the final norm is fused into the single layer's last kernel"
    assert S % TILE == 0
    x2 = x.reshape(B * S, D)
    out = _layer(x2, p[0].reshape(B * S, PLE_DIM), B, S, ln1[0], w_in[0], w_a2[0], b_a[0],
                 gla_gn[0], w_o_gla[0], w_o_attn[0], w_out[0], ln2[0], w_mlp1[0], w_mlp2[0],
                 ln3[0], w_pp[0], w_pg[0], rel_bias, ln_f)
    return out.reshape(B, S, D)
```

```python
import functools

import numpy as np
import jax
import jax.numpy as jnp
from jax import lax
from jax.experimental import pallas as pl
from jax.experimental.pallas import tpu as pltpu

F32 = jnp.float32
BF16 = jnp.bfloat16

D_MODEL = 1024
PLE_DIM = 256
EPS = 1e-6
GLA_HEADS = 4
GLA_DK = 128
GLA_DV = 256
GLA_RANK = 16
GLA_TAU = 16.0
GLA_CHUNK = 64
GLA_QK_W = GLA_HEADS * GLA_DK
GLA_V_W = GLA_HEADS * GLA_DV
ATT_GROUPS = ((128, 1), (512, 4), (2048, 16))
ATT_DILS = tuple(d for _, d in ATT_GROUPS)
ATT_HEADS_PER_GROUP = 4
ATT_HEAD_DIM = 128
N_ATT_GROUPS = len(ATT_GROUPS)
ATT_W = ATT_HEADS_PER_GROUP * ATT_HEAD_DIM
ATT_BLOCK = 128
REL_BUCKETS = 32
REL_MAX_DIST = 2048
D_FF = 4 * D_MODEL
NEG_INF = -1e30

LANES = 128

COL_Q = 0
COL_K = COL_Q + GLA_QK_W
COL_V = COL_K + GLA_QK_W
COL_R = COL_V + GLA_V_W
COL_GATE = COL_R + GLA_V_W
COL_ATT = COL_GATE + 2 * D_MODEL
ATT_GROUP_W = 3 * ATT_W
Y_COLS = COL_ATT + N_ATT_GROUPS * ATT_GROUP_W

TILE = max(ATT_DILS) * ATT_BLOCK

VMEM_LIMIT = 56 * 1024 * 1024


def _params(semantics):
    return pltpu.CompilerParams(dimension_semantics=semantics, vmem_limit_bytes=VMEM_LIMIT)


def _rms(x, g):
    return x * lax.rsqrt(jnp.mean(x * x, axis=-1, keepdims=True) + EPS) * g


def _dot(a, b):
    return jnp.dot(a, b, preferred_element_type=F32)


def _dot_nt(a, b):
    return lax.dot_general(a, b, (((1,), (1,)), ((), ())), preferred_element_type=F32)


def _dot_tn(a, b):
    return lax.dot_general(a, b, (((0,), (0,)), ((), ())), preferred_element_type=F32)


W_GATE_COL = COL_R + GLA_V_W
W_ATT_COL = W_GATE_COL + GLA_RANK
W_MIXGATE_COL = W_ATT_COL + N_ATT_GROUPS * ATT_GROUP_W


def _wprep_kernel(a_ref, b_ref, scale_ref, o_ref, *, n_aligned, tn):
    t = pl.program_id(0)

    @pl.when(t < n_aligned)
    def _():
        o_ref[...] = (a_ref[...].T * scale_ref[...]).astype(BF16)

    @pl.when(t >= n_aligned)
    def _():
        w = jnp.concatenate([a_ref[...], b_ref[...]], axis=0)[GLA_RANK:GLA_RANK + tn]
        o_ref[...] = (w.T * scale_ref[...]).astype(BF16)


def _wprep(w_in_t, tn=512):
    n_aligned = W_GATE_COL // tn
    n_att = (W_MIXGATE_COL - W_ATT_COL) // tn

    def out_col(t):
        return jnp.where(t < n_aligned, t,
                         jnp.where(t < n_aligned + n_att, t - n_aligned + COL_ATT // tn,
                                   t - n_aligned - n_att + COL_GATE // tn))

    col_scale = np.ones((1, Y_COLS), np.float32)
    col_scale[:, COL_Q:COL_K] = GLA_DK ** -0.5
    for g in range(N_ATT_GROUPS):
        col_scale[:, COL_ATT + g * ATT_GROUP_W:COL_ATT + g * ATT_GROUP_W + ATT_W] = ATT_HEAD_DIM ** -0.5
    return pl.pallas_call(
        functools.partial(_wprep_kernel, n_aligned=n_aligned, tn=tn),
        grid=(Y_COLS // tn,),
        in_specs=[
            pl.BlockSpec((tn, D_MODEL), lambda t: (t, 0)),
            pl.BlockSpec((GLA_RANK, D_MODEL), lambda t: ((t + 1) * (tn // GLA_RANK), 0)),
            pl.BlockSpec((1, tn), lambda t: (0, out_col(t))),
        ],
        out_specs=pl.BlockSpec((D_MODEL, tn), lambda t: (0, out_col(t))),
        out_shape=jax.ShapeDtypeStruct((D_MODEL, Y_COLS), BF16),
        compiler_params=_params(("parallel",)),
        name="wprep",
    )(w_in_t, w_in_t, jnp.asarray(col_scale))


def _inproj_kernel(*refs, n_natural, n_group):
    n_slab = D_MODEL // LANES
    x_slabs = refs[:n_slab]
    ln_ref, w_ref, wa1_ref, wa2_ref, ba_ref, y_ref, la_ref, h_scr, inv_scr, tmp_scr = refs[n_slab:]
    j = pl.program_id(1)

    @pl.when(j == 0)
    def _():
        chunk = 256
        for c in range(TILE // chunk):
            rows = pl.ds(c * chunk, chunk)
            parts = [xr[rows, :] for xr in x_slabs]
            sq = parts[0] * parts[0]
            for part in parts[1:]:
                sq = sq + part * part
            ms = jnp.sum(sq, axis=-1, keepdims=True) * (1.0 / D_MODEL)
            inv = jnp.broadcast_to(lax.rsqrt(ms + EPS), (chunk, LANES))
            inv_scr[rows, :] = inv
            for s, part in enumerate(parts):
                cols = pl.ds(s * LANES, LANES)
                h_scr[0, rows, cols] = (part * inv * ln_ref[:, cols]).astype(BF16)
        assert ATT_DILS == (1, 4, 16)
        n4, n16 = TILE // 4, TILE // 16
        for s, xr in enumerate(x_slabs):
            cols = pl.ds(s * LANES, LANES)
            ln_s = ln_ref[:, cols]
            for a in range(4):
                for c in range(n4 // chunk):
                    src = pl.ds(a + 4 * c * chunk, chunk, stride=4)
                    dst = pl.ds(a * n4 + c * chunk, chunk)
                    hn = xr[src, :] * inv_scr[src, :] * ln_s
                    tmp_scr[dst, :] = hn
                    h_scr[1, dst, cols] = hn.astype(BF16)
            for a in range(4):
                for b in range(4):
                    src = pl.ds(a * n4 + b, n16, stride=4)
                    h_scr[2, pl.ds((a + 4 * b) * n16, n16), cols] = tmp_scr[src, :].astype(BF16)
        wa1_t = jnp.concatenate([wa1_ref[...], jnp.zeros((LANES - GLA_RANK, D_MODEL), F32)], axis=0)
        ha = _dot_nt(h_scr[0], wa1_t.astype(BF16))
        a_hi, a_mid, _ = _split3(ha)
        w_hi, w_mid, _ = _split3(wa2_ref[...])
        z = _dot(a_hi, w_hi) + (_dot(a_mid, w_hi) + _dot(a_hi, w_mid)) + ba_ref[...]
        log_sig = jnp.minimum(z, 0.0) - jnp.log(1.0 + jnp.exp(-jnp.abs(z)))
        la_ref[...] = log_sig * (1.0 / GLA_TAU)

    d = (j >= n_natural).astype(jnp.int32) + (j >= n_natural + n_group).astype(jnp.int32)
    y_ref[...] = _dot(h_scr[d], w_ref[...]).astype(BF16)


def _inproj(x2, ln1, w_main, w_in_t, wa2, ba, tn=512):
    T = x2.shape[0]
    n_natural = (COL_ATT + ATT_GROUP_W) // tn
    n_group = ATT_GROUP_W // tn
    n_slab = D_MODEL // LANES
    return pl.pallas_call(
        functools.partial(_inproj_kernel, n_natural=n_natural, n_group=n_group),
        grid=(T // TILE, Y_COLS // tn),
        in_specs=[pl.BlockSpec((TILE, LANES), functools.partial(lambda i, j, s: (i, s), s=s))
                  for s in range(n_slab)] + [
            pl.BlockSpec((1, D_MODEL), lambda i, j: (0, 0)),
            pl.BlockSpec((D_MODEL, tn), lambda i, j: (0, j)),
            pl.BlockSpec((GLA_RANK, D_MODEL), lambda i, j: (W_GATE_COL // GLA_RANK, 0)),
            pl.BlockSpec((LANES, GLA_QK_W), lambda i, j: (0, 0)),
            pl.BlockSpec((1, GLA_QK_W), lambda i, j: (0, 0)),
        ],
        out_specs=[
            pl.BlockSpec((TILE, tn), lambda i, j: (i, j)),
            pl.BlockSpec((TILE, GLA_QK_W), lambda i, j: (i, 0)),
        ],
        out_shape=[
            jax.ShapeDtypeStruct((T, Y_COLS), BF16),
            jax.ShapeDtypeStruct((T, GLA_QK_W), F32),
        ],
        scratch_shapes=[pltpu.VMEM((N_ATT_GROUPS, TILE, D_MODEL), BF16),
                        pltpu.VMEM((TILE, LANES), F32),
                        pltpu.VMEM((TILE, LANES), F32)],
        compiler_params=_params(("parallel", "arbitrary")),
        name="inproj",
    )(*([x2] * n_slab), ln1, w_main, w_in_t, wa2, ba)


def _split3(g):
    hi = g.astype(BF16)
    r1 = g - hi.astype(F32)
    mid = r1.astype(BF16)
    lo = (r1 - mid.astype(F32)).astype(BF16)
    return hi, mid, lo


def _gla_kernel(q_ref, k_ref, v_ref, r_ref, la_ref, gn_ref, o_ref, st_ref, *, nchunk):
    C = GLA_CHUNK

    @pl.when(pl.program_id(1) == 0)
    def _():
        st_ref[...] = jnp.zeros_like(st_ref)

    row = lax.broadcasted_iota(jnp.int32, (C, C), 0)
    col = lax.broadcasted_iota(jnp.int32, (C, C), 1)
    causal = row >= col
    tri = causal.astype(BF16)

    H = range(GLA_HEADS)
    ck = [slice(h * GLA_DK, (h + 1) * GLA_DK) for h in H]
    cv = [slice(h * GLA_DV, (h + 1) * GLA_DV) for h in H]
    for c in range(nchunk):
        sl = pl.ds(c * C, C)
        hi, mid, lo = _split3(la_ref[0, sl, :])
        b = _dot(tri, hi) + _dot(tri, mid) + _dot(tri, lo)
        decay = jnp.exp(b[C - 1:C, :])
        q_dec = (q_ref[0, sl, :].astype(F32) * jnp.exp(b)).astype(BF16)
        k_in = k_ref[0, sl, :].astype(F32) * jnp.exp(-b)
        k_out = (k_in * decay).astype(BF16)
        k_in = k_in.astype(BF16)
        v = [v_ref[0, sl, cv[h]] for h in H]
        attn = [_dot_nt(q_dec[:, ck[h]], k_in[:, ck[h]]) for h in H]
        st = [st_ref[h] for h in H]
        o_inter = [_dot_nt(q_dec[:, ck[h]], st[h].astype(BF16)) for h in H]
        upd = [_dot_tn(v[h], k_out[:, ck[h]]) for h in H]
        o = [_dot(jnp.where(causal, attn[h], 0.0).astype(BF16), v[h]) + o_inter[h] for h in H]
        for h in H:
            st_ref[h] = st[h] * decay[:, ck[h]] + upd[h]
        for h in H:
            r = r_ref[0, sl, cv[h]].astype(F32)
            o_ref[0, sl, cv[h]] = (_rms(o[h], gn_ref[:, cv[h]]) * (r * jax.nn.sigmoid(r))).astype(BF16)


def _gla(y3, la3, gn, tb=512):
    B, S, _ = y3.shape
    return pl.pallas_call(
        functools.partial(_gla_kernel, nchunk=tb // GLA_CHUNK),
        grid=(B, S // tb),
        in_specs=[
            pl.BlockSpec((1, tb, GLA_QK_W), lambda b, t: (b, t, COL_Q // GLA_QK_W)),
            pl.BlockSpec((1, tb, GLA_QK_W), lambda b, t: (b, t, COL_K // GLA_QK_W)),
            pl.BlockSpec((1, tb, GLA_V_W), lambda b, t: (b, t, COL_V // GLA_V_W)),
            pl.BlockSpec((1, tb, GLA_V_W), lambda b, t: (b, t, COL_R // GLA_V_W)),
            pl.BlockSpec((1, tb, GLA_QK_W), lambda b, t: (b, t, 0)),
            pl.BlockSpec((1, GLA_V_W), lambda b, t: (0, 0)),
        ],
        out_specs=pl.BlockSpec((1, tb, GLA_V_W), lambda b, t: (b, t, 0)),
        out_shape=jax.ShapeDtypeStruct((B, S, GLA_V_W), BF16),
        scratch_shapes=[pltpu.VMEM((GLA_HEADS, GLA_DV, GLA_DK), F32)],
        compiler_params=_params(("parallel", "arbitrary")),
        name="gla",
    )(y3, y3, y3, y3, la3, gn)


def _t5_causal_bucket(n):
    max_exact = REL_BUCKETS // 2
    nf = np.maximum(n, 1).astype(np.float32)
    large = max_exact + (np.log(nf / max_exact) / np.log(REL_MAX_DIST / max_exact)
                         * (REL_BUCKETS - max_exact)).astype(np.int32)
    large = np.minimum(large, REL_BUCKETS - 1)
    return np.where(n < max_exact, n, large).astype(np.int32)


def _bucket_table():
    qi = np.arange(ATT_BLOCK)[:, None]
    kj = np.arange(2 * ATT_BLOCK)[None, :]
    delta = qi + ATT_BLOCK - kj
    out = []
    for win, dil in ATT_GROUPS:
        band = (delta >= 0) & (delta <= win // dil)
        bucket = np.where(band, _t5_causal_bucket(np.maximum(delta, 0) * dil), -1)
        out.append(bucket)
    return np.stack(out).astype(np.int32)


def _att_unit(q, k, v, bias):
    BLK = ATT_BLOCK
    s = _dot_nt(q, k) + bias
    m = jnp.max(jnp.maximum(s[:, :BLK], s[:, BLK:]), axis=-1, keepdims=True)
    p = jnp.exp(s - m).astype(BF16)
    v_ext = jnp.concatenate([v, jnp.ones_like(v)], axis=1)
    pv = _dot(p, v_ext)
    denom = pv[:, BLK:]
    return pv[:, :BLK] / denom, m + jnp.log(denom)


def _att_kernel(tab_ref, bucket_ref,
                q0, k0, v0, kp0, vp0, q1, k1, v1, kp1, vp1, q2, k2, v2, kp2, vp2,
                o_ref, bias_scr, o_scr, lse_scr, *, tiles_per_batch):
    BLK = ATT_BLOCK
    ti = pl.program_id(0)
    h = pl.program_id(1)

    @pl.when((ti == 0) & (h == 0))
    def _():
        for g in range(N_ATT_GROUPS):
            bucket = bucket_ref[g]
            for hh in range(ATT_HEADS_PER_GROUP):
                bias = jnp.full(bucket.shape, NEG_INF, F32)
                for b in range(REL_BUCKETS):
                    bias = jnp.where(bucket == b, tab_ref[b, g * ATT_HEADS_PER_GROUP + hh], bias)
                bias_scr[g, hh] = bias

    lane = lax.broadcasted_iota(jnp.int32, (BLK, 2 * BLK), 1)
    no_prev = jnp.logical_and((ti % tiles_per_batch) == 0, lane < BLK)
    groups = ((q0, k0, v0, kp0, vp0), (q1, k1, v1, kp1, vp1), (q2, k2, v2, kp2, vp2))
    for g, (q, k, v, kp, vp) in enumerate(groups):
        dil = ATT_DILS[g]
        n_r = TILE // dil
        kp_rows = kp.shape[0]
        bias = bias_scr[g, h]
        bias_first = jnp.where(no_prev, NEG_INF, bias)
        for r in range(dil):
            for j in range(n_r // BLK):
                cur = pl.ds(r * n_r + j * BLK, BLK)
                if j > 0:
                    both = pl.ds(r * n_r + (j - 1) * BLK, 2 * BLK)
                    k_all, v_all, b = k[both, :], v[both, :], bias
                else:
                    prev = pl.ds((r * n_r + n_r - BLK) % kp_rows, BLK)
                    k_all = jnp.concatenate([kp[prev, :], k[cur, :]], axis=0)
                    v_all = jnp.concatenate([vp[prev, :], v[cur, :]], axis=0)
                    b = bias_first
                if dil == 1:
                    rows = pl.ds(j * BLK, BLK)
                else:
                    rows = pl.ds(dil * j * BLK + r, BLK, stride=dil)
                o_scr[g, rows, :], lse_scr[g, rows, :] = _att_unit(q[cur, :], k_all, v_all, b)

    for c in range(TILE // 256):
        rows = pl.ds(c * 256, 256)
        lse = [lse_scr[g, rows, :] for g in range(N_ATT_GROUPS)]
        top = functools.reduce(jnp.maximum, lse)
        w = [jnp.exp(x - top) for x in lse]
        num = functools.reduce(lambda a, b: a + b, [w[g] * o_scr[g, rows, :] for g in range(N_ATT_GROUPS)])
        den = functools.reduce(lambda a, b: a + b, w)
        o_ref[rows, :] = (num / den).astype(BF16)


def _attention(y2, rel_bias, S):
    T = y2.shape[0]
    tiles_per_batch = S // TILE
    hd = ATT_HEAD_DIM

    def col(g, c):
        return (COL_ATT + g * ATT_GROUP_W + c * ATT_W) // hd

    def cur(g, c):
        return pl.BlockSpec((TILE, hd), lambda t, h: (t, col(g, c) + h))

    def prev_tile(g, c):
        return pl.BlockSpec((TILE, hd), lambda t, h: (jnp.maximum(t - 1, 0), col(g, c) + h))

    def prev_block(g, c):
        nb = TILE // ATT_BLOCK
        return pl.BlockSpec((ATT_BLOCK, hd), lambda t, h: (jnp.maximum(t * nb - 1, 0), col(g, c) + h))

    in_specs = [pl.BlockSpec(memory_space=pltpu.SMEM),
                pl.BlockSpec((N_ATT_GROUPS, ATT_BLOCK, 2 * ATT_BLOCK), lambda t, h: (0, 0, 0))]
    for g in range(N_ATT_GROUPS):
        prv = prev_block if ATT_DILS[g] == 1 else prev_tile
        in_specs += [cur(g, 0), cur(g, 1), cur(g, 2), prv(g, 1), prv(g, 2)]
    return pl.pallas_call(
        functools.partial(_att_kernel, tiles_per_batch=tiles_per_batch),
        grid=(T // TILE, ATT_HEADS_PER_GROUP),
        in_specs=in_specs,
        out_specs=pl.BlockSpec((TILE, hd), lambda t, h: (t, h)),
        out_shape=jax.ShapeDtypeStruct((T, ATT_W), BF16),
        scratch_shapes=[
            pltpu.VMEM((N_ATT_GROUPS, ATT_HEADS_PER_GROUP, ATT_BLOCK, 2 * ATT_BLOCK), F32),
            pltpu.VMEM((N_ATT_GROUPS, TILE, hd), F32),
            pltpu.VMEM((N_ATT_GROUPS, TILE, LANES), F32),
        ],
        compiler_params=_params(("arbitrary", "arbitrary")),
        name="attention",
    )(rel_bias, jnp.asarray(_bucket_table()), *([y2] * (5 * N_ATT_GROUPS)))


def _mix_kernel(x_ref, og_ref, oa_ref, ga_ref, gb_ref, wog_ref, woa_ref, wout_ref, x1_ref):
    y_gla = _dot(og_ref[...], wog_ref[...])
    y_att = _dot(oa_ref[...], woa_ref[...])
    mix = (jax.nn.sigmoid(ga_ref[...].astype(F32)) * y_gla
           + jax.nn.sigmoid(gb_ref[...].astype(F32)) * y_att)
    x1_ref[...] = x_ref[...] + _dot(mix.astype(BF16), wout_ref[...])


def _mix(x2, y2, o_gla, o_att, wog, woa, wout, tm=512):
    T = x2.shape[0]
    gcol = COL_GATE // D_MODEL
    row = lambda w: pl.BlockSpec((tm, w), lambda i: (i, 0))
    full = lambda a: pl.BlockSpec(a.shape, lambda i: (0, 0))
    return pl.pallas_call(
        _mix_kernel,
        grid=(T // tm,),
        in_specs=[row(D_MODEL), row(GLA_V_W), row(ATT_W),
                  pl.BlockSpec((tm, D_MODEL), lambda i: (i, gcol)),
                  pl.BlockSpec((tm, D_MODEL), lambda i: (i, gcol + 1)),
                  full(wog), full(woa), full(wout)],
        out_specs=row(D_MODEL),
        out_shape=jax.ShapeDtypeStruct((T, D_MODEL), F32),
        compiler_params=_params(("parallel",)),
        name="mix",
    )(x2, o_gla, o_att, y2, y2, wog, woa, wout)


def _mlp_kernel(x_ref, p_ref, ln2_ref, w1_ref, w2_ref, ln3_ref, wpg_ref, wpp_ref, lnf_ref,
                out_ref, h_scr, acc_scr):
    f = pl.program_id(1)

    @pl.when(f == 0)
    def _():
        h_scr[...] = _rms(x_ref[...], ln2_ref[...]).astype(BF16)
        acc_scr[...] = jnp.zeros_like(acc_scr)

    u = jnp.maximum(_dot(h_scr[...], w1_ref[...]), 0.0)
    acc_scr[...] += _dot((u * u).astype(BF16), w2_ref[...])

    @pl.when(f == pl.num_programs(1) - 1)
    def _():
        x2 = x_ref[...] + acc_scr[...]
        h3 = _rms(x2, ln3_ref[...]).astype(BF16)
        gate = jax.nn.sigmoid(_dot(h3, wpg_ref[...]))
        x3 = x2 + gate * _dot(p_ref[...].astype(BF16), wpp_ref[...])
        out_ref[...] = _rms(x3, lnf_ref[...])


def _mlp(x1, p2, ln2, w1, w2, ln3, wpg, wpp, lnf, tm=1024, tf=1024):
    T = x1.shape[0]
    vec = pl.BlockSpec((1, D_MODEL), lambda i, f: (0, 0))
    return pl.pallas_call(
        _mlp_kernel,
        grid=(T // tm, D_FF // tf),
        in_specs=[
            pl.BlockSpec((tm, D_MODEL), lambda i, f: (i, 0)),
            pl.BlockSpec((tm, PLE_DIM), lambda i, f: (i, 0)),
            vec,
            pl.BlockSpec((D_MODEL, tf), lambda i, f: (0, f)),
            pl.BlockSpec((tf, D_MODEL), lambda i, f: (f, 0)),
            vec,
            pl.BlockSpec((D_MODEL, D_MODEL), lambda i, f: (0, 0)),
            pl.BlockSpec((PLE_DIM, D_MODEL), lambda i, f: (0, 0)),
            vec,
        ],
        out_specs=pl.BlockSpec((tm, D_MODEL), lambda i, f: (i, 0)),
        out_shape=jax.ShapeDtypeStruct((T, D_MODEL), F32),
        scratch_shapes=[pltpu.VMEM((tm, D_MODEL), BF16), pltpu.VMEM((tm, D_MODEL), F32)],
        compiler_params=_params(("parallel", "arbitrary")),
        name="mlp",
    )(x1, p2, ln2, w1, w2, ln3, wpg, wpp, lnf)


def _layer(x2, p2, B, S, ln1, w_in, w_a2, b_a, gla_gn, w_o_gla, w_o_attn, w_out,
           ln2, w_mlp1, w_mlp2, ln3, w_pp, w_pg, rel_bias, ln_out):
    wa2 = jnp.pad(w_a2, ((0, LANES - GLA_RANK), (0, 0)))

    w_in_t = w_in.T
    y2, la = _inproj(x2, ln1[None], _wprep(w_in_t), w_in_t, wa2, b_a[None])
    o_gla = _gla(y2.reshape(B, S, Y_COLS), la.reshape(B, S, GLA_QK_W), gla_gn[None])
    o_att = _attention(y2, rel_bias, S)
    x1 = _mix(x2, y2, o_gla.reshape(B * S, GLA_V_W), o_att,
              w_o_gla.astype(BF16), w_o_attn.astype(BF16), w_out.astype(BF16))
    return _mlp(x1, p2, ln2[None], w_mlp1.astype(BF16), w_mlp2.astype(BF16), ln3[None],
                w_pg.astype(BF16), w_pp.astype(BF16), ln_out[None])


def kernel(x, p, ln1, w_in, w_a2, b_a, gla_gn, w_o_gla, w_o_attn, w_out, ln2, w_mlp1, w_mlp2,
           ln3, w_pp, w_pg, rel_bias, ln_f):
    B, S, D = x.shape
    assert p.shape[0] == 1, "the final norm is fused into the single layer's last kernel"
    assert S % TILE == 0
    x2 = x.reshape(B * S, D)
    out = _layer(x2, p[0].reshape(B * S, PLE_DIM), B, S, ln1[0], w_in[0], w_a2[0], b_a[0],
                 gla_gn[0], w_o_gla[0], w_o_attn[0], w_out[0], ln2[0], w_mlp1[0], w_mlp2[0],
                 ln3[0], w_pp[0], w_pg[0], rel_bias, ln_f)
    return out.reshape(B, S, D)
```

```python
import functools

import numpy as np
import jax
import jax.numpy as jnp
from jax import lax
from jax.experimental import pallas as pl
from jax.experimental.pallas import tpu as pltpu

F32 = jnp.float32
BF16 = jnp.bfloat16

D_MODEL = 1024
PLE_DIM = 256
EPS = 1e-6
GLA_HEADS = 4
GLA_DK = 128
GLA_DV = 256
GLA_RANK = 16
GLA_TAU = 16.0
GLA_CHUNK = 64
GLA_QK_W = GLA_HEADS * GLA_DK
GLA_V_W = GLA_HEADS * GLA_DV
ATT_GROUPS = ((128, 1), (512, 4), (2048, 16))
ATT_DILS = tuple(d for _, d in ATT_GROUPS)
ATT_HEADS_PER_GROUP = 4
ATT_HEAD_DIM = 128
N_ATT_GROUPS = len(ATT_GROUPS)
ATT_W = ATT_HEADS_PER_GROUP * ATT_HEAD_DIM
ATT_BLOCK = 128
REL_BUCKETS = 32
REL_MAX_DIST = 2048
D_FF = 4 * D_MODEL
NEG_INF = -1e30

LANES = 128

COL_Q = 0
COL_K = COL_Q + GLA_QK_W
COL_V = COL_K + GLA_QK_W
COL_R = COL_V + GLA_V_W
COL_GATE = COL_R + GLA_V_W
COL_ATT = COL_GATE + 2 * D_MODEL
ATT_GROUP_W = 3 * ATT_W
Y_COLS = COL_ATT + N_ATT_GROUPS * ATT_GROUP_W

TILE = max(ATT_DILS) * ATT_BLOCK

VMEM_LIMIT = 56 * 1024 * 1024


def _params(semantics):
    return pltpu.CompilerParams(dimension_semantics=semantics, vmem_limit_bytes=VMEM_LIMIT)


def _rms(x, g):
    return x * lax.rsqrt(jnp.mean(x * x, axis=-1, keepdims=True) + EPS) * g


def _dot(a, b):
    return jnp.dot(a, b, preferred_element_type=F32)


def _dot_nt(a, b):
    return lax.dot_general(a, b, (((1,), (1,)), ((), ())), preferred_element_type=F32)


def _dot_tn(a, b):
    return lax.dot_general(a, b, (((0,), (0,)), ((), ())), preferred_element_type=F32)


W_GATE_COL = COL_R + GLA_V_W
W_ATT_COL = W_GATE_COL + GLA_RANK
W_MIXGATE_COL = W_ATT_COL + N_ATT_GROUPS * ATT_GROUP_W


def _wprep_kernel(a_ref, b_ref, scale_ref, o_ref, *, n_aligned, tn):
    t = pl.program_id(0)

    @pl.when(t < n_aligned)
    def _():
        o_ref[...] = (a_ref[...].T * scale_ref[...]).astype(BF16)

    @pl.when(t >= n_aligned)
    def _():
        w = jnp.concatenate([a_ref[...], b_ref[...]], axis=0)[GLA_RANK:GLA_RANK + tn]
        o_ref[...] = (w.T * scale_ref[...]).astype(BF16)


def _wprep(w_in_t, tn=512):
    n_aligned = W_GATE_COL // tn
    n_att = (W_MIXGATE_COL - W_ATT_COL) // tn

    def out_col(t):
        return jnp.where(t < n_aligned, t,
                         jnp.where(t < n_aligned + n_att, t - n_aligned + COL_ATT // tn,
                                   t - n_aligned - n_att + COL_GATE // tn))

    col_scale = np.ones((1, Y_COLS), np.float32)
    col_scale[:, COL_Q:COL_K] = GLA_DK ** -0.5
    for g in range(N_ATT_GROUPS):
        col_scale[:, COL_ATT + g * ATT_GROUP_W:COL_ATT + g * ATT_GROUP_W + ATT_W] = ATT_HEAD_DIM ** -0.5
    return pl.pallas_call(
        functools.partial(_wprep_kernel, n_aligned=n_aligned, tn=tn),
        grid=(Y_COLS // tn,),
        in_specs=[
            pl.BlockSpec((tn, D_MODEL), lambda t: (t, 0)),
            pl.BlockSpec((GLA_RANK, D_MODEL), lambda t: ((t + 1) * (tn // GLA_RANK), 0)),
            pl.BlockSpec((1, tn), lambda t: (0, out_col(t))),
        ],
        out_specs=pl.BlockSpec((D_MODEL, tn), lambda t: (0, out_col(t))),
        out_shape=jax.ShapeDtypeStruct((D_MODEL, Y_COLS), BF16),
        compiler_params=_params(("parallel",)),
        name="wprep",
    )(w_in_t, w_in_t, jnp.asarray(col_scale))


def _inproj_kernel(*refs, n_natural, n_group):
    n_slab = D_MODEL // LANES
    x_slabs = refs[:n_slab]
    ln_ref, w_ref, wa1_ref, wa2_ref, ba_ref, y_ref, la_ref, h_scr, inv_scr, tmp_scr = refs[n_slab:]
    j = pl.program_id(1)

    @pl.when(j == 0)
    def _():
        chunk = 256
        for c in range(TILE // chunk):
            rows = pl.ds(c * chunk, chunk)
            parts = [xr[rows, :] for xr in x_slabs]
            sq = parts[0] * parts[0]
            for part in parts[1:]:
                sq = sq + part * part
            ms = jnp.sum(sq, axis=-1, keepdims=True) * (1.0 / D_MODEL)
            inv = jnp.broadcast_to(lax.rsqrt(ms + EPS), (chunk, LANES))
            inv_scr[rows, :] = inv
            for s, part in enumerate(parts):
                cols = pl.ds(s * LANES, LANES)
                h_scr[0, rows, cols] = (part * inv * ln_ref[:, cols]).astype(BF16)
        assert ATT_DILS == (1, 4, 16)
        n4, n16 = TILE // 4, TILE // 16
        for s, xr in enumerate(x_slabs):
            cols = pl.ds(s * LANES, LANES)
            ln_s = ln_ref[:, cols]
            for a in range(4):
                for c in range(n4 // chunk):
                    src = pl.ds(a + 4 * c * chunk, chunk, stride=4)
                    dst = pl.ds(a * n4 + c * chunk, chunk)
                    hn = xr[src, :] * inv_scr[src, :] * ln_s
                    tmp_scr[dst, :] = hn
                    h_scr[1, dst, cols] = hn.astype(BF16)
            for a in range(4):
                for b in range(4):
                    src = pl.ds(a * n4 + b, n16, stride=4)
                    h_scr[2, pl.ds((a + 4 * b) * n16, n16), cols] = tmp_scr[src, :].astype(BF16)
        wa1_t = jnp.concatenate([wa1_ref[...], jnp.zeros((LANES - GLA_RANK, D_MODEL), F32)], axis=0)
        ha = _dot_nt(h_scr[0], wa1_t.astype(BF16))
        a_hi, a_mid, _ = _split3(ha)
        w_hi, w_mid, _ = _split3(wa2_ref[...])
        z = _dot(a_hi, w_hi) + (_dot(a_mid, w_hi) + _dot(a_hi, w_mid)) + ba_ref[...]
        log_sig = jnp.minimum(z, 0.0) - jnp.log(1.0 + jnp.exp(-jnp.abs(z)))
        la_ref[...] = log_sig * (1.0 / GLA_TAU)

    d = (j >= n_natural).astype(jnp.int32) + (j >= n_natural + n_group).astype(jnp.int32)
    y_ref[...] = _dot(h_scr[d], w_ref[...]).astype(BF16)


def _inproj(x2, ln1, w_main, w_in_t, wa2, ba, tn=512):
    T = x2.shape[0]
    n_natural = (COL_ATT + ATT_GROUP_W) // tn
    n_group = ATT_GROUP_W // tn
    n_slab = D_MODEL // LANES
    return pl.pallas_call(
        functools.partial(_inproj_kernel, n_natural=n_natural, n_group=n_group),
        grid=(T // TILE, Y_COLS // tn),
        in_specs=[pl.BlockSpec((TILE, LANES), functools.partial(lambda i, j, s: (i, s), s=s))
                  for s in range(n_slab)] + [
            pl.BlockSpec((1, D_MODEL), lambda i, j: (0, 0)),
            pl.BlockSpec((D_MODEL, tn), lambda i, j: (0, j)),
            pl.BlockSpec((GLA_RANK, D_MODEL), lambda i, j: (W_GATE_COL // GLA_RANK, 0)),
            pl.BlockSpec((LANES, GLA_QK_W), lambda i, j: (0, 0)),
            pl.BlockSpec((1, GLA_QK_W), lambda i, j: (0, 0)),
        ],
        out_specs=[
            pl.BlockSpec((TILE, tn), lambda i, j: (i, j)),
            pl.BlockSpec((TILE, GLA_QK_W), lambda i, j: (i, 0)),
        ],
        out_shape=[
            jax.ShapeDtypeStruct((T, Y_COLS), BF16),
            jax.ShapeDtypeStruct((T, GLA_QK_W), F32),
        ],
        scratch_shapes=[pltpu.VMEM((N_ATT_GROUPS, TILE, D_MODEL), BF16),
                        pltpu.VMEM((TILE, LANES), F32),
                        pltpu.VMEM((TILE, LANES), F32)],
        compiler_params=_params(("parallel", "arbitrary")),
        name="inproj",
    )(*([x2] * n_slab), ln1, w_main, w_in_t, wa2, ba)


def _split3(g):
    hi = g.astype(BF16)
    r1 = g - hi.astype(F32)
    mid = r1.astype(BF16)
    lo = (r1 - mid.astype(F32)).astype(BF16)
    return hi, mid, lo


def _gla_kernel(q_ref, k_ref, v_ref, r_ref, la_ref, gn_ref, o_ref, st_ref, *, nchunk):
    C = GLA_CHUNK

    @pl.when(pl.program_id(1) == 0)
    def _():
        st_ref[...] = jnp.zeros_like(st_ref)

    row = lax.broadcasted_iota(jnp.int32, (C, C), 0)
    col = lax.broadcasted_iota(jnp.int32, (C, C), 1)
    causal = row >= col
    tri = causal.astype(BF16)

    H = range(GLA_HEADS)
    ck = [slice(h * GLA_DK, (h + 1) * GLA_DK) for h in H]
    cv = [slice(h * GLA_DV, (h + 1) * GLA_DV) for h in H]
    for c in range(nchunk):
        sl = pl.ds(c * C, C)
        hi, mid, lo = _split3(la_ref[0, sl, :])
        b = _dot(tri, hi) + _dot(tri, mid) + _dot(tri, lo)
        decay = jnp.exp(b[C - 1:C, :])
        q_dec = (q_ref[0, sl, :].astype(F32) * jnp.exp(b)).astype(BF16)
        k_in = k_ref[0, sl, :].astype(F32) * jnp.exp(-b)
        k_out = (k_in * decay).astype(BF16)
        k_in = k_in.astype(BF16)
        v = [v_ref[0, sl, cv[h]] for h in H]
        attn = [_dot_nt(q_dec[:, ck[h]], k_in[:, ck[h]]) for h in H]
        st = [st_ref[h] for h in H]
        o_inter = [_dot_nt(q_dec[:, ck[h]], st[h].astype(BF16)) for h in H]
        upd = [_dot_tn(v[h], k_out[:, ck[h]]) for h in H]
        o = [_dot(jnp.where(causal, attn[h], 0.0).astype(BF16), v[h]) + o_inter[h] for h in H]
        for h in H:
            st_ref[h] = st[h] * decay[:, ck[h]] + upd[h]
        for h in H:
            r = r_ref[0, sl, cv[h]].astype(F32)
            o_ref[0, sl, cv[h]] = (_rms(o[h], gn_ref[:, cv[h]]) * (r * jax.nn.sigmoid(r))).astype(BF16)


def _gla(y3, la3, gn, tb=512):
    B, S, _ = y3.shape
    return pl.pallas_call(
        functools.partial(_gla_kernel, nchunk=tb // GLA_CHUNK),
        grid=(B, S // tb),
        in_specs=[
            pl.BlockSpec((1, tb, GLA_QK_W), lambda b, t: (b, t, COL_Q // GLA_QK_W)),
            pl.BlockSpec((1, tb, GLA_QK_W), lambda b, t: (b, t, COL_K // GLA_QK_W)),
            pl.BlockSpec((1, tb, GLA_V_W), lambda b, t: (b, t, COL_V // GLA_V_W)),
            pl.BlockSpec((1, tb, GLA_V_W), lambda b, t: (b, t, COL_R // GLA_V_W)),
            pl.BlockSpec((1, tb, GLA_QK_W), lambda b, t: (b, t, 0)),
            pl.BlockSpec((1, GLA_V_W), lambda b, t: (0, 0)),
        ],
        out_specs=pl.BlockSpec((1, tb, GLA_V_W), lambda b, t: (b, t, 0)),
        out_shape=jax.ShapeDtypeStruct((B, S, GLA_V_W), BF16),
        scratch_shapes=[pltpu.VMEM((GLA_HEADS, GLA_DV, GLA_DK), F32)],
        compiler_params=_params(("parallel", "arbitrary")),
        name="gla",
    )(y3, y3, y3, y3, la3, gn)


def _t5_causal_bucket(n):
    max_exact = REL_BUCKETS // 2
    nf = np.maximum(n, 1).astype(np.float32)
    large = max_exact + (np.log(nf / max_exact) / np.log(REL_MAX_DIST / max_exact)
                         * (REL_BUCKETS - max_exact)).astype(np.int32)
    large = np.minimum(large, REL_BUCKETS - 1)
    return np.where(n < max_exact, n, large).astype(np.int32)


def _bucket_table():
    qi = np.arange(ATT_BLOCK)[:, None]
    kj = np.arange(2 * ATT_BLOCK)[None, :]
    delta = qi + ATT_BLOCK - kj
    out = []
    for win, dil in ATT_GROUPS:
        band = (delta >= 0) & (delta <= win // dil)
        bucket = np.where(band, _t5_causal_bucket(np.maximum(delta, 0) * dil), -1)
        out.append(bucket)
    return np.stack(out).astype(np.int32)


def _att_unit(q, k, v, bias):
    BLK = ATT_BLOCK
    s = _dot_nt(q, k) + bias
    m = jnp.max(jnp.maximum(s[:, :BLK], s[:, BLK:]), axis=-1, keepdims=True)
    p = jnp.exp(s - m).astype(BF16)
    v_ext = jnp.concatenate([v, jnp.ones_like(v)], axis=1)
    pv = _dot(p, v_ext)
    denom = pv[:, BLK:]
    return pv[:, :BLK] / denom, m + jnp.log(denom)


def _att_kernel(tab_ref, bucket_ref,
                q0, k0, v0, kp0, vp0, q1, k1, v1, kp1, vp1, q2, k2, v2, kp2, vp2,
                o_ref, bias_scr, o_scr, lse_scr, *, tiles_per_batch):
    BLK = ATT_BLOCK
    ti = pl.program_id(0)
    h = pl.program_id(1)

    @pl.when((ti == 0) & (h == 0))
    def _():
        for g in range(N_ATT_GROUPS):
            bucket = bucket_ref[g]
            for hh in range(ATT_HEADS_PER_GROUP):
                bias = jnp.full(bucket.shape, NEG_INF, F32)
                for b in range(REL_BUCKETS):
                    bias = jnp.where(bucket == b, tab_ref[b, g * ATT_HEADS_PER_GROUP + hh], bias)
                bias_scr[g, hh] = bias

    lane = lax.broadcasted_iota(jnp.int32, (BLK, 2 * BLK), 1)
    no_prev = jnp.logical_and((ti % tiles_per_batch) == 0, lane < BLK)
    groups = ((q0, k0, v0, kp0, vp0), (q1, k1, v1, kp1, vp1), (q2, k2, v2, kp2, vp2))
    for g, (q, k, v, kp, vp) in enumerate(groups):
        dil = ATT_DILS[g]
        n_r = TILE // dil
        kp_rows = kp.shape[0]
        bias = bias_scr[g, h]
        bias_first = jnp.where(no_prev, NEG_INF, bias)
        for r in range(dil):
            for j in range(n_r // BLK):
                cur = pl.ds(r * n_r + j * BLK, BLK)
                if j > 0:
                    both = pl.ds(r * n_r + (j - 1) * BLK, 2 * BLK)
                    k_all, v_all, b = k[both, :], v[both, :], bias
                else:
                    prev = pl.ds((r * n_r + n_r - BLK) % kp_rows, BLK)
                    k_all = jnp.concatenate([kp[prev, :], k[cur, :]], axis=0)
                    v_all = jnp.concatenate([vp[prev, :], v[cur, :]], axis=0)
                    b = bias_first
                if dil == 1:
                    rows = pl.ds(j * BLK, BLK)
                else:
                    rows = pl.ds(dil * j * BLK + r, BLK, stride=dil)
                o_scr[g, rows, :], lse_scr[g, rows, :] = _att_unit(q[cur, :], k_all, v_all, b)

    for c in range(TILE // 256):
        rows = pl.ds(c * 256, 256)
        lse = [lse_scr[g, rows, :] for g in range(N_ATT_GROUPS)]
        top = functools.reduce(jnp.maximum, lse)
        w = [jnp.exp(x - top) for x in lse]
        num = functools.reduce(lambda a, b: a + b, [w[g] * o_scr[g, rows, :] for g in range(N_ATT_GROUPS)])
        den = functools.reduce(lambda a, b: a + b, w)
        o_ref[rows, :] = (num / den).astype(BF16)


def _attention(y2, rel_bias, S):
    T = y2.shape[0]
    tiles_per_batch = S // TILE
    hd = ATT_HEAD_DIM

    def col(g, c):
        return (COL_ATT + g * ATT_GROUP_W + c * ATT_W) // hd

    def cur(g, c):
        return pl.BlockSpec((TILE, hd), lambda t, h: (t, col(g, c) + h))

    def prev_tile(g, c):
        return pl.BlockSpec((TILE, hd), lambda t, h: (jnp.maximum(t - 1, 0), col(g, c) + h))

    def prev_block(g, c):
        nb = TILE // ATT_BLOCK
        return pl.BlockSpec((ATT_BLOCK, hd), lambda t, h: (jnp.maximum(t * nb - 1, 0), col(g, c) + h))

    in_specs = [pl.BlockSpec(memory_space=pltpu.SMEM),
                pl.BlockSpec((N_ATT_GROUPS, ATT_BLOCK, 2 * ATT_BLOCK), lambda t, h: (0, 0, 0))]
    for g in range(N_ATT_GROUPS):
        prv = prev_block if ATT_DILS[g] == 1 else prev_tile
        in_specs += [cur(g, 0), cur(g, 1), cur(g, 2), prv(g, 1), prv(g, 2)]
    return pl.pallas_call(
        functools.partial(_att_kernel, tiles_per_batch=tiles_per_batch),
        grid=(T // TILE, ATT_HEADS_PER_GROUP),
        in_specs=in_specs,
        out_specs=pl.BlockSpec((TILE, hd), lambda t, h: (t, h)),
        out_shape=jax.ShapeDtypeStruct((T, ATT_W), BF16),
        scratch_shapes=[
            pltpu.VMEM((N_ATT_GROUPS, ATT_HEADS_PER_GROUP, ATT_BLOCK, 2 * ATT_BLOCK), F32),
            pltpu.VMEM((N_ATT_GROUPS, TILE, hd), F32),
            pltpu.VMEM((N_ATT_GROUPS, TILE, LANES), F32),
        ],
        compiler_params=_params(("arbitrary", "arbitrary")),
        name="attention",
    )(rel_bias, jnp.asarray(_bucket_table()), *([y2] * (5 * N_ATT_GROUPS)))


def _mix_kernel(x_ref, og_ref, oa_ref, ga_ref, gb_ref, wog_ref, woa_ref, wout_ref, x1_ref):
    y_gla = _dot(og_ref[...], wog_ref[...])
    y_att = _dot(oa_ref[...], woa_ref[...])
    mix = (jax.nn.sigmoid(ga_ref[...].astype(F32)) * y_gla
           + jax.nn.sigmoid(gb_ref[...].astype(F32)) * y_att)
    x1_ref[...] = x_ref[...] + _dot(mix.astype(BF16), wout_ref[...])


def _mix(x2, y2, o_gla, o_att, wog, woa, wout, tm=512):
    T = x2.shape[0]
    gcol = COL_GATE // D_MODEL
    row = lambda w: pl.BlockSpec((tm, w), lambda i: (i, 0))
    full = lambda a: pl.BlockSpec(a.shape, lambda i: (0, 0))
    return pl.pallas_call(
        _mix_kernel,
        grid=(T // tm,),
        in_specs=[row(D_MODEL), row(GLA_V_W), row(ATT_W),
                  pl.BlockSpec((tm, D_MODEL), lambda i: (i, gcol)),
                  pl.BlockSpec((tm, D_MODEL), lambda i: (i, gcol + 1)),
                  full(wog), full(woa), full(wout)],
        out_specs=row(D_MODEL),
        out_shape=jax.ShapeDtypeStruct((T, D_MODEL), F32),
        compiler_params=_params(("parallel",)),
        name="mix",
    )(x2, o_gla, o_att, y2, y2, wog, woa, wout)


def _mlp_kernel(x_ref, p_ref, ln2_ref, w1_ref, w2_ref, ln3_ref, wpg_ref, wpp_ref, lnf_ref,
                out_ref, h_scr, acc_scr):
    f = pl.program_id(1)

    @pl.when(f == 0)
    def _():
        h_scr[...] = _rms(x_ref[...], ln2_ref[...]).astype(BF16)
        acc_scr[...] = jnp.zeros_like(acc_scr)

    u = jnp.maximum(_dot(h_scr[...], w1_ref[...]), 0.0)
    acc_scr[...] += _dot((u * u).astype(BF16), w2_ref[...])

    @pl.when(f == pl.num_programs(1) - 1)
    def _():
        x2 = x_ref[...] + acc_scr[...]
        h3 = _rms(x2, ln3_ref[...]).astype(BF16)
        gate = jax.nn.sigmoid(_dot(h3, wpg_ref[...]))
        x3 = x2 + gate * _dot(p_ref[...].astype(BF16), wpp_ref[...])
        out_ref[...] = _rms(x3, lnf_ref[...])


def _mlp(x1, p2, ln2, w1, w2, ln3, wpg, wpp, lnf, tm=1024, tf=2048):
    T = x1.shape[0]
    vec = pl.BlockSpec((1, D_MODEL), lambda i, f: (0, 0))
    return pl.pallas_call(
        _mlp_kernel,
        grid=(T // tm, D_FF // tf),
        in_specs=[
            pl.BlockSpec((tm, D_MODEL), lambda i, f: (i, 0)),
            pl.BlockSpec((tm, PLE_DIM), lambda i, f: (i, 0)),
            vec,
            pl.BlockSpec((D_MODEL, tf), lambda i, f: (0, f)),
            pl.BlockSpec((tf, D_MODEL), lambda i, f: (f, 0)),
            vec,
            pl.BlockSpec((D_MODEL, D_MODEL), lambda i, f: (0, 0)),
            pl.BlockSpec((PLE_DIM, D_MODEL), lambda i, f: (0, 0)),
            vec,
        ],
        out_specs=pl.BlockSpec((tm, D_MODEL), lambda i, f: (i, 0)),
        out_shape=jax.ShapeDtypeStruct((T, D_MODEL), F32),
        scratch_shapes=[pltpu.VMEM((tm, D_MODEL), BF16), pltpu.VMEM((tm, D_MODEL), F32)],
        compiler_params=_params(("parallel", "arbitrary")),
        name="mlp",
    )(x1, p2, ln2, w1, w2, ln3, wpg, wpp, lnf)


def _layer(x2, p2, B, S, ln1, w_in, w_a2, b_a, gla_gn, w_o_gla, w_o_attn, w_out,
           ln2, w_mlp1, w_mlp2, ln3, w_pp, w_pg, rel_bias, ln_out):
    wa2 = jnp.pad(w_a2, ((0, LANES - GLA_RANK), (0, 0)))

    w_in_t = w_in.T
    y2, la = _inproj(x2, ln1[None], _wprep(w_in_t), w_in_t, wa2, b_a[None])
    o_gla = _gla(y2.reshape(B, S, Y_COLS), la.reshape(B, S, GLA_QK_W), gla_gn[None])
    o_att = _attention(y2, rel_bias, S)
    x1 = _mix(x2, y2, o_gla.reshape(B * S, GLA_V_W), o_att,
              w_o_gla.astype(BF16), w_o_attn.astype(BF16), w_out.astype(BF16))
    return _mlp(x1, p2, ln2[None], w_mlp1.astype(BF16), w_mlp2.astype(BF16), ln3[None],
                w_pg.astype(BF16), w_pp.astype(BF16), ln_out[None])


def kernel(x, p, ln1, w_in, w_a2, b_a, gla_gn, w_o_gla, w_o_attn, w_out, ln2, w_mlp1, w_mlp2,
           ln3, w_pp, w_pg, rel_bias, ln_f):
    B, S, D = x.shape
    assert p.shape[0] == 1, "the final norm is fused into the single layer's last kernel"
    assert S % TILE == 0
    x2 = x.reshape(B * S, D)
    out = _layer(x2, p[0].reshape(B * S, PLE_DIM), B, S, ln1[0], w_in[0], w_a2[0], b_a[0],
                 gla_gn[0], w_o_gla[0], w_o_attn[0], w_out[0], ln2[0], w_mlp1[0], w_mlp2[0],
                 ln3[0], w_pp[0], w_pg[0], rel_bias, ln_f)
    return out.reshape(B, S, D)
```

```python
import functools

import numpy as np
import jax
import jax.numpy as jnp
from jax import lax
from jax.experimental import pallas as pl
from jax.experimental.pallas import tpu as pltpu

F32 = jnp.float32
BF16 = jnp.bfloat16

D_MODEL = 1024
PLE_DIM = 256
EPS = 1e-6
GLA_HEADS = 4
GLA_DK = 128
GLA_DV = 256
GLA_RANK = 16
GLA_TAU = 16.0
GLA_CHUNK = 64
GLA_QK_W = GLA_HEADS * GLA_DK
GLA_V_W = GLA_HEADS * GLA_DV
ATT_GROUPS = ((128, 1), (512, 4), (2048, 16))
ATT_DILS = tuple(d for _, d in ATT_GROUPS)
ATT_HEADS_PER_GROUP = 4
ATT_HEAD_DIM = 128
N_ATT_GROUPS = len(ATT_GROUPS)
ATT_W = ATT_HEADS_PER_GROUP * ATT_HEAD_DIM
ATT_BLOCK = 128
REL_BUCKETS = 32
REL_MAX_DIST = 2048
D_FF = 4 * D_MODEL
NEG_INF = -1e30

LANES = 128

COL_Q = 0
COL_K = COL_Q + GLA_QK_W
COL_V = COL_K + GLA_QK_W
COL_R = COL_V + GLA_V_W
COL_GATE = COL_R + GLA_V_W
COL_ATT = COL_GATE + 2 * D_MODEL
ATT_GROUP_W = 3 * ATT_W
Y_COLS = COL_ATT + N_ATT_GROUPS * ATT_GROUP_W

TILE = max(ATT_DILS) * ATT_BLOCK

VMEM_LIMIT = 56 * 1024 * 1024


def _params(semantics):
    return pltpu.CompilerParams(dimension_semantics=semantics, vmem_limit_bytes=VMEM_LIMIT)


def _rms(x, g):
    return x * lax.rsqrt(jnp.mean(x * x, axis=-1, keepdims=True) + EPS) * g


def _dot(a, b):
    return jnp.dot(a, b, preferred_element_type=F32)


def _dot_nt(a, b):
    return lax.dot_general(a, b, (((1,), (1,)), ((), ())), preferred_element_type=F32)


def _dot_tn(a, b):
    return lax.dot_general(a, b, (((0,), (0,)), ((), ())), preferred_element_type=F32)


W_GATE_COL = COL_R + GLA_V_W
W_ATT_COL = W_GATE_COL + GLA_RANK
W_MIXGATE_COL = W_ATT_COL + N_ATT_GROUPS * ATT_GROUP_W


def _wprep_kernel(a_ref, b_ref, scale_ref, o_ref, *, n_aligned, tn):
    t = pl.program_id(0)

    @pl.when(t < n_aligned)
    def _():
        o_ref[...] = (a_ref[...].T * scale_ref[...]).astype(BF16)

    @pl.when(t >= n_aligned)
    def _():
        w = jnp.concatenate([a_ref[...], b_ref[...]], axis=0)[GLA_RANK:GLA_RANK + tn]
        o_ref[...] = (w.T * scale_ref[...]).astype(BF16)


def _wprep(w_in_t, tn=512):
    n_aligned = W_GATE_COL // tn
    n_att = (W_MIXGATE_COL - W_ATT_COL) // tn

    def out_col(t):
        return jnp.where(t < n_aligned, t,
                         jnp.where(t < n_aligned + n_att, t - n_aligned + COL_ATT // tn,
                                   t - n_aligned - n_att + COL_GATE // tn))

    col_scale = np.ones((1, Y_COLS), np.float32)
    col_scale[:, COL_Q:COL_K] = GLA_DK ** -0.5
    for g in range(N_ATT_GROUPS):
        col_scale[:, COL_ATT + g * ATT_GROUP_W:COL_ATT + g * ATT_GROUP_W + ATT_W] = ATT_HEAD_DIM ** -0.5
    return pl.pallas_call(
        functools.partial(_wprep_kernel, n_aligned=n_aligned, tn=tn),
        grid=(Y_COLS // tn,),
        in_specs=[
            pl.BlockSpec((tn, D_MODEL), lambda t: (t, 0)),
            pl.BlockSpec((GLA_RANK, D_MODEL), lambda t: ((t + 1) * (tn // GLA_RANK), 0)),
            pl.BlockSpec((1, tn), lambda t: (0, out_col(t))),
        ],
        out_specs=pl.BlockSpec((D_MODEL, tn), lambda t: (0, out_col(t))),
        out_shape=jax.ShapeDtypeStruct((D_MODEL, Y_COLS), BF16),
        compiler_params=_params(("parallel",)),
        name="wprep",
    )(w_in_t, w_in_t, jnp.asarray(col_scale))


def _inproj_kernel(*refs, n_natural, n_group):
    n_slab = D_MODEL // LANES
    x_slabs = refs[:n_slab]
    ln_ref, w_ref, wa1_ref, wa2_ref, ba_ref, y_ref, la_ref, h_scr, inv_scr, tmp_scr = refs[n_slab:]
    j = pl.program_id(1)

    @pl.when(j == 0)
    def _():
        chunk = 256
        for c in range(TILE // chunk):
            rows = pl.ds(c * chunk, chunk)
            parts = [xr[rows, :] for xr in x_slabs]
            sq = parts[0] * parts[0]
            for part in parts[1:]:
                sq = sq + part * part
            ms = jnp.sum(sq, axis=-1, keepdims=True) * (1.0 / D_MODEL)
            inv = jnp.broadcast_to(lax.rsqrt(ms + EPS), (chunk, LANES))
            inv_scr[rows, :] = inv
            for s, part in enumerate(parts):
                cols = pl.ds(s * LANES, LANES)
                h_scr[0, rows, cols] = (part * inv * ln_ref[:, cols]).astype(BF16)
        assert ATT_DILS == (1, 4, 16)
        n4, n16 = TILE // 4, TILE // 16
        for s, xr in enumerate(x_slabs):
            cols = pl.ds(s * LANES, LANES)
            ln_s = ln_ref[:, cols]
            for a in range(4):
                for c in range(n4 // chunk):
                    src = pl.ds(a + 4 * c * chunk, chunk, stride=4)
                    dst = pl.ds(a * n4 + c * chunk, chunk)
                    hn = xr[src, :] * inv_scr[src, :] * ln_s
                    tmp_scr[dst, :] = hn
                    h_scr[1, dst, cols] = hn.astype(BF16)
            for a in range(4):
                for b in range(4):
                    src = pl.ds(a * n4 + b, n16, stride=4)
                    h_scr[2, pl.ds((a + 4 * b) * n16, n16), cols] = tmp_scr[src, :].astype(BF16)
        wa1_t = jnp.concatenate([wa1_ref[...], jnp.zeros((LANES - GLA_RANK, D_MODEL), F32)], axis=0)
        ha = _dot_nt(h_scr[0], wa1_t.astype(BF16))
        a_hi, a_mid, _ = _split3(ha)
        w_hi, w_mid, _ = _split3(wa2_ref[...])
        z = _dot(a_hi, w_hi) + (_dot(a_mid, w_hi) + _dot(a_hi, w_mid)) + ba_ref[...]
        log_sig = jnp.minimum(z, 0.0) - jnp.log(1.0 + jnp.exp(-jnp.abs(z)))
        la_ref[...] = log_sig * (1.0 / GLA_TAU)

    d = (j >= n_natural).astype(jnp.int32) + (j >= n_natural + n_group).astype(jnp.int32)
    y_ref[...] = _dot(h_scr[d], w_ref[...]).astype(BF16)


def _inproj(x2, ln1, w_main, w_in_t, wa2, ba, tn=512):
    T = x2.shape[0]
    n_natural = (COL_ATT + ATT_GROUP_W) // tn
    n_group = ATT_GROUP_W // tn
    n_slab = D_MODEL // LANES
    return pl.pallas_call(
        functools.partial(_inproj_kernel, n_natural=n_natural, n_group=n_group),
        grid=(T // TILE, Y_COLS // tn),
        in_specs=[pl.BlockSpec((TILE, LANES), functools.partial(lambda i, j, s: (i, s), s=s))
                  for s in range(n_slab)] + [
            pl.BlockSpec((1, D_MODEL), lambda i, j: (0, 0)),
            pl.BlockSpec((D_MODEL, tn), lambda i, j: (0, j)),
            pl.BlockSpec((GLA_RANK, D_MODEL), lambda i, j: (W_GATE_COL // GLA_RANK, 0)),
            pl.BlockSpec((LANES, GLA_QK_W), lambda i, j: (0, 0)),
            pl.BlockSpec((1, GLA_QK_W), lambda i, j: (0, 0)),
        ],
        out_specs=[
            pl.BlockSpec((TILE, tn), lambda i, j: (i, j)),
            pl.BlockSpec((TILE, GLA_QK_W), lambda i, j: (i, 0)),
        ],
        out_shape=[
            jax.ShapeDtypeStruct((T, Y_COLS), BF16),
            jax.ShapeDtypeStruct((T, GLA_QK_W), F32),
        ],
        scratch_shapes=[pltpu.VMEM((N_ATT_GROUPS, TILE, D_MODEL), BF16),
                        pltpu.VMEM((TILE, LANES), F32),
                        pltpu.VMEM((TILE, LANES), F32)],
        compiler_params=_params(("parallel", "arbitrary")),
        name="inproj",
    )(*([x2] * n_slab), ln1, w_main, w_in_t, wa2, ba)


def _split3(g):
    hi = g.astype(BF16)
    r1 = g - hi.astype(F32)
    mid = r1.astype(BF16)
    lo = (r1 - mid.astype(F32)).astype(BF16)
    return hi, mid, lo


def _gla_kernel(q_ref, k_ref, v_ref, r_ref, la_ref, gn_ref, o_ref, st_ref, *, nchunk):
    C = GLA_CHUNK

    @pl.when(pl.program_id(1) == 0)
    def _():
        st_ref[...] = jnp.zeros_like(st_ref)

    row = lax.broadcasted_iota(jnp.int32, (C, C), 0)
    col = lax.broadcasted_iota(jnp.int32, (C, C), 1)
    causal = row >= col
    tri = causal.astype(BF16)

    H = range(GLA_HEADS)
    ck = [slice(h * GLA_DK, (h + 1) * GLA_DK) for h in H]
    cv = [slice(h * GLA_DV, (h + 1) * GLA_DV) for h in H]
    for c in range(nchunk):
        sl = pl.ds(c * C, C)
        hi, mid, lo = _split3(la_ref[0, sl, :])
        b = _dot(tri, hi) + _dot(tri, mid) + _dot(tri, lo)
        decay = jnp.exp(b[C - 1:C, :])
        q_dec = (q_ref[0, sl, :].astype(F32) * jnp.exp(b)).astype(BF16)
        k_in = k_ref[0, sl, :].astype(F32) * jnp.exp(-b)
        k_out = (k_in * decay).astype(BF16)
        k_in = k_in.astype(BF16)
        v = [v_ref[0, sl, cv[h]] for h in H]
        attn = [_dot_nt(q_dec[:, ck[h]], k_in[:, ck[h]]) for h in H]
        st = [st_ref[h] for h in H]
        o_inter = [_dot_nt(q_dec[:, ck[h]], st[h].astype(BF16)) for h in H]
        upd = [_dot_tn(v[h], k_out[:, ck[h]]) for h in H]
        o = [_dot(jnp.where(causal, attn[h], 0.0).astype(BF16), v[h]) + o_inter[h] for h in H]
        for h in H:
            st_ref[h] = st[h] * decay[:, ck[h]] + upd[h]
        for h in H:
            r = r_ref[0, sl, cv[h]].astype(F32)
            o_ref[0, sl, cv[h]] = (_rms(o[h], gn_ref[:, cv[h]]) * (r * jax.nn.sigmoid(r))).astype(BF16)


def _gla(y3, la3, gn, tb=512):
    B, S, _ = y3.shape
    return pl.pallas_call(
        functools.partial(_gla_kernel, nchunk=tb // GLA_CHUNK),
        grid=(B, S // tb),
        in_specs=[
            pl.BlockSpec((1, tb, GLA_QK_W), lambda b, t: (b, t, COL_Q // GLA_QK_W)),
            pl.BlockSpec((1, tb, GLA_QK_W), lambda b, t: (b, t, COL_K // GLA_QK_W)),
            pl.BlockSpec((1, tb, GLA_V_W), lambda b, t: (b, t, COL_V // GLA_V_W)),
            pl.BlockSpec((1, tb, GLA_V_W), lambda b, t: (b, t, COL_R // GLA_V_W)),
            pl.BlockSpec((1, tb, GLA_QK_W), lambda b, t: (b, t, 0)),
            pl.BlockSpec((1, GLA_V_W), lambda b, t: (0, 0)),
        ],
        out_specs=pl.BlockSpec((1, tb, GLA_V_W), lambda b, t: (b, t, 0)),
        out_shape=jax.ShapeDtypeStruct((B, S, GLA_V_W), BF16),
        scratch_shapes=[pltpu.VMEM((GLA_HEADS, GLA_DV, GLA_DK), F32)],
        compiler_params=_params(("parallel", "arbitrary")),
        name="gla",
    )(y3, y3, y3, y3, la3, gn)


def _t5_causal_bucket(n):
    max_exact = REL_BUCKETS // 2
    nf = np.maximum(n, 1).astype(np.float32)
    large = max_exact + (np.log(nf / max_exact) / np.log(REL_MAX_DIST / max_exact)
                         * (REL_BUCKETS - max_exact)).astype(np.int32)
    large = np.minimum(large, REL_BUCKETS - 1)
    return np.where(n < max_exact, n, large).astype(np.int32)


def _bucket_table():
    qi = np.arange(ATT_BLOCK)[:, None]
    kj = np.arange(2 * ATT_BLOCK)[None, :]
    delta = qi + ATT_BLOCK - kj
    out = []
    for win, dil in ATT_GROUPS:
        band = (delta >= 0) & (delta <= win // dil)
        bucket = np.where(band, _t5_causal_bucket(np.maximum(delta, 0) * dil), -1)
        out.append(bucket)
    return np.stack(out).astype(np.int32)


def _att_unit(q, k, v, bias):
    BLK = ATT_BLOCK
    s = _dot_nt(q, k) + bias
    m = jnp.max(jnp.maximum(s[:, :BLK], s[:, BLK:]), axis=-1, keepdims=True)
    p = jnp.exp(s - m).astype(BF16)
    v_ext = jnp.concatenate([v, jnp.ones_like(v)], axis=1)
    pv = _dot(p, v_ext)
    denom = pv[:, BLK:]
    return pv[:, :BLK] / denom, m + jnp.log(denom)


def _att_kernel(tab_ref, bucket_ref,
                q0, k0, v0, kp0, vp0, q1, k1, v1, kp1, vp1, q2, k2, v2, kp2, vp2,
                o_ref, bias_scr, o_scr, lse_scr, *, tiles_per_batch):
    BLK = ATT_BLOCK
    ti = pl.program_id(0)
    h = pl.program_id(1)

    @pl.when((ti == 0) & (h == 0))
    def _():
        for g in range(N_ATT_GROUPS):
            bucket = bucket_ref[g]
            for hh in range(ATT_HEADS_PER_GROUP):
                bias = jnp.full(bucket.shape, NEG_INF, F32)
                for b in range(REL_BUCKETS):
                    bias = jnp.where(bucket == b, tab_ref[b, g * ATT_HEADS_PER_GROUP + hh], bias)
                bias_scr[g, hh] = bias

    lane = lax.broadcasted_iota(jnp.int32, (BLK, 2 * BLK), 1)
    no_prev = jnp.logical_and((ti % tiles_per_batch) == 0, lane < BLK)
    groups = ((q0, k0, v0, kp0, vp0), (q1, k1, v1, kp1, vp1), (q2, k2, v2, kp2, vp2))
    for g, (q, k, v, kp, vp) in enumerate(groups):
        dil = ATT_DILS[g]
        n_r = TILE // dil
        kp_rows = kp.shape[0]
        bias = bias_scr[g, h]
        bias_first = jnp.where(no_prev, NEG_INF, bias)
        for r in range(dil):
            for j in range(n_r // BLK):
                cur = pl.ds(r * n_r + j * BLK, BLK)
                if j > 0:
                    both = pl.ds(r * n_r + (j - 1) * BLK, 2 * BLK)
                    k_all, v_all, b = k[both, :], v[both, :], bias
                else:
                    prev = pl.ds((r * n_r + n_r - BLK) % kp_rows, BLK)
                    k_all = jnp.concatenate([kp[prev, :], k[cur, :]], axis=0)
                    v_all = jnp.concatenate([vp[prev, :], v[cur, :]], axis=0)
                    b = bias_first
                if dil == 1:
                    rows = pl.ds(j * BLK, BLK)
                else:
                    rows = pl.ds(dil * j * BLK + r, BLK, stride=dil)
                o_scr[g, rows, :], lse_scr[g, rows, :] = _att_unit(q[cur, :], k_all, v_all, b)

    for c in range(TILE // 256):
        rows = pl.ds(c * 256, 256)
        lse = [lse_scr[g, rows, :] for g in range(N_ATT_GROUPS)]
        top = functools.reduce(jnp.maximum, lse)
        w = [jnp.exp(x - top) for x in lse]
        num = functools.reduce(lambda a, b: a + b, [w[g] * o_scr[g, rows, :] for g in range(N_ATT_GROUPS)])
        den = functools.reduce(lambda a, b: a + b, w)
        o_ref[rows, :] = (num / den).astype(BF16)


def _attention(y2, rel_bias, S):
    T = y2.shape[0]
    tiles_per_batch = S // TILE
    hd = ATT_HEAD_DIM

    def col(g, c):
        return (COL_ATT + g * ATT_GROUP_W + c * ATT_W) // hd

    def cur(g, c):
        return pl.BlockSpec((TILE, hd), lambda t, h: (t, col(g, c) + h))

    def prev_tile(g, c):
        return pl.BlockSpec((TILE, hd), lambda t, h: (jnp.maximum(t - 1, 0), col(g, c) + h))

    def prev_block(g, c):
        nb = TILE // ATT_BLOCK
        return pl.BlockSpec((ATT_BLOCK, hd), lambda t, h: (jnp.maximum(t * nb - 1, 0), col(g, c) + h))

    in_specs = [pl.BlockSpec(memory_space=pltpu.SMEM),
                pl.BlockSpec((N_ATT_GROUPS, ATT_BLOCK, 2 * ATT_BLOCK), lambda t, h: (0, 0, 0))]
    for g in range(N_ATT_GROUPS):
        prv = prev_block if ATT_DILS[g] == 1 else prev_tile
        in_specs += [cur(g, 0), cur(g, 1), cur(g, 2), prv(g, 1), prv(g, 2)]
    return pl.pallas_call(
        functools.partial(_att_kernel, tiles_per_batch=tiles_per_batch),
        grid=(T // TILE, ATT_HEADS_PER_GROUP),
        in_specs=in_specs,
        out_specs=pl.BlockSpec((TILE, hd), lambda t, h: (t, h)),
        out_shape=jax.ShapeDtypeStruct((T, ATT_W), BF16),
        scratch_shapes=[
            pltpu.VMEM((N_ATT_GROUPS, ATT_HEADS_PER_GROUP, ATT_BLOCK, 2 * ATT_BLOCK), F32),
            pltpu.VMEM((N_ATT_GROUPS, TILE, hd), F32),
            pltpu.VMEM((N_ATT_GROUPS, TILE, LANES), F32),
        ],
        compiler_params=_params(("arbitrary", "arbitrary")),
        name="attention",
    )(rel_bias, jnp.asarray(_bucket_table()), *([y2] * (5 * N_ATT_GROUPS)))


def _tail_kernel(x_ref, og_ref, oa_ref, ga_ref, gb_ref, p_ref, wog_ref, woa_ref, wout_ref,
                 ln2_ref, w1_ref, w2_ref, ln3_ref, wpg_ref, wpp_ref, lnf_ref, out_ref, *, tf):
    y_gla = _dot(og_ref[...], wog_ref[...])
    y_att = _dot(oa_ref[...], woa_ref[...])
    mix = (jax.nn.sigmoid(ga_ref[...].astype(F32)) * y_gla
           + jax.nn.sigmoid(gb_ref[...].astype(F32)) * y_att)
    x1 = x_ref[...] + _dot(mix.astype(BF16), wout_ref[...])

    h = _rms(x1, ln2_ref[...]).astype(BF16)
    x2 = x1
    for f in range(D_FF // tf):
        cols = pl.ds(f * tf, tf)
        u = jnp.maximum(_dot(h, w1_ref[:, cols]), 0.0)
        x2 = x2 + _dot((u * u).astype(BF16), w2_ref[cols, :])

    h3 = _rms(x2, ln3_ref[...]).astype(BF16)
    gate = jax.nn.sigmoid(_dot(h3, wpg_ref[...]))
    x3 = x2 + gate * _dot(p_ref[...].astype(BF16), wpp_ref[...])
    out_ref[...] = _rms(x3, lnf_ref[...])


def _tail(x2, y2, o_gla, o_att, p2, wog, woa, wout, ln2, w1, w2, ln3, wpg, wpp, lnf, tm=512, tf=1024):
    T = x2.shape[0]
    gcol = COL_GATE // D_MODEL
    row = lambda w: pl.BlockSpec((tm, w), lambda i: (i, 0))
    const = lambda a: pl.BlockSpec(a.shape, lambda i: (0, 0), pipeline_mode=pl.Buffered(1))
    return pl.pallas_call(
        functools.partial(_tail_kernel, tf=tf),
        grid=(T // tm,),
        in_specs=[row(D_MODEL), row(GLA_V_W), row(ATT_W),
                  pl.BlockSpec((tm, D_MODEL), lambda i: (i, gcol)),
                  pl.BlockSpec((tm, D_MODEL), lambda i: (i, gcol + 1)),
                  row(PLE_DIM),
                  const(wog), const(woa), const(wout), const(ln2), const(w1), const(w2),
                  const(ln3), const(wpg), const(wpp), const(lnf)],
        out_specs=row(D_MODEL),
        out_shape=jax.ShapeDtypeStruct((T, D_MODEL), F32),
        compiler_params=_params(("parallel",)),
        name="tail",
    )(x2, o_gla, o_att, y2, y2, p2, wog, woa, wout, ln2, w1, w2, ln3, wpg, wpp, lnf)


def _layer(x2, p2, B, S, ln1, w_in, w_a2, b_a, gla_gn, w_o_gla, w_o_attn, w_out,
           ln2, w_mlp1, w_mlp2, ln3, w_pp, w_pg, rel_bias, ln_out):
    wa2 = jnp.pad(w_a2, ((0, LANES - GLA_RANK), (0, 0)))

    w_in_t = w_in.T
    y2, la = _inproj(x2, ln1[None], _wprep(w_in_t), w_in_t, wa2, b_a[None])
    o_gla = _gla(y2.reshape(B, S, Y_COLS), la.reshape(B, S, GLA_QK_W), gla_gn[None])
    o_att = _attention(y2, rel_bias, S)
    return _tail(x2, y2, o_gla.reshape(B * S, GLA_V_W), o_att, p2,
                 w_o_gla.astype(BF16), w_o_attn.astype(BF16), w_out.astype(BF16), ln2[None],
                 w_mlp1.astype(BF16), w_mlp2.astype(BF16), ln3[None],
                 w_pg.astype(BF16), w_pp.astype(BF16), ln_out[None])


def kernel(x, p, ln1, w_in, w_a2, b_a, gla_gn, w_o_gla, w_o_attn, w_out, ln2, w_mlp1, w_mlp2,
           ln3, w_pp, w_pg, rel_bias, ln_f):
    B, S, D = x.shape
    assert p.shape[0] == 1, "the final norm is fused into the single layer's last kernel"
    assert S % TILE == 0
    x2 = x.reshape(B * S, D)
    out = _layer(x2, p[0].reshape(B * S, PLE_DIM), B, S, ln1[0], w_in[0], w_a2[0], b_a[0],
                 gla_gn[0], w_o_gla[0], w_o_attn[0], w_out[0], ln2[0], w_mlp1[0], w_mlp2[0],
                 ln3[0], w_pp[0], w_pg[0], rel_bias, ln_f)
    return out.reshape(B, S, D)
```

```python
import functools

import numpy as np
import jax
import jax.numpy as jnp
from jax import lax
from jax.experimental import pallas as pl
from jax.experimental.pallas import tpu as pltpu

F32 = jnp.float32
BF16 = jnp.bfloat16

D_MODEL = 1024
PLE_DIM = 256
EPS = 1e-6
GLA_HEADS = 4
GLA_DK = 128
GLA_DV = 256
GLA_RANK = 16
GLA_TAU = 16.0
GLA_CHUNK = 64
GLA_QK_W = GLA_HEADS * GLA_DK
GLA_V_W = GLA_HEADS * GLA_DV
ATT_GROUPS = ((128, 1), (512, 4), (2048, 16))
ATT_DILS = tuple(d for _, d in ATT_GROUPS)
ATT_HEADS_PER_GROUP = 4
ATT_HEAD_DIM = 128
N_ATT_GROUPS = len(ATT_GROUPS)
ATT_W = ATT_HEADS_PER_GROUP * ATT_HEAD_DIM
ATT_BLOCK = 128
REL_BUCKETS = 32
REL_MAX_DIST = 2048
D_FF = 4 * D_MODEL
NEG_INF = -1e30
LOG2_E = 1.4426950408889634

LANES = 128

COL_Q = 0
COL_K = COL_Q + GLA_QK_W
COL_V = COL_K + GLA_QK_W
COL_R = COL_V + GLA_V_W
COL_GATE = COL_R + GLA_V_W
COL_ATT = COL_GATE + 2 * D_MODEL
ATT_GROUP_W = 3 * ATT_W
Y_COLS = COL_ATT + N_ATT_GROUPS * ATT_GROUP_W

TILE = max(ATT_DILS) * ATT_BLOCK

VMEM_LIMIT = 56 * 1024 * 1024


def _params(semantics):
    return pltpu.CompilerParams(dimension_semantics=semantics, vmem_limit_bytes=VMEM_LIMIT)


def _rms(x, g):
    return x * lax.rsqrt(jnp.mean(x * x, axis=-1, keepdims=True) + EPS) * g


def _dot(a, b):
    return jnp.dot(a, b, preferred_element_type=F32)


def _dot_nt(a, b):
    return lax.dot_general(a, b, (((1,), (1,)), ((), ())), preferred_element_type=F32)


def _dot_tn(a, b):
    return lax.dot_general(a, b, (((0,), (0,)), ((), ())), preferred_element_type=F32)


def _split3(g):
    hi = g.astype(BF16)
    r1 = g - hi.astype(F32)
    mid = r1.astype(BF16)
    lo = (r1 - mid.astype(F32)).astype(BF16)
    return hi, mid, lo


W_GATE_COL = COL_R + GLA_V_W
W_ATT_COL = W_GATE_COL + GLA_RANK
W_MIXGATE_COL = W_ATT_COL + N_ATT_GROUPS * ATT_GROUP_W


def _wprep_kernel(a_ref, b_ref, scale_ref, o_ref, *, n_aligned, tn):
    t = pl.program_id(0)

    @pl.when(t < n_aligned)
    def _():
        o_ref[...] = (a_ref[...].T * scale_ref[...]).astype(BF16)

    @pl.when(t >= n_aligned)
    def _():
        w = jnp.concatenate([a_ref[...], b_ref[...]], axis=0)[GLA_RANK:GLA_RANK + tn]
        o_ref[...] = (w.T * scale_ref[...]).astype(BF16)


def _wprep(w_in_t, tn=512):
    n_aligned = W_GATE_COL // tn
    n_att = (W_MIXGATE_COL - W_ATT_COL) // tn

    def out_col(t):
        return jnp.where(t < n_aligned, t,
                         jnp.where(t < n_aligned + n_att, t - n_aligned + COL_ATT // tn,
                                   t - n_aligned - n_att + COL_GATE // tn))

    col_scale = np.ones((1, Y_COLS), np.float32)
    col_scale[:, COL_Q:COL_K] = GLA_DK ** -0.5
    for g in range(N_ATT_GROUPS):
        col_scale[:, COL_ATT + g * ATT_GROUP_W:COL_ATT + g * ATT_GROUP_W + ATT_W] = ATT_HEAD_DIM ** -0.5 * LOG2_E
    return pl.pallas_call(
        functools.partial(_wprep_kernel, n_aligned=n_aligned, tn=tn),
        grid=(Y_COLS // tn,),
        in_specs=[
            pl.BlockSpec((tn, D_MODEL), lambda t: (t, 0)),
            pl.BlockSpec((GLA_RANK, D_MODEL), lambda t: ((t + 1) * (tn // GLA_RANK), 0)),
            pl.BlockSpec((1, tn), lambda t: (0, out_col(t))),
        ],
        out_specs=pl.BlockSpec((D_MODEL, tn), lambda t: (0, out_col(t))),
        out_shape=jax.ShapeDtypeStruct((D_MODEL, Y_COLS), BF16),
        compiler_params=_params(("parallel",)),
        name="wprep",
    )(w_in_t, w_in_t, jnp.asarray(col_scale))


def _inproj_kernel(*refs, n_natural, n_group):
    n_slab = D_MODEL // LANES
    x_slabs = refs[:n_slab]
    ln_ref, w_ref, wa1_ref, y_ref, ha_ref, h_scr, inv_scr, tmp_scr = refs[n_slab:]
    j = pl.program_id(1)
    chunk = 256
    assert ATT_DILS == (1, 4, 16) and n_natural > 3

    def project(d):
        y_ref[...] = _dot(h_scr[d], w_ref[...]).astype(BF16)

    def normalise():
        for c in range(TILE // chunk):
            rows = pl.ds(c * chunk, chunk)
            parts = [xr[rows, :] for xr in x_slabs]
            sq = parts[0] * parts[0]
            for part in parts[1:]:
                sq = sq + part * part
            ms = jnp.sum(sq, axis=-1, keepdims=True) * (1.0 / D_MODEL)
            inv = jnp.broadcast_to(lax.rsqrt(ms + EPS), (chunk, LANES))
            inv_scr[rows, :] = inv
            for s, part in enumerate(parts):
                cols = pl.ds(s * LANES, LANES)
                h_scr[0, rows, cols] = (part * inv * ln_ref[:, cols]).astype(BF16)

    def regroup(slabs):
        n4, n16 = TILE // 4, TILE // 16
        for s in slabs:
            xr = x_slabs[s]
            cols = pl.ds(s * LANES, LANES)
            ln_s = ln_ref[:, cols]
            for a in range(4):
                for c in range(n4 // chunk):
                    src = pl.ds(a + 4 * c * chunk, chunk, stride=4)
                    dst = pl.ds(a * n4 + c * chunk, chunk)
                    hn = xr[src, :] * inv_scr[src, :] * ln_s
                    tmp_scr[dst, :] = hn
                    h_scr[1, dst, cols] = hn.astype(BF16)
            for a in range(4):
                for b in range(4):
                    src = pl.ds(a * n4 + b, n16, stride=4)
                    h_scr[2, pl.ds((a + 4 * b) * n16, n16), cols] = tmp_scr[src, :].astype(BF16)

    def gate_input():
        wa1_t = jnp.concatenate([wa1_ref[...], jnp.zeros((LANES - GLA_RANK, D_MODEL), F32)], axis=0)
        ha_ref[...] = _dot_nt(h_scr[0], wa1_t.astype(BF16))

    @pl.when(j == 0)
    def _():
        normalise()
        project(0)

    @pl.when(j == 1)
    def _():
        regroup(range(0, n_slab // 2))
        project(0)

    @pl.when(j == 2)
    def _():
        regroup(range(n_slab // 2, n_slab))
        project(0)

    @pl.when(j == 3)
    def _():
        gate_input()
        project(0)

    @pl.when(j > 3)
    def _():
        project((j >= n_natural).astype(jnp.int32) + (j >= n_natural + n_group).astype(jnp.int32))


def _inproj(x2, ln1, w_main, w_in_t, tn=512):
    T = x2.shape[0]
    n_natural = (COL_ATT + ATT_GROUP_W) // tn
    n_group = ATT_GROUP_W // tn
    n_slab = D_MODEL // LANES
    return pl.pallas_call(
        functools.partial(_inproj_kernel, n_natural=n_natural, n_group=n_group),
        grid=(T // TILE, Y_COLS // tn),
        in_specs=[pl.BlockSpec((TILE, LANES), functools.partial(lambda i, j, s: (i, s), s=s))
                  for s in range(n_slab)] + [
            pl.BlockSpec((1, D_MODEL), lambda i, j: (0, 0)),
            pl.BlockSpec((D_MODEL, tn), lambda i, j: (0, j)),
            pl.BlockSpec((GLA_RANK, D_MODEL), lambda i, j: (W_GATE_COL // GLA_RANK, 0)),
        ],
        out_specs=[
            pl.BlockSpec((TILE, tn), lambda i, j: (i, j)),
            pl.BlockSpec((TILE, LANES), lambda i, j: (i, 0)),
        ],
        out_shape=[
            jax.ShapeDtypeStruct((T, Y_COLS), BF16),
            jax.ShapeDtypeStruct((T, LANES), F32),
        ],
        scratch_shapes=[pltpu.VMEM((N_ATT_GROUPS, TILE, D_MODEL), BF16),
                        pltpu.VMEM((TILE, LANES), F32),
                        pltpu.VMEM((TILE, LANES), F32)],
        compiler_params=_params(("parallel", "arbitrary")),
        name="inproj",
    )(*([x2] * n_slab), ln1, w_main, w_in_t)


def _gla_kernel(q_ref, k_ref, v_ref, r_ref, ha_ref, wa2_ref, ba_ref, gn_ref, o_ref, st_ref, la_scr,
                *, nchunk):
    C = GLA_CHUNK

    @pl.when(pl.program_id(1) == 0)
    def _():
        st_ref[...] = jnp.zeros_like(st_ref)

    row = lax.broadcasted_iota(jnp.int32, (C, C), 0)
    col = lax.broadcasted_iota(jnp.int32, (C, C), 1)
    causal = row >= col
    tri = causal.astype(BF16)

    a_hi, a_mid, _ = _split3(ha_ref[0])
    w_hi, w_mid, _ = _split3(wa2_ref[...])
    z = _dot(a_hi, w_hi) + (_dot(a_mid, w_hi) + _dot(a_hi, w_mid)) + ba_ref[...]
    la_scr[...] = (jnp.minimum(z, 0.0) - jnp.log(1.0 + jnp.exp(-jnp.abs(z)))) * (1.0 / GLA_TAU)

    H = range(GLA_HEADS)
    ck = [slice(h * GLA_DK, (h + 1) * GLA_DK) for h in H]
    cv = [slice(h * GLA_DV, (h + 1) * GLA_DV) for h in H]
    for c in range(nchunk):
        sl = pl.ds(c * C, C)
        hi, mid, lo = _split3(la_scr[sl, :])
        b = _dot(tri, hi) + _dot(tri, mid) + _dot(tri, lo)
        decay = jnp.exp(b[C - 1:C, :])
        q_dec = (q_ref[0, sl, :].astype(F32) * jnp.exp(b)).astype(BF16)
        k_in = k_ref[0, sl, :].astype(F32) * jnp.exp(-b)
        k_out = (k_in * decay).astype(BF16)
        k_in = k_in.astype(BF16)
        v = [v_ref[0, sl, cv[h]] for h in H]
        attn = [_dot_nt(q_dec[:, ck[h]], k_in[:, ck[h]]) for h in H]
        st = [st_ref[h] for h in H]
        o_inter = [_dot_nt(q_dec[:, ck[h]], st[h].astype(BF16)) for h in H]
        upd = [_dot_tn(v[h], k_out[:, ck[h]]) for h in H]
        o = [_dot(jnp.where(causal, attn[h], 0.0).astype(BF16), v[h]) + o_inter[h] for h in H]
        for h in H:
            st_ref[h] = st[h] * decay[:, ck[h]] + upd[h]
        for h in H:
            r = r_ref[0, sl, cv[h]].astype(F32)
            o_ref[0, sl, cv[h]] = (_rms(o[h], gn_ref[:, cv[h]]) * (r * jax.nn.sigmoid(r))).astype(BF16)


def _gla(y3, ha3, wa2, ba, gn, tb=512):
    B, S, _ = y3.shape
    return pl.pallas_call(
        functools.partial(_gla_kernel, nchunk=tb // GLA_CHUNK),
        grid=(B, S // tb),
        in_specs=[
            pl.BlockSpec((1, tb, GLA_QK_W), lambda b, t: (b, t, COL_Q // GLA_QK_W)),
            pl.BlockSpec((1, tb, GLA_QK_W), lambda b, t: (b, t, COL_K // GLA_QK_W)),
            pl.BlockSpec((1, tb, GLA_V_W), lambda b, t: (b, t, COL_V // GLA_V_W)),
            pl.BlockSpec((1, tb, GLA_V_W), lambda b, t: (b, t, COL_R // GLA_V_W)),
            pl.BlockSpec((1, tb, LANES), lambda b, t: (b, t, 0)),
            pl.BlockSpec((LANES, GLA_QK_W), lambda b, t: (0, 0)),
            pl.BlockSpec((1, GLA_QK_W), lambda b, t: (0, 0)),
            pl.BlockSpec((1, GLA_V_W), lambda b, t: (0, 0)),
        ],
        out_specs=pl.BlockSpec((1, tb, GLA_V_W), lambda b, t: (b, t, 0)),
        out_shape=jax.ShapeDtypeStruct((B, S, GLA_V_W), BF16),
        scratch_shapes=[pltpu.VMEM((GLA_HEADS, GLA_DV, GLA_DK), F32), pltpu.VMEM((tb, GLA_QK_W), F32)],
        compiler_params=_params(("parallel", "arbitrary")),
        name="gla",
    )(y3, y3, y3, y3, ha3, wa2, ba, gn)


def _t5_causal_bucket(n):
    max_exact = REL_BUCKETS // 2
    nf = np.maximum(n, 1).astype(np.float32)
    large = max_exact + (np.log(nf / max_exact) / np.log(REL_MAX_DIST / max_exact)
                         * (REL_BUCKETS - max_exact)).astype(np.int32)
    large = np.minimum(large, REL_BUCKETS - 1)
    return np.where(n < max_exact, n, large).astype(np.int32)


def _bucket_table():
    qi = np.arange(ATT_BLOCK)[:, None]
    kj = np.arange(2 * ATT_BLOCK)[None, :]
    delta = qi + ATT_BLOCK - kj
    out = []
    for win, dil in ATT_GROUPS:
        band = (delta >= 0) & (delta <= win // dil)
        bucket = np.where(band, _t5_causal_bucket(np.maximum(delta, 0) * dil), -1)
        out.append(bucket)
    return np.stack(out).astype(np.int32)


def _att_unit(q, k, v, bias):
    BLK = ATT_BLOCK
    s = _dot_nt(q, k) + bias
    m = jnp.max(jnp.maximum(s[:, :BLK], s[:, BLK:]), axis=-1, keepdims=True)
    p = jnp.exp2((s - m).astype(BF16))
    v_ext = jnp.concatenate([v, jnp.ones_like(v)], axis=1)
    pv = _dot(p, v_ext)
    denom = pv[:, BLK:]
    return pv[:, :BLK] / denom, m + jnp.log2(denom)


def _att_kernel(tab_ref, bucket_ref,
                q0, k0, v0, kp0, vp0, q1, k1, v1, kp1, vp1, q2, k2, v2, kp2, vp2,
                o_ref, bias_scr, o_scr, lse_scr, *, tiles_per_batch):
    BLK = ATT_BLOCK
    ti = pl.program_id(0)
    h = pl.program_id(1)

    @pl.when((ti == 0) & (h == 0))
    def _():
        for g in range(N_ATT_GROUPS):
            bucket = bucket_ref[g]
            for hh in range(ATT_HEADS_PER_GROUP):
                bias = jnp.full(bucket.shape, NEG_INF, F32)
                for b in range(REL_BUCKETS):
                    bias = jnp.where(bucket == b, tab_ref[b, g * ATT_HEADS_PER_GROUP + hh] * LOG2_E, bias)
                bias_scr[g, hh] = bias

    lane = lax.broadcasted_iota(jnp.int32, (BLK, 2 * BLK), 1)
    no_prev = jnp.logical_and((ti % tiles_per_batch) == 0, lane < BLK)
    groups = ((q0, k0, v0, kp0, vp0), (q1, k1, v1, kp1, vp1), (q2, k2, v2, kp2, vp2))
    for g, (q, k, v, kp, vp) in enumerate(groups):
        dil = ATT_DILS[g]
        n_r = TILE // dil
        kp_rows = kp.shape[0]
        bias = bias_scr[g, h]
        bias_first = jnp.where(no_prev, NEG_INF, bias)
        for r in range(dil):
            for j in range(n_r // BLK):
                cur = pl.ds(r * n_r + j * BLK, BLK)
                if j > 0:
                    both = pl.ds(r * n_r + (j - 1) * BLK, 2 * BLK)
                    k_all, v_all, b = k[both, :], v[both, :], bias
                else:
                    prev = pl.ds((r * n_r + n_r - BLK) % kp_rows, BLK)
                    k_all = jnp.concatenate([kp[prev, :], k[cur, :]], axis=0)
                    v_all = jnp.concatenate([vp[prev, :], v[cur, :]], axis=0)
                    b = bias_first
                if dil == 1:
                    rows = pl.ds(j * BLK, BLK)
                else:
                    rows = pl.ds(dil * j * BLK + r, BLK, stride=dil)
                o_scr[g, rows, :], lse_scr[g, rows, :] = _att_unit(q[cur, :], k_all, v_all, b)

    for c in range(TILE // 256):
        rows = pl.ds(c * 256, 256)
        lse = [lse_scr[g, rows, :] for g in range(N_ATT_GROUPS)]
        top = functools.reduce(jnp.maximum, lse)
        w = [jnp.exp2(x - top) for x in lse]
        num = functools.reduce(lambda a, b: a + b, [w[g] * o_scr[g, rows, :] for g in range(N_ATT_GROUPS)])
        den = functools.reduce(lambda a, b: a + b, w)
        o_ref[rows, :] = (num / den).astype(BF16)


def _attention(y2, rel_bias, S):
    T = y2.shape[0]
    tiles_per_batch = S // TILE
    hd = ATT_HEAD_DIM

    def col(g, c):
        return (COL_ATT + g * ATT_GROUP_W + c * ATT_W) // hd

    def cur(g, c):
        return pl.BlockSpec((TILE, hd), lambda t, h: (t, col(g, c) + h))

    def prev_tile(g, c):
        return pl.BlockSpec((TILE, hd), lambda t, h: (jnp.maximum(t - 1, 0), col(g, c) + h))

    def prev_block(g, c):
        nb = TILE // ATT_BLOCK
        return pl.BlockSpec((ATT_BLOCK, hd), lambda t, h: (jnp.maximum(t * nb - 1, 0), col(g, c) + h))

    in_specs = [pl.BlockSpec(memory_space=pltpu.SMEM),
                pl.BlockSpec((N_ATT_GROUPS, ATT_BLOCK, 2 * ATT_BLOCK), lambda t, h: (0, 0, 0))]
    for g in range(N_ATT_GROUPS):
        prv = prev_block if ATT_DILS[g] == 1 else prev_tile
        in_specs += [cur(g, 0), cur(g, 1), cur(g, 2), prv(g, 1), prv(g, 2)]
    return pl.pallas_call(
        functools.partial(_att_kernel, tiles_per_batch=tiles_per_batch),
        grid=(T // TILE, ATT_HEADS_PER_GROUP),
        in_specs=in_specs,
        out_specs=pl.BlockSpec((TILE, hd), lambda t, h: (t, h)),
        out_shape=jax.ShapeDtypeStruct((T, ATT_W), BF16),
        scratch_shapes=[
            pltpu.VMEM((N_ATT_GROUPS, ATT_HEADS_PER_GROUP, ATT_BLOCK, 2 * ATT_BLOCK), F32),
            pltpu.VMEM((N_ATT_GROUPS, TILE, hd), F32),
            pltpu.VMEM((N_ATT_GROUPS, TILE, LANES), F32),
        ],
        compiler_params=_params(("arbitrary", "arbitrary")),
        name="attention",
    )(rel_bias, jnp.asarray(_bucket_table()), *([y2] * (5 * N_ATT_GROUPS)))


def _tail_kernel(x_ref, og_ref, oa_ref, ga_ref, gb_ref, p_ref, wog_ref, woa_ref, wout_ref,
                 ln2_ref, w1_ref, w2_ref, ln3_ref, wpg_ref, wpp_ref, lnf_ref, out_ref, *, tf):
    y_gla = _dot(og_ref[...], wog_ref[...])
    y_att = _dot(oa_ref[...], woa_ref[...])
    mix = (jax.nn.sigmoid(ga_ref[...].astype(F32)) * y_gla
           + jax.nn.sigmoid(gb_ref[...].astype(F32)) * y_att)
    x1 = x_ref[...] + _dot(mix.astype(BF16), wout_ref[...])

    h = _rms(x1, ln2_ref[...]).astype(BF16)
    x2 = x1
    for f in range(D_FF // tf):
        cols = pl.ds(f * tf, tf)
        u = jnp.maximum(_dot(h, w1_ref[:, cols]), 0.0)
        x2 = x2 + _dot((u * u).astype(BF16), w2_ref[cols, :])

    h3 = _rms(x2, ln3_ref[...]).astype(BF16)
    gate = jax.nn.sigmoid(_dot(h3, wpg_ref[...]))
    x3 = x2 + gate * _dot(p_ref[...].astype(BF16), wpp_ref[...])
    out_ref[...] = _rms(x3, lnf_ref[...])


def _tail(x2, y2, o_gla, o_att, p2, wog, woa, wout, ln2, w1, w2, ln3, wpg, wpp, lnf, tm=512, tf=1024):
    T = x2.shape[0]
    gcol = COL_GATE // D_MODEL
    row = lambda w: pl.BlockSpec((tm, w), lambda i: (i, 0))
    const = lambda a: pl.BlockSpec(a.shape, lambda i: (0, 0), pipeline_mode=pl.Buffered(1))
    return pl.pallas_call(
        functools.partial(_tail_kernel, tf=tf),
        grid=(T // tm,),
        in_specs=[row(D_MODEL), row(GLA_V_W), row(ATT_W),
                  pl.BlockSpec((tm, D_MODEL), lambda i: (i, gcol)),
                  pl.BlockSpec((tm, D_MODEL), lambda i: (i, gcol + 1)),
                  row(PLE_DIM),
                  const(wog), const(woa), const(wout), const(ln2), const(w1), const(w2),
                  const(ln3), const(wpg), const(wpp), const(lnf)],
        out_specs=row(D_MODEL),
        out_shape=jax.ShapeDtypeStruct((T, D_MODEL), F32),
        compiler_params=_params(("parallel",)),
        name="tail",
    )(x2, o_gla, o_att, y2, y2, p2, wog, woa, wout, ln2, w1, w2, ln3, wpg, wpp, lnf)


def _layer(x2, p2, B, S, ln1, w_in, w_a2, b_a, gla_gn, w_o_gla, w_o_attn, w_out,
           ln2, w_mlp1, w_mlp2, ln3, w_pp, w_pg, rel_bias, ln_out):
    wa2 = jnp.pad(w_a2, ((0, LANES - GLA_RANK), (0, 0)))

    w_in_t = w_in.T
    y2, ha = _inproj(x2, ln1[None], _wprep(w_in_t), w_in_t)
    o_gla = _gla(y2.reshape(B, S, Y_COLS), ha.reshape(B, S, LANES), wa2, b_a[None], gla_gn[None])
    o_att = _attention(y2, rel_bias, S)
    return _tail(x2, y2, o_gla.reshape(B * S, GLA_V_W), o_att, p2,
                 w_o_gla.astype(BF16), w_o_attn.astype(BF16), w_out.astype(BF16), ln2[None],
                 w_mlp1.astype(BF16), w_mlp2.astype(BF16), ln3[None],
                 w_pg.astype(BF16), w_pp.astype(BF16), ln_out[None])


def kernel(x, p, ln1, w_in, w_a2, b_a, gla_gn, w_o_gla, w_o_attn, w_out, ln2, w_mlp1, w_mlp2,
           ln3, w_pp, w_pg, rel_bias, ln_f):
    B, S, D = x.shape
    assert p.shape[0] == 1, "the final norm is fused into the single layer's last kernel"
    assert S % TILE == 0
    x2 = x.reshape(B * S, D)
    out = _layer(x2, p[0].reshape(B * S, PLE_DIM), B, S, ln1[0], w_in[0], w_a2[0], b_a[0],
                 gla_gn[0], w_o_gla[0], w_o_attn[0], w_out[0], ln2[0], w_mlp1[0], w_mlp2[0],
                 ln3[0], w_pp[0], w_pg[0], rel_bias, ln_f)
    return out.reshape(B, S, D)
```

```python
import functools

import numpy as np
import jax
import jax.numpy as jnp
from jax import lax
from jax.experimental import pallas as pl
from jax.experimental.pallas import tpu as pltpu

F32 = jnp.float32
BF16 = jnp.bfloat16

D_MODEL = 1024
PLE_DIM = 256
EPS = 1e-6
GLA_HEADS = 4
GLA_DK = 128
GLA_DV = 256
GLA_RANK = 16
GLA_TAU = 16.0
GLA_CHUNK = 64
GLA_QK_W = GLA_HEADS * GLA_DK
GLA_V_W = GLA_HEADS * GLA_DV
ATT_GROUPS = ((128, 1), (512, 4), (2048, 16))
ATT_DILS = tuple(d for _, d in ATT_GROUPS)
ATT_HEADS_PER_GROUP = 4
ATT_HEAD_DIM = 128
N_ATT_GROUPS = len(ATT_GROUPS)
ATT_W = ATT_HEADS_PER_GROUP * ATT_HEAD_DIM
ATT_BLOCK = 128
REL_BUCKETS = 32
REL_MAX_DIST = 2048
D_FF = 4 * D_MODEL
NEG_INF = -1e30
LOG2_E = 1.4426950408889634

LANES = 128

COL_Q = 0
COL_K = COL_Q + GLA_QK_W
COL_V = COL_K + GLA_QK_W
COL_R = COL_V + GLA_V_W
COL_GATE = COL_R + GLA_V_W
COL_ATT = COL_GATE + 2 * D_MODEL
ATT_GROUP_W = 3 * ATT_W
Y_COLS = COL_ATT + N_ATT_GROUPS * ATT_GROUP_W

TILE = max(ATT_DILS) * ATT_BLOCK

VMEM_LIMIT = 56 * 1024 * 1024


def _params(semantics):
    return pltpu.CompilerParams(dimension_semantics=semantics, vmem_limit_bytes=VMEM_LIMIT)


def _rms(x, g):
    return x * lax.rsqrt(jnp.mean(x * x, axis=-1, keepdims=True) + EPS) * g


def _dot(a, b):
    return jnp.dot(a, b, preferred_element_type=F32)


def _dot_nt(a, b):
    return lax.dot_general(a, b, (((1,), (1,)), ((), ())), preferred_element_type=F32)


def _dot_tn(a, b):
    return lax.dot_general(a, b, (((0,), (0,)), ((), ())), preferred_element_type=F32)


def _split3(g):
    hi = g.astype(BF16)
    r1 = g - hi.astype(F32)
    mid = r1.astype(BF16)
    lo = (r1 - mid.astype(F32)).astype(BF16)
    return hi, mid, lo


W_GATE_COL = COL_R + GLA_V_W
W_ATT_COL = W_GATE_COL + GLA_RANK
W_MIXGATE_COL = W_ATT_COL + N_ATT_GROUPS * ATT_GROUP_W


def _wprep_kernel(a_ref, b_ref, scale_ref, o_ref, *, n_aligned, tn):
    t = pl.program_id(0)

    @pl.when(t < n_aligned)
    def _():
        o_ref[...] = (a_ref[...].T * scale_ref[...]).astype(BF16)

    @pl.when(t >= n_aligned)
    def _():
        w = jnp.concatenate([a_ref[...], b_ref[...]], axis=0)[GLA_RANK:GLA_RANK + tn]
        o_ref[...] = (w.T * scale_ref[...]).astype(BF16)


def _wprep(w_in_t, tn=512):
    n_aligned = W_GATE_COL // tn
    n_att = (W_MIXGATE_COL - W_ATT_COL) // tn

    def out_col(t):
        return jnp.where(t < n_aligned, t,
                         jnp.where(t < n_aligned + n_att, t - n_aligned + COL_ATT // tn,
                                   t - n_aligned - n_att + COL_GATE // tn))

    col_scale = np.ones((1, Y_COLS), np.float32)
    col_scale[:, COL_Q:COL_K] = GLA_DK ** -0.5
    for g in range(N_ATT_GROUPS):
        col_scale[:, COL_ATT + g * ATT_GROUP_W:COL_ATT + g * ATT_GROUP_W + ATT_W] = ATT_HEAD_DIM ** -0.5 * LOG2_E
    return pl.pallas_call(
        functools.partial(_wprep_kernel, n_aligned=n_aligned, tn=tn),
        grid=(Y_COLS // tn,),
        in_specs=[
            pl.BlockSpec((tn, D_MODEL), lambda t: (t, 0)),
            pl.BlockSpec((GLA_RANK, D_MODEL), lambda t: ((t + 1) * (tn // GLA_RANK), 0)),
            pl.BlockSpec((1, tn), lambda t: (0, out_col(t))),
        ],
        out_specs=pl.BlockSpec((D_MODEL, tn), lambda t: (0, out_col(t))),
        out_shape=jax.ShapeDtypeStruct((D_MODEL, Y_COLS), BF16),
        compiler_params=_params(("parallel",)),
        name="wprep",
    )(w_in_t, w_in_t, jnp.asarray(col_scale))


def _inproj_kernel(*refs, n_natural, n_group):
    n_slab = D_MODEL // LANES
    x_ref, ln_ref, w_ref, wa1_ref, y_ref, ha_ref, h_scr, inv_scr, tmp_scr, slab_scr = refs
    j = pl.program_id(1)
    chunk = 256
    assert ATT_DILS == (1, 4, 16) and n_natural > 3

    def project(d):
        y_ref[...] = _dot(h_scr[d], w_ref[...]).astype(BF16)

    def normalise():
        for c in range(TILE // chunk):
            rows = pl.ds(c * chunk, chunk)
            parts = [x_ref[rows, pl.ds(s * LANES, LANES)] for s in range(n_slab)]
            for s, part in enumerate(parts):
                slab_scr[s, rows, :] = part
            sq = parts[0] * parts[0]
            for part in parts[1:]:
                sq = sq + part * part
            ms = jnp.sum(sq, axis=-1, keepdims=True) * (1.0 / D_MODEL)
            inv = jnp.broadcast_to(lax.rsqrt(ms + EPS), (chunk, LANES))
            inv_scr[rows, :] = inv
            for s, part in enumerate(parts):
                cols = pl.ds(s * LANES, LANES)
                h_scr[0, rows, cols] = (part * inv * ln_ref[:, cols]).astype(BF16)

    def regroup(slabs):
        n4, n16 = TILE // 4, TILE // 16
        for s in slabs:
            xr = slab_scr.at[s]
            cols = pl.ds(s * LANES, LANES)
            ln_s = ln_ref[:, cols]
            for a in range(4):
                for c in range(n4 // chunk):
                    src = pl.ds(a + 4 * c * chunk, chunk, stride=4)
                    dst = pl.ds(a * n4 + c * chunk, chunk)
                    hn = xr[src, :] * inv_scr[src, :] * ln_s
                    tmp_scr[dst, :] = hn
                    h_scr[1, dst, cols] = hn.astype(BF16)
            for a in range(4):
                for b in range(4):
                    src = pl.ds(a * n4 + b, n16, stride=4)
                    h_scr[2, pl.ds((a + 4 * b) * n16, n16), cols] = tmp_scr[src, :].astype(BF16)

    def gate_input():
        wa1_t = jnp.concatenate([wa1_ref[...], jnp.zeros((LANES - GLA_RANK, D_MODEL), F32)], axis=0)
        ha_ref[...] = _dot_nt(h_scr[0], wa1_t.astype(BF16))

    @pl.when(j == 0)
    def _():
        normalise()
        project(0)

    @pl.when(j == 1)
    def _():
        regroup(range(0, n_slab // 2))
        project(0)

    @pl.when(j == 2)
    def _():
        regroup(range(n_slab // 2, n_slab))
        project(0)

    @pl.when(j == 3)
    def _():
        gate_input()
        project(0)

    @pl.when(j > 3)
    def _():
        project((j >= n_natural).astype(jnp.int32) + (j >= n_natural + n_group).astype(jnp.int32))


def _inproj(x2, ln1, w_main, w_in_t, tn=512):
    T = x2.shape[0]
    n_natural = (COL_ATT + ATT_GROUP_W) // tn
    n_group = ATT_GROUP_W // tn
    n_slab = D_MODEL // LANES
    return pl.pallas_call(
        functools.partial(_inproj_kernel, n_natural=n_natural, n_group=n_group),
        grid=(T // TILE, Y_COLS // tn),
        in_specs=[
            pl.BlockSpec((TILE, D_MODEL), lambda i, j: (i, 0)),
            pl.BlockSpec((1, D_MODEL), lambda i, j: (0, 0)),
            pl.BlockSpec((D_MODEL, tn), lambda i, j: (0, j)),
            pl.BlockSpec((GLA_RANK, D_MODEL), lambda i, j: (W_GATE_COL // GLA_RANK, 0)),
        ],
        out_specs=[
            pl.BlockSpec((TILE, tn), lambda i, j: (i, j)),
            pl.BlockSpec((TILE, LANES), lambda i, j: (i, 0)),
        ],
        out_shape=[
            jax.ShapeDtypeStruct((T, Y_COLS), BF16),
            jax.ShapeDtypeStruct((T, LANES), F32),
        ],
        scratch_shapes=[pltpu.VMEM((N_ATT_GROUPS, TILE, D_MODEL), BF16),
                        pltpu.VMEM((TILE, LANES), F32),
                        pltpu.VMEM((TILE, LANES), F32),
                        pltpu.VMEM((n_slab, TILE, LANES), F32)],
        compiler_params=_params(("parallel", "arbitrary")),
        name="inproj",
    )(x2, ln1, w_main, w_in_t)


def _gla_kernel(q_ref, k_ref, v_ref, r_ref, ha_ref, wa2_ref, ba_ref, gn_ref, o_ref, st_ref, la_scr,
                *, nchunk):
    C = GLA_CHUNK

    @pl.when(pl.program_id(1) == 0)
    def _():
        st_ref[...] = jnp.zeros_like(st_ref)

    row = lax.broadcasted_iota(jnp.int32, (C, C), 0)
    col = lax.broadcasted_iota(jnp.int32, (C, C), 1)
    causal = row >= col
    tri = causal.astype(BF16)

    a_hi, a_mid, _ = _split3(ha_ref[0])
    w_hi, w_mid, _ = _split3(wa2_ref[...])
    z = _dot(a_hi, w_hi) + (_dot(a_mid, w_hi) + _dot(a_hi, w_mid)) + ba_ref[...]
    la_scr[...] = (jnp.minimum(z, 0.0) - jnp.log(1.0 + jnp.exp(-jnp.abs(z)))) * (1.0 / GLA_TAU)

    H = range(GLA_HEADS)
    ck = [slice(h * GLA_DK, (h + 1) * GLA_DK) for h in H]
    cv = [slice(h * GLA_DV, (h + 1) * GLA_DV) for h in H]
    for c in range(nchunk):
        sl = pl.ds(c * C, C)
        hi, mid, lo = _split3(la_scr[sl, :])
        b = _dot(tri, hi) + _dot(tri, mid) + _dot(tri, lo)
        decay = jnp.exp(b[C - 1:C, :])
        q_dec = (q_ref[0, sl, :].astype(F32) * jnp.exp(b)).astype(BF16)
        k_in = k_ref[0, sl, :].astype(F32) * jnp.exp(-b)
        k_out = (k_in * decay).astype(BF16)
        k_in = k_in.astype(BF16)
        v = [v_ref[0, sl, cv[h]] for h in H]
        attn = [_dot_nt(q_dec[:, ck[h]], k_in[:, ck[h]]) for h in H]
        st = [st_ref[h] for h in H]
        o_inter = [_dot_nt(q_dec[:, ck[h]], st[h].astype(BF16)) for h in H]
        upd = [_dot_tn(v[h], k_out[:, ck[h]]) for h in H]
        o = [_dot(jnp.where(causal, attn[h], 0.0).astype(BF16), v[h]) + o_inter[h] for h in H]
        for h in H:
            st_ref[h] = st[h] * decay[:, ck[h]] + upd[h]
        for h in H:
            r = r_ref[0, sl, cv[h]].astype(F32)
            o_ref[0, sl, cv[h]] = (_rms(o[h], gn_ref[:, cv[h]]) * (r * jax.nn.sigmoid(r))).astype(BF16)


def _gla(y3, ha3, wa2, ba, gn, tb=512):
    B, S, _ = y3.shape
    return pl.pallas_call(
        functools.partial(_gla_kernel, nchunk=tb // GLA_CHUNK),
        grid=(B, S // tb),
        in_specs=[
            pl.BlockSpec((1, tb, GLA_QK_W), lambda b, t: (b, t, COL_Q // GLA_QK_W)),
            pl.BlockSpec((1, tb, GLA_QK_W), lambda b, t: (b, t, COL_K // GLA_QK_W)),
            pl.BlockSpec((1, tb, GLA_V_W), lambda b, t: (b, t, COL_V // GLA_V_W)),
            pl.BlockSpec((1, tb, GLA_V_W), lambda b, t: (b, t, COL_R // GLA_V_W)),
            pl.BlockSpec((1, tb, LANES), lambda b, t: (b, t, 0)),
            pl.BlockSpec((LANES, GLA_QK_W), lambda b, t: (0, 0)),
            pl.BlockSpec((1, GLA_QK_W), lambda b, t: (0, 0)),
            pl.BlockSpec((1, GLA_V_W), lambda b, t: (0, 0)),
        ],
        out_specs=pl.BlockSpec((1, tb, GLA_V_W), lambda b, t: (b, t, 0)),
        out_shape=jax.ShapeDtypeStruct((B, S, GLA_V_W), BF16),
        scratch_shapes=[pltpu.VMEM((GLA_HEADS, GLA_DV, GLA_DK), F32), pltpu.VMEM((tb, GLA_QK_W), F32)],
        compiler_params=_params(("parallel", "arbitrary")),
        name="gla",
    )(y3, y3, y3, y3, ha3, wa2, ba, gn)


def _t5_causal_bucket(n):
    max_exact = REL_BUCKETS // 2
    nf = np.maximum(n, 1).astype(np.float32)
    large = max_exact + (np.log(nf / max_exact) / np.log(REL_MAX_DIST / max_exact)
                         * (REL_BUCKETS - max_exact)).astype(np.int32)
    large = np.minimum(large, REL_BUCKETS - 1)
    return np.where(n < max_exact, n, large).astype(np.int32)


def _bucket_table():
    qi = np.arange(ATT_BLOCK)[:, None]
    kj = np.arange(2 * ATT_BLOCK)[None, :]
    delta = qi + ATT_BLOCK - kj
    out = []
    for win, dil in ATT_GROUPS:
        band = (delta >= 0) & (delta <= win // dil)
        bucket = np.where(band, _t5_causal_bucket(np.maximum(delta, 0) * dil), -1)
        out.append(bucket)
    return np.stack(out).astype(np.int32)


def _att_unit(q, k, v, bias):
    BLK = ATT_BLOCK
    s = _dot_nt(q, k) + bias
    m = jnp.max(jnp.maximum(s[:, :BLK], s[:, BLK:]), axis=-1, keepdims=True)
    p = jnp.exp2((s - m).astype(BF16))
    v_ext = jnp.concatenate([v, jnp.ones_like(v)], axis=1)
    pv = _dot(p, v_ext)
    denom = pv[:, BLK:]
    return pv[:, :BLK] / denom, m + jnp.log2(denom)


def _att_kernel(tab_ref, bucket_ref,
                q0, k0, v0, kp0, vp0, q1, k1, v1, kp1, vp1, q2, k2, v2, kp2, vp2,
                o_ref, bias_scr, o_scr, lse_scr, *, tiles_per_batch):
    BLK = ATT_BLOCK
    ti = pl.program_id(0)
    h = pl.program_id(1)

    @pl.when((ti == 0) & (h == 0))
    def _():
        for g in range(N_ATT_GROUPS):
            bucket = bucket_ref[g]
            for hh in range(ATT_HEADS_PER_GROUP):
                bias = jnp.full(bucket.shape, NEG_INF, F32)
                for b in range(REL_BUCKETS):
                    bias = jnp.where(bucket == b, tab_ref[b, g * ATT_HEADS_PER_GROUP + hh] * LOG2_E, bias)
                bias_scr[g, hh] = bias

    lane = lax.broadcasted_iota(jnp.int32, (BLK, 2 * BLK), 1)
    no_prev = jnp.logical_and((ti % tiles_per_batch) == 0, lane < BLK)
    groups = ((q0, k0, v0, kp0, vp0), (q1, k1, v1, kp1, vp1), (q2, k2, v2, kp2, vp2))
    for g, (q, k, v, kp, vp) in enumerate(groups):
        dil = ATT_DILS[g]
        n_r = TILE // dil
        kp_rows = kp.shape[0]
        bias = bias_scr[g, h]
        bias_first = jnp.where(no_prev, NEG_INF, bias)
        for r in range(dil):
            for j in range(n_r // BLK):
                cur = pl.ds(r * n_r + j * BLK, BLK)
                if j > 0:
                    both = pl.ds(r * n_r + (j - 1) * BLK, 2 * BLK)
                    k_all, v_all, b = k[both, :], v[both, :], bias
                else:
                    prev = pl.ds((r * n_r + n_r - BLK) % kp_rows, BLK)
                    k_all = jnp.concatenate([kp[prev, :], k[cur, :]], axis=0)
                    v_all = jnp.concatenate([vp[prev, :], v[cur, :]], axis=0)
                    b = bias_first
                if dil == 1:
                    rows = pl.ds(j * BLK, BLK)
                else:
                    rows = pl.ds(dil * j * BLK + r, BLK, stride=dil)
                o_scr[g, rows, :], lse_scr[g, rows, :] = _att_unit(q[cur, :], k_all, v_all, b)

    for c in range(TILE // 256):
        rows = pl.ds(c * 256, 256)
        lse = [lse_scr[g, rows, :] for g in range(N_ATT_GROUPS)]
        top = functools.reduce(jnp.maximum, lse)
        w = [jnp.exp2(x - top) for x in lse]
        num = functools.reduce(lambda a, b: a + b, [w[g] * o_scr[g, rows, :] for g in range(N_ATT_GROUPS)])
        den = functools.reduce(lambda a, b: a + b, w)
        o_ref[rows, :] = (num / den).astype(BF16)


def _attention(y2, rel_bias, S):
    T = y2.shape[0]
    tiles_per_batch = S // TILE
    hd = ATT_HEAD_DIM

    def col(g, c):
        return (COL_ATT + g * ATT_GROUP_W + c * ATT_W) // hd

    def cur(g, c):
        return pl.BlockSpec((TILE, hd), lambda t, h: (t, col(g, c) + h))

    def prev_tile(g, c):
        return pl.BlockSpec((TILE, hd), lambda t, h: (jnp.maximum(t - 1, 0), col(g, c) + h))

    def prev_block(g, c):
        nb = TILE // ATT_BLOCK
        return pl.BlockSpec((ATT_BLOCK, hd), lambda t, h: (jnp.maximum(t * nb - 1, 0), col(g, c) + h))

    in_specs = [pl.BlockSpec(memory_space=pltpu.SMEM),
                pl.BlockSpec((N_ATT_GROUPS, ATT_BLOCK, 2 * ATT_BLOCK), lambda t, h: (0, 0, 0))]
    for g in range(N_ATT_GROUPS):
        prv = prev_block if ATT_DILS[g] == 1 else prev_tile
        in_specs += [cur(g, 0), cur(g, 1), cur(g, 2), prv(g, 1), prv(g, 2)]
    return pl.pallas_call(
        functools.partial(_att_kernel, tiles_per_batch=tiles_per_batch),
        grid=(T // TILE, ATT_HEADS_PER_GROUP),
        in_specs=in_specs,
        out_specs=pl.BlockSpec((TILE, hd), lambda t, h: (t, h)),
        out_shape=jax.ShapeDtypeStruct((T, ATT_W), BF16),
        scratch_shapes=[
            pltpu.VMEM((N_ATT_GROUPS, ATT_HEADS_PER_GROUP, ATT_BLOCK, 2 * ATT_BLOCK), F32),
            pltpu.VMEM((N_ATT_GROUPS, TILE, hd), F32),
            pltpu.VMEM((N_ATT_GROUPS, TILE, LANES), F32),
        ],
        compiler_params=_params(("arbitrary", "arbitrary")),
        name="attention",
    )(rel_bias, jnp.asarray(_bucket_table()), *([y2] * (5 * N_ATT_GROUPS)))


def _tail_kernel(x_ref, og_ref, oa_ref, ga_ref, gb_ref, p_ref, wog_ref, woa_ref, wout_ref,
                 ln2_ref, w1_ref, w2_ref, ln3_ref, wpg_ref, wpp_ref, lnf_ref, out_ref, *, tf):
    y_gla = _dot(og_ref[...], wog_ref[...])
    y_att = _dot(oa_ref[...], woa_ref[...])
    mix = (jax.nn.sigmoid(ga_ref[...].astype(F32)) * y_gla
           + jax.nn.sigmoid(gb_ref[...].astype(F32)) * y_att)
    x1 = x_ref[...] + _dot(mix.astype(BF16), wout_ref[...])

    h = _rms(x1, ln2_ref[...]).astype(BF16)
    x2 = x1
    for f in range(D_FF // tf):
        cols = pl.ds(f * tf, tf)
        u = jnp.maximum(_dot(h, w1_ref[:, cols]), 0.0)
        x2 = x2 + _dot((u * u).astype(BF16), w2_ref[cols, :])

    h3 = _rms(x2, ln3_ref[...]).astype(BF16)
    gate = jax.nn.sigmoid(_dot(h3, wpg_ref[...]))
    x3 = x2 + gate * _dot(p_ref[...].astype(BF16), wpp_ref[...])
    out_ref[...] = _rms(x3, lnf_ref[...])


def _tail(x2, y2, o_gla, o_att, p2, wog, woa, wout, ln2, w1, w2, ln3, wpg, wpp, lnf, tm=512, tf=1024):
    T = x2.shape[0]
    gcol = COL_GATE // D_MODEL
    row = lambda w: pl.BlockSpec((tm, w), lambda i: (i, 0))
    const = lambda a: pl.BlockSpec(a.shape, lambda i: (0, 0), pipeline_mode=pl.Buffered(1))
    return pl.pallas_call(
        functools.partial(_tail_kernel, tf=tf),
        grid=(T // tm,),
        in_specs=[row(D_MODEL), row(GLA_V_W), row(ATT_W),
                  pl.BlockSpec((tm, D_MODEL), lambda i: (i, gcol)),
                  pl.BlockSpec((tm, D_MODEL), lambda i: (i, gcol + 1)),
                  row(PLE_DIM),
                  const(wog), const(woa), const(wout), const(ln2), const(w1), const(w2),
                  const(ln3), const(wpg), const(wpp), const(lnf)],
        out_specs=row(D_MODEL),
        out_shape=jax.ShapeDtypeStruct((T, D_MODEL), F32),
        compiler_params=_params(("parallel",)),
        name="tail",
    )(x2, o_gla, o_att, y2, y2, p2, wog, woa, wout, ln2, w1, w2, ln3, wpg, wpp, lnf)


def _layer(x2, p2, B, S, ln1, w_in, w_a2, b_a, gla_gn, w_o_gla, w_o_attn, w_out,
           ln2, w_mlp1, w_mlp2, ln3, w_pp, w_pg, rel_bias, ln_out):
    wa2 = jnp.pad(w_a2, ((0, LANES - GLA_RANK), (0, 0)))

    w_in_t = w_in.T
    y2, ha = _inproj(x2, ln1[None], _wprep(w_in_t), w_in_t)
    o_gla = _gla(y2.reshape(B, S, Y_COLS), ha.reshape(B, S, LANES), wa2, b_a[None], gla_gn[None])
    o_att = _attention(y2, rel_bias, S)
    return _tail(x2, y2, o_gla.reshape(B * S, GLA_V_W), o_att, p2,
                 w_o_gla.astype(BF16), w_o_attn.astype(BF16), w_out.astype(BF16), ln2[None],
                 w_mlp1.astype(BF16), w_mlp2.astype(BF16), ln3[None],
                 w_pg.astype(BF16), w_pp.astype(BF16), ln_out[None])


def kernel(x, p, ln1, w_in, w_a2, b_a, gla_gn, w_o_gla, w_o_attn, w_out, ln2, w_mlp1, w_mlp2,
           ln3, w_pp, w_pg, rel_bias, ln_f):
    B, S, D = x.shape
    assert p.shape[0] == 1, "the final norm is fused into the single layer's last kernel"
    assert S % TILE == 0
    x2 = x.reshape(B * S, D)
    out = _layer(x2, p[0].reshape(B * S, PLE_DIM), B, S, ln1[0], w_in[0], w_a2[0], b_a[0],
                 gla_gn[0], w_o_gla[0], w_o_attn[0], w_out[0], ln2[0], w_mlp1[0], w_mlp2[0],
                 ln3[0], w_pp[0], w_pg[0], rel_bias, ln_f)
    return out.reshape(B, S, D)
```

```python
import functools

import numpy as np
import jax
import jax.numpy as jnp
from jax import lax
from jax.experimental import pallas as pl
from jax.experimental.pallas import tpu as pltpu

F32 = jnp.float32
BF16 = jnp.bfloat16

D_MODEL = 1024
PLE_DIM = 256
EPS = 1e-6
GLA_HEADS = 4
GLA_DK = 128
GLA_DV = 256
GLA_RANK = 16
GLA_TAU = 16.0
GLA_CHUNK = 64
GLA_QK_W = GLA_HEADS * GLA_DK
GLA_V_W = GLA_HEADS * GLA_DV
ATT_GROUPS = ((128, 1), (512, 4), (2048, 16))
ATT_DILS = tuple(d for _, d in ATT_GROUPS)
ATT_HEADS_PER_GROUP = 4
ATT_HEAD_DIM = 128
N_ATT_GROUPS = len(ATT_GROUPS)
ATT_W = ATT_HEADS_PER_GROUP * ATT_HEAD_DIM
ATT_BLOCK = 128
REL_BUCKETS = 32
REL_MAX_DIST = 2048
D_FF = 4 * D_MODEL
NEG_INF = -1e30
LOG2_E = 1.4426950408889634

LANES = 128

COL_Q = 0
COL_K = COL_Q + GLA_QK_W
COL_V = COL_K + GLA_QK_W
COL_R = COL_V + GLA_V_W
COL_GATE = COL_R + GLA_V_W
COL_ATT = COL_GATE + 2 * D_MODEL
ATT_GROUP_W = 3 * ATT_W
Y_COLS = COL_ATT + N_ATT_GROUPS * ATT_GROUP_W

TILE = max(ATT_DILS) * ATT_BLOCK

VMEM_LIMIT = 56 * 1024 * 1024


def _params(semantics):
    return pltpu.CompilerParams(dimension_semantics=semantics, vmem_limit_bytes=VMEM_LIMIT)


def _rms(x, g):
    return x * lax.rsqrt(jnp.mean(x * x, axis=-1, keepdims=True) + EPS) * g


def _dot(a, b):
    return jnp.dot(a, b, preferred_element_type=F32)


def _dot_nt(a, b):
    return lax.dot_general(a, b, (((1,), (1,)), ((), ())), preferred_element_type=F32)


def _dot_tn(a, b):
    return lax.dot_general(a, b, (((0,), (0,)), ((), ())), preferred_element_type=F32)


def _split3(g):
    hi = g.astype(BF16)
    r1 = g - hi.astype(F32)
    mid = r1.astype(BF16)
    lo = (r1 - mid.astype(F32)).astype(BF16)
    return hi, mid, lo


W_GATE_COL = COL_R + GLA_V_W
W_ATT_COL = W_GATE_COL + GLA_RANK
W_MIXGATE_COL = W_ATT_COL + N_ATT_GROUPS * ATT_GROUP_W


def _wprep_kernel(a_ref, b_ref, scale_ref, o_ref, *, n_aligned, tn):
    t = pl.program_id(0)

    @pl.when(t < n_aligned)
    def _():
        o_ref[...] = (a_ref[...].T * scale_ref[...]).astype(BF16)

    @pl.when(t >= n_aligned)
    def _():
        w = jnp.concatenate([a_ref[...], b_ref[...]], axis=0)[GLA_RANK:GLA_RANK + tn]
        o_ref[...] = (w.T * scale_ref[...]).astype(BF16)


def _wprep(w_in_t, tn=512):
    n_aligned = W_GATE_COL // tn
    n_att = (W_MIXGATE_COL - W_ATT_COL) // tn

    def out_col(t):
        return jnp.where(t < n_aligned, t,
                         jnp.where(t < n_aligned + n_att, t - n_aligned + COL_ATT // tn,
                                   t - n_aligned - n_att + COL_GATE // tn))

    col_scale = np.ones((1, Y_COLS), np.float32)
    col_scale[:, COL_Q:COL_K] = GLA_DK ** -0.5
    for g in range(N_ATT_GROUPS):
        col_scale[:, COL_ATT + g * ATT_GROUP_W:COL_ATT + g * ATT_GROUP_W + ATT_W] = ATT_HEAD_DIM ** -0.5 * LOG2_E
    return pl.pallas_call(
        functools.partial(_wprep_kernel, n_aligned=n_aligned, tn=tn),
        grid=(Y_COLS // tn,),
        in_specs=[
            pl.BlockSpec((tn, D_MODEL), lambda t: (t, 0)),
            pl.BlockSpec((GLA_RANK, D_MODEL), lambda t: ((t + 1) * (tn // GLA_RANK), 0)),
            pl.BlockSpec((1, tn), lambda t: (0, out_col(t))),
        ],
        out_specs=pl.BlockSpec((D_MODEL, tn), lambda t: (0, out_col(t))),
        out_shape=jax.ShapeDtypeStruct((D_MODEL, Y_COLS), BF16),
        compiler_params=_params(("parallel",)),
        name="wprep",
    )(w_in_t, w_in_t, jnp.asarray(col_scale))


def _inproj_kernel(*refs, n_natural, n_group):
    n_slab = D_MODEL // LANES
    x_ref, ln_ref, w_ref, wa1_ref, y_ref, ha_ref, h_scr, inv_scr, tmp_scr, slab_scr = refs
    j = pl.program_id(1)
    chunk = 256
    assert ATT_DILS == (1, 4, 16) and n_natural > 3

    def project(d):
        y_ref[...] = _dot(h_scr[d], w_ref[...]).astype(BF16)

    def normalise():
        for c in range(TILE // chunk):
            rows = pl.ds(c * chunk, chunk)
            parts = [x_ref[rows, pl.ds(s * LANES, LANES)] for s in range(n_slab)]
            for s, part in enumerate(parts):
                slab_scr[s, rows, :] = part
            sq = parts[0] * parts[0]
            for part in parts[1:]:
                sq = sq + part * part
            ms = jnp.sum(sq, axis=-1, keepdims=True) * (1.0 / D_MODEL)
            inv = jnp.broadcast_to(lax.rsqrt(ms + EPS), (chunk, LANES))
            inv_scr[rows, :] = inv
            for s, part in enumerate(parts):
                cols = pl.ds(s * LANES, LANES)
                h_scr[0, rows, cols] = (part * inv * ln_ref[:, cols]).astype(BF16)

    def regroup(slabs):
        n4, n16 = TILE // 4, TILE // 16
        for s in slabs:
            xr = slab_scr.at[s]
            cols = pl.ds(s * LANES, LANES)
            ln_s = ln_ref[:, cols]
            for a in range(4):
                for c in range(n4 // chunk):
                    src = pl.ds(a + 4 * c * chunk, chunk, stride=4)
                    dst = pl.ds(a * n4 + c * chunk, chunk)
                    hn = xr[src, :] * inv_scr[src, :] * ln_s
                    tmp_scr[dst, :] = hn
                    h_scr[1, dst, cols] = hn.astype(BF16)
            for a in range(4):
                for b in range(4):
                    src = pl.ds(a * n4 + b, n16, stride=4)
                    h_scr[2, pl.ds((a + 4 * b) * n16, n16), cols] = tmp_scr[src, :].astype(BF16)

    def gate_input():
        wa1_t = jnp.concatenate([wa1_ref[...], jnp.zeros((LANES - GLA_RANK, D_MODEL), F32)], axis=0)
        ha_ref[...] = _dot_nt(h_scr[0], wa1_t.astype(BF16))

    @pl.when(j == 0)
    def _():
        normalise()
        project(0)

    @pl.when(j == 1)
    def _():
        regroup(range(0, n_slab // 2))
        project(0)

    @pl.when(j == 2)
    def _():
        regroup(range(n_slab // 2, n_slab))
        project(0)

    @pl.when(j == 3)
    def _():
        gate_input()
        project(0)

    @pl.when(j > 3)
    def _():
        project((j >= n_natural).astype(jnp.int32) + (j >= n_natural + n_group).astype(jnp.int32))


def _inproj(x2, ln1, w_main, w_in_t, tn=256):
    T = x2.shape[0]
    n_natural = (COL_ATT + ATT_GROUP_W) // tn
    n_group = ATT_GROUP_W // tn
    n_slab = D_MODEL // LANES
    return pl.pallas_call(
        functools.partial(_inproj_kernel, n_natural=n_natural, n_group=n_group),
        grid=(T // TILE, Y_COLS // tn),
        in_specs=[
            pl.BlockSpec((TILE, D_MODEL), lambda i, j: (i, 0)),
            pl.BlockSpec((1, D_MODEL), lambda i, j: (0, 0)),
            pl.BlockSpec((D_MODEL, tn), lambda i, j: (0, j)),
            pl.BlockSpec((GLA_RANK, D_MODEL), lambda i, j: (W_GATE_COL // GLA_RANK, 0)),
        ],
        out_specs=[
            pl.BlockSpec((TILE, tn), lambda i, j: (i, j)),
            pl.BlockSpec((TILE, LANES), lambda i, j: (i, 0)),
        ],
        out_shape=[
            jax.ShapeDtypeStruct((T, Y_COLS), BF16),
            jax.ShapeDtypeStruct((T, LANES), F32),
        ],
        scratch_shapes=[pltpu.VMEM((N_ATT_GROUPS, TILE, D_MODEL), BF16),
                        pltpu.VMEM((TILE, LANES), F32),
                        pltpu.VMEM((TILE, LANES), F32),
                        pltpu.VMEM((n_slab, TILE, LANES), F32)],
        compiler_params=_params(("parallel", "arbitrary")),
        name="inproj",
    )(x2, ln1, w_main, w_in_t)


def _gla_kernel(q_ref, k_ref, v_ref, r_ref, ha_ref, wa2_ref, ba_ref, gn_ref, o_ref, st_ref, la_scr,
                *, nchunk):
    C = GLA_CHUNK

    @pl.when(pl.program_id(1) == 0)
    def _():
        st_ref[...] = jnp.zeros_like(st_ref)

    row = lax.broadcasted_iota(jnp.int32, (C, C), 0)
    col = lax.broadcasted_iota(jnp.int32, (C, C), 1)
    causal = row >= col
    tri = causal.astype(BF16)

    a_hi, a_mid, _ = _split3(ha_ref[0])
    w_hi, w_mid, _ = _split3(wa2_ref[...])
    z = _dot(a_hi, w_hi) + (_dot(a_mid, w_hi) + _dot(a_hi, w_mid)) + ba_ref[...]
    la_scr[...] = (jnp.minimum(z, 0.0) - jnp.log(1.0 + jnp.exp(-jnp.abs(z)))) * (1.0 / GLA_TAU)

    H = range(GLA_HEADS)
    ck = [slice(h * GLA_DK, (h + 1) * GLA_DK) for h in H]
    cv = [slice(h * GLA_DV, (h + 1) * GLA_DV) for h in H]
    for c in range(nchunk):
        sl = pl.ds(c * C, C)
        hi, mid, lo = _split3(la_scr[sl, :])
        b = _dot(tri, hi) + _dot(tri, mid) + _dot(tri, lo)
        decay = jnp.exp(b[C - 1:C, :])
        q_dec = (q_ref[0, sl, :].astype(F32) * jnp.exp(b)).astype(BF16)
        k_in = k_ref[0, sl, :].astype(F32) * jnp.exp(-b)
        k_out = (k_in * decay).astype(BF16)
        k_in = k_in.astype(BF16)
        v = [v_ref[0, sl, cv[h]] for h in H]
        attn = [_dot_nt(q_dec[:, ck[h]], k_in[:, ck[h]]) for h in H]
        st = [st_ref[h] for h in H]
        o_inter = [_dot_nt(q_dec[:, ck[h]], st[h].astype(BF16)) for h in H]
        upd = [_dot_tn(v[h], k_out[:, ck[h]]) for h in H]
        o = [_dot(jnp.where(causal, attn[h], 0.0).astype(BF16), v[h]) + o_inter[h] for h in H]
        for h in H:
            st_ref[h] = st[h] * decay[:, ck[h]] + upd[h]
        for h in H:
            r = r_ref[0, sl, cv[h]].astype(F32)
            o_ref[0, sl, cv[h]] = (_rms(o[h], gn_ref[:, cv[h]]) * (r * jax.nn.sigmoid(r))).astype(BF16)


def _gla(y3, ha3, wa2, ba, gn, tb=512):
    B, S, _ = y3.shape
    return pl.pallas_call(
        functools.partial(_gla_kernel, nchunk=tb // GLA_CHUNK),
        grid=(B, S // tb),
        in_specs=[
            pl.BlockSpec((1, tb, GLA_QK_W), lambda b, t: (b, t, COL_Q // GLA_QK_W)),
            pl.BlockSpec((1, tb, GLA_QK_W), lambda b, t: (b, t, COL_K // GLA_QK_W)),
            pl.BlockSpec((1, tb, GLA_V_W), lambda b, t: (b, t, COL_V // GLA_V_W)),
            pl.BlockSpec((1, tb, GLA_V_W), lambda b, t: (b, t, COL_R // GLA_V_W)),
            pl.BlockSpec((1, tb, LANES), lambda b, t: (b, t, 0)),
            pl.BlockSpec((LANES, GLA_QK_W), lambda b, t: (0, 0)),
            pl.BlockSpec((1, GLA_QK_W), lambda b, t: (0, 0)),
            pl.BlockSpec((1, GLA_V_W), lambda b, t: (0, 0)),
        ],
        out_specs=pl.BlockSpec((1, tb, GLA_V_W), lambda b, t: (b, t, 0)),
        out_shape=jax.ShapeDtypeStruct((B, S, GLA_V_W), BF16),
        scratch_shapes=[pltpu.VMEM((GLA_HEADS, GLA_DV, GLA_DK), F32), pltpu.VMEM((tb, GLA_QK_W), F32)],
        compiler_params=_params(("parallel", "arbitrary")),
        name="gla",
    )(y3, y3, y3, y3, ha3, wa2, ba, gn)


def _t5_causal_bucket(n):
    max_exact = REL_BUCKETS // 2
    nf = np.maximum(n, 1).astype(np.float32)
    large = max_exact + (np.log(nf / max_exact) / np.log(REL_MAX_DIST / max_exact)
                         * (REL_BUCKETS - max_exact)).astype(np.int32)
    large = np.minimum(large, REL_BUCKETS - 1)
    return np.where(n < max_exact, n, large).astype(np.int32)


def _bucket_table():
    qi = np.arange(ATT_BLOCK)[:, None]
    kj = np.arange(2 * ATT_BLOCK)[None, :]
    delta = qi + ATT_BLOCK - kj
    out = []
    for win, dil in ATT_GROUPS:
        band = (delta >= 0) & (delta <= win // dil)
        bucket = np.where(band, _t5_causal_bucket(np.maximum(delta, 0) * dil), -1)
        out.append(bucket)
    return np.stack(out).astype(np.int32)


def _att_unit(q, k, v, bias):
    BLK = ATT_BLOCK
    s = _dot_nt(q, k) + bias
    m = jnp.max(jnp.maximum(s[:, :BLK], s[:, BLK:]), axis=-1, keepdims=True)
    p = jnp.exp2((s - m).astype(BF16))
    v_ext = jnp.concatenate([v, jnp.ones_like(v)], axis=1)
    pv = _dot(p, v_ext)
    denom = pv[:, BLK:]
    return pv[:, :BLK] / denom, m + jnp.log2(denom)


def _att_kernel(tab_ref, bucket_ref,
                q0, k0, v0, kp0, vp0, q1, k1, v1, kp1, vp1, q2, k2, v2, kp2, vp2,
                o_ref, bias_scr, o_scr, lse_scr, *, tiles_per_batch):
    BLK = ATT_BLOCK
    ti = pl.program_id(0)
    h = pl.program_id(1)

    @pl.when((ti == 0) & (h == 0))
    def _():
        for g in range(N_ATT_GROUPS):
            bucket = bucket_ref[g]
            for hh in range(ATT_HEADS_PER_GROUP):
                bias = jnp.full(bucket.shape, NEG_INF, F32)
                for b in range(REL_BUCKETS):
                    bias = jnp.where(bucket == b, tab_ref[b, g * ATT_HEADS_PER_GROUP + hh] * LOG2_E, bias)
                bias_scr[g, hh] = bias

    lane = lax.broadcasted_iota(jnp.int32, (BLK, 2 * BLK), 1)
    no_prev = jnp.logical_and((ti % tiles_per_batch) == 0, lane < BLK)
    groups = ((q0, k0, v0, kp0, vp0), (q1, k1, v1, kp1, vp1), (q2, k2, v2, kp2, vp2))
    for g, (q, k, v, kp, vp) in enumerate(groups):
        dil = ATT_DILS[g]
        n_r = TILE // dil
        kp_rows = kp.shape[0]
        bias = bias_scr[g, h]
        bias_first = jnp.where(no_prev, NEG_INF, bias)
        for r in range(dil):
            for j in range(n_r // BLK):
                cur = pl.ds(r * n_r + j * BLK, BLK)
                if j > 0:
                    both = pl.ds(r * n_r + (j - 1) * BLK, 2 * BLK)
                    k_all, v_all, b = k[both, :], v[both, :], bias
                else:
                    prev = pl.ds((r * n_r + n_r - BLK) % kp_rows, BLK)
                    k_all = jnp.concatenate([kp[prev, :], k[cur, :]], axis=0)
                    v_all = jnp.concatenate([vp[prev, :], v[cur, :]], axis=0)
                    b = bias_first
                if dil == 1:
                    rows = pl.ds(j * BLK, BLK)
                else:
                    rows = pl.ds(dil * j * BLK + r, BLK, stride=dil)
                o_scr[g, rows, :], lse_scr[g, rows, :] = _att_unit(q[cur, :], k_all, v_all, b)

    for c in range(TILE // 256):
        rows = pl.ds(c * 256, 256)
        lse = [lse_scr[g, rows, :] for g in range(N_ATT_GROUPS)]
        top = functools.reduce(jnp.maximum, lse)
        w = [jnp.exp2(x - top) for x in lse]
        num = functools.reduce(lambda a, b: a + b, [w[g] * o_scr[g, rows, :] for g in range(N_ATT_GROUPS)])
        den = functools.reduce(lambda a, b: a + b, w)
        o_ref[rows, :] = (num / den).astype(BF16)


def _attention(y2, rel_bias, S):
    T = y2.shape[0]
    tiles_per_batch = S // TILE
    hd = ATT_HEAD_DIM

    def col(g, c):
        return (COL_ATT + g * ATT_GROUP_W + c * ATT_W) // hd

    def cur(g, c):
        return pl.BlockSpec((TILE, hd), lambda t, h: (t, col(g, c) + h))

    def prev_tile(g, c):
        return pl.BlockSpec((TILE, hd), lambda t, h: (jnp.maximum(t - 1, 0), col(g, c) + h))

    def prev_block(g, c):
        nb = TILE // ATT_BLOCK
        return pl.BlockSpec((ATT_BLOCK, hd), lambda t, h: (jnp.maximum(t * nb - 1, 0), col(g, c) + h))

    in_specs = [pl.BlockSpec(memory_space=pltpu.SMEM),
                pl.BlockSpec((N_ATT_GROUPS, ATT_BLOCK, 2 * ATT_BLOCK), lambda t, h: (0, 0, 0))]
    for g in range(N_ATT_GROUPS):
        prv = prev_block if ATT_DILS[g] == 1 else prev_tile
        in_specs += [cur(g, 0), cur(g, 1), cur(g, 2), prv(g, 1), prv(g, 2)]
    return pl.pallas_call(
        functools.partial(_att_kernel, tiles_per_batch=tiles_per_batch),
        grid=(T // TILE, ATT_HEADS_PER_GROUP),
        in_specs=in_specs,
        out_specs=pl.BlockSpec((TILE, hd), lambda t, h: (t, h)),
        out_shape=jax.ShapeDtypeStruct((T, ATT_W), BF16),
        scratch_shapes=[
            pltpu.VMEM((N_ATT_GROUPS, ATT_HEADS_PER_GROUP, ATT_BLOCK, 2 * ATT_BLOCK), F32),
            pltpu.VMEM((N_ATT_GROUPS, TILE, hd), F32),
            pltpu.VMEM((N_ATT_GROUPS, TILE, LANES), F32),
        ],
        compiler_params=_params(("arbitrary", "arbitrary")),
        name="attention",
    )(rel_bias, jnp.asarray(_bucket_table()), *([y2] * (5 * N_ATT_GROUPS)))


def _tail_kernel(x_ref, og_ref, oa_ref, ga_ref, gb_ref, p_ref, wog_ref, woa_ref, wout_ref,
                 ln2_ref, w1_ref, w2_ref, ln3_ref, wpg_ref, wpp_ref, lnf_ref, out_ref, *, tf):
    y_gla = _dot(og_ref[...], wog_ref[...])
    y_att = _dot(oa_ref[...], woa_ref[...])
    mix = (jax.nn.sigmoid(ga_ref[...].astype(F32)) * y_gla
           + jax.nn.sigmoid(gb_ref[...].astype(F32)) * y_att)
    x1 = x_ref[...] + _dot(mix.astype(BF16), wout_ref[...])

    h = _rms(x1, ln2_ref[...]).astype(BF16)
    x2 = x1
    for f in range(D_FF // tf):
        cols = pl.ds(f * tf, tf)
        u = jnp.maximum(_dot(h, w1_ref[:, cols]), 0.0)
        x2 = x2 + _dot((u * u).astype(BF16), w2_ref[cols, :])

    h3 = _rms(x2, ln3_ref[...]).astype(BF16)
    gate = jax.nn.sigmoid(_dot(h3, wpg_ref[...]))
    x3 = x2 + gate * _dot(p_ref[...].astype(BF16), wpp_ref[...])
    out_ref[...] = _rms(x3, lnf_ref[...])


def _tail(x2, y2, o_gla, o_att, p2, wog, woa, wout, ln2, w1, w2, ln3, wpg, wpp, lnf, tm=512, tf=1024):
    T = x2.shape[0]
    gcol = COL_GATE // D_MODEL
    row = lambda w: pl.BlockSpec((tm, w), lambda i: (i, 0))
    const = lambda a: pl.BlockSpec(a.shape, lambda i: (0, 0), pipeline_mode=pl.Buffered(1))
    return pl.pallas_call(
        functools.partial(_tail_kernel, tf=tf),
        grid=(T // tm,),
        in_specs=[row(D_MODEL), row(GLA_V_W), row(ATT_W),
                  pl.BlockSpec((tm, D_MODEL), lambda i: (i, gcol)),
                  pl.BlockSpec((tm, D_MODEL), lambda i: (i, gcol + 1)),
                  row(PLE_DIM),
                  const(wog), const(woa), const(wout), const(ln2), const(w1), const(w2),
                  const(ln3), const(wpg), const(wpp), const(lnf)],
        out_specs=row(D_MODEL),
        out_shape=jax.ShapeDtypeStruct((T, D_MODEL), F32),
        compiler_params=_params(("parallel",)),
        name="tail",
    )(x2, o_gla, o_att, y2, y2, p2, wog, woa, wout, ln2, w1, w2, ln3, wpg, wpp, lnf)


def _layer(x2, p2, B, S, ln1, w_in, w_a2, b_a, gla_gn, w_o_gla, w_o_attn, w_out,
           ln2, w_mlp1, w_mlp2, ln3, w_pp, w_pg, rel_bias, ln_out):
    wa2 = jnp.pad(w_a2, ((0, LANES - GLA_RANK), (0, 0)))

    w_in_t = w_in.T
    y2, ha = _inproj(x2, ln1[None], _wprep(w_in_t), w_in_t)
    o_gla = _gla(y2.reshape(B, S, Y_COLS), ha.reshape(B, S, LANES), wa2, b_a[None], gla_gn[None])
    o_att = _attention(y2, rel_bias, S)
    return _tail(x2, y2, o_gla.reshape(B * S, GLA_V_W), o_att, p2,
                 w_o_gla.astype(BF16), w_o_attn.astype(BF16), w_out.astype(BF16), ln2[None],
                 w_mlp1.astype(BF16), w_mlp2.astype(BF16), ln3[None],
                 w_pg.astype(BF16), w_pp.astype(BF16), ln_out[None])


def kernel(x, p, ln1, w_in, w_a2, b_a, gla_gn, w_o_gla, w_o_attn, w_out, ln2, w_mlp1, w_mlp2,
           ln3, w_pp, w_pg, rel_bias, ln_f):
    B, S, D = x.shape
    assert p.shape[0] == 1, "the final norm is fused into the single layer's last kernel"
    assert S % TILE == 0
    x2 = x.reshape(B * S, D)
    out = _layer(x2, p[0].reshape(B * S, PLE_DIM), B, S, ln1[0], w_in[0], w_a2[0], b_a[0],
                 gla_gn[0], w_o_gla[0], w_o_attn[0], w_out[0], ln2[0], w_mlp1[0], w_mlp2[0],
                 ln3[0], w_pp[0], w_pg[0], rel_bias, ln_f)
    return out.reshape(B, S, D)
```

```python
import functools

import numpy as np
import jax
import jax.numpy as jnp
from jax import lax
from jax.experimental import pallas as pl
from jax.experimental.pallas import tpu as pltpu

F32 = jnp.float32
BF16 = jnp.bfloat16

D_MODEL = 1024
PLE_DIM = 256
EPS = 1e-6
GLA_HEADS = 4
GLA_DK = 128
GLA_DV = 256
GLA_RANK = 16
GLA_TAU = 16.0
GLA_CHUNK = 64
GLA_QK_W = GLA_HEADS * GLA_DK
GLA_V_W = GLA_HEADS * GLA_DV
ATT_GROUPS = ((128, 1), (512, 4), (2048, 16))
ATT_DILS = tuple(d for _, d in ATT_GROUPS)
ATT_HEADS_PER_GROUP = 4
ATT_HEAD_DIM = 128
N_ATT_GROUPS = len(ATT_GROUPS)
ATT_W = ATT_HEADS_PER_GROUP * ATT_HEAD_DIM
ATT_BLOCK = 128
REL_BUCKETS = 32
REL_MAX_DIST = 2048
D_FF = 4 * D_MODEL
NEG_INF = -1e30
LOG2_E = 1.4426950408889634

LANES = 128

COL_Q = 0
COL_K = COL_Q + GLA_QK_W
COL_V = COL_K + GLA_QK_W
COL_R = COL_V + GLA_V_W
COL_GATE = COL_R + GLA_V_W
COL_ATT = COL_GATE + 2 * D_MODEL
ATT_GROUP_W = 3 * ATT_W
Y_COLS = COL_ATT + N_ATT_GROUPS * ATT_GROUP_W

TILE = max(ATT_DILS) * ATT_BLOCK

VMEM_LIMIT = 56 * 1024 * 1024


def _params(semantics):
    return pltpu.CompilerParams(dimension_semantics=semantics, vmem_limit_bytes=VMEM_LIMIT)


def _rms(x, g):
    return x * lax.rsqrt(jnp.mean(x * x, axis=-1, keepdims=True) + EPS) * g


def _dot(a, b):
    return jnp.dot(a, b, preferred_element_type=F32)


def _dot_nt(a, b):
    return lax.dot_general(a, b, (((1,), (1,)), ((), ())), preferred_element_type=F32)


def _dot_tn(a, b):
    return lax.dot_general(a, b, (((0,), (0,)), ((), ())), preferred_element_type=F32)


def _split3(g):
    hi = g.astype(BF16)
    r1 = g - hi.astype(F32)
    mid = r1.astype(BF16)
    lo = (r1 - mid.astype(F32)).astype(BF16)
    return hi, mid, lo


W_GATE_COL = COL_R + GLA_V_W
W_ATT_COL = W_GATE_COL + GLA_RANK
W_MIXGATE_COL = W_ATT_COL + N_ATT_GROUPS * ATT_GROUP_W


def _wprep_kernel(a_ref, b_ref, scale_ref, o_ref, *, n_aligned, tn):
    t = pl.program_id(0)

    @pl.when(t < n_aligned)
    def _():
        o_ref[...] = (a_ref[...].T * scale_ref[...]).astype(BF16)

    @pl.when(t >= n_aligned)
    def _():
        w = jnp.concatenate([a_ref[...], b_ref[...]], axis=0)[GLA_RANK:GLA_RANK + tn]
        o_ref[...] = (w.T * scale_ref[...]).astype(BF16)


def _wprep(w_in_t, tn=512):
    n_aligned = W_GATE_COL // tn
    n_att = (W_MIXGATE_COL - W_ATT_COL) // tn

    def out_col(t):
        return jnp.where(t < n_aligned, t,
                         jnp.where(t < n_aligned + n_att, t - n_aligned + COL_ATT // tn,
                                   t - n_aligned - n_att + COL_GATE // tn))

    col_scale = np.ones((1, Y_COLS), np.float32)
    col_scale[:, COL_Q:COL_K] = GLA_DK ** -0.5
    for g in range(N_ATT_GROUPS):
        col_scale[:, COL_ATT + g * ATT_GROUP_W:COL_ATT + g * ATT_GROUP_W + ATT_W] = ATT_HEAD_DIM ** -0.5 * LOG2_E
    return pl.pallas_call(
        functools.partial(_wprep_kernel, n_aligned=n_aligned, tn=tn),
        grid=(Y_COLS // tn,),
        in_specs=[
            pl.BlockSpec((tn, D_MODEL), lambda t: (t, 0)),
            pl.BlockSpec((GLA_RANK, D_MODEL), lambda t: ((t + 1) * (tn // GLA_RANK), 0)),
            pl.BlockSpec((1, tn), lambda t: (0, out_col(t))),
        ],
        out_specs=pl.BlockSpec((D_MODEL, tn), lambda t: (0, out_col(t))),
        out_shape=jax.ShapeDtypeStruct((D_MODEL, Y_COLS), BF16),
        compiler_params=_params(("parallel",)),
        name="wprep",
    )(w_in_t, w_in_t, jnp.asarray(col_scale))


def _inproj_kernel(*refs, tile_orders):
    n_slab = D_MODEL // LANES
    x_slabs = refs[:n_slab]
    ln_ref, w_ref, wa1_ref, y_ref, ha_ref, h_scr, inv_scr, tmp_scr = refs[n_slab:]
    j = pl.program_id(1)
    chunk = 256
    assert ATT_DILS == (1, 4, 16) and all(order == (0, 0) for order in tile_orders[:4])

    def project(d_left, d_right):
        half = y_ref.shape[1] // 2
        y_ref[:, :half] = _dot(h_scr[d_left], w_ref[:, :half]).astype(BF16)
        if d_right is not None:
            y_ref[:, half:] = _dot(h_scr[d_right], w_ref[:, half:]).astype(BF16)

    def normalise():
        for c in range(TILE // chunk):
            rows = pl.ds(c * chunk, chunk)
            parts = [xr[rows, :] for xr in x_slabs]
            sq = parts[0] * parts[0]
            for part in parts[1:]:
                sq = sq + part * part
            ms = jnp.sum(sq, axis=-1, keepdims=True) * (1.0 / D_MODEL)
            inv = jnp.broadcast_to(lax.rsqrt(ms + EPS), (chunk, LANES))
            inv_scr[rows, :] = inv
            for s, part in enumerate(parts):
                cols = pl.ds(s * LANES, LANES)
                h_scr[0, rows, cols] = (part * inv * ln_ref[:, cols]).astype(BF16)

    def regroup(slabs):
        n4, n16 = TILE // 4, TILE // 16
        for s in slabs:
            xr = x_slabs[s]
            cols = pl.ds(s * LANES, LANES)
            ln_s = ln_ref[:, cols]
            for a in range(4):
                for c in range(n4 // chunk):
                    src = pl.ds(a + 4 * c * chunk, chunk, stride=4)
                    dst = pl.ds(a * n4 + c * chunk, chunk)
                    hn = xr[src, :] * inv_scr[src, :] * ln_s
                    tmp_scr[dst, :] = hn
                    h_scr[1, dst, cols] = hn.astype(BF16)
            for a in range(4):
                for b in range(4):
                    src = pl.ds(a * n4 + b, n16, stride=4)
                    h_scr[2, pl.ds((a + 4 * b) * n16, n16), cols] = tmp_scr[src, :].astype(BF16)

    def gate_input():
        wa1_t = jnp.concatenate([wa1_ref[...], jnp.zeros((LANES - GLA_RANK, D_MODEL), F32)], axis=0)
        ha_ref[...] = _dot_nt(h_scr[0], wa1_t.astype(BF16))

    prep = {0: normalise, 1: functools.partial(regroup, range(0, n_slab // 2)),
            2: functools.partial(regroup, range(n_slab // 2, n_slab)), 3: gate_input}
    special = [t for t, (left, right) in enumerate(tile_orders) if t in prep or left != right]
    for t in special:
        @pl.when(j == t)
        def _(t=t):
            if t in prep:
                prep[t]()
            project(*tile_orders[t])

    @pl.when(functools.reduce(jnp.logical_and, [j != t for t in special]))
    def _():
        firsts = [min(t for t, o in enumerate(tile_orders) if o[0] == d) for d in (1, 2)]
        d = sum((j >= f).astype(jnp.int32) for f in firsts)
        project(d, d)


def _inproj(x2, ln1, w_main, w_in_t, tn=1024):
    T = x2.shape[0]
    n_slab = D_MODEL // LANES
    half = tn // 2
    bounds = [COL_ATT + ATT_GROUP_W * (g + 1) for g in range(N_ATT_GROUPS)]
    order_of = lambda col: None if col >= Y_COLS else sum(col >= b for b in bounds)
    n_tiles = pl.cdiv(Y_COLS, tn)
    tile_orders = tuple((order_of(t * tn), order_of(t * tn + half)) for t in range(n_tiles))
    assert all(b % half == 0 for b in bounds) and Y_COLS % half == 0
    return pl.pallas_call(
        functools.partial(_inproj_kernel, tile_orders=tile_orders),
        grid=(T // TILE, n_tiles),
        in_specs=[pl.BlockSpec((TILE, LANES), functools.partial(lambda i, j, s: (i, s), s=s))
                  for s in range(n_slab)] + [
            pl.BlockSpec((1, D_MODEL), lambda i, j: (0, 0)),
            pl.BlockSpec((D_MODEL, tn), lambda i, j: (0, j)),
            pl.BlockSpec((GLA_RANK, D_MODEL), lambda i, j: (W_GATE_COL // GLA_RANK, 0)),
        ],
        out_specs=[
            pl.BlockSpec((TILE, tn), lambda i, j: (i, j)),
            pl.BlockSpec((TILE, LANES), lambda i, j: (i, 0)),
        ],
        out_shape=[
            jax.ShapeDtypeStruct((T, Y_COLS), BF16),
            jax.ShapeDtypeStruct((T, LANES), F32),
        ],
        scratch_shapes=[pltpu.VMEM((N_ATT_GROUPS, TILE, D_MODEL), BF16),
                        pltpu.VMEM((TILE, LANES), F32),
                        pltpu.VMEM((TILE, LANES), F32)],
        compiler_params=_params(("parallel", "arbitrary")),
        name="inproj",
    )(*([x2] * n_slab), ln1, w_main, w_in_t)


def _gla_kernel(q_ref, k_ref, v_ref, r_ref, ha_ref, wa2_ref, ba_ref, gn_ref, o_ref, st_ref, la_scr,
                *, nchunk):
    C = GLA_CHUNK

    @pl.when(pl.program_id(1) == 0)
    def _():
        st_ref[...] = jnp.zeros_like(st_ref)

    row = lax.broadcasted_iota(jnp.int32, (C, C), 0)
    col = lax.broadcasted_iota(jnp.int32, (C, C), 1)
    causal = row >= col
    tri = causal.astype(BF16)

    a_hi, a_mid, _ = _split3(ha_ref[0])
    w_hi, w_mid, _ = _split3(wa2_ref[...])
    z = _dot(a_hi, w_hi) + (_dot(a_mid, w_hi) + _dot(a_hi, w_mid)) + ba_ref[...]
    la_scr[...] = (jnp.minimum(z, 0.0) - jnp.log(1.0 + jnp.exp(-jnp.abs(z)))) * (1.0 / GLA_TAU)

    H = range(GLA_HEADS)
    ck = [slice(h * GLA_DK, (h + 1) * GLA_DK) for h in H]
    cv = [slice(h * GLA_DV, (h + 1) * GLA_DV) for h in H]
    for c in range(nchunk):
        sl = pl.ds(c * C, C)
        hi, mid, _ = _split3(la_scr[sl, :])
        b = _dot(tri, hi) + _dot(tri, mid)
        decay = jnp.exp(b[C - 1:C, :])
        q_dec = q_ref[0, sl, :] * jnp.exp(b).astype(BF16)
        k_in = k_ref[0, sl, :] * jnp.exp(-b).astype(BF16)
        k_out = k_in * decay.astype(BF16)
        v = [v_ref[0, sl, cv[h]] for h in H]
        attn = [_dot_nt(q_dec[:, ck[h]], k_in[:, ck[h]]) for h in H]
        st = [st_ref[h] for h in H]
        o_inter = [_dot(q_dec[:, ck[h]], st[h].astype(BF16)) for h in H]
        upd = [_dot_tn(k_out[:, ck[h]], v[h]) for h in H]
        o = [_dot(jnp.where(causal, attn[h], 0.0).astype(BF16), v[h]) + o_inter[h] for h in H]
        for h in H:
            dcol = jnp.broadcast_to(decay[:, ck[h]], (GLA_DK, GLA_DK)).T
            st_ref[h] = st[h] * jnp.concatenate([dcol] * (GLA_DV // GLA_DK), axis=1) + upd[h]
        for h in H:
            r = r_ref[0, sl, cv[h]]
            o_ref[0, sl, cv[h]] = _rms(o[h], gn_ref[:, cv[h]]).astype(BF16) * (r * jax.nn.sigmoid(r))


def _gla(y3, ha3, wa2, ba, gn, tb=512):
    B, S, _ = y3.shape
    return pl.pallas_call(
        functools.partial(_gla_kernel, nchunk=tb // GLA_CHUNK),
        grid=(B, S // tb),
        in_specs=[
            pl.BlockSpec((1, tb, GLA_QK_W), lambda b, t: (b, t, COL_Q // GLA_QK_W)),
            pl.BlockSpec((1, tb, GLA_QK_W), lambda b, t: (b, t, COL_K // GLA_QK_W)),
            pl.BlockSpec((1, tb, GLA_V_W), lambda b, t: (b, t, COL_V // GLA_V_W)),
            pl.BlockSpec((1, tb, GLA_V_W), lambda b, t: (b, t, COL_R // GLA_V_W)),
            pl.BlockSpec((1, tb, LANES), lambda b, t: (b, t, 0)),
            pl.BlockSpec((LANES, GLA_QK_W), lambda b, t: (0, 0)),
            pl.BlockSpec((1, GLA_QK_W), lambda b, t: (0, 0)),
            pl.BlockSpec((1, GLA_V_W), lambda b, t: (0, 0)),
        ],
        out_specs=pl.BlockSpec((1, tb, GLA_V_W), lambda b, t: (b, t, 0)),
        out_shape=jax.ShapeDtypeStruct((B, S, GLA_V_W), BF16),
        scratch_shapes=[pltpu.VMEM((GLA_HEADS, GLA_DK, GLA_DV), F32), pltpu.VMEM((tb, GLA_QK_W), F32)],
        compiler_params=_params(("parallel", "arbitrary")),
        name="gla",
    )(y3, y3, y3, y3, ha3, wa2, ba, gn)


def _t5_causal_bucket(n):
    max_exact = REL_BUCKETS // 2
    nf = np.maximum(n, 1).astype(np.float32)
    large = max_exact + (np.log(nf / max_exact) / np.log(REL_MAX_DIST / max_exact)
                         * (REL_BUCKETS - max_exact)).astype(np.int32)
    large = np.minimum(large, REL_BUCKETS - 1)
    return np.where(n < max_exact, n, large).astype(np.int32)


def _bucket_table():
    qi = np.arange(ATT_BLOCK)[:, None]
    kj = np.arange(2 * ATT_BLOCK)[None, :]
    delta = qi + ATT_BLOCK - kj
    out = []
    for win, dil in ATT_GROUPS:
        band = (delta >= 0) & (delta <= win // dil)
        bucket = np.where(band, _t5_causal_bucket(np.maximum(delta, 0) * dil), -1)
        out.append(bucket)
    return np.stack(out).astype(np.int32)


def _att_unit(q, k, v, bias):
    BLK = ATT_BLOCK
    s = _dot_nt(q, k) + bias
    m = jnp.max(jnp.maximum(s[:, :BLK], s[:, BLK:]), axis=-1, keepdims=True)
    p = jnp.exp2((s - m).astype(BF16))
    v_ext = jnp.concatenate([v, jnp.ones_like(v)], axis=1)
    pv = _dot(p, v_ext)
    denom = pv[:, BLK:]
    return pv[:, :BLK] / denom, m + jnp.log2(denom)


def _att_kernel(tab_ref, bucket_ref,
                q0, k0, v0, kp0, vp0, q1, k1, v1, kp1, vp1, q2, k2, v2, kp2, vp2,
                o_ref, bias_scr, o_scr, lse_scr, *, tiles_per_batch):
    BLK = ATT_BLOCK
    ti = pl.program_id(0)
    h = pl.program_id(1)

    @pl.when((ti == 0) & (h == 0))
    def _():
        for g in range(N_ATT_GROUPS):
            bucket = bucket_ref[g]
            for hh in range(ATT_HEADS_PER_GROUP):
                bias = jnp.full(bucket.shape, NEG_INF, F32)
                for b in range(REL_BUCKETS):
                    bias = jnp.where(bucket == b, tab_ref[b, g * ATT_HEADS_PER_GROUP + hh] * LOG2_E, bias)
                bias_scr[g, hh] = bias

    lane = lax.broadcasted_iota(jnp.int32, (BLK, 2 * BLK), 1)
    no_prev = jnp.logical_and((ti % tiles_per_batch) == 0, lane < BLK)
    groups = ((q0, k0, v0, kp0, vp0), (q1, k1, v1, kp1, vp1), (q2, k2, v2, kp2, vp2))
    for g, (q, k, v, kp, vp) in enumerate(groups):
        dil = ATT_DILS[g]
        n_r = TILE // dil
        kp_rows = kp.shape[0]
        bias = bias_scr[g, h]
        bias_first = jnp.where(no_prev, NEG_INF, bias)
        for r in range(dil):
            for j in range(n_r // BLK):
                cur = pl.ds(r * n_r + j * BLK, BLK)
                if j > 0:
                    both = pl.ds(r * n_r + (j - 1) * BLK, 2 * BLK)
                    k_all, v_all, b = k[both, :], v[both, :], bias
                else:
                    prev = pl.ds((r * n_r + n_r - BLK) % kp_rows, BLK)
                    k_all = jnp.concatenate([kp[prev, :], k[cur, :]], axis=0)
                    v_all = jnp.concatenate([vp[prev, :], v[cur, :]], axis=0)
                    b = bias_first
                if dil == 1:
                    rows = pl.ds(j * BLK, BLK)
                else:
                    rows = pl.ds(dil * j * BLK + r, BLK, stride=dil)
                o_scr[g, rows, :], lse_scr[g, rows, :] = _att_unit(q[cur, :], k_all, v_all, b)

    for c in range(TILE // 256):
        rows = pl.ds(c * 256, 256)
        lse = [lse_scr[g, rows, :] for g in range(N_ATT_GROUPS)]
        top = functools.reduce(jnp.maximum, lse)
        w = [jnp.exp2(x - top) for x in lse]
        num = functools.reduce(lambda a, b: a + b, [w[g] * o_scr[g, rows, :] for g in range(N_ATT_GROUPS)])
        den = functools.reduce(lambda a, b: a + b, w)
        o_ref[rows, :] = (num / den).astype(BF16)


def _attention(y2, rel_bias, S):
    T = y2.shape[0]
    tiles_per_batch = S // TILE
    hd = ATT_HEAD_DIM

    def col(g, c):
        return (COL_ATT + g * ATT_GROUP_W + c * ATT_W) // hd

    def cur(g, c):
        return pl.BlockSpec((TILE, hd), lambda t, h: (t, col(g, c) + h))

    def prev_tile(g, c):
        return pl.BlockSpec((TILE, hd), lambda t, h: (jnp.maximum(t - 1, 0), col(g, c) + h))

    def prev_block(g, c):
        nb = TILE // ATT_BLOCK
        return pl.BlockSpec((ATT_BLOCK, hd), lambda t, h: (jnp.maximum(t * nb - 1, 0), col(g, c) + h))

    in_specs = [pl.BlockSpec(memory_space=pltpu.SMEM),
                pl.BlockSpec((N_ATT_GROUPS, ATT_BLOCK, 2 * ATT_BLOCK), lambda t, h: (0, 0, 0))]
    for g in range(N_ATT_GROUPS):
        prv = prev_block if ATT_DILS[g] == 1 else prev_tile
        in_specs += [cur(g, 0), cur(g, 1), cur(g, 2), prv(g, 1), prv(g, 2)]
    return pl.pallas_call(
        functools.partial(_att_kernel, tiles_per_batch=tiles_per_batch),
        grid=(T // TILE, ATT_HEADS_PER_GROUP),
        in_specs=in_specs,
        out_specs=pl.BlockSpec((TILE, hd), lambda t, h: (t, h)),
        out_shape=jax.ShapeDtypeStruct((T, ATT_W), BF16),
        scratch_shapes=[
            pltpu.VMEM((N_ATT_GROUPS, ATT_HEADS_PER_GROUP, ATT_BLOCK, 2 * ATT_BLOCK), F32),
            pltpu.VMEM((N_ATT_GROUPS, TILE, hd), F32),
            pltpu.VMEM((N_ATT_GROUPS, TILE, LANES), F32),
        ],
        compiler_params=_params(("arbitrary", "arbitrary")),
        name="attention",
    )(rel_bias, jnp.asarray(_bucket_table()), *([y2] * (5 * N_ATT_GROUPS)))


def _tail_kernel(x_ref, og_ref, oa_ref, ga_ref, gb_ref, p_ref, wog_ref, woa_ref, wout_ref,
                 ln2_ref, w1_ref, w2_ref, ln3_ref, wpg_ref, wpp_ref, lnf_ref, out_ref, *, tf):
    y_gla = _dot(og_ref[...], wog_ref[...])
    y_att = _dot(oa_ref[...], woa_ref[...])
    mix = (jax.nn.sigmoid(ga_ref[...].astype(F32)) * y_gla
           + jax.nn.sigmoid(gb_ref[...].astype(F32)) * y_att)
    x1 = x_ref[...] + _dot(mix.astype(BF16), wout_ref[...])

    h = _rms(x1, ln2_ref[...]).astype(BF16)
    x2 = x1
    for f in range(D_FF // tf):
        cols = pl.ds(f * tf, tf)
        u = jnp.maximum(_dot(h, w1_ref[:, cols]), 0.0)
        x2 = x2 + _dot((u * u).astype(BF16), w2_ref[cols, :])

    h3 = _rms(x2, ln3_ref[...]).astype(BF16)
    gate = jax.nn.sigmoid(_dot(h3, wpg_ref[...]))
    x3 = x2 + gate * _dot(p_ref[...].astype(BF16), wpp_ref[...])
    out_ref[...] = _rms(x3, lnf_ref[...])


def _tail(x2, y2, o_gla, o_att, p2, wog, woa, wout, ln2, w1, w2, ln3, wpg, wpp, lnf, tm=512, tf=1024):
    T = x2.shape[0]
    gcol = COL_GATE // D_MODEL
    row = lambda w: pl.BlockSpec((tm, w), lambda i: (i, 0))
    const = lambda a: pl.BlockSpec(a.shape, lambda i: (0, 0), pipeline_mode=pl.Buffered(1))
    return pl.pallas_call(
        functools.partial(_tail_kernel, tf=tf),
        grid=(T // tm,),
        in_specs=[row(D_MODEL), row(GLA_V_W), row(ATT_W),
                  pl.BlockSpec((tm, D_MODEL), lambda i: (i, gcol)),
                  pl.BlockSpec((tm, D_MODEL), lambda i: (i, gcol + 1)),
                  row(PLE_DIM),
                  const(wog), const(woa), const(wout), const(ln2), const(w1), const(w2),
                  const(ln3), const(wpg), const(wpp), const(lnf)],
        out_specs=row(D_MODEL),
        out_shape=jax.ShapeDtypeStruct((T, D_MODEL), F32),
        compiler_params=_params(("parallel",)),
        name="tail",
    )(x2, o_gla, o_att, y2, y2, p2, wog, woa, wout, ln2, w1, w2, ln3, wpg, wpp, lnf)


def _layer(x2, p2, B, S, ln1, w_in, w_a2, b_a, gla_gn, w_o_gla, w_o_attn, w_out,
           ln2, w_mlp1, w_mlp2, ln3, w_pp, w_pg, rel_bias, ln_out):
    wa2 = jnp.pad(w_a2, ((0, LANES - GLA_RANK), (0, 0)))

    w_in_t = w_in.T
    y2, ha = _inproj(x2, ln1[None], _wprep(w_in_t), w_in_t)
    o_gla = _gla(y2.reshape(B, S, Y_COLS), ha.reshape(B, S, LANES), wa2, b_a[None], gla_gn[None])
    o_att = _attention(y2, rel_bias, S)
    return _tail(x2, y2, o_gla.reshape(B * S, GLA_V_W), o_att, p2,
                 w_o_gla.astype(BF16), w_o_attn.astype(BF16), w_out.astype(BF16), ln2[None],
                 w_mlp1.astype(BF16), w_mlp2.astype(BF16), ln3[None],
                 w_pg.astype(BF16), w_pp.astype(BF16), ln_out[None])


def kernel(x, p, ln1, w_in, w_a2, b_a, gla_gn, w_o_gla, w_o_attn, w_out, ln2, w_mlp1, w_mlp2,
           ln3, w_pp, w_pg, rel_bias, ln_f):
    B, S, D = x.shape
    assert p.shape[0] == 1, "the final norm is fused into the single layer's last kernel"
    assert S % TILE == 0
    x2 = x.reshape(B * S, D)
    out = _layer(x2, p[0].reshape(B * S, PLE_DIM), B, S, ln1[0], w_in[0], w_a2[0], b_a[0],
                 gla_gn[0], w_o_gla[0], w_o_attn[0], w_out[0], ln2[0], w_mlp1[0], w_mlp2[0],
                 ln3[0], w_pp[0], w_pg[0], rel_bias, ln_f)
    return out.reshape(B, S, D)
```

```python
import functools

import numpy as np
import jax
import jax.numpy as jnp
from jax import lax
from jax.experimental import pallas as pl
from jax.experimental.pallas import tpu as pltpu

F32 = jnp.float32
BF16 = jnp.bfloat16

D_MODEL = 1024
PLE_DIM = 256
EPS = 1e-6
GLA_HEADS = 4
GLA_DK = 128
GLA_DV = 256
GLA_RANK = 16
GLA_TAU = 16.0
GLA_CHUNK = 64
GLA_QK_W = GLA_HEADS * GLA_DK
GLA_V_W = GLA_HEADS * GLA_DV
ATT_GROUPS = ((128, 1), (512, 4), (2048, 16))
ATT_DILS = tuple(d for _, d in ATT_GROUPS)
ATT_HEADS_PER_GROUP = 4
ATT_HEAD_DIM = 128
N_ATT_GROUPS = len(ATT_GROUPS)
ATT_W = ATT_HEADS_PER_GROUP * ATT_HEAD_DIM
ATT_BLOCK = 128
REL_BUCKETS = 32
REL_MAX_DIST = 2048
D_FF = 4 * D_MODEL
NEG_INF = -1e30
LOG2_E = 1.4426950408889634

LANES = 128

COL_Q = 0
COL_K = COL_Q + GLA_QK_W
COL_V = COL_K + GLA_QK_W
COL_R = COL_V + GLA_V_W
COL_GATE = COL_R + GLA_V_W
COL_ATT = COL_GATE + 2 * D_MODEL
ATT_GROUP_W = 3 * ATT_W
Y_COLS = COL_ATT + N_ATT_GROUPS * ATT_GROUP_W

TILE = max(ATT_DILS) * ATT_BLOCK

VMEM_LIMIT = 56 * 1024 * 1024


def _params(semantics):
    return pltpu.CompilerParams(dimension_semantics=semantics, vmem_limit_bytes=VMEM_LIMIT)


def _rms(x, g):
    return x * lax.rsqrt(jnp.mean(x * x, axis=-1, keepdims=True) + EPS) * g


def _dot(a, b):
    return jnp.dot(a, b, preferred_element_type=F32)


def _dot_nt(a, b):
    return lax.dot_general(a, b, (((1,), (1,)), ((), ())), preferred_element_type=F32)


def _dot_tn(a, b):
    return lax.dot_general(a, b, (((0,), (0,)), ((), ())), preferred_element_type=F32)


def _split3(g):
    hi = g.astype(BF16)
    r1 = g - hi.astype(F32)
    mid = r1.astype(BF16)
    lo = (r1 - mid.astype(F32)).astype(BF16)
    return hi, mid, lo


W_GATE_COL = COL_R + GLA_V_W
W_ATT_COL = W_GATE_COL + GLA_RANK
W_MIXGATE_COL = W_ATT_COL + N_ATT_GROUPS * ATT_GROUP_W


def _wprep_kernel(a_ref, b_ref, scale_ref, o_ref, *, n_aligned, tn):
    t = pl.program_id(0)

    @pl.when(t < n_aligned)
    def _():
        o_ref[...] = (a_ref[...].T * scale_ref[...]).astype(BF16)

    @pl.when(t >= n_aligned)
    def _():
        w = jnp.concatenate([a_ref[...], b_ref[...]], axis=0)[GLA_RANK:GLA_RANK + tn]
        o_ref[...] = (w.T * scale_ref[...]).astype(BF16)


def _wprep(w_in_t, tn=512):
    n_aligned = W_GATE_COL // tn
    n_att = (W_MIXGATE_COL - W_ATT_COL) // tn

    def out_col(t):
        return jnp.where(t < n_aligned, t,
                         jnp.where(t < n_aligned + n_att, t - n_aligned + COL_ATT // tn,
                                   t - n_aligned - n_att + COL_GATE // tn))

    col_scale = np.ones((1, Y_COLS), np.float32)
    col_scale[:, COL_Q:COL_K] = GLA_DK ** -0.5
    for g in range(N_ATT_GROUPS):
        col_scale[:, COL_ATT + g * ATT_GROUP_W:COL_ATT + g * ATT_GROUP_W + ATT_W] = ATT_HEAD_DIM ** -0.5 * LOG2_E
    return pl.pallas_call(
        functools.partial(_wprep_kernel, n_aligned=n_aligned, tn=tn),
        grid=(Y_COLS // tn,),
        in_specs=[
            pl.BlockSpec((tn, D_MODEL), lambda t: (t, 0)),
            pl.BlockSpec((GLA_RANK, D_MODEL), lambda t: ((t + 1) * (tn // GLA_RANK), 0)),
            pl.BlockSpec((1, tn), lambda t: (0, out_col(t))),
        ],
        out_specs=pl.BlockSpec((D_MODEL, tn), lambda t: (0, out_col(t))),
        out_shape=jax.ShapeDtypeStruct((D_MODEL, Y_COLS), BF16),
        compiler_params=_params(("arbitrary",)),
        name="wprep",
    )(w_in_t, w_in_t, jnp.asarray(col_scale))


def _inproj_kernel(*refs, tile_orders):
    n_slab = D_MODEL // LANES
    x_slabs = refs[:n_slab]
    ln_ref, w_ref, wa1_ref, y_ref, ha_ref, h_scr, inv_scr, tmp_scr = refs[n_slab:]
    j = pl.program_id(1)
    chunk = 256
    assert ATT_DILS == (1, 4, 16) and all(order == (0, 0) for order in tile_orders[:4])

    def project(d_left, d_right):
        half = y_ref.shape[1] // 2
        y_ref[:, :half] = _dot(h_scr[d_left], w_ref[:, :half]).astype(BF16)
        if d_right is not None:
            y_ref[:, half:] = _dot(h_scr[d_right], w_ref[:, half:]).astype(BF16)

    def normalise():
        for c in range(TILE // chunk):
            rows = pl.ds(c * chunk, chunk)
            parts = [xr[rows, :] for xr in x_slabs]
            sq = parts[0] * parts[0]
            for part in parts[1:]:
                sq = sq + part * part
            ms = jnp.sum(sq, axis=-1, keepdims=True) * (1.0 / D_MODEL)
            inv = jnp.broadcast_to(lax.rsqrt(ms + EPS), (chunk, LANES))
            inv_scr[rows, :] = inv
            for s, part in enumerate(parts):
                cols = pl.ds(s * LANES, LANES)
                h_scr[0, rows, cols] = (part * inv * ln_ref[:, cols]).astype(BF16)

    def regroup(slabs):
        n4, n16 = TILE // 4, TILE // 16
        for s in slabs:
            xr = x_slabs[s]
            cols = pl.ds(s * LANES, LANES)
            ln_s = ln_ref[:, cols]
            for a in range(4):
                for c in range(n4 // chunk):
                    src = pl.ds(a + 4 * c * chunk, chunk, stride=4)
                    dst = pl.ds(a * n4 + c * chunk, chunk)
                    hn = xr[src, :] * inv_scr[src, :] * ln_s
                    tmp_scr[dst, :] = hn
                    h_scr[1, dst, cols] = hn.astype(BF16)
            for a in range(4):
                for b in range(4):
                    src = pl.ds(a * n4 + b, n16, stride=4)
                    h_scr[2, pl.ds((a + 4 * b) * n16, n16), cols] = tmp_scr[src, :].astype(BF16)

    def gate_input():
        wa1_t = jnp.concatenate([wa1_ref[...], jnp.zeros((LANES - GLA_RANK, D_MODEL), F32)], axis=0)
        ha_ref[...] = _dot_nt(h_scr[0], wa1_t.astype(BF16))

    prep = {0: normalise, 1: functools.partial(regroup, range(0, n_slab // 2)),
            2: functools.partial(regroup, range(n_slab // 2, n_slab)), 3: gate_input}
    special = [t for t, (left, right) in enumerate(tile_orders) if t in prep or left != right]
    for t in special:
        @pl.when(j == t)
        def _(t=t):
            if t in prep:
                prep[t]()
            project(*tile_orders[t])

    @pl.when(functools.reduce(jnp.logical_and, [j != t for t in special]))
    def _():
        firsts = [min(t for t, o in enumerate(tile_orders) if o[0] == d) for d in (1, 2)]
        d = sum((j >= f).astype(jnp.int32) for f in firsts)
        project(d, d)


def _inproj(x2, ln1, w_main, w_in_t, tn=1024):
    T = x2.shape[0]
    n_slab = D_MODEL // LANES
    half = tn // 2
    bounds = [COL_ATT + ATT_GROUP_W * (g + 1) for g in range(N_ATT_GROUPS)]
    order_of = lambda col: None if col >= Y_COLS else sum(col >= b for b in bounds)
    n_tiles = pl.cdiv(Y_COLS, tn)
    tile_orders = tuple((order_of(t * tn), order_of(t * tn + half)) for t in range(n_tiles))
    assert all(b % half == 0 for b in bounds) and Y_COLS % half == 0
    return pl.pallas_call(
        functools.partial(_inproj_kernel, tile_orders=tile_orders),
        grid=(T // TILE, n_tiles),
        in_specs=[pl.BlockSpec((TILE, LANES), functools.partial(lambda i, j, s: (i, s), s=s))
                  for s in range(n_slab)] + [
            pl.BlockSpec((1, D_MODEL), lambda i, j: (0, 0)),
            pl.BlockSpec((D_MODEL, tn), lambda i, j: (0, j)),
            pl.BlockSpec((GLA_RANK, D_MODEL), lambda i, j: (W_GATE_COL // GLA_RANK, 0)),
        ],
        out_specs=[
            pl.BlockSpec((TILE, tn), lambda i, j: (i, j)),
            pl.BlockSpec((TILE, LANES), lambda i, j: (i, 0)),
        ],
        out_shape=[
            jax.ShapeDtypeStruct((T, Y_COLS), BF16),
            jax.ShapeDtypeStruct((T, LANES), F32),
        ],
        scratch_shapes=[pltpu.VMEM((N_ATT_GROUPS, TILE, D_MODEL), BF16),
                        pltpu.VMEM((TILE, LANES), F32),
                        pltpu.VMEM((TILE, LANES), F32)],
        compiler_params=_params(("arbitrary", "arbitrary")),
        name="inproj",
    )(*([x2] * n_slab), ln1, w_main, w_in_t)


def _gla_kernel(q_ref, k_ref, v_ref, r_ref, ha_ref, wa2_ref, ba_ref, gn_ref, o_ref, st_ref, la_scr,
                *, nchunk):
    C = GLA_CHUNK

    @pl.when(pl.program_id(1) == 0)
    def _():
        st_ref[...] = jnp.zeros_like(st_ref)

    row = lax.broadcasted_iota(jnp.int32, (C, C), 0)
    col = lax.broadcasted_iota(jnp.int32, (C, C), 1)
    causal = row >= col
    tri = causal.astype(BF16)

    a_hi, a_mid, _ = _split3(ha_ref[0])
    w_hi, w_mid, _ = _split3(wa2_ref[...])
    z = _dot(a_hi, w_hi) + (_dot(a_mid, w_hi) + _dot(a_hi, w_mid)) + ba_ref[...]
    la_scr[...] = (jnp.minimum(z, 0.0) - jnp.log(1.0 + jnp.exp(-jnp.abs(z)))) * (1.0 / GLA_TAU)

    H = range(GLA_HEADS)
    ck = [slice(h * GLA_DK, (h + 1) * GLA_DK) for h in H]
    cv = [slice(h * GLA_DV, (h + 1) * GLA_DV) for h in H]
    for c in range(nchunk):
        sl = pl.ds(c * C, C)
        hi, mid, _ = _split3(la_scr[sl, :])
        b = _dot(tri, hi) + _dot(tri, mid)
        decay = jnp.exp(b[C - 1:C, :])
        q_dec = q_ref[0, sl, :] * jnp.exp(b).astype(BF16)
        k_in = k_ref[0, sl, :] * jnp.exp(-b).astype(BF16)
        k_out = k_in * decay.astype(BF16)
        v = [v_ref[0, sl, cv[h]] for h in H]
        attn = [_dot_nt(q_dec[:, ck[h]], k_in[:, ck[h]]) for h in H]
        st = [st_ref[h] for h in H]
        o_inter = [_dot(q_dec[:, ck[h]], st[h].astype(BF16)) for h in H]
        upd = [_dot_tn(k_out[:, ck[h]], v[h]) for h in H]
        o = [_dot(jnp.where(causal, attn[h], 0.0).astype(BF16), v[h]) + o_inter[h] for h in H]
        for h in H:
            dcol = jnp.broadcast_to(decay[:, ck[h]], (GLA_DK, GLA_DK)).T
            st_ref[h] = st[h] * jnp.concatenate([dcol] * (GLA_DV // GLA_DK), axis=1) + upd[h]
        for h in H:
            r = r_ref[0, sl, cv[h]]
            o_ref[0, sl, cv[h]] = _rms(o[h], gn_ref[:, cv[h]]).astype(BF16) * (r * jax.nn.sigmoid(r))


def _gla(y3, ha3, wa2, ba, gn, tb=512):
    B, S, _ = y3.shape
    return pl.pallas_call(
        functools.partial(_gla_kernel, nchunk=tb // GLA_CHUNK),
        grid=(B, S // tb),
        in_specs=[
            pl.BlockSpec((1, tb, GLA_QK_W), lambda b, t: (b, t, COL_Q // GLA_QK_W)),
            pl.BlockSpec((1, tb, GLA_QK_W), lambda b, t: (b, t, COL_K // GLA_QK_W)),
            pl.BlockSpec((1, tb, GLA_V_W), lambda b, t: (b, t, COL_V // GLA_V_W)),
            pl.BlockSpec((1, tb, GLA_V_W), lambda b, t: (b, t, COL_R // GLA_V_W)),
            pl.BlockSpec((1, tb, LANES), lambda b, t: (b, t, 0)),
            pl.BlockSpec((LANES, GLA_QK_W), lambda b, t: (0, 0)),
            pl.BlockSpec((1, GLA_QK_W), lambda b, t: (0, 0)),
            pl.BlockSpec((1, GLA_V_W), lambda b, t: (0, 0)),
        ],
        out_specs=pl.BlockSpec((1, tb, GLA_V_W), lambda b, t: (b, t, 0)),
        out_shape=jax.ShapeDtypeStruct((B, S, GLA_V_W), BF16),
        scratch_shapes=[pltpu.VMEM((GLA_HEADS, GLA_DK, GLA_DV), F32), pltpu.VMEM((tb, GLA_QK_W), F32)],
        compiler_params=_params(("arbitrary", "arbitrary")),
        name="gla",
    )(y3, y3, y3, y3, ha3, wa2, ba, gn)


def _t5_causal_bucket(n):
    max_exact = REL_BUCKETS // 2
    nf = np.maximum(n, 1).astype(np.float32)
    large = max_exact + (np.log(nf / max_exact) / np.log(REL_MAX_DIST / max_exact)
                         * (REL_BUCKETS - max_exact)).astype(np.int32)
    large = np.minimum(large, REL_BUCKETS - 1)
    return np.where(n < max_exact, n, large).astype(np.int32)


def _bucket_table():
    qi = np.arange(ATT_BLOCK)[:, None]
    kj = np.arange(2 * ATT_BLOCK)[None, :]
    delta = qi + ATT_BLOCK - kj
    out = []
    for win, dil in ATT_GROUPS:
        band = (delta >= 0) & (delta <= win // dil)
        bucket = np.where(band, _t5_causal_bucket(np.maximum(delta, 0) * dil), -1)
        out.append(bucket)
    return np.stack(out).astype(np.int32)


def _att_unit(q, k, v, bias):
    BLK = ATT_BLOCK
    s = _dot_nt(q, k) + bias
    m = jnp.max(jnp.maximum(s[:, :BLK], s[:, BLK:]), axis=-1, keepdims=True)
    p = jnp.exp2((s - m).astype(BF16))
    v_ext = jnp.concatenate([v, jnp.ones_like(v)], axis=1)
    pv = _dot(p, v_ext)
    denom = pv[:, BLK:]
    return pv[:, :BLK] / denom, m + jnp.log2(denom)


def _att_kernel(tab_ref, bucket_ref,
                q0, k0, v0, kp0, vp0, q1, k1, v1, kp1, vp1, q2, k2, v2, kp2, vp2,
                o_ref, bias_scr, o_scr, lse_scr, *, tiles_per_batch):
    BLK = ATT_BLOCK
    ti = pl.program_id(0)
    h = pl.program_id(1)

    @pl.when((ti == 0) & (h == 0))
    def _():
        for g in range(N_ATT_GROUPS):
            bucket = bucket_ref[g]
            for hh in range(ATT_HEADS_PER_GROUP):
                bias = jnp.full(bucket.shape, NEG_INF, F32)
                for b in range(REL_BUCKETS):
                    bias = jnp.where(bucket == b, tab_ref[b, g * ATT_HEADS_PER_GROUP + hh] * LOG2_E, bias)
                bias_scr[g, hh] = bias

    lane = lax.broadcasted_iota(jnp.int32, (BLK, 2 * BLK), 1)
    no_prev = jnp.logical_and((ti % tiles_per_batch) == 0, lane < BLK)
    groups = ((q0, k0, v0, kp0, vp0), (q1, k1, v1, kp1, vp1), (q2, k2, v2, kp2, vp2))
    for g, (q, k, v, kp, vp) in enumerate(groups):
        dil = ATT_DILS[g]
        n_r = TILE // dil
        kp_rows = kp.shape[0]
        bias = bias_scr[g, h]
        bias_first = jnp.where(no_prev, NEG_INF, bias)
        for r in range(dil):
            for j in range(n_r // BLK):
                cur = pl.ds(r * n_r + j * BLK, BLK)
                if j > 0:
                    both = pl.ds(r * n_r + (j - 1) * BLK, 2 * BLK)
                    k_all, v_all, b = k[both, :], v[both, :], bias
                else:
                    prev = pl.ds((r * n_r + n_r - BLK) % kp_rows, BLK)
                    k_all = jnp.concatenate([kp[prev, :], k[cur, :]], axis=0)
                    v_all = jnp.concatenate([vp[prev, :], v[cur, :]], axis=0)
                    b = bias_first
                if dil == 1:
                    rows = pl.ds(j * BLK, BLK)
                else:
                    rows = pl.ds(dil * j * BLK + r, BLK, stride=dil)
                o_scr[g, rows, :], lse_scr[g, rows, :] = _att_unit(q[cur, :], k_all, v_all, b)

    for c in range(TILE // 256):
        rows = pl.ds(c * 256, 256)
        lse = [lse_scr[g, rows, :] for g in range(N_ATT_GROUPS)]
        top = functools.reduce(jnp.maximum, lse)
        w = [jnp.exp2(x - top) for x in lse]
        num = functools.reduce(lambda a, b: a + b, [w[g] * o_scr[g, rows, :] for g in range(N_ATT_GROUPS)])
        den = functools.reduce(lambda a, b: a + b, w)
        o_ref[rows, :] = (num / den).astype(BF16)


def _attention(y2, rel_bias, S):
    T = y2.shape[0]
    tiles_per_batch = S // TILE
    hd = ATT_HEAD_DIM

    def col(g, c):
        return (COL_ATT + g * ATT_GROUP_W + c * ATT_W) // hd

    def cur(g, c):
        return pl.BlockSpec((TILE, hd), lambda t, h: (t, col(g, c) + h))

    def prev_tile(g, c):
        return pl.BlockSpec((TILE, hd), lambda t, h: (jnp.maximum(t - 1, 0), col(g, c) + h))

    def prev_block(g, c):
        nb = TILE // ATT_BLOCK
        return pl.BlockSpec((ATT_BLOCK, hd), lambda t, h: (jnp.maximum(t * nb - 1, 0), col(g, c) + h))

    in_specs = [pl.BlockSpec(memory_space=pltpu.SMEM),
                pl.BlockSpec((N_ATT_GROUPS, ATT_BLOCK, 2 * ATT_BLOCK), lambda t, h: (0, 0, 0))]
    for g in range(N_ATT_GROUPS):
        prv = prev_block if ATT_DILS[g] == 1 else prev_tile
        in_specs += [cur(g, 0), cur(g, 1), cur(g, 2), prv(g, 1), prv(g, 2)]
    return pl.pallas_call(
        functools.partial(_att_kernel, tiles_per_batch=tiles_per_batch),
        grid=(T // TILE, ATT_HEADS_PER_GROUP),
        in_specs=in_specs,
        out_specs=pl.BlockSpec((TILE, hd), lambda t, h: (t, h)),
        out_shape=jax.ShapeDtypeStruct((T, ATT_W), BF16),
        scratch_shapes=[
            pltpu.VMEM((N_ATT_GROUPS, ATT_HEADS_PER_GROUP, ATT_BLOCK, 2 * ATT_BLOCK), F32),
            pltpu.VMEM((N_ATT_GROUPS, TILE, hd), F32),
            pltpu.VMEM((N_ATT_GROUPS, TILE, LANES), F32),
        ],
        compiler_params=_params(("arbitrary", "arbitrary")),
        name="attention",
    )(rel_bias, jnp.asarray(_bucket_table()), *([y2] * (5 * N_ATT_GROUPS)))


def _tail_kernel(x_ref, og_ref, oa_ref, ga_ref, gb_ref, p_ref, wog_ref, woa_ref, wout_ref,
                 ln2_ref, w1_ref, w2_ref, ln3_ref, wpg_ref, wpp_ref, lnf_ref, out_ref, *, tf):
    y_gla = _dot(og_ref[...], wog_ref[...])
    y_att = _dot(oa_ref[...], woa_ref[...])
    mix = (jax.nn.sigmoid(ga_ref[...].astype(F32)) * y_gla
           + jax.nn.sigmoid(gb_ref[...].astype(F32)) * y_att)
    x1 = x_ref[...] + _dot(mix.astype(BF16), wout_ref[...])

    h = _rms(x1, ln2_ref[...]).astype(BF16)
    x2 = x1
    for f in range(D_FF // tf):
        cols = pl.ds(f * tf, tf)
        u = jnp.maximum(_dot(h, w1_ref[:, cols]), 0.0)
        x2 = x2 + _dot((u * u).astype(BF16), w2_ref[cols, :])

    h3 = _rms(x2, ln3_ref[...]).astype(BF16)
    gate = jax.nn.sigmoid(_dot(h3, wpg_ref[...]))
    x3 = x2 + gate * _dot(p_ref[...].astype(BF16), wpp_ref[...])
    out_ref[...] = _rms(x3, lnf_ref[...])


def _tail(x2, y2, o_gla, o_att, p2, wog, woa, wout, ln2, w1, w2, ln3, wpg, wpp, lnf, tm=512, tf=1024):
    T = x2.shape[0]
    gcol = COL_GATE // D_MODEL
    row = lambda w: pl.BlockSpec((tm, w), lambda i: (i, 0))
    const = lambda a: pl.BlockSpec(a.shape, lambda i: (0, 0), pipeline_mode=pl.Buffered(1))
    return pl.pallas_call(
        functools.partial(_tail_kernel, tf=tf),
        grid=(T // tm,),
        in_specs=[row(D_MODEL), row(GLA_V_W), row(ATT_W),
                  pl.BlockSpec((tm, D_MODEL), lambda i: (i, gcol)),
                  pl.BlockSpec((tm, D_MODEL), lambda i: (i, gcol + 1)),
                  row(PLE_DIM),
                  const(wog), const(woa), const(wout), const(ln2), const(w1), const(w2),
                  const(ln3), const(wpg), const(wpp), const(lnf)],
        out_specs=row(D_MODEL),
        out_shape=jax.ShapeDtypeStruct((T, D_MODEL), F32),
        compiler_params=_params(("arbitrary",)),
        name="tail",
    )(x2, o_gla, o_att, y2, y2, p2, wog, woa, wout, ln2, w1, w2, ln3, wpg, wpp, lnf)


def _layer(x2, p2, B, S, ln1, w_in, w_a2, b_a, gla_gn, w_o_gla, w_o_attn, w_out,
           ln2, w_mlp1, w_mlp2, ln3, w_pp, w_pg, rel_bias, ln_out):
    wa2 = jnp.pad(w_a2, ((0, LANES - GLA_RANK), (0, 0)))

    w_in_t = w_in.T
    y2, ha = _inproj(x2, ln1[None], _wprep(w_in_t), w_in_t)
    o_gla = _gla(y2.reshape(B, S, Y_COLS), ha.reshape(B, S, LANES), wa2, b_a[None], gla_gn[None])
    o_att = _attention(y2, rel_bias, S)
    return _tail(x2, y2, o_gla.reshape(B * S, GLA_V_W), o_att, p2,
                 w_o_gla.astype(BF16), w_o_attn.astype(BF16), w_out.astype(BF16), ln2[None],
                 w_mlp1.astype(BF16), w_mlp2.astype(BF16), ln3[None],
                 w_pg.astype(BF16), w_pp.astype(BF16), ln_out[None])


def kernel(x, p, ln1, w_in, w_a2, b_a, gla_gn, w_o_gla, w_o_attn, w_out, ln2, w_mlp1, w_mlp2,
           ln3, w_pp, w_pg, rel_bias, ln_f):
    B, S, D = x.shape
    assert p.shape[0] == 1, "the final norm is fused into the single layer's last kernel"
    assert S % TILE == 0
    x2 = x.reshape(B * S, D)
    out = _layer(x2, p[0].reshape(B * S, PLE_DIM), B, S, ln1[0], w_in[0], w_a2[0], b_a[0],
                 gla_gn[0], w_o_gla[0], w_o_attn[0], w_out[0], ln2[0], w_mlp1[0], w_mlp2[0],
                 ln3[0], w_pp[0], w_pg[0], rel_bias, ln_f)
    return out.reshape(B, S, D)
```

```python
import functools

import numpy as np
import jax
import jax.numpy as jnp
from jax import lax
from jax.experimental import pallas as pl
from jax.experimental.pallas import tpu as pltpu

F32 = jnp.float32
BF16 = jnp.bfloat16

D_MODEL = 1024
PLE_DIM = 256
EPS = 1e-6
GLA_HEADS = 4
GLA_DK = 128
GLA_DV = 256
GLA_RANK = 16
GLA_TAU = 16.0
GLA_CHUNK = 64
GLA_QK_W = GLA_HEADS * GLA_DK
GLA_V_W = GLA_HEADS * GLA_DV
ATT_GROUPS = ((128, 1), (512, 4), (2048, 16))
ATT_DILS = tuple(d for _, d in ATT_GROUPS)
ATT_HEADS_PER_GROUP = 4
ATT_HEAD_DIM = 128
N_ATT_GROUPS = len(ATT_GROUPS)
ATT_W = ATT_HEADS_PER_GROUP * ATT_HEAD_DIM
ATT_BLOCK = 128
REL_BUCKETS = 32
REL_MAX_DIST = 2048
D_FF = 4 * D_MODEL
NEG_INF = -1e30
LOG2_E = 1.4426950408889634

LANES = 128

COL_Q = 0
COL_K = COL_Q + GLA_QK_W
COL_V = COL_K + GLA_QK_W
COL_R = COL_V + GLA_V_W
COL_GATE = COL_R + GLA_V_W
COL_ATT = COL_GATE + 2 * D_MODEL
ATT_GROUP_W = 3 * ATT_W
Y_COLS = COL_ATT + N_ATT_GROUPS * ATT_GROUP_W

TILE = max(ATT_DILS) * ATT_BLOCK

VMEM_LIMIT = 56 * 1024 * 1024


def _params(semantics):
    return pltpu.CompilerParams(dimension_semantics=semantics, vmem_limit_bytes=VMEM_LIMIT)


def _rms(x, g):
    return x * lax.rsqrt(jnp.mean(x * x, axis=-1, keepdims=True) + EPS) * g


def _dot(a, b):
    return jnp.dot(a, b, preferred_element_type=F32)


def _dot_nt(a, b):
    return lax.dot_general(a, b, (((1,), (1,)), ((), ())), preferred_element_type=F32)


def _dot_tn(a, b):
    return lax.dot_general(a, b, (((0,), (0,)), ((), ())), preferred_element_type=F32)


def _split3(g):
    hi = g.astype(BF16)
    r1 = g - hi.astype(F32)
    mid = r1.astype(BF16)
    lo = (r1 - mid.astype(F32)).astype(BF16)
    return hi, mid, lo


W_GATE_COL = COL_R + GLA_V_W
W_ATT_COL = W_GATE_COL + GLA_RANK
W_MIXGATE_COL = W_ATT_COL + N_ATT_GROUPS * ATT_GROUP_W


def _wprep_kernel(a_ref, b_ref, scale_ref, o_ref, *, n_aligned, tn):
    t = pl.program_id(0)

    @pl.when(t < n_aligned)
    def _():
        o_ref[...] = (a_ref[...].T * scale_ref[...]).astype(BF16)

    @pl.when(t >= n_aligned)
    def _():
        w = jnp.concatenate([a_ref[...], b_ref[...]], axis=0)[GLA_RANK:GLA_RANK + tn]
        o_ref[...] = (w.T * scale_ref[...]).astype(BF16)


def _wprep(w_in_t, tn=512):
    n_aligned = W_GATE_COL // tn
    n_att = (W_MIXGATE_COL - W_ATT_COL) // tn

    def out_col(t):
        return jnp.where(t < n_aligned, t,
                         jnp.where(t < n_aligned + n_att, t - n_aligned + COL_ATT // tn,
                                   t - n_aligned - n_att + COL_GATE // tn))

    col_scale = np.ones((1, Y_COLS), np.float32)
    col_scale[:, COL_Q:COL_K] = GLA_DK ** -0.5
    for g in range(N_ATT_GROUPS):
        col_scale[:, COL_ATT + g * ATT_GROUP_W:COL_ATT + g * ATT_GROUP_W + ATT_W] = ATT_HEAD_DIM ** -0.5 * LOG2_E
    return pl.pallas_call(
        functools.partial(_wprep_kernel, n_aligned=n_aligned, tn=tn),
        grid=(Y_COLS // tn,),
        in_specs=[
            pl.BlockSpec((tn, D_MODEL), lambda t: (t, 0)),
            pl.BlockSpec((GLA_RANK, D_MODEL), lambda t: ((t + 1) * (tn // GLA_RANK), 0)),
            pl.BlockSpec((1, tn), lambda t: (0, out_col(t))),
        ],
        out_specs=pl.BlockSpec((D_MODEL, tn), lambda t: (0, out_col(t))),
        out_shape=jax.ShapeDtypeStruct((D_MODEL, Y_COLS), BF16),
        compiler_params=_params(("arbitrary",)),
        name="wprep",
    )(w_in_t, w_in_t, jnp.asarray(col_scale))


def _inproj_kernel(*refs, tile_orders):
    n_slab = D_MODEL // LANES
    x_slabs = refs[:n_slab]
    ln_ref, w_ref, wa1_ref, y_ref, ha_ref, h_scr, inv_scr, tmp_scr = refs[n_slab:]
    j = pl.program_id(1)
    chunk = 256
    assert ATT_DILS == (1, 4, 16) and all(order == (0, 0) for order in tile_orders[:4])

    def project(d_left, d_right):
        half = y_ref.shape[1] // 2
        y_ref[:, :half] = _dot(h_scr[d_left], w_ref[:, :half]).astype(BF16)
        if d_right is not None:
            y_ref[:, half:] = _dot(h_scr[d_right], w_ref[:, half:]).astype(BF16)

    def normalise():
        for c in range(TILE // chunk):
            rows = pl.ds(c * chunk, chunk)
            parts = [xr[rows, :] for xr in x_slabs]
            sq = parts[0] * parts[0]
            for part in parts[1:]:
                sq = sq + part * part
            ms = jnp.sum(sq, axis=-1, keepdims=True) * (1.0 / D_MODEL)
            inv = jnp.broadcast_to(lax.rsqrt(ms + EPS), (chunk, LANES))
            inv_scr[rows, :] = inv
            for s, part in enumerate(parts):
                cols = pl.ds(s * LANES, LANES)
                h_scr[0, rows, cols] = (part * inv * ln_ref[:, cols]).astype(BF16)

    def regroup(slabs):
        n4, n16 = TILE // 4, TILE // 16
        for s in slabs:
            xr = x_slabs[s]
            cols = pl.ds(s * LANES, LANES)
            ln_s = ln_ref[:, cols]
            for a in range(4):
                for c in range(n4 // chunk):
                    src = pl.ds(a + 4 * c * chunk, chunk, stride=4)
                    dst = pl.ds(a * n4 + c * chunk, chunk)
                    hn = xr[src, :] * inv_scr[src, :] * ln_s
                    tmp_scr[dst, :] = hn
                    h_scr[1, dst, cols] = hn.astype(BF16)
            for a in range(4):
                for b in range(4):
                    src = pl.ds(a * n4 + b, n16, stride=4)
                    h_scr[2, pl.ds((a + 4 * b) * n16, n16), cols] = tmp_scr[src, :].astype(BF16)

    def gate_input():
        wa1_t = jnp.concatenate([wa1_ref[...], jnp.zeros((LANES - GLA_RANK, D_MODEL), F32)], axis=0)
        ha_ref[...] = _dot_nt(h_scr[0], wa1_t.astype(BF16))

    prep = {0: normalise, 1: functools.partial(regroup, range(0, n_slab // 2)),
            2: functools.partial(regroup, range(n_slab // 2, n_slab)), 3: gate_input}
    special = [t for t, (left, right) in enumerate(tile_orders) if t in prep or left != right]
    for t in special:
        @pl.when(j == t)
        def _(t=t):
            if t in prep:
                prep[t]()
            project(*tile_orders[t])

    @pl.when(functools.reduce(jnp.logical_and, [j != t for t in special]))
    def _():
        firsts = [min(t for t, o in enumerate(tile_orders) if o[0] == d) for d in (1, 2)]
        d = sum((j >= f).astype(jnp.int32) for f in firsts)
        project(d, d)


def _inproj(x2, ln1, w_main, w_in_t, tn=1024):
    T = x2.shape[0]
    n_slab = D_MODEL // LANES
    half = tn // 2
    bounds = [COL_ATT + ATT_GROUP_W * (g + 1) for g in range(N_ATT_GROUPS)]
    order_of = lambda col: None if col >= Y_COLS else sum(col >= b for b in bounds)
    n_tiles = pl.cdiv(Y_COLS, tn)
    tile_orders = tuple((order_of(t * tn), order_of(t * tn + half)) for t in range(n_tiles))
    assert all(b % half == 0 for b in bounds) and Y_COLS % half == 0

    def slab_index(i, j, s):
        moved = (j >= min(3 + s, n_tiles - 1)).astype(jnp.int32)
        return jnp.minimum(i + moved, T // TILE - 1), s

    return pl.pallas_call(
        functools.partial(_inproj_kernel, tile_orders=tile_orders),
        grid=(T // TILE, n_tiles),
        in_specs=[pl.BlockSpec((TILE, LANES), functools.partial(slab_index, s=s))
                  for s in range(n_slab)] + [
            pl.BlockSpec((1, D_MODEL), lambda i, j: (0, 0)),
            pl.BlockSpec((D_MODEL, tn), lambda i, j: (0, j)),
            pl.BlockSpec((GLA_RANK, D_MODEL), lambda i, j: (W_GATE_COL // GLA_RANK, 0)),
        ],
        out_specs=[
            pl.BlockSpec((TILE, tn), lambda i, j: (i, j)),
            pl.BlockSpec((TILE, LANES), lambda i, j: (i, 0)),
        ],
        out_shape=[
            jax.ShapeDtypeStruct((T, Y_COLS), BF16),
            jax.ShapeDtypeStruct((T, LANES), F32),
        ],
        scratch_shapes=[pltpu.VMEM((N_ATT_GROUPS, TILE, D_MODEL), BF16),
                        pltpu.VMEM((TILE, LANES), F32),
                        pltpu.VMEM((TILE, LANES), F32)],
        compiler_params=_params(("arbitrary", "arbitrary")),
        name="inproj",
    )(*([x2] * n_slab), ln1, w_main, w_in_t)


def _gla_kernel(q_ref, k_ref, v_ref, r_ref, ha_ref, wa2_ref, ba_ref, gn_ref, o_ref, st_ref, la_scr,
                *, nchunk):
    C = GLA_CHUNK

    @pl.when(pl.program_id(1) == 0)
    def _():
        st_ref[...] = jnp.zeros_like(st_ref)

    row = lax.broadcasted_iota(jnp.int32, (C, C), 0)
    col = lax.broadcasted_iota(jnp.int32, (C, C), 1)
    causal = row >= col
    tri = causal.astype(BF16)

    a_hi, a_mid, _ = _split3(ha_ref[0])
    w_hi, w_mid, _ = _split3(wa2_ref[...])
    z = _dot(a_hi, w_hi) + (_dot(a_mid, w_hi) + _dot(a_hi, w_mid)) + ba_ref[...]
    la_scr[...] = (jnp.minimum(z, 0.0) - jnp.log(1.0 + jnp.exp(-jnp.abs(z)))) * (1.0 / GLA_TAU)

    H = range(GLA_HEADS)
    ck = [slice(h * GLA_DK, (h + 1) * GLA_DK) for h in H]
    cv = [slice(h * GLA_DV, (h + 1) * GLA_DV) for h in H]
    for c in range(nchunk):
        sl = pl.ds(c * C, C)
        hi, mid, _ = _split3(la_scr[sl, :])
        b = _dot(tri, hi) + _dot(tri, mid)
        decay = jnp.exp(b[C - 1:C, :])
        q_dec = q_ref[0, sl, :] * jnp.exp(b).astype(BF16)
        k_in = k_ref[0, sl, :] * jnp.exp(-b).astype(BF16)
        k_out = k_in * decay.astype(BF16)
        v = [v_ref[0, sl, cv[h]] for h in H]
        attn = [_dot_nt(q_dec[:, ck[h]], k_in[:, ck[h]]) for h in H]
        st = [st_ref[h] for h in H]
        o_inter = [_dot(q_dec[:, ck[h]], st[h].astype(BF16)) for h in H]
        upd = [_dot_tn(k_out[:, ck[h]], v[h]) for h in H]
        o = [_dot(jnp.where(causal, attn[h], 0.0).astype(BF16), v[h]) + o_inter[h] for h in H]
        for h in H:
            dcol = jnp.broadcast_to(decay[:, ck[h]], (GLA_DK, GLA_DK)).T
            st_ref[h] = st[h] * jnp.concatenate([dcol] * (GLA_DV // GLA_DK), axis=1) + upd[h]
        for h in H:
            r = r_ref[0, sl, cv[h]]
            o_ref[0, sl, cv[h]] = _rms(o[h], gn_ref[:, cv[h]]).astype(BF16) * (r * jax.nn.sigmoid(r))


def _gla(y3, ha3, wa2, ba, gn, tb=512):
    B, S, _ = y3.shape
    return pl.pallas_call(
        functools.partial(_gla_kernel, nchunk=tb // GLA_CHUNK),
        grid=(B, S // tb),
        in_specs=[
            pl.BlockSpec((1, tb, GLA_QK_W), lambda b, t: (b, t, COL_Q // GLA_QK_W)),
            pl.BlockSpec((1, tb, GLA_QK_W), lambda b, t: (b, t, COL_K // GLA_QK_W)),
            pl.BlockSpec((1, tb, GLA_V_W), lambda b, t: (b, t, COL_V // GLA_V_W)),
            pl.BlockSpec((1, tb, GLA_V_W), lambda b, t: (b, t, COL_R // GLA_V_W)),
            pl.BlockSpec((1, tb, LANES), lambda b, t: (b, t, 0)),
            pl.BlockSpec((LANES, GLA_QK_W), lambda b, t: (0, 0)),
            pl.BlockSpec((1, GLA_QK_W), lambda b, t: (0, 0)),
            pl.BlockSpec((1, GLA_V_W), lambda b, t: (0, 0)),
        ],
        out_specs=pl.BlockSpec((1, tb, GLA_V_W), lambda b, t: (b, t, 0)),
        out_shape=jax.ShapeDtypeStruct((B, S, GLA_V_W), BF16),
        scratch_shapes=[pltpu.VMEM((GLA_HEADS, GLA_DK, GLA_DV), F32), pltpu.VMEM((tb, GLA_QK_W), F32)],
        compiler_params=_params(("arbitrary", "arbitrary")),
        name="gla",
    )(y3, y3, y3, y3, ha3, wa2, ba, gn)


def _t5_causal_bucket(n):
    max_exact = REL_BUCKETS // 2
    nf = np.maximum(n, 1).astype(np.float32)
    large = max_exact + (np.log(nf / max_exact) / np.log(REL_MAX_DIST / max_exact)
                         * (REL_BUCKETS - max_exact)).astype(np.int32)
    large = np.minimum(large, REL_BUCKETS - 1)
    return np.where(n < max_exact, n, large).astype(np.int32)


def _bucket_table():
    qi = np.arange(ATT_BLOCK)[:, None]
    kj = np.arange(2 * ATT_BLOCK)[None, :]
    delta = qi + ATT_BLOCK - kj
    out = []
    for win, dil in ATT_GROUPS:
        band = (delta >= 0) & (delta <= win // dil)
        bucket = np.where(band, _t5_causal_bucket(np.maximum(delta, 0) * dil), -1)
        out.append(bucket)
    return np.stack(out).astype(np.int32)


def _att_unit(q, k, v, bias):
    BLK = ATT_BLOCK
    s = _dot_nt(q, k) + bias
    m = jnp.max(jnp.maximum(s[:, :BLK], s[:, BLK:]), axis=-1, keepdims=True)
    p = jnp.exp2((s - m).astype(BF16))
    v_ext = jnp.concatenate([v, jnp.ones_like(v)], axis=1)
    pv = _dot(p, v_ext)
    denom = pv[:, BLK:]
    return pv[:, :BLK] / denom, m + jnp.log2(denom)


def _att_kernel(tab_ref, bucket_ref,
                q0, k0, v0, kp0, vp0, q1, k1, v1, kp1, vp1, q2, k2, v2, kp2, vp2,
                o_ref, bias_scr, o_scr, lse_scr, *, tiles_per_batch):
    BLK = ATT_BLOCK
    ti = pl.program_id(0)
    h = pl.program_id(1)

    @pl.when((ti == 0) & (h == 0))
    def _():
        for g in range(N_ATT_GROUPS):
            bucket = bucket_ref[g]
            for hh in range(ATT_HEADS_PER_GROUP):
                bias = jnp.full(bucket.shape, NEG_INF, F32)
                for b in range(REL_BUCKETS):
                    bias = jnp.where(bucket == b, tab_ref[b, g * ATT_HEADS_PER_GROUP + hh] * LOG2_E, bias)
                bias_scr[g, hh] = bias

    lane = lax.broadcasted_iota(jnp.int32, (BLK, 2 * BLK), 1)
    no_prev = jnp.logical_and((ti % tiles_per_batch) == 0, lane < BLK)
    groups = ((q0, k0, v0, kp0, vp0), (q1, k1, v1, kp1, vp1), (q2, k2, v2, kp2, vp2))
    for g, (q, k, v, kp, vp) in enumerate(groups):
        dil = ATT_DILS[g]
        n_r = TILE // dil
        kp_rows = kp.shape[0]
        bias = bias_scr[g, h]
        bias_first = jnp.where(no_prev, NEG_INF, bias)
        for r in range(dil):
            for j in range(n_r // BLK):
                cur = pl.ds(r * n_r + j * BLK, BLK)
                if j > 0:
                    both = pl.ds(r * n_r + (j - 1) * BLK, 2 * BLK)
                    k_all, v_all, b = k[both, :], v[both, :], bias
                else:
                    prev = pl.ds((r * n_r + n_r - BLK) % kp_rows, BLK)
                    k_all = jnp.concatenate([kp[prev, :], k[cur, :]], axis=0)
                    v_all = jnp.concatenate([vp[prev, :], v[cur, :]], axis=0)
                    b = bias_first
                if dil == 1:
                    rows = pl.ds(j * BLK, BLK)
                else:
                    rows = pl.ds(dil * j * BLK + r, BLK, stride=dil)
                o_scr[g, rows, :], lse_scr[g, rows, :] = _att_unit(q[cur, :], k_all, v_all, b)

    for c in range(TILE // 256):
        rows = pl.ds(c * 256, 256)
        lse = [lse_scr[g, rows, :] for g in range(N_ATT_GROUPS)]
        top = functools.reduce(jnp.maximum, lse)
        w = [jnp.exp2(x - top) for x in lse]
        num = functools.reduce(lambda a, b: a + b, [w[g] * o_scr[g, rows, :] for g in range(N_ATT_GROUPS)])
        den = functools.reduce(lambda a, b: a + b, w)
        o_ref[rows, :] = (num / den).astype(BF16)


def _attention(y2, rel_bias, S):
    T = y2.shape[0]
    tiles_per_batch = S // TILE
    hd = ATT_HEAD_DIM

    def col(g, c):
        return (COL_ATT + g * ATT_GROUP_W + c * ATT_W) // hd

    def cur(g, c):
        return pl.BlockSpec((TILE, hd), lambda t, h: (t, col(g, c) + h))

    def prev_tile(g, c):
        return pl.BlockSpec((TILE, hd), lambda t, h: (jnp.maximum(t - 1, 0), col(g, c) + h))

    def prev_block(g, c):
        nb = TILE // ATT_BLOCK
        return pl.BlockSpec((ATT_BLOCK, hd), lambda t, h: (jnp.maximum(t * nb - 1, 0), col(g, c) + h))

    in_specs = [pl.BlockSpec(memory_space=pltpu.SMEM),
                pl.BlockSpec((N_ATT_GROUPS, ATT_BLOCK, 2 * ATT_BLOCK), lambda t, h: (0, 0, 0))]
    for g in range(N_ATT_GROUPS):
        prv = prev_block if ATT_DILS[g] == 1 else prev_tile
        in_specs += [cur(g, 0), cur(g, 1), cur(g, 2), prv(g, 1), prv(g, 2)]
    return pl.pallas_call(
        functools.partial(_att_kernel, tiles_per_batch=tiles_per_batch),
        grid=(T // TILE, ATT_HEADS_PER_GROUP),
        in_specs=in_specs,
        out_specs=pl.BlockSpec((TILE, hd), lambda t, h: (t, h)),
        out_shape=jax.ShapeDtypeStruct((T, ATT_W), BF16),
        scratch_shapes=[
            pltpu.VMEM((N_ATT_GROUPS, ATT_HEADS_PER_GROUP, ATT_BLOCK, 2 * ATT_BLOCK), F32),
            pltpu.VMEM((N_ATT_GROUPS, TILE, hd), F32),
            pltpu.VMEM((N_ATT_GROUPS, TILE, LANES), F32),
        ],
        compiler_params=_params(("arbitrary", "arbitrary")),
        name="attention",
    )(rel_bias, jnp.asarray(_bucket_table()), *([y2] * (5 * N_ATT_GROUPS)))


def _tail_kernel(x_ref, og_ref, oa_ref, ga_ref, gb_ref, p_ref, wog_ref, woa_ref, wout_ref,
                 ln2_ref, w1_ref, w2_ref, ln3_ref, wpg_ref, wpp_ref, lnf_ref, out_ref, *, tf):
    y_gla = _dot(og_ref[...], wog_ref[...])
    y_att = _dot(oa_ref[...], woa_ref[...])
    mix = (jax.nn.sigmoid(ga_ref[...].astype(F32)) * y_gla
           + jax.nn.sigmoid(gb_ref[...].astype(F32)) * y_att)
    x1 = x_ref[...] + _dot(mix.astype(BF16), wout_ref[...])

    h = _rms(x1, ln2_ref[...]).astype(BF16)
    x2 = x1
    for f in range(D_FF // tf):
        cols = pl.ds(f * tf, tf)
        u = jnp.maximum(_dot(h, w1_ref[:, cols]), 0.0)
        x2 = x2 + _dot((u * u).astype(BF16), w2_ref[cols, :])

    h3 = _rms(x2, ln3_ref[...]).astype(BF16)
    gate = jax.nn.sigmoid(_dot(h3, wpg_ref[...]))
    x3 = x2 + gate * _dot(p_ref[...].astype(BF16), wpp_ref[...])
    out_ref[...] = _rms(x3, lnf_ref[...])


def _tail(x2, y2, o_gla, o_att, p2, wog, woa, wout, ln2, w1, w2, ln3, wpg, wpp, lnf, tm=512, tf=1024):
    T = x2.shape[0]
    gcol = COL_GATE // D_MODEL
    row = lambda w: pl.BlockSpec((tm, w), lambda i: (i, 0))
    const = lambda a: pl.BlockSpec(a.shape, lambda i: (0, 0), pipeline_mode=pl.Buffered(1))
    return pl.pallas_call(
        functools.partial(_tail_kernel, tf=tf),
        grid=(T // tm,),
        in_specs=[row(D_MODEL), row(GLA_V_W), row(ATT_W),
                  pl.BlockSpec((tm, D_MODEL), lambda i: (i, gcol)),
                  pl.BlockSpec((tm, D_MODEL), lambda i: (i, gcol + 1)),
                  row(PLE_DIM),
                  const(wog), const(woa), const(wout), const(ln2), const(w1), const(w2),
                  const(ln3), const(wpg), const(wpp), const(lnf)],
        out_specs=row(D_MODEL),
        out_shape=jax.ShapeDtypeStruct((T, D_MODEL), F32),
        compiler_params=_params(("arbitrary",)),
        name="tail",
    )(x2, o_gla, o_att, y2, y2, p2, wog, woa, wout, ln2, w1, w2, ln3, wpg, wpp, lnf)


def _layer(x2, p2, B, S, ln1, w_in, w_a2, b_a, gla_gn, w_o_gla, w_o_attn, w_out,
           ln2, w_mlp1, w_mlp2, ln3, w_pp, w_pg, rel_bias, ln_out):
    wa2 = jnp.pad(w_a2, ((0, LANES - GLA_RANK), (0, 0)))

    w_in_t = w_in.T
    y2, ha = _inproj(x2, ln1[None], _wprep(w_in_t), w_in_t)
    o_gla = _gla(y2.reshape(B, S, Y_COLS), ha.reshape(B, S, LANES), wa2, b_a[None], gla_gn[None])
    o_att = _attention(y2, rel_bias, S)
    return _tail(x2, y2, o_gla.reshape(B * S, GLA_V_W), o_att, p2,
                 w_o_gla.astype(BF16), w_o_attn.astype(BF16), w_out.astype(BF16), ln2[None],
                 w_mlp1.astype(BF16), w_mlp2.astype(BF16), ln3[None],
                 w_pg.astype(BF16), w_pp.astype(BF16), ln_out[None])


def kernel(x, p, ln1, w_in, w_a2, b_a, gla_gn, w_o_gla, w_o_attn, w_out, ln2, w_mlp1, w_mlp2,
           ln3, w_pp, w_pg, rel_bias, ln_f):
    B, S, D = x.shape
    assert p.shape[0] == 1, "the final norm is fused into the single layer's last kernel"
    assert S % TILE == 0
    x2 = x.reshape(B * S, D)
    out = _layer(x2, p[0].reshape(B * S, PLE_DIM), B, S, ln1[0], w_in[0], w_a2[0], b_a[0],
                 gla_gn[0], w_o_gla[0], w_o_attn[0], w_out[0], ln2[0], w_mlp1[0], w_mlp2[0],
                 ln3[0], w_pp[0], w_pg[0], rel_bias, ln_f)
    return out.reshape(B, S, D)
```

```python
import functools

import numpy as np
import jax
import jax.numpy as jnp
from jax import lax
from jax.experimental import pallas as pl
from jax.experimental.pallas import tpu as pltpu

F32 = jnp.float32
BF16 = jnp.bfloat16

D_MODEL = 1024
PLE_DIM = 256
EPS = 1e-6
GLA_HEADS = 4
GLA_DK = 128
GLA_DV = 256
GLA_RANK = 16
GLA_TAU = 16.0
GLA_CHUNK = 64
GLA_QK_W = GLA_HEADS * GLA_DK
GLA_V_W = GLA_HEADS * GLA_DV
ATT_GROUPS = ((128, 1), (512, 4), (2048, 16))
ATT_DILS = tuple(d for _, d in ATT_GROUPS)
ATT_HEADS_PER_GROUP = 4
ATT_HEAD_DIM = 128
N_ATT_GROUPS = len(ATT_GROUPS)
ATT_W = ATT_HEADS_PER_GROUP * ATT_HEAD_DIM
ATT_BLOCK = 128
REL_BUCKETS = 32
REL_MAX_DIST = 2048
D_FF = 4 * D_MODEL
NEG_INF = -1e30
LOG2_E = 1.4426950408889634

LANES = 128

COL_Q = 0
COL_K = COL_Q + GLA_QK_W
COL_V = COL_K + GLA_QK_W
COL_R = COL_V + GLA_V_W
COL_GATE = COL_R + GLA_V_W
COL_ATT = COL_GATE + 2 * D_MODEL
ATT_GROUP_W = 3 * ATT_W
Y_COLS = COL_ATT + N_ATT_GROUPS * ATT_GROUP_W

TILE = max(ATT_DILS) * ATT_BLOCK

VMEM_LIMIT = 56 * 1024 * 1024


def _params(semantics):
    return pltpu.CompilerParams(dimension_semantics=semantics, vmem_limit_bytes=VMEM_LIMIT)


def _rms(x, g):
    return x * lax.rsqrt(jnp.mean(x * x, axis=-1, keepdims=True) + EPS) * g


def _dot(a, b):
    return jnp.dot(a, b, preferred_element_type=F32)


def _dot_nt(a, b):
    return lax.dot_general(a, b, (((1,), (1,)), ((), ())), preferred_element_type=F32)


def _dot_tn(a, b):
    return lax.dot_general(a, b, (((0,), (0,)), ((), ())), preferred_element_type=F32)


def _split3(g):
    hi = g.astype(BF16)
    r1 = g - hi.astype(F32)
    mid = r1.astype(BF16)
    lo = (r1 - mid.astype(F32)).astype(BF16)
    return hi, mid, lo


W_GATE_COL = COL_R + GLA_V_W
W_ATT_COL = W_GATE_COL + GLA_RANK
W_MIXGATE_COL = W_ATT_COL + N_ATT_GROUPS * ATT_GROUP_W


def _wprep_kernel(a_ref, b_ref, scale_ref, o_ref, *, n_aligned, tn):
    t = pl.program_id(0)

    @pl.when(t < n_aligned)
    def _():
        o_ref[...] = (a_ref[...].T * scale_ref[...]).astype(BF16)

    @pl.when(t >= n_aligned)
    def _():
        w = jnp.concatenate([a_ref[...], b_ref[...]], axis=0)[GLA_RANK:GLA_RANK + tn]
        o_ref[...] = (w.T * scale_ref[...]).astype(BF16)


def _wprep(w_in_t, tn=512):
    n_aligned = W_GATE_COL // tn
    n_att = (W_MIXGATE_COL - W_ATT_COL) // tn

    def out_col(t):
        return jnp.where(t < n_aligned, t,
                         jnp.where(t < n_aligned + n_att, t - n_aligned + COL_ATT // tn,
                                   t - n_aligned - n_att + COL_GATE // tn))

    col_scale = np.ones((1, Y_COLS), np.float32)
    col_scale[:, COL_Q:COL_K] = GLA_DK ** -0.5
    for g in range(N_ATT_GROUPS):
        col_scale[:, COL_ATT + g * ATT_GROUP_W:COL_ATT + g * ATT_GROUP_W + ATT_W] = ATT_HEAD_DIM ** -0.5 * LOG2_E
    return pl.pallas_call(
        functools.partial(_wprep_kernel, n_aligned=n_aligned, tn=tn),
        grid=(Y_COLS // tn,),
        in_specs=[
            pl.BlockSpec((tn, D_MODEL), lambda t: (t, 0)),
            pl.BlockSpec((GLA_RANK, D_MODEL), lambda t: ((t + 1) * (tn // GLA_RANK), 0)),
            pl.BlockSpec((1, tn), lambda t: (0, out_col(t))),
        ],
        out_specs=pl.BlockSpec((D_MODEL, tn), lambda t: (0, out_col(t))),
        out_shape=jax.ShapeDtypeStruct((D_MODEL, Y_COLS), BF16),
        compiler_params=_params(("arbitrary",)),
        name="wprep",
    )(w_in_t, w_in_t, jnp.asarray(col_scale))


def _inproj_kernel(*refs, tile_orders):
    n_slab = D_MODEL // LANES
    x_slabs = refs[:n_slab]
    ln_ref, w_ref, wa1_ref, y_ref, ha_ref, h_scr, inv_scr, tmp_scr = refs[n_slab:]
    j = pl.program_id(1)
    chunk = 256
    assert ATT_DILS == (1, 4, 16) and all(order == (0, 0) for order in tile_orders[:4])

    def project(d_left, d_right):
        half = y_ref.shape[1] // 2
        y_ref[:, :half] = _dot(h_scr[d_left], w_ref[:, :half]).astype(BF16)
        if d_right is not None:
            y_ref[:, half:] = _dot(h_scr[d_right], w_ref[:, half:]).astype(BF16)

    def normalise():
        for c in range(TILE // chunk):
            rows = pl.ds(c * chunk, chunk)
            parts = [xr[rows, :] for xr in x_slabs]
            sq = parts[0] * parts[0]
            for part in parts[1:]:
                sq = sq + part * part
            ms = jnp.sum(sq, axis=-1, keepdims=True) * (1.0 / D_MODEL)
            inv = jnp.broadcast_to(lax.rsqrt(ms + EPS), (chunk, LANES))
            inv_scr[rows, :] = inv
            for s, part in enumerate(parts):
                cols = pl.ds(s * LANES, LANES)
                h_scr[0, rows, cols] = (part * inv * ln_ref[:, cols]).astype(BF16)

    def regroup(slabs):
        n4, n16 = TILE // 4, TILE // 16
        for s in slabs:
            xr = x_slabs[s]
            cols = pl.ds(s * LANES, LANES)
            ln_s = ln_ref[:, cols]
            for a in range(4):
                for c in range(n4 // chunk):
                    src = pl.ds(a + 4 * c * chunk, chunk, stride=4)
                    dst = pl.ds(a * n4 + c * chunk, chunk)
                    hn = xr[src, :] * inv_scr[src, :] * ln_s
                    tmp_scr[dst, :] = hn
                    h_scr[1, dst, cols] = hn.astype(BF16)
            for a in range(4):
                for b in range(4):
                    src = pl.ds(a * n4 + b, n16, stride=4)
                    h_scr[2, pl.ds((a + 4 * b) * n16, n16), cols] = tmp_scr[src, :].astype(BF16)

    def gate_input():
        wa1_t = jnp.concatenate([wa1_ref[...], jnp.zeros((LANES - GLA_RANK, D_MODEL), F32)], axis=0)
        ha_ref[...] = _dot_nt(h_scr[0], wa1_t.astype(BF16))

    prep = {0: normalise, 1: functools.partial(regroup, range(0, n_slab // 2)),
            2: functools.partial(regroup, range(n_slab // 2, n_slab)), 3: gate_input}
    special = [t for t, (left, right) in enumerate(tile_orders) if t in prep or left != right]
    for t in special:
        @pl.when(j == t)
        def _(t=t):
            if t in prep:
                prep[t]()
            project(*tile_orders[t])

    @pl.when(functools.reduce(jnp.logical_and, [j != t for t in special]))
    def _():
        firsts = [min(t for t, o in enumerate(tile_orders) if o[0] == d) for d in (1, 2)]
        d = sum((j >= f).astype(jnp.int32) for f in firsts)
        project(d, d)


def _inproj(x2, ln1, w_main, w_in_t, tn=1024):
    T = x2.shape[0]
    n_slab = D_MODEL // LANES
    half = tn // 2
    bounds = [COL_ATT + ATT_GROUP_W * (g + 1) for g in range(N_ATT_GROUPS)]
    order_of = lambda col: None if col >= Y_COLS else sum(col >= b for b in bounds)
    n_tiles = pl.cdiv(Y_COLS, tn)
    tile_orders = tuple((order_of(t * tn), order_of(t * tn + half)) for t in range(n_tiles))
    assert all(b % half == 0 for b in bounds) and Y_COLS % half == 0

    def slab_index(i, j, s):
        moved = (j >= min(3 + s, n_tiles - 1)).astype(jnp.int32)
        return jnp.minimum(i + moved, T // TILE - 1), s

    return pl.pallas_call(
        functools.partial(_inproj_kernel, tile_orders=tile_orders),
        grid=(T // TILE, n_tiles),
        in_specs=[pl.BlockSpec((TILE, LANES), functools.partial(slab_index, s=s))
                  for s in range(n_slab)] + [
            pl.BlockSpec((1, D_MODEL), lambda i, j: (0, 0)),
            pl.BlockSpec((D_MODEL, tn), lambda i, j: (0, j)),
            pl.BlockSpec((GLA_RANK, D_MODEL), lambda i, j: (W_GATE_COL // GLA_RANK, 0)),
        ],
        out_specs=[
            pl.BlockSpec((TILE, tn), lambda i, j: (i, j)),
            pl.BlockSpec((TILE, LANES), lambda i, j: (i, 0)),
        ],
        out_shape=[
            jax.ShapeDtypeStruct((T, Y_COLS), BF16),
            jax.ShapeDtypeStruct((T, LANES), F32),
        ],
        scratch_shapes=[pltpu.VMEM((N_ATT_GROUPS, TILE, D_MODEL), BF16),
                        pltpu.VMEM((TILE, LANES), F32),
                        pltpu.VMEM((TILE, LANES), F32)],
        compiler_params=_params(("arbitrary", "arbitrary")),
        name="inproj",
    )(*([x2] * n_slab), ln1, w_main, w_in_t)


def _gla_kernel(q_ref, k_ref, v_ref, r_ref, ha_ref, wa2_ref, ba_ref, gn_ref, o_ref, st_ref, la_scr,
                *, nchunk):
    C = GLA_CHUNK

    @pl.when(pl.program_id(1) == 0)
    def _():
        st_ref[...] = jnp.zeros_like(st_ref)

    row = lax.broadcasted_iota(jnp.int32, (C, C), 0)
    col = lax.broadcasted_iota(jnp.int32, (C, C), 1)
    causal = row >= col
    tri = causal.astype(BF16)

    a_hi, a_mid, _ = _split3(ha_ref[0])
    w_hi, w_mid, _ = _split3(wa2_ref[...])
    z = _dot(a_hi, w_hi) + (_dot(a_mid, w_hi) + _dot(a_hi, w_mid)) + ba_ref[...]
    la_scr[...] = (jnp.minimum(z, 0.0) - jnp.log(1.0 + jnp.exp(-jnp.abs(z)))) * (1.0 / GLA_TAU)

    H = range(GLA_HEADS)
    ck = [slice(h * GLA_DK, (h + 1) * GLA_DK) for h in H]
    cv = [slice(h * GLA_DV, (h + 1) * GLA_DV) for h in H]
    for c in range(nchunk):
        sl = pl.ds(c * C, C)
        hi, mid, _ = _split3(la_scr[sl, :])
        b = _dot(tri, hi) + _dot(tri, mid)
        decay = jnp.exp(b[C - 1:C, :])
        q_dec = q_ref[0, sl, :] * jnp.exp(b).astype(BF16)
        k_in = k_ref[0, sl, :] * jnp.exp(-b).astype(BF16)
        k_out = k_in * decay.astype(BF16)
        v = [v_ref[0, sl, cv[h]] for h in H]
        attn = [_dot_nt(q_dec[:, ck[h]], k_in[:, ck[h]]) for h in H]
        st = [st_ref[h] for h in H]
        o_inter = [_dot(q_dec[:, ck[h]], st[h].astype(BF16)) for h in H]
        upd = [_dot_tn(k_out[:, ck[h]], v[h]) for h in H]
        o = [_dot(jnp.where(causal, attn[h], 0.0).astype(BF16), v[h]) + o_inter[h] for h in H]
        for h in H:
            dcol = jnp.broadcast_to(decay[:, ck[h]], (GLA_DK, GLA_DK)).T
            st_ref[h] = st[h] * jnp.concatenate([dcol] * (GLA_DV // GLA_DK), axis=1) + upd[h]
        for h in H:
            r = r_ref[0, sl, cv[h]]
            o_ref[0, sl, cv[h]] = _rms(o[h], gn_ref[:, cv[h]]).astype(BF16) * (r * jax.nn.sigmoid(r))


def _gla(y3, ha3, wa2, ba, gn, tb=512):
    B, S, _ = y3.shape
    return pl.pallas_call(
        functools.partial(_gla_kernel, nchunk=tb // GLA_CHUNK),
        grid=(B, S // tb),
        in_specs=[
            pl.BlockSpec((1, tb, GLA_QK_W), lambda b, t: (b, t, COL_Q // GLA_QK_W)),
            pl.BlockSpec((1, tb, GLA_QK_W), lambda b, t: (b, t, COL_K // GLA_QK_W)),
            pl.BlockSpec((1, tb, GLA_V_W), lambda b, t: (b, t, COL_V // GLA_V_W)),
            pl.BlockSpec((1, tb, GLA_V_W), lambda b, t: (b, t, COL_R // GLA_V_W)),
            pl.BlockSpec((1, tb, LANES), lambda b, t: (b, t, 0)),
            pl.BlockSpec((LANES, GLA_QK_W), lambda b, t: (0, 0)),
            pl.BlockSpec((1, GLA_QK_W), lambda b, t: (0, 0)),
            pl.BlockSpec((1, GLA_V_W), lambda b, t: (0, 0)),
        ],
        out_specs=pl.BlockSpec((1, tb, GLA_V_W), lambda b, t: (b, t, 0)),
        out_shape=jax.ShapeDtypeStruct((B, S, GLA_V_W), BF16),
        scratch_shapes=[pltpu.VMEM((GLA_HEADS, GLA_DK, GLA_DV), F32), pltpu.VMEM((tb, GLA_QK_W), F32)],
        compiler_params=_params(("arbitrary", "arbitrary")),
        name="gla",
    )(y3, y3, y3, y3, ha3, wa2, ba, gn)


def _t5_causal_bucket(n):
    max_exact = REL_BUCKETS // 2
    nf = np.maximum(n, 1).astype(np.float32)
    large = max_exact + (np.log(nf / max_exact) / np.log(REL_MAX_DIST / max_exact)
                         * (REL_BUCKETS - max_exact)).astype(np.int32)
    large = np.minimum(large, REL_BUCKETS - 1)
    return np.where(n < max_exact, n, large).astype(np.int32)


def _bucket_table():
    qi = np.arange(ATT_BLOCK)[:, None]
    kj = np.arange(2 * ATT_BLOCK)[None, :]
    delta = qi + ATT_BLOCK - kj
    out = []
    for win, dil in ATT_GROUPS:
        band = (delta >= 0) & (delta <= win // dil)
        bucket = np.where(band, _t5_causal_bucket(np.maximum(delta, 0) * dil), -1)
        out.append(bucket)
    return np.stack(out).astype(np.int32)


def _att_unit(q, k, v, bias):
    BLK = ATT_BLOCK
    s = _dot_nt(q, k) + bias
    m = jnp.max(jnp.maximum(s[:, :BLK], s[:, BLK:]), axis=-1, keepdims=True)
    p = jnp.exp2((s - m).astype(BF16))
    v_ext = jnp.concatenate([v, jnp.ones_like(v)], axis=1)
    pv = _dot(p, v_ext)
    denom = pv[:, BLK:]
    return pv[:, :BLK] / denom, m + jnp.log2(denom)


def _att_kernel(tab_ref, bucket_ref,
                q0, k0, v0, kp0, vp0, q1, k1, v1, kp1, vp1, q2, k2, v2, kp2, vp2,
                o_ref, bias_scr, o_scr, lse_scr, *, tiles_per_batch, heads_per_step):
    BLK = ATT_BLOCK
    hd = ATT_HEAD_DIM
    ti = pl.program_id(0)
    h0 = pl.program_id(1) * heads_per_step

    @pl.when((ti == 0) & (h0 == 0))
    def _():
        for g in range(N_ATT_GROUPS):
            bucket = bucket_ref[g]
            for hh in range(ATT_HEADS_PER_GROUP):
                bias = jnp.full(bucket.shape, NEG_INF, F32)
                for b in range(REL_BUCKETS):
                    bias = jnp.where(bucket == b, tab_ref[b, g * ATT_HEADS_PER_GROUP + hh] * LOG2_E, bias)
                bias_scr[g, hh] = bias

    lane = lax.broadcasted_iota(jnp.int32, (BLK, 2 * BLK), 1)
    no_prev = jnp.logical_and((ti % tiles_per_batch) == 0, lane < BLK)
    groups = ((q0, k0, v0, kp0, vp0), (q1, k1, v1, kp1, vp1), (q2, k2, v2, kp2, vp2))
    for hh in range(heads_per_step):
        hc = pl.ds(hh * hd, hd)
        for g, (q, k, v, kp, vp) in enumerate(groups):
            dil = ATT_DILS[g]
            n_r = TILE // dil
            kp_rows = kp.shape[0]
            bias = bias_scr[g, h0 + hh]
            bias_first = jnp.where(no_prev, NEG_INF, bias)
            for r in range(dil):
                for j in range(n_r // BLK):
                    cur = pl.ds(r * n_r + j * BLK, BLK)
                    if j > 0:
                        both = pl.ds(r * n_r + (j - 1) * BLK, 2 * BLK)
                        k_all, v_all, b = k[both, hc], v[both, hc], bias
                    else:
                        prev = pl.ds((r * n_r + n_r - BLK) % kp_rows, BLK)
                        k_all = jnp.concatenate([kp[prev, hc], k[cur, hc]], axis=0)
                        v_all = jnp.concatenate([vp[prev, hc], v[cur, hc]], axis=0)
                        b = bias_first
                    if dil == 1:
                        rows = pl.ds(j * BLK, BLK)
                    else:
                        rows = pl.ds(dil * j * BLK + r, BLK, stride=dil)
                    o_scr[g, rows, :], lse_scr[g, rows, :] = _att_unit(q[cur, hc], k_all, v_all, b)

        for c in range(TILE // 256):
            rows = pl.ds(c * 256, 256)
            lse = [lse_scr[g, rows, :] for g in range(N_ATT_GROUPS)]
            top = functools.reduce(jnp.maximum, lse)
            w = [jnp.exp2(x - top) for x in lse]
            num = functools.reduce(lambda a, b: a + b,
                                   [w[g] * o_scr[g, rows, :] for g in range(N_ATT_GROUPS)])
            den = functools.reduce(lambda a, b: a + b, w)
            o_ref[rows, hc] = (num / den).astype(BF16)


def _attention(y2, rel_bias, S, heads_per_step=2):
    T = y2.shape[0]
    tiles_per_batch = S // TILE
    hd = ATT_HEAD_DIM
    bw = heads_per_step * hd

    def col(g, c):
        return (COL_ATT + g * ATT_GROUP_W + c * ATT_W) // bw

    def cur(g, c):
        return pl.BlockSpec((TILE, bw), lambda t, h: (t, col(g, c) + h))

    def prev_tile(g, c):
        return pl.BlockSpec((TILE, bw), lambda t, h: (jnp.maximum(t - 1, 0), col(g, c) + h))

    def prev_block(g, c):
        nb = TILE // ATT_BLOCK
        return pl.BlockSpec((ATT_BLOCK, bw), lambda t, h: (jnp.maximum(t * nb - 1, 0), col(g, c) + h))

    in_specs = [pl.BlockSpec(memory_space=pltpu.SMEM),
                pl.BlockSpec((N_ATT_GROUPS, ATT_BLOCK, 2 * ATT_BLOCK), lambda t, h: (0, 0, 0))]
    for g in range(N_ATT_GROUPS):
        prv = prev_block if ATT_DILS[g] == 1 else prev_tile
        in_specs += [cur(g, 0), cur(g, 1), cur(g, 2), prv(g, 1), prv(g, 2)]
    return pl.pallas_call(
        functools.partial(_att_kernel, tiles_per_batch=tiles_per_batch, heads_per_step=heads_per_step),
        grid=(T // TILE, ATT_HEADS_PER_GROUP // heads_per_step),
        in_specs=in_specs,
        out_specs=pl.BlockSpec((TILE, bw), lambda t, h: (t, h)),
        out_shape=jax.ShapeDtypeStruct((T, ATT_W), BF16),
        scratch_shapes=[
            pltpu.VMEM((N_ATT_GROUPS, ATT_HEADS_PER_GROUP, ATT_BLOCK, 2 * ATT_BLOCK), F32),
            pltpu.VMEM((N_ATT_GROUPS, TILE, hd), F32),
            pltpu.VMEM((N_ATT_GROUPS, TILE, LANES), F32),
        ],
        compiler_params=_params(("arbitrary", "arbitrary")),
        name="attention",
    )(rel_bias, jnp.asarray(_bucket_table()), *([y2] * (5 * N_ATT_GROUPS)))


def _tail_kernel(x_ref, og_ref, oa_ref, ga_ref, gb_ref, p_ref, wog_ref, woa_ref, wout_ref,
                 ln2_ref, w1_ref, w2_ref, ln3_ref, wpg_ref, wpp_ref, lnf_ref, out_ref, *, tf):
    y_gla = _dot(og_ref[...], wog_ref[...])
    y_att = _dot(oa_ref[...], woa_ref[...])
    mix = (jax.nn.sigmoid(ga_ref[...].astype(F32)) * y_gla
           + jax.nn.sigmoid(gb_ref[...].astype(F32)) * y_att)
    x1 = x_ref[...] + _dot(mix.astype(BF16), wout_ref[...])

    h = _rms(x1, ln2_ref[...]).astype(BF16)
    x2 = x1
    for f in range(D_FF // tf):
        cols = pl.ds(f * tf, tf)
        u = jnp.maximum(_dot(h, w1_ref[:, cols]), 0.0)
        x2 = x2 + _dot((u * u).astype(BF16), w2_ref[cols, :])

    h3 = _rms(x2, ln3_ref[...]).astype(BF16)
    gate = jax.nn.sigmoid(_dot(h3, wpg_ref[...]))
    x3 = x2 + gate * _dot(p_ref[...].astype(BF16), wpp_ref[...])
    out_ref[...] = _rms(x3, lnf_ref[...])


def _tail(x2, y2, o_gla, o_att, p2, wog, woa, wout, ln2, w1, w2, ln3, wpg, wpp, lnf, tm=512, tf=1024):
    T = x2.shape[0]
    gcol = COL_GATE // D_MODEL
    row = lambda w: pl.BlockSpec((tm, w), lambda i: (i, 0))
    const = lambda a: pl.BlockSpec(a.shape, lambda i: (0, 0), pipeline_mode=pl.Buffered(1))
    return pl.pallas_call(
        functools.partial(_tail_kernel, tf=tf),
        grid=(T // tm,),
        in_specs=[row(D_MODEL), row(GLA_V_W), row(ATT_W),
                  pl.BlockSpec((tm, D_MODEL), lambda i: (i, gcol)),
                  pl.BlockSpec((tm, D_MODEL), lambda i: (i, gcol + 1)),
                  row(PLE_DIM),
                  const(wog), const(woa), const(wout), const(ln2), const(w1), const(w2),
                  const(ln3), const(wpg), const(wpp), const(lnf)],
        out_specs=row(D_MODEL),
        out_shape=jax.ShapeDtypeStruct((T, D_MODEL), F32),
        compiler_params=_params(("arbitrary",)),
        name="tail",
    )(x2, o_gla, o_att, y2, y2, p2, wog, woa, wout, ln2, w1, w2, ln3, wpg, wpp, lnf)


def _layer(x2, p2, B, S, ln1, w_in, w_a2, b_a, gla_gn, w_o_gla, w_o_attn, w_out,
           ln2, w_mlp1, w_mlp2, ln3, w_pp, w_pg, rel_bias, ln_out):
    wa2 = jnp.pad(w_a2, ((0, LANES - GLA_RANK), (0, 0)))

    w_in_t = w_in.T
    y2, ha = _inproj(x2, ln1[None], _wprep(w_in_t), w_in_t)
    o_gla = _gla(y2.reshape(B, S, Y_COLS), ha.reshape(B, S, LANES), wa2, b_a[None], gla_gn[None])
    o_att = _attention(y2, rel_bias, S)
    return _tail(x2, y2, o_gla.reshape(B * S, GLA_V_W), o_att, p2,
                 w_o_gla.astype(BF16), w_o_attn.astype(BF16), w_out.astype(BF16), ln2[None],
                 w_mlp1.astype(BF16), w_mlp2.astype(BF16), ln3[None],
                 w_pg.astype(BF16), w_pp.astype(BF16), ln_out[None])


def kernel(x, p, ln1, w_in, w_a2, b_a, gla_gn, w_o_gla, w_o_attn, w_out, ln2, w_mlp1, w_mlp2,
           ln3, w_pp, w_pg, rel_bias, ln_f):
    B, S, D = x.shape
    assert p.shape[0] == 1, "the final norm is fused into the single layer's last kernel"
    assert S % TILE == 0
    x2 = x.reshape(B * S, D)
    out = _layer(x2, p[0].reshape(B * S, PLE_DIM), B, S, ln1[0], w_in[0], w_a2[0], b_a[0],
                 gla_gn[0], w_o_gla[0], w_o_attn[0], w_out[0], ln2[0], w_mlp1[0], w_mlp2[0],
                 ln3[0], w_pp[0], w_pg[0], rel_bias, ln_f)
    return out.reshape(B, S, D)
```

```python
import functools

import numpy as np
import jax
import jax.numpy as jnp
from jax import lax
from jax.experimental import pallas as pl
from jax.experimental.pallas import tpu as pltpu

F32 = jnp.float32
BF16 = jnp.bfloat16

D_MODEL = 1024
PLE_DIM = 256
EPS = 1e-6
GLA_HEADS = 4
GLA_DK = 128
GLA_DV = 256
GLA_RANK = 16
GLA_TAU = 16.0
GLA_CHUNK = 64
GLA_QK_W = GLA_HEADS * GLA_DK
GLA_V_W = GLA_HEADS * GLA_DV
ATT_GROUPS = ((128, 1), (512, 4), (2048, 16))
ATT_DILS = tuple(d for _, d in ATT_GROUPS)
ATT_HEADS_PER_GROUP = 4
ATT_HEAD_DIM = 128
N_ATT_GROUPS = len(ATT_GROUPS)
ATT_W = ATT_HEADS_PER_GROUP * ATT_HEAD_DIM
ATT_BLOCK = 128
REL_BUCKETS = 32
REL_MAX_DIST = 2048
D_FF = 4 * D_MODEL
NEG_INF = -1e30
LOG2_E = 1.4426950408889634

LANES = 128

COL_Q = 0
COL_K = COL_Q + GLA_QK_W
COL_V = COL_K + GLA_QK_W
COL_R = COL_V + GLA_V_W
COL_GATE = COL_R + GLA_V_W
COL_ATT = COL_GATE + 2 * D_MODEL
ATT_GROUP_W = 3 * ATT_W
Y_COLS = COL_ATT + N_ATT_GROUPS * ATT_GROUP_W

TILE = max(ATT_DILS) * ATT_BLOCK

VMEM_LIMIT = 56 * 1024 * 1024


def _params(semantics):
    return pltpu.CompilerParams(dimension_semantics=semantics, vmem_limit_bytes=VMEM_LIMIT)


def _rms(x, g):
    return x * lax.rsqrt(jnp.mean(x * x, axis=-1, keepdims=True) + EPS) * g


def _dot(a, b):
    return jnp.dot(a, b, preferred_element_type=F32)


def _dot_nt(a, b):
    return lax.dot_general(a, b, (((1,), (1,)), ((), ())), preferred_element_type=F32)


def _dot_tn(a, b):
    return lax.dot_general(a, b, (((0,), (0,)), ((), ())), preferred_element_type=F32)


def _split3(g):
    hi = g.astype(BF16)
    r1 = g - hi.astype(F32)
    mid = r1.astype(BF16)
    lo = (r1 - mid.astype(F32)).astype(BF16)
    return hi, mid, lo


W_GATE_COL = COL_R + GLA_V_W
W_ATT_COL = W_GATE_COL + GLA_RANK
W_MIXGATE_COL = W_ATT_COL + N_ATT_GROUPS * ATT_GROUP_W


def _wprep_kernel(scale_ref, a_ref, b_ref, o_ref, *, n_aligned, tn):
    t = pl.program_id(0)
    scale = scale_ref[t]

    @pl.when(t < n_aligned)
    def _():
        o_ref[...] = (a_ref[...] * scale).astype(BF16)

    @pl.when(t >= n_aligned)
    def _():
        w = jnp.concatenate([a_ref[...], b_ref[...]], axis=0)[GLA_RANK:GLA_RANK + tn]
        o_ref[...] = (w * scale).astype(BF16)


def _wprep(w_in_t, tn=512):
    n_aligned = W_GATE_COL // tn
    n_att = (W_MIXGATE_COL - W_ATT_COL) // tn
    n_tiles = Y_COLS // tn

    def out_tile(t, where):
        return where(t < n_aligned, t,
                     where(t < n_aligned + n_att, t - n_aligned + COL_ATT // tn,
                           t - n_aligned - n_att + COL_GATE // tn))

    q_sections = [(COL_Q, COL_K, GLA_DK ** -0.5)] + [
        (COL_ATT + g * ATT_GROUP_W, COL_ATT + g * ATT_GROUP_W + ATT_W, ATT_HEAD_DIM ** -0.5 * LOG2_E)
        for g in range(N_ATT_GROUPS)]
    assert all(lo % tn == 0 and hi % tn == 0 for lo, hi, _ in q_sections)
    tile_scale = np.ones((n_tiles,), np.float32)
    for t in range(n_tiles):
        col = out_tile(t, lambda c, a, b: a if c else b) * tn
        for lo, hi, sc in q_sections:
            if lo <= col < hi:
                tile_scale[t] = sc
    out_index = lambda t: (out_tile(t, jnp.where), 0)
    return pl.pallas_call(
        functools.partial(_wprep_kernel, n_aligned=n_aligned, tn=tn),
        grid=(n_tiles,),
        in_specs=[
            pl.BlockSpec(memory_space=pltpu.SMEM),
            pl.BlockSpec((tn, D_MODEL), lambda t: (t, 0)),
            pl.BlockSpec((GLA_RANK, D_MODEL), lambda t: ((t + 1) * (tn // GLA_RANK), 0)),
        ],
        out_specs=pl.BlockSpec((tn, D_MODEL), out_index),
        out_shape=jax.ShapeDtypeStruct((Y_COLS, D_MODEL), BF16),
        compiler_params=_params(("arbitrary",)),
        name="wprep",
    )(jnp.asarray(tile_scale), w_in_t, w_in_t)


def _inproj_kernel(*refs, tile_orders):
    n_slab = D_MODEL // LANES
    x_slabs = refs[:n_slab]
    ln_ref, w_ref, wa1_ref, y_ref, ha_ref, h_scr, inv_scr, tmp_scr = refs[n_slab:]
    j = pl.program_id(1)
    chunk = 256
    assert ATT_DILS == (1, 4, 16) and all(order == (0, 0) for order in tile_orders[:4])

    def project(d_left, d_right):
        half = y_ref.shape[1] // 2
        y_ref[:, :half] = _dot_nt(h_scr[d_left], w_ref[:half, :]).astype(BF16)
        if d_right is not None:
            y_ref[:, half:] = _dot_nt(h_scr[d_right], w_ref[half:, :]).astype(BF16)

    def normalise():
        for c in range(TILE // chunk):
            rows = pl.ds(c * chunk, chunk)
            parts = [xr[rows, :] for xr in x_slabs]
            sq = parts[0] * parts[0]
            for part in parts[1:]:
                sq = sq + part * part
            ms = jnp.sum(sq, axis=-1, keepdims=True) * (1.0 / D_MODEL)
            inv = jnp.broadcast_to(lax.rsqrt(ms + EPS), (chunk, LANES))
            inv_scr[rows, :] = inv
            for s, part in enumerate(parts):
                cols = pl.ds(s * LANES, LANES)
                h_scr[0, rows, cols] = (part * inv * ln_ref[:, cols]).astype(BF16)

    def regroup(slabs):
        n4, n16 = TILE // 4, TILE // 16
        for s in slabs:
            xr = x_slabs[s]
            cols = pl.ds(s * LANES, LANES)
            ln_s = ln_ref[:, cols]
            for a in range(4):
                for c in range(n4 // chunk):
                    src = pl.ds(a + 4 * c * chunk, chunk, stride=4)
                    dst = pl.ds(a * n4 + c * chunk, chunk)
                    hn = xr[src, :] * inv_scr[src, :] * ln_s
                    tmp_scr[dst, :] = hn
                    h_scr[1, dst, cols] = hn.astype(BF16)
            for a in range(4):
                for b in range(4):
                    src = pl.ds(a * n4 + b, n16, stride=4)
                    h_scr[2, pl.ds((a + 4 * b) * n16, n16), cols] = tmp_scr[src, :].astype(BF16)

    def gate_input():
        wa1_t = jnp.concatenate([wa1_ref[...], jnp.zeros((LANES - GLA_RANK, D_MODEL), F32)], axis=0)
        ha_ref[...] = _dot_nt(h_scr[0], wa1_t.astype(BF16))

    prep = {0: normalise, 1: functools.partial(regroup, range(0, n_slab // 2)),
            2: functools.partial(regroup, range(n_slab // 2, n_slab)), 3: gate_input}
    special = [t for t, (left, right) in enumerate(tile_orders) if t in prep or left != right]
    for t in special:
        @pl.when(j == t)
        def _(t=t):
            if t in prep:
                prep[t]()
            project(*tile_orders[t])

    @pl.when(functools.reduce(jnp.logical_and, [j != t for t in special]))
    def _():
        firsts = [min(t for t, o in enumerate(tile_orders) if o[0] == d) for d in (1, 2)]
        d = sum((j >= f).astype(jnp.int32) for f in firsts)
        project(d, d)


def _inproj(x2, ln1, w_main, w_in_t, tn=1024):
    T = x2.shape[0]
    n_slab = D_MODEL // LANES
    half = tn // 2
    bounds = [COL_ATT + ATT_GROUP_W * (g + 1) for g in range(N_ATT_GROUPS)]
    order_of = lambda col: None if col >= Y_COLS else sum(col >= b for b in bounds)
    n_tiles = pl.cdiv(Y_COLS, tn)
    tile_orders = [(order_of(t * tn), order_of(t * tn + half)) for t in range(n_tiles)]
    assert all(b % half == 0 for b in bounds) and Y_COLS % half == 0
    tile_orders[-2:] = tile_orders[:-3:-1]
    tile_orders = tuple(tile_orders)

    def tile_of(j):
        return jnp.where(j == n_tiles - 2, n_tiles - 1, jnp.where(j == n_tiles - 1, n_tiles - 2, j))

    def slab_index(i, j, s):
        moved = (j >= min(3 + s, n_tiles - 1)).astype(jnp.int32)
        return jnp.minimum(i + moved, T // TILE - 1), s

    return pl.pallas_call(
        functools.partial(_inproj_kernel, tile_orders=tile_orders),
        grid=(T // TILE, n_tiles),
        in_specs=[pl.BlockSpec((TILE, LANES), functools.partial(slab_index, s=s))
                  for s in range(n_slab)] + [
            pl.BlockSpec((1, D_MODEL), lambda i, j: (0, 0)),
            pl.BlockSpec((tn, D_MODEL), lambda i, j: (tile_of(j), 0)),
            pl.BlockSpec((GLA_RANK, D_MODEL), lambda i, j: (W_GATE_COL // GLA_RANK, 0)),
        ],
        out_specs=[
            pl.BlockSpec((TILE, tn), lambda i, j: (i, tile_of(j))),
            pl.BlockSpec((TILE, LANES), lambda i, j: (i, 0)),
        ],
        out_shape=[
            jax.ShapeDtypeStruct((T, Y_COLS), BF16),
            jax.ShapeDtypeStruct((T, LANES), F32),
        ],
        scratch_shapes=[pltpu.VMEM((N_ATT_GROUPS, TILE, D_MODEL), BF16),
                        pltpu.VMEM((TILE, LANES), F32),
                        pltpu.VMEM((TILE, LANES), F32)],
        compiler_params=_params(("arbitrary", "arbitrary")),
        name="inproj",
    )(*([x2] * n_slab), ln1, w_main, w_in_t)


def _gla_kernel(q_ref, k_ref, v_ref, r_ref, ha_ref, wa2_ref, ba_ref, gn_ref, o_ref, st_ref, la_scr,
                *, nchunk):
    C = GLA_CHUNK

    @pl.when(pl.program_id(1) == 0)
    def _():
        st_ref[...] = jnp.zeros_like(st_ref)

    row = lax.broadcasted_iota(jnp.int32, (C, C), 0)
    col = lax.broadcasted_iota(jnp.int32, (C, C), 1)
    causal = row >= col
    tri = causal.astype(BF16)

    a_hi, a_mid, _ = _split3(ha_ref[0])
    w_hi, w_mid, _ = _split3(wa2_ref[...])
    z = _dot(a_hi, w_hi) + (_dot(a_mid, w_hi) + _dot(a_hi, w_mid)) + ba_ref[...]
    la_scr[...] = (jnp.minimum(z, 0.0) - jnp.log(1.0 + jnp.exp(-jnp.abs(z)))) * (1.0 / GLA_TAU)

    H = range(GLA_HEADS)
    ck = [slice(h * GLA_DK, (h + 1) * GLA_DK) for h in H]
    cv = [slice(h * GLA_DV, (h + 1) * GLA_DV) for h in H]
    for c in range(nchunk):
        sl = pl.ds(c * C, C)
        hi, mid, _ = _split3(la_scr[sl, :])
        b = _dot(tri, hi) + _dot(tri, mid)
        decay = jnp.exp(b[C - 1:C, :])
        q_dec = q_ref[0, sl, :] * jnp.exp(b).astype(BF16)
        k_in = k_ref[0, sl, :] * jnp.exp(-b).astype(BF16)
        k_out = k_in * decay.astype(BF16)
        v = [v_ref[0, sl, cv[h]] for h in H]
        attn = [_dot_nt(q_dec[:, ck[h]], k_in[:, ck[h]]) for h in H]
        st = [st_ref[h] for h in H]
        o_inter = [_dot(q_dec[:, ck[h]], st[h].astype(BF16)) for h in H]
        upd = [_dot_tn(k_out[:, ck[h]], v[h]) for h in H]
        o = [_dot(jnp.where(causal, attn[h], 0.0).astype(BF16), v[h]) + o_inter[h] for h in H]
        for h in H:
            dcol = jnp.broadcast_to(decay[:, ck[h]], (GLA_DK, GLA_DK)).T
            st_ref[h] = st[h] * jnp.concatenate([dcol] * (GLA_DV // GLA_DK), axis=1) + upd[h]
        for h in H:
            r = r_ref[0, sl, cv[h]]
            o_ref[0, sl, cv[h]] = _rms(o[h], gn_ref[:, cv[h]]).astype(BF16) * (r * jax.nn.sigmoid(r))


def _gla(y3, ha3, wa2, ba, gn, tb=512):
    B, S, _ = y3.shape
    return pl.pallas_call(
        functools.partial(_gla_kernel, nchunk=tb // GLA_CHUNK),
        grid=(B, S // tb),
        in_specs=[
            pl.BlockSpec((1, tb, GLA_QK_W), lambda b, t: (b, t, COL_Q // GLA_QK_W)),
            pl.BlockSpec((1, tb, GLA_QK_W), lambda b, t: (b, t, COL_K // GLA_QK_W)),
            pl.BlockSpec((1, tb, GLA_V_W), lambda b, t: (b, t, COL_V // GLA_V_W)),
            pl.BlockSpec((1, tb, GLA_V_W), lambda b, t: (b, t, COL_R // GLA_V_W)),
            pl.BlockSpec((1, tb, LANES), lambda b, t: (b, t, 0)),
            pl.BlockSpec((LANES, GLA_QK_W), lambda b, t: (0, 0)),
            pl.BlockSpec((1, GLA_QK_W), lambda b, t: (0, 0)),
            pl.BlockSpec((1, GLA_V_W), lambda b, t: (0, 0)),
        ],
        out_specs=pl.BlockSpec((1, tb, GLA_V_W), lambda b, t: (b, t, 0)),
        out_shape=jax.ShapeDtypeStruct((B, S, GLA_V_W), BF16),
        scratch_shapes=[pltpu.VMEM((GLA_HEADS, GLA_DK, GLA_DV), F32), pltpu.VMEM((tb, GLA_QK_W), F32)],
        compiler_params=_params(("arbitrary", "arbitrary")),
        name="gla",
    )(y3, y3, y3, y3, ha3, wa2, ba, gn)


def _t5_causal_bucket(n):
    max_exact = REL_BUCKETS // 2
    nf = np.maximum(n, 1).astype(np.float32)
    large = max_exact + (np.log(nf / max_exact) / np.log(REL_MAX_DIST / max_exact)
                         * (REL_BUCKETS - max_exact)).astype(np.int32)
    large = np.minimum(large, REL_BUCKETS - 1)
    return np.where(n < max_exact, n, large).astype(np.int32)


def _bucket_table():
    qi = np.arange(ATT_BLOCK)[:, None]
    kj = np.arange(2 * ATT_BLOCK)[None, :]
    delta = qi + ATT_BLOCK - kj
    out = []
    for win, dil in ATT_GROUPS:
        band = (delta >= 0) & (delta <= win // dil)
        bucket = np.where(band, _t5_causal_bucket(np.maximum(delta, 0) * dil), -1)
        out.append(bucket)
    return np.stack(out).astype(np.int32)


def _att_unit(q, k, v, bias):
    BLK = ATT_BLOCK
    s = _dot_nt(q, k) + bias
    m = jnp.max(jnp.maximum(s[:, :BLK], s[:, BLK:]), axis=-1, keepdims=True)
    p = jnp.exp2((s - m).astype(BF16))
    v_ext = jnp.concatenate([v, jnp.ones_like(v)], axis=1)
    pv = _dot(p, v_ext)
    denom = pv[:, BLK:]
    return pv[:, :BLK] / denom, m + jnp.log2(denom)


def _att_kernel(tab_ref, bucket_ref,
                q0, k0, v0, kp0, vp0, q1, k1, v1, kp1, vp1, q2, k2, v2, kp2, vp2,
                o_ref, bias_scr, o_scr, lse_scr, *, tiles_per_batch, heads_per_step):
    BLK = ATT_BLOCK
    hd = ATT_HEAD_DIM
    ti = pl.program_id(0)
    h0 = pl.program_id(1) * heads_per_step

    @pl.when((ti == 0) & (h0 == 0))
    def _():
        for g in range(N_ATT_GROUPS):
            bucket = bucket_ref[g]
            for hh in range(ATT_HEADS_PER_GROUP):
                bias = jnp.full(bucket.shape, NEG_INF, F32)
                for b in range(REL_BUCKETS):
                    bias = jnp.where(bucket == b, tab_ref[b, g * ATT_HEADS_PER_GROUP + hh] * LOG2_E, bias)
                bias_scr[g, hh] = bias

    lane = lax.broadcasted_iota(jnp.int32, (BLK, 2 * BLK), 1)
    no_prev = jnp.logical_and((ti % tiles_per_batch) == 0, lane < BLK)
    groups = ((q0, k0, v0, kp0, vp0), (q1, k1, v1, kp1, vp1), (q2, k2, v2, kp2, vp2))
    for hh in range(heads_per_step):
        hc = pl.ds(hh * hd, hd)
        for g, (q, k, v, kp, vp) in enumerate(groups):
            dil = ATT_DILS[g]
            n_r = TILE // dil
            kp_rows = kp.shape[0]
            bias = bias_scr[g, h0 + hh]
            bias_first = jnp.where(no_prev, NEG_INF, bias)
            for r in range(dil):
                for j in range(n_r // BLK):
                    cur = pl.ds(r * n_r + j * BLK, BLK)
                    if j > 0:
                        both = pl.ds(r * n_r + (j - 1) * BLK, 2 * BLK)
                        k_all, v_all, b = k[both, hc], v[both, hc], bias
                    else:
                        prev = pl.ds((r * n_r + n_r - BLK) % kp_rows, BLK)
                        k_all = jnp.concatenate([kp[prev, hc], k[cur, hc]], axis=0)
                        v_all = jnp.concatenate([vp[prev, hc], v[cur, hc]], axis=0)
                        b = bias_first
                    if dil == 1:
                        rows = pl.ds(j * BLK, BLK)
                    else:
                        rows = pl.ds(dil * j * BLK + r, BLK, stride=dil)
                    o_scr[g, rows, :], lse_scr[g, rows, :] = _att_unit(q[cur, hc], k_all, v_all, b)

        for c in range(TILE // 256):
            rows = pl.ds(c * 256, 256)
            lse = [lse_scr[g, rows, :] for g in range(N_ATT_GROUPS)]
            top = functools.reduce(jnp.maximum, lse)
            w = [jnp.exp2(x - top) for x in lse]
            num = functools.reduce(lambda a, b: a + b,
                                   [w[g] * o_scr[g, rows, :] for g in range(N_ATT_GROUPS)])
            den = functools.reduce(lambda a, b: a + b, w)
            o_ref[rows, hc] = (num / den).astype(BF16)


def _attention(y2, rel_bias, S, heads_per_step=2):
    T = y2.shape[0]
    tiles_per_batch = S // TILE
    hd = ATT_HEAD_DIM
    bw = heads_per_step * hd

    def col(g, c):
        return (COL_ATT + g * ATT_GROUP_W + c * ATT_W) // bw

    def cur(g, c):
        return pl.BlockSpec((TILE, bw), lambda t, h: (t, col(g, c) + h))

    def prev_tile(g, c):
        return pl.BlockSpec((TILE, bw), lambda t, h: (jnp.maximum(t - 1, 0), col(g, c) + h))

    def prev_block(g, c):
        nb = TILE // ATT_BLOCK
        return pl.BlockSpec((ATT_BLOCK, bw), lambda t, h: (jnp.maximum(t * nb - 1, 0), col(g, c) + h))

    in_specs = [pl.BlockSpec(memory_space=pltpu.SMEM),
                pl.BlockSpec((N_ATT_GROUPS, ATT_BLOCK, 2 * ATT_BLOCK), lambda t, h: (0, 0, 0))]
    for g in range(N_ATT_GROUPS):
        prv = prev_block if ATT_DILS[g] == 1 else prev_tile
        in_specs += [cur(g, 0), cur(g, 1), cur(g, 2), prv(g, 1), prv(g, 2)]
    return pl.pallas_call(
        functools.partial(_att_kernel, tiles_per_batch=tiles_per_batch, heads_per_step=heads_per_step),
        grid=(T // TILE, ATT_HEADS_PER_GROUP // heads_per_step),
        in_specs=in_specs,
        out_specs=pl.BlockSpec((TILE, bw), lambda t, h: (t, h)),
        out_shape=jax.ShapeDtypeStruct((T, ATT_W), BF16),
        scratch_shapes=[
            pltpu.VMEM((N_ATT_GROUPS, ATT_HEADS_PER_GROUP, ATT_BLOCK, 2 * ATT_BLOCK), F32),
            pltpu.VMEM((N_ATT_GROUPS, TILE, hd), F32),
            pltpu.VMEM((N_ATT_GROUPS, TILE, LANES), F32),
        ],
        compiler_params=_params(("arbitrary", "arbitrary")),
        name="attention",
    )(rel_bias, jnp.asarray(_bucket_table()), *([y2] * (5 * N_ATT_GROUPS)))


def _tail_kernel(x_ref, og_ref, oa_ref, ga_ref, gb_ref, p_ref, wog_ref, woa_ref, wout_ref,
                 ln2_ref, w1_ref, w2_ref, ln3_ref, wpg_ref, wpp_ref, lnf_ref, out_ref, *, tf):
    y_gla = _dot(og_ref[...], wog_ref[...])
    y_att = _dot(oa_ref[...], woa_ref[...])
    mix = (jax.nn.sigmoid(ga_ref[...].astype(F32)) * y_gla
           + jax.nn.sigmoid(gb_ref[...].astype(F32)) * y_att)
    x1 = x_ref[...] + _dot(mix.astype(BF16), wout_ref[...])

    h = _rms(x1, ln2_ref[...]).astype(BF16)
    x2 = x1
    for f in range(D_FF // tf):
        cols = pl.ds(f * tf, tf)
        u = jnp.maximum(_dot(h, w1_ref[:, cols]), 0.0)
        x2 = x2 + _dot((u * u).astype(BF16), w2_ref[cols, :])

    h3 = _rms(x2, ln3_ref[...]).astype(BF16)
    gate = jax.nn.sigmoid(_dot(h3, wpg_ref[...]))
    x3 = x2 + gate * _dot(p_ref[...].astype(BF16), wpp_ref[...])
    out_ref[...] = _rms(x3, lnf_ref[...])


def _tail(x2, y2, o_gla, o_att, p2, wog, woa, wout, ln2, w1, w2, ln3, wpg, wpp, lnf, tm=512, tf=1024):
    T = x2.shape[0]
    gcol = COL_GATE // D_MODEL
    row = lambda w: pl.BlockSpec((tm, w), lambda i: (i, 0))
    const = lambda a: pl.BlockSpec(a.shape, lambda i: (0, 0), pipeline_mode=pl.Buffered(1))
    return pl.pallas_call(
        functools.partial(_tail_kernel, tf=tf),
        grid=(T // tm,),
        in_specs=[row(D_MODEL), row(GLA_V_W), row(ATT_W),
                  pl.BlockSpec((tm, D_MODEL), lambda i: (i, gcol)),
                  pl.BlockSpec((tm, D_MODEL), lambda i: (i, gcol + 1)),
                  row(PLE_DIM),
                  const(wog), const(woa), const(wout), const(ln2), const(w1), const(w2),
                  const(ln3), const(wpg), const(wpp), const(lnf)],
        out_specs=row(D_MODEL),
        out_shape=jax.ShapeDtypeStruct((T, D_MODEL), F32),
        compiler_params=_params(("arbitrary",)),
        name="tail",
    )(x2, o_gla, o_att, y2, y2, p2, wog, woa, wout, ln2, w1, w2, ln3, wpg, wpp, lnf)


def _layer(x2, p2, B, S, ln1, w_in, w_a2, b_a, gla_gn, w_o_gla, w_o_attn, w_out,
           ln2, w_mlp1, w_mlp2, ln3, w_pp, w_pg, rel_bias, ln_out):
    wa2 = jnp.pad(w_a2, ((0, LANES - GLA_RANK), (0, 0)))

    w_in_t = w_in.T
    y2, ha = _inproj(x2, ln1[None], _wprep(w_in_t), w_in_t)
    o_gla = _gla(y2.reshape(B, S, Y_COLS), ha.reshape(B, S, LANES), wa2, b_a[None], gla_gn[None])
    o_att = _attention(y2, rel_bias, S)
    return _tail(x2, y2, o_gla.reshape(B * S, GLA_V_W), o_att, p2,
                 w_o_gla.astype(BF16), w_o_attn.astype(BF16), w_out.astype(BF16), ln2[None],
                 w_mlp1.astype(BF16), w_mlp2.astype(BF16), ln3[None],
                 w_pg.astype(BF16), w_pp.astype(BF16), ln_out[None])


def kernel(x, p, ln1, w_in, w_a2, b_a, gla_gn, w_o_gla, w_o_attn, w_out, ln2, w_mlp1, w_mlp2,
           ln3, w_pp, w_pg, rel_bias, ln_f):
    B, S, D = x.shape
    assert p.shape[0] == 1, "the final norm is fused into the single layer's last kernel"
    assert S % TILE == 0
    x2 = x.reshape(B * S, D)
    out = _layer(x2, p[0].reshape(B * S, PLE_DIM), B, S, ln1[0], w_in[0], w_a2[0], b_a[0],
                 gla_gn[0], w_o_gla[0], w_o_attn[0], w_out[0], ln2[0], w_mlp1[0], w_mlp2[0],
                 ln3[0], w_pp[0], w_pg[0], rel_bias, ln_f)
    return out.reshape(B, S, D)
```

```python
import functools

import numpy as np
import jax
import jax.numpy as jnp
from jax import lax
from jax.experimental import pallas as pl
from jax.experimental.pallas import tpu as pltpu

F32 = jnp.float32
BF16 = jnp.bfloat16

D_MODEL = 1024
PLE_DIM = 256
EPS = 1e-6
GLA_HEADS = 4
GLA_DK = 128
GLA_DV = 256
GLA_RANK = 16
GLA_TAU = 16.0
GLA_CHUNK = 64
GLA_QK_W = GLA_HEADS * GLA_DK
GLA_V_W = GLA_HEADS * GLA_DV
ATT_GROUPS = ((128, 1), (512, 4), (2048, 16))
ATT_DILS = tuple(d for _, d in ATT_GROUPS)
ATT_HEADS_PER_GROUP = 4
ATT_HEAD_DIM = 128
N_ATT_GROUPS = len(ATT_GROUPS)
ATT_W = ATT_HEADS_PER_GROUP * ATT_HEAD_DIM
ATT_BLOCK = 128
REL_BUCKETS = 32
REL_MAX_DIST = 2048
D_FF = 4 * D_MODEL
NEG_INF = -1e30
LOG2_E = 1.4426950408889634

LANES = 128

COL_Q = 0
COL_K = COL_Q + GLA_QK_W
COL_V = COL_K + GLA_QK_W
COL_R = COL_V + GLA_V_W
COL_GATE = COL_R + GLA_V_W
COL_ATT = COL_GATE + 2 * D_MODEL
ATT_GROUP_W = 3 * ATT_W
Y_COLS = COL_ATT + N_ATT_GROUPS * ATT_GROUP_W

TILE = max(ATT_DILS) * ATT_BLOCK

VMEM_LIMIT = 56 * 1024 * 1024


def _params(semantics):
    return pltpu.CompilerParams(dimension_semantics=semantics, vmem_limit_bytes=VMEM_LIMIT)


def _rms(x, g):
    return x * lax.rsqrt(jnp.mean(x * x, axis=-1, keepdims=True) + EPS) * g


def _dot(a, b):
    return jnp.dot(a, b, preferred_element_type=F32)


def _dot_nt(a, b):
    return lax.dot_general(a, b, (((1,), (1,)), ((), ())), preferred_element_type=F32)


def _dot_tn(a, b):
    return lax.dot_general(a, b, (((0,), (0,)), ((), ())), preferred_element_type=F32)


def _split3(g):
    hi = g.astype(BF16)
    r1 = g - hi.astype(F32)
    mid = r1.astype(BF16)
    lo = (r1 - mid.astype(F32)).astype(BF16)
    return hi, mid, lo


W_GATE_COL = COL_R + GLA_V_W
W_ATT_COL = W_GATE_COL + GLA_RANK
W_MIXGATE_COL = W_ATT_COL + N_ATT_GROUPS * ATT_GROUP_W


def _wprep_kernel(scale_ref, a_ref, b_ref, o_ref, *, n_aligned, tn):
    t = pl.program_id(0)
    scale = scale_ref[t]

    @pl.when(t < n_aligned)
    def _():
        o_ref[...] = (a_ref[...] * scale).astype(BF16)

    @pl.when(t >= n_aligned)
    def _():
        w = jnp.concatenate([a_ref[...], b_ref[...]], axis=0)[GLA_RANK:GLA_RANK + tn]
        o_ref[...] = (w * scale).astype(BF16)


def _wprep(w_in_t, tn=512):
    n_aligned = W_GATE_COL // tn
    n_att = (W_MIXGATE_COL - W_ATT_COL) // tn
    n_tiles = Y_COLS // tn

    def out_tile(t, where):
        return where(t < n_aligned, t,
                     where(t < n_aligned + n_att, t - n_aligned + COL_ATT // tn,
                           t - n_aligned - n_att + COL_GATE // tn))

    q_sections = [(COL_Q, COL_K, GLA_DK ** -0.5)] + [
        (COL_ATT + g * ATT_GROUP_W, COL_ATT + g * ATT_GROUP_W + ATT_W, ATT_HEAD_DIM ** -0.5 * LOG2_E)
        for g in range(N_ATT_GROUPS)]
    assert all(lo % tn == 0 and hi % tn == 0 for lo, hi, _ in q_sections)
    tile_scale = np.ones((n_tiles,), np.float32)
    for t in range(n_tiles):
        col = out_tile(t, lambda c, a, b: a if c else b) * tn
        for lo, hi, sc in q_sections:
            if lo <= col < hi:
                tile_scale[t] = sc
    out_index = lambda t: (out_tile(t, jnp.where), 0)
    return pl.pallas_call(
        functools.partial(_wprep_kernel, n_aligned=n_aligned, tn=tn),
        grid=(n_tiles,),
        in_specs=[
            pl.BlockSpec(memory_space=pltpu.SMEM),
            pl.BlockSpec((tn, D_MODEL), lambda t: (t, 0)),
            pl.BlockSpec((GLA_RANK, D_MODEL), lambda t: ((t + 1) * (tn // GLA_RANK), 0)),
        ],
        out_specs=pl.BlockSpec((tn, D_MODEL), out_index),
        out_shape=jax.ShapeDtypeStruct((Y_COLS, D_MODEL), BF16),
        compiler_params=_params(("arbitrary",)),
        name="wprep",
    )(jnp.asarray(tile_scale), w_in_t, w_in_t)


def _inproj_kernel(*refs, tile_orders):
    n_slab = D_MODEL // LANES
    x_slabs = refs[:n_slab]
    ln_ref, w_ref, wa1_ref, y_ref, ha_ref, h_scr, inv_scr, tmp_scr = refs[n_slab:]
    j = pl.program_id(1)
    chunk = 256
    assert ATT_DILS == (1, 4, 16) and all(order == (0, 0) for order in tile_orders[:4])

    def project(d_left, d_right):
        half = y_ref.shape[1] // 2
        y_ref[:, :half] = _dot_nt(h_scr[d_left], w_ref[:half, :]).astype(BF16)
        if d_right is not None:
            y_ref[:, half:] = _dot_nt(h_scr[d_right], w_ref[half:, :]).astype(BF16)

    def normalise():
        for c in range(TILE // chunk):
            rows = pl.ds(c * chunk, chunk)
            parts = [xr[rows, :] for xr in x_slabs]
            sq = parts[0] * parts[0]
            for part in parts[1:]:
                sq = sq + part * part
            ms = jnp.sum(sq, axis=-1, keepdims=True) * (1.0 / D_MODEL)
            inv = jnp.broadcast_to(lax.rsqrt(ms + EPS), (chunk, LANES))
            inv_scr[rows, :] = inv
            for s, part in enumerate(parts):
                cols = pl.ds(s * LANES, LANES)
                h_scr[0, rows, cols] = (part * inv * ln_ref[:, cols]).astype(BF16)

    def regroup(slabs):
        n4, n16 = TILE // 4, TILE // 16
        for s in slabs:
            xr = x_slabs[s]
            cols = pl.ds(s * LANES, LANES)
            ln_s = ln_ref[:, cols]
            for a in range(4):
                for c in range(n4 // chunk):
                    src = pl.ds(a + 4 * c * chunk, chunk, stride=4)
                    dst = pl.ds(a * n4 + c * chunk, chunk)
                    hn = xr[src, :] * inv_scr[src, :] * ln_s
                    tmp_scr[dst, :] = hn
                    h_scr[1, dst, cols] = hn.astype(BF16)
            for a in range(4):
                for b in range(4):
                    src = pl.ds(a * n4 + b, n16, stride=4)
                    h_scr[2, pl.ds((a + 4 * b) * n16, n16), cols] = tmp_scr[src, :].astype(BF16)

    def gate_input():
        wa1_t = jnp.concatenate([wa1_ref[...], jnp.zeros((LANES - GLA_RANK, D_MODEL), F32)], axis=0)
        ha_ref[...] = _dot_nt(h_scr[0], wa1_t.astype(BF16))

    prep = {0: normalise, 1: functools.partial(regroup, range(0, n_slab // 2)),
            2: functools.partial(regroup, range(n_slab // 2, n_slab)), 3: gate_input}
    special = [t for t, (left, right) in enumerate(tile_orders) if t in prep or left != right]
    for t in special:
        @pl.when(j == t)
        def _(t=t):
            if t in prep:
                prep[t]()
            project(*tile_orders[t])

    @pl.when(functools.reduce(jnp.logical_and, [j != t for t in special]))
    def _():
        firsts = [min(t for t, o in enumerate(tile_orders) if o[0] == d) for d in (1, 2)]
        d = sum((j >= f).astype(jnp.int32) for f in firsts)
        project(d, d)


def _inproj(x2, ln1, w_main, w_in_t, tn=1024):
    T = x2.shape[0]
    n_slab = D_MODEL // LANES
    half = tn // 2
    bounds = [COL_ATT + ATT_GROUP_W * (g + 1) for g in range(N_ATT_GROUPS)]
    order_of = lambda col: None if col >= Y_COLS else sum(col >= b for b in bounds)
    n_tiles = pl.cdiv(Y_COLS, tn)
    tile_orders = [(order_of(t * tn), order_of(t * tn + half)) for t in range(n_tiles)]
    assert all(b % half == 0 for b in bounds) and Y_COLS % half == 0
    tile_orders[-2:] = tile_orders[:-3:-1]
    tile_orders = tuple(tile_orders)

    def tile_of(j):
        return jnp.where(j == n_tiles - 2, n_tiles - 1, jnp.where(j == n_tiles - 1, n_tiles - 2, j))

    def slab_index(i, j, s):
        moved = (j >= min(3 + s, n_tiles - 1)).astype(jnp.int32)
        return jnp.minimum(i + moved, T // TILE - 1), s

    return pl.pallas_call(
        functools.partial(_inproj_kernel, tile_orders=tile_orders),
        grid=(T // TILE, n_tiles),
        in_specs=[pl.BlockSpec((TILE, LANES), functools.partial(slab_index, s=s))
                  for s in range(n_slab)] + [
            pl.BlockSpec((1, D_MODEL), lambda i, j: (0, 0)),
            pl.BlockSpec((tn, D_MODEL), lambda i, j: (tile_of(j), 0)),
            pl.BlockSpec((GLA_RANK, D_MODEL), lambda i, j: (W_GATE_COL // GLA_RANK, 0)),
        ],
        out_specs=[
            pl.BlockSpec((TILE, tn), lambda i, j: (i, tile_of(j))),
            pl.BlockSpec((TILE, LANES), lambda i, j: (i, 0)),
        ],
        out_shape=[
            jax.ShapeDtypeStruct((T, Y_COLS), BF16),
            jax.ShapeDtypeStruct((T, LANES), F32),
        ],
        scratch_shapes=[pltpu.VMEM((N_ATT_GROUPS, TILE, D_MODEL), BF16),
                        pltpu.VMEM((TILE, LANES), F32),
                        pltpu.VMEM((TILE, LANES), F32)],
        compiler_params=_params(("arbitrary", "arbitrary")),
        name="inproj",
    )(*([x2] * n_slab), ln1, w_main, w_in_t)


def _gla_kernel(q_ref, k_ref, v_ref, r_ref, ha_ref, wa2_ref, ba_ref, gn_ref, o_ref, st_ref, la_scr,
                *, nchunk):
    C = GLA_CHUNK

    @pl.when(pl.program_id(1) == 0)
    def _():
        st_ref[...] = jnp.zeros_like(st_ref)

    row = lax.broadcasted_iota(jnp.int32, (C, C), 0)
    col = lax.broadcasted_iota(jnp.int32, (C, C), 1)
    causal = row >= col
    tri = causal.astype(BF16)

    a_hi, a_mid, _ = _split3(ha_ref[0])
    w_hi, w_mid, _ = _split3(wa2_ref[...])
    z = _dot(a_hi, w_hi) + (_dot(a_mid, w_hi) + _dot(a_hi, w_mid)) + ba_ref[...]
    la_scr[...] = (jnp.minimum(z, 0.0) - jnp.log(1.0 + jnp.exp(-jnp.abs(z)))) * (1.0 / GLA_TAU)

    H = range(GLA_HEADS)
    ck = [slice(h * GLA_DK, (h + 1) * GLA_DK) for h in H]
    cv = [slice(h * GLA_DV, (h + 1) * GLA_DV) for h in H]
    for c in range(nchunk):
        sl = pl.ds(c * C, C)
        hi, mid, _ = _split3(la_scr[sl, :])
        b = _dot(tri, hi) + _dot(tri, mid)
        decay = jnp.exp(b[C - 1:C, :])
        q_dec = q_ref[0, sl, :] * jnp.exp(b).astype(BF16)
        k_in = k_ref[0, sl, :] * jnp.exp(-b).astype(BF16)
        k_out = k_in * decay.astype(BF16)
        v = [v_ref[0, sl, cv[h]] for h in H]
        attn = [_dot_nt(q_dec[:, ck[h]], k_in[:, ck[h]]) for h in H]
        st = [st_ref[h] for h in H]
        o_inter = [_dot(q_dec[:, ck[h]], st[h].astype(BF16)) for h in H]
        upd = [_dot_tn(k_out[:, ck[h]], v[h]) for h in H]
        o = [_dot(jnp.where(causal, attn[h], 0.0).astype(BF16), v[h]) + o_inter[h] for h in H]
        for h in H:
            dcol = jnp.broadcast_to(decay[:, ck[h]], (GLA_DK, GLA_DK)).T
            st_ref[h] = st[h] * jnp.concatenate([dcol] * (GLA_DV // GLA_DK), axis=1) + upd[h]
        for h in H:
            r = r_ref[0, sl, cv[h]]
            o_ref[0, sl, cv[h]] = _rms(o[h], gn_ref[:, cv[h]]).astype(BF16) * (r * jax.nn.sigmoid(r))


def _gla(y3, ha3, wa2, ba, gn, tb=1024):
    B, S, _ = y3.shape
    return pl.pallas_call(
        functools.partial(_gla_kernel, nchunk=tb // GLA_CHUNK),
        grid=(B, S // tb),
        in_specs=[
            pl.BlockSpec((1, tb, GLA_QK_W), lambda b, t: (b, t, COL_Q // GLA_QK_W)),
            pl.BlockSpec((1, tb, GLA_QK_W), lambda b, t: (b, t, COL_K // GLA_QK_W)),
            pl.BlockSpec((1, tb, GLA_V_W), lambda b, t: (b, t, COL_V // GLA_V_W)),
            pl.BlockSpec((1, tb, GLA_V_W), lambda b, t: (b, t, COL_R // GLA_V_W)),
            pl.BlockSpec((1, tb, LANES), lambda b, t: (b, t, 0)),
            pl.BlockSpec((LANES, GLA_QK_W), lambda b, t: (0, 0)),
            pl.BlockSpec((1, GLA_QK_W), lambda b, t: (0, 0)),
            pl.BlockSpec((1, GLA_V_W), lambda b, t: (0, 0)),
        ],
        out_specs=pl.BlockSpec((1, tb, GLA_V_W), lambda b, t: (b, t, 0)),
        out_shape=jax.ShapeDtypeStruct((B, S, GLA_V_W), BF16),
        scratch_shapes=[pltpu.VMEM((GLA_HEADS, GLA_DK, GLA_DV), F32), pltpu.VMEM((tb, GLA_QK_W), F32)],
        compiler_params=_params(("arbitrary", "arbitrary")),
        name="gla",
    )(y3, y3, y3, y3, ha3, wa2, ba, gn)


def _t5_causal_bucket(n):
    max_exact = REL_BUCKETS // 2
    nf = np.maximum(n, 1).astype(np.float32)
    large = max_exact + (np.log(nf / max_exact) / np.log(REL_MAX_DIST / max_exact)
                         * (REL_BUCKETS - max_exact)).astype(np.int32)
    large = np.minimum(large, REL_BUCKETS - 1)
    return np.where(n < max_exact, n, large).astype(np.int32)


def _bucket_table():
    qi = np.arange(ATT_BLOCK)[:, None]
    kj = np.arange(2 * ATT_BLOCK)[None, :]
    delta = qi + ATT_BLOCK - kj
    out = []
    for win, dil in ATT_GROUPS:
        band = (delta >= 0) & (delta <= win // dil)
        bucket = np.where(band, _t5_causal_bucket(np.maximum(delta, 0) * dil), -1)
        out.append(bucket)
    return np.stack(out).astype(np.int32)


def _att_unit(q, k, v, bias):
    BLK = ATT_BLOCK
    s = _dot_nt(q, k) + bias
    m = jnp.max(jnp.maximum(s[:, :BLK], s[:, BLK:]), axis=-1, keepdims=True)
    p = jnp.exp2((s - m).astype(BF16))
    v_ext = jnp.concatenate([v, jnp.ones_like(v)], axis=1)
    pv = _dot(p, v_ext)
    denom = pv[:, BLK:]
    return pv[:, :BLK] / denom, m + jnp.log2(denom)


def _att_kernel(tab_ref, bucket_ref,
                q0, k0, v0, kp0, vp0, q1, k1, v1, kp1, vp1, q2, k2, v2, kp2, vp2,
                o_ref, bias_scr, o_scr, lse_scr, *, tiles_per_batch, heads_per_step):
    BLK = ATT_BLOCK
    hd = ATT_HEAD_DIM
    ti = pl.program_id(0)
    h0 = pl.program_id(1) * heads_per_step

    @pl.when((ti == 0) & (h0 == 0))
    def _():
        for g in range(N_ATT_GROUPS):
            bucket = bucket_ref[g]
            for hh in range(ATT_HEADS_PER_GROUP):
                bias = jnp.full(bucket.shape, NEG_INF, F32)
                for b in range(REL_BUCKETS):
                    bias = jnp.where(bucket == b, tab_ref[b, g * ATT_HEADS_PER_GROUP + hh] * LOG2_E, bias)
                bias_scr[g, hh] = bias

    lane = lax.broadcasted_iota(jnp.int32, (BLK, 2 * BLK), 1)
    no_prev = jnp.logical_and((ti % tiles_per_batch) == 0, lane < BLK)
    groups = ((q0, k0, v0, kp0, vp0), (q1, k1, v1, kp1, vp1), (q2, k2, v2, kp2, vp2))
    for hh in range(heads_per_step):
        hc = pl.ds(hh * hd, hd)
        for g, (q, k, v, kp, vp) in enumerate(groups):
            dil = ATT_DILS[g]
            n_r = TILE // dil
            kp_rows = kp.shape[0]
            bias = bias_scr[g, h0 + hh]
            bias_first = jnp.where(no_prev, NEG_INF, bias)
            for r in range(dil):
                for j in range(n_r // BLK):
                    cur = pl.ds(r * n_r + j * BLK, BLK)
                    if j > 0:
                        both = pl.ds(r * n_r + (j - 1) * BLK, 2 * BLK)
                        k_all, v_all, b = k[both, hc], v[both, hc], bias
                    else:
                        prev = pl.ds((r * n_r + n_r - BLK) % kp_rows, BLK)
                        k_all = jnp.concatenate([kp[prev, hc], k[cur, hc]], axis=0)
                        v_all = jnp.concatenate([vp[prev, hc], v[cur, hc]], axis=0)
                        b = bias_first
                    if dil == 1:
                        rows = pl.ds(j * BLK, BLK)
                    else:
                        rows = pl.ds(dil * j * BLK + r, BLK, stride=dil)
                    o_scr[g, rows, :], lse_scr[g, rows, :] = _att_unit(q[cur, hc], k_all, v_all, b)

        for c in range(TILE // 256):
            rows = pl.ds(c * 256, 256)
            lse = [lse_scr[g, rows, :] for g in range(N_ATT_GROUPS)]
            top = functools.reduce(jnp.maximum, lse)
            w = [jnp.exp2(x - top) for x in lse]
            num = functools.reduce(lambda a, b: a + b,
                                   [w[g] * o_scr[g, rows, :] for g in range(N_ATT_GROUPS)])
            den = functools.reduce(lambda a, b: a + b, w)
            o_ref[rows, hc] = (num / den).astype(BF16)


def _attention(y2, rel_bias, S, heads_per_step=2):
    T = y2.shape[0]
    tiles_per_batch = S // TILE
    hd = ATT_HEAD_DIM
    bw = heads_per_step * hd

    def col(g, c):
        return (COL_ATT + g * ATT_GROUP_W + c * ATT_W) // bw

    def cur(g, c):
        return pl.BlockSpec((TILE, bw), lambda t, h: (t, col(g, c) + h))

    def prev_tile(g, c):
        return pl.BlockSpec((TILE, bw), lambda t, h: (jnp.maximum(t - 1, 0), col(g, c) + h))

    def prev_block(g, c):
        nb = TILE // ATT_BLOCK
        return pl.BlockSpec((ATT_BLOCK, bw), lambda t, h: (jnp.maximum(t * nb - 1, 0), col(g, c) + h))

    in_specs = [pl.BlockSpec(memory_space=pltpu.SMEM),
                pl.BlockSpec((N_ATT_GROUPS, ATT_BLOCK, 2 * ATT_BLOCK), lambda t, h: (0, 0, 0))]
    for g in range(N_ATT_GROUPS):
        prv = prev_block if ATT_DILS[g] == 1 else prev_tile
        in_specs += [cur(g, 0), cur(g, 1), cur(g, 2), prv(g, 1), prv(g, 2)]
    return pl.pallas_call(
        functools.partial(_att_kernel, tiles_per_batch=tiles_per_batch, heads_per_step=heads_per_step),
        grid=(T // TILE, ATT_HEADS_PER_GROUP // heads_per_step),
        in_specs=in_specs,
        out_specs=pl.BlockSpec((TILE, bw), lambda t, h: (t, h)),
        out_shape=jax.ShapeDtypeStruct((T, ATT_W), BF16),
        scratch_shapes=[
            pltpu.VMEM((N_ATT_GROUPS, ATT_HEADS_PER_GROUP, ATT_BLOCK, 2 * ATT_BLOCK), F32),
            pltpu.VMEM((N_ATT_GROUPS, TILE, hd), F32),
            pltpu.VMEM((N_ATT_GROUPS, TILE, LANES), F32),
        ],
        compiler_params=_params(("arbitrary", "arbitrary")),
        name="attention",
    )(rel_bias, jnp.asarray(_bucket_table()), *([y2] * (5 * N_ATT_GROUPS)))


def _tail_kernel(x_ref, og_ref, oa_ref, ga_ref, gb_ref, p_ref, wog_ref, woa_ref, wout_ref,
                 ln2_ref, w1_ref, w2_ref, ln3_ref, wpg_ref, wpp_ref, lnf_ref, out_ref, *, tf):
    y_gla = _dot(og_ref[...], wog_ref[...])
    y_att = _dot(oa_ref[...], woa_ref[...])
    mix = (jax.nn.sigmoid(ga_ref[...].astype(F32)) * y_gla
           + jax.nn.sigmoid(gb_ref[...].astype(F32)) * y_att)
    x1 = x_ref[...] + _dot(mix.astype(BF16), wout_ref[...])

    h = _rms(x1, ln2_ref[...]).astype(BF16)
    x2 = x1
    for f in range(D_FF // tf):
        cols = pl.ds(f * tf, tf)
        u = jnp.maximum(_dot(h, w1_ref[:, cols]), 0.0)
        x2 = x2 + _dot((u * u).astype(BF16), w2_ref[cols, :])

    h3 = _rms(x2, ln3_ref[...]).astype(BF16)
    gate = jax.nn.sigmoid(_dot(h3, wpg_ref[...]))
    x3 = x2 + gate * _dot(p_ref[...].astype(BF16), wpp_ref[...])
    out_ref[...] = _rms(x3, lnf_ref[...])


def _tail(x2, y2, o_gla, o_att, p2, wog, woa, wout, ln2, w1, w2, ln3, wpg, wpp, lnf, tm=512, tf=1024):
    T = x2.shape[0]
    gcol = COL_GATE // D_MODEL
    row = lambda w: pl.BlockSpec((tm, w), lambda i: (i, 0))
    const = lambda a: pl.BlockSpec(a.shape, lambda i: (0, 0), pipeline_mode=pl.Buffered(1))
    return pl.pallas_call(
        functools.partial(_tail_kernel, tf=tf),
        grid=(T // tm,),
        in_specs=[row(D_MODEL), row(GLA_V_W), row(ATT_W),
                  pl.BlockSpec((tm, D_MODEL), lambda i: (i, gcol)),
                  pl.BlockSpec((tm, D_MODEL), lambda i: (i, gcol + 1)),
                  row(PLE_DIM),
                  const(wog), const(woa), const(wout), const(ln2), const(w1), const(w2),
                  const(ln3), const(wpg), const(wpp), const(lnf)],
        out_specs=row(D_MODEL),
        out_shape=jax.ShapeDtypeStruct((T, D_MODEL), F32),
        compiler_params=_params(("arbitrary",)),
        name="tail",
    )(x2, o_gla, o_att, y2, y2, p2, wog, woa, wout, ln2, w1, w2, ln3, wpg, wpp, lnf)


def _layer(x2, p2, B, S, ln1, w_in, w_a2, b_a, gla_gn, w_o_gla, w_o_attn, w_out,
           ln2, w_mlp1, w_mlp2, ln3, w_pp, w_pg, rel_bias, ln_out):
    wa2 = jnp.pad(w_a2, ((0, LANES - GLA_RANK), (0, 0)))

    w_in_t = w_in.T
    y2, ha = _inproj(x2, ln1[None], _wprep(w_in_t), w_in_t)
    o_gla = _gla(y2.reshape(B, S, Y_COLS), ha.reshape(B, S, LANES), wa2, b_a[None], gla_gn[None])
    o_att = _attention(y2, rel_bias, S)
    return _tail(x2, y2, o_gla.reshape(B * S, GLA_V_W), o_att, p2,
                 w_o_gla.astype(BF16), w_o_attn.astype(BF16), w_out.astype(BF16), ln2[None],
                 w_mlp1.astype(BF16), w_mlp2.astype(BF16), ln3[None],
                 w_pg.astype(BF16), w_pp.astype(BF16), ln_out[None])


def kernel(x, p, ln1, w_in, w_a2, b_a, gla_gn, w_o_gla, w_o_attn, w_out, ln2, w_mlp1, w_mlp2,
           ln3, w_pp, w_pg, rel_bias, ln_f):
    B, S, D = x.shape
    assert p.shape[0] == 1, "the final norm is fused into the single layer's last kernel"
    assert S % TILE == 0
    x2 = x.reshape(B * S, D)
    out = _layer(x2, p[0].reshape(B * S, PLE_DIM), B, S, ln1[0], w_in[0], w_a2[0], b_a[0],
                 gla_gn[0], w_o_gla[0], w_o_attn[0], w_out[0], ln2[0], w_mlp1[0], w_mlp2[0],
                 ln3[0], w_pp[0], w_pg[0], rel_bias, ln_f)
    return out.reshape(B, S, D)
```

```python
import functools

import numpy as np
import jax
import jax.numpy as jnp
from jax import lax
from jax.experimental import pallas as pl
from jax.experimental.pallas import tpu as pltpu

F32 = jnp.float32
BF16 = jnp.bfloat16

D_MODEL = 1024
PLE_DIM = 256
EPS = 1e-6
GLA_HEADS = 4
GLA_DK = 128
GLA_DV = 256
GLA_RANK = 16
GLA_TAU = 16.0
GLA_CHUNK = 64
GLA_QK_W = GLA_HEADS * GLA_DK
GLA_V_W = GLA_HEADS * GLA_DV
ATT_GROUPS = ((128, 1), (512, 4), (2048, 16))
ATT_DILS = tuple(d for _, d in ATT_GROUPS)
ATT_HEADS_PER_GROUP = 4
ATT_HEAD_DIM = 128
N_ATT_GROUPS = len(ATT_GROUPS)
ATT_W = ATT_HEADS_PER_GROUP * ATT_HEAD_DIM
ATT_BLOCK = 128
REL_BUCKETS = 32
REL_MAX_DIST = 2048
D_FF = 4 * D_MODEL
NEG_INF = -1e30
LOG2_E = 1.4426950408889634

LANES = 128

COL_Q = 0
COL_K = COL_Q + GLA_QK_W
COL_V = COL_K + GLA_QK_W
COL_R = COL_V + GLA_V_W
COL_GATE = COL_R + GLA_V_W
COL_ATT = COL_GATE + 2 * D_MODEL
ATT_GROUP_W = 3 * ATT_W
Y_COLS = COL_ATT + N_ATT_GROUPS * ATT_GROUP_W

TILE = max(ATT_DILS) * ATT_BLOCK

VMEM_LIMIT = 56 * 1024 * 1024


def _params(semantics):
    return pltpu.CompilerParams(dimension_semantics=semantics, vmem_limit_bytes=VMEM_LIMIT)


def _rms(x, g):
    return x * lax.rsqrt(jnp.mean(x * x, axis=-1, keepdims=True) + EPS) * g


def _dot(a, b):
    return jnp.dot(a, b, preferred_element_type=F32)


def _dot_nt(a, b):
    return lax.dot_general(a, b, (((1,), (1,)), ((), ())), preferred_element_type=F32)


def _dot_tn(a, b):
    return lax.dot_general(a, b, (((0,), (0,)), ((), ())), preferred_element_type=F32)


def _split3(g):
    hi = g.astype(BF16)
    r1 = g - hi.astype(F32)
    mid = r1.astype(BF16)
    lo = (r1 - mid.astype(F32)).astype(BF16)
    return hi, mid, lo


W_GATE_COL = COL_R + GLA_V_W
W_ATT_COL = W_GATE_COL + GLA_RANK
W_MIXGATE_COL = W_ATT_COL + N_ATT_GROUPS * ATT_GROUP_W


def _wprep_kernel(scale_ref, a_ref, b_ref, o_ref, *, n_aligned, tn):
    t = pl.program_id(0)
    scale = scale_ref[t]

    @pl.when(t < n_aligned)
    def _():
        o_ref[...] = (a_ref[...] * scale).astype(BF16)

    @pl.when(t >= n_aligned)
    def _():
        w = jnp.concatenate([a_ref[...], b_ref[...]], axis=0)[GLA_RANK:GLA_RANK + tn]
        o_ref[...] = (w * scale).astype(BF16)


def _wprep(w_in_t, tn=512):
    n_aligned = W_GATE_COL // tn
    n_att = (W_MIXGATE_COL - W_ATT_COL) // tn
    n_tiles = Y_COLS // tn

    def out_tile(t, where):
        return where(t < n_aligned, t,
                     where(t < n_aligned + n_att, t - n_aligned + COL_ATT // tn,
                           t - n_aligned - n_att + COL_GATE // tn))

    q_sections = [(COL_Q, COL_K, GLA_DK ** -0.5)] + [
        (COL_ATT + g * ATT_GROUP_W, COL_ATT + g * ATT_GROUP_W + ATT_W, ATT_HEAD_DIM ** -0.5 * LOG2_E)
        for g in range(N_ATT_GROUPS)]
    assert all(lo % tn == 0 and hi % tn == 0 for lo, hi, _ in q_sections)
    tile_scale = np.ones((n_tiles,), np.float32)
    for t in range(n_tiles):
        col = out_tile(t, lambda c, a, b: a if c else b) * tn
        for lo, hi, sc in q_sections:
            if lo <= col < hi:
                tile_scale[t] = sc
    out_index = lambda t: (out_tile(t, jnp.where), 0)
    return pl.pallas_call(
        functools.partial(_wprep_kernel, n_aligned=n_aligned, tn=tn),
        grid=(n_tiles,),
        in_specs=[
            pl.BlockSpec(memory_space=pltpu.SMEM),
            pl.BlockSpec((tn, D_MODEL), lambda t: (t, 0)),
            pl.BlockSpec((GLA_RANK, D_MODEL), lambda t: ((t + 1) * (tn // GLA_RANK), 0)),
        ],
        out_specs=pl.BlockSpec((tn, D_MODEL), out_index),
        out_shape=jax.ShapeDtypeStruct((Y_COLS, D_MODEL), BF16),
        compiler_params=_params(("arbitrary",)),
        name="wprep",
    )(jnp.asarray(tile_scale), w_in_t, w_in_t)


def _inproj_kernel(*refs, tile_orders):
    n_slab = D_MODEL // LANES
    x_slabs = refs[:n_slab]
    ln_ref, w_ref, wa1_ref, y_ref, ha_ref, h_scr, inv_scr, tmp_scr = refs[n_slab:]
    j = pl.program_id(1)
    chunk = 256
    assert ATT_DILS == (1, 4, 16) and all(order == (0, 0) for order in tile_orders[:4])

    def project(d_left, d_right):
        half = y_ref.shape[1] // 2
        y_ref[:, :half] = _dot_nt(h_scr[d_left], w_ref[:half, :]).astype(BF16)
        if d_right is not None:
            y_ref[:, half:] = _dot_nt(h_scr[d_right], w_ref[half:, :]).astype(BF16)

    def normalise():
        for c in range(TILE // chunk):
            rows = pl.ds(c * chunk, chunk)
            parts = [xr[rows, :] for xr in x_slabs]
            sq = parts[0] * parts[0]
            for part in parts[1:]:
                sq = sq + part * part
            ms = jnp.sum(sq, axis=-1, keepdims=True) * (1.0 / D_MODEL)
            inv = jnp.broadcast_to(lax.rsqrt(ms + EPS), (chunk, LANES))
            inv_scr[rows, :] = inv
            for s, part in enumerate(parts):
                cols = pl.ds(s * LANES, LANES)
                h_scr[0, rows, cols] = (part * inv * ln_ref[:, cols]).astype(BF16)

    def regroup(slabs):
        n4, n16 = TILE // 4, TILE // 16
        for s in slabs:
            xr = x_slabs[s]
            cols = pl.ds(s * LANES, LANES)
            ln_s = ln_ref[:, cols]
            for a in range(4):
                for c in range(n4 // chunk):
                    src = pl.ds(a + 4 * c * chunk, chunk, stride=4)
                    dst = pl.ds(a * n4 + c * chunk, chunk)
                    hn = xr[src, :] * inv_scr[src, :] * ln_s
                    tmp_scr[dst, :] = hn
                    h_scr[1, dst, cols] = hn.astype(BF16)
            for a in range(4):
                for b in range(4):
                    src = pl.ds(a * n4 + b, n16, stride=4)
                    h_scr[2, pl.ds((a + 4 * b) * n16, n16), cols] = tmp_scr[src, :].astype(BF16)

    def gate_input():
        wa1_t = jnp.concatenate([wa1_ref[...], jnp.zeros((LANES - GLA_RANK, D_MODEL), F32)], axis=0)
        ha_ref[...] = _dot_nt(h_scr[0], wa1_t.astype(BF16))

    prep = {0: normalise, 1: functools.partial(regroup, range(0, n_slab // 2)),
            2: functools.partial(regroup, range(n_slab // 2, n_slab)), 3: gate_input}
    special = [t for t, (left, right) in enumerate(tile_orders) if t in prep or left != right]
    for t in special:
        @pl.when(j == t)
        def _(t=t):
            if t in prep:
                prep[t]()
            project(*tile_orders[t])

    @pl.when(functools.reduce(jnp.logical_and, [j != t for t in special]))
    def _():
        firsts = [min(t for t, o in enumerate(tile_orders) if o[0] == d) for d in (1, 2)]
        d = sum((j >= f).astype(jnp.int32) for f in firsts)
        project(d, d)


def _inproj(x2, ln1, w_main, w_in_t, tn=1024):
    T = x2.shape[0]
    n_slab = D_MODEL // LANES
    half = tn // 2
    bounds = [COL_ATT + ATT_GROUP_W * (g + 1) for g in range(N_ATT_GROUPS)]
    order_of = lambda col: None if col >= Y_COLS else sum(col >= b for b in bounds)
    n_tiles = pl.cdiv(Y_COLS, tn)
    tile_orders = [(order_of(t * tn), order_of(t * tn + half)) for t in range(n_tiles)]
    assert all(b % half == 0 for b in bounds) and Y_COLS % half == 0
    tile_orders[-2:] = tile_orders[:-3:-1]
    tile_orders = tuple(tile_orders)

    def tile_of(j):
        return jnp.where(j == n_tiles - 2, n_tiles - 1, jnp.where(j == n_tiles - 1, n_tiles - 2, j))

    def slab_index(i, j, s):
        moved = (j >= min(3 + s, n_tiles - 1)).astype(jnp.int32)
        return jnp.minimum(i + moved, T // TILE - 1), s

    return pl.pallas_call(
        functools.partial(_inproj_kernel, tile_orders=tile_orders),
        grid=(T // TILE, n_tiles),
        in_specs=[pl.BlockSpec((TILE, LANES), functools.partial(slab_index, s=s))
                  for s in range(n_slab)] + [
            pl.BlockSpec((1, D_MODEL), lambda i, j: (0, 0)),
            pl.BlockSpec((tn, D_MODEL), lambda i, j: (tile_of(j), 0)),
            pl.BlockSpec((GLA_RANK, D_MODEL), lambda i, j: (W_GATE_COL // GLA_RANK, 0)),
        ],
        out_specs=[
            pl.BlockSpec((TILE, tn), lambda i, j: (i, tile_of(j))),
            pl.BlockSpec((TILE, LANES), lambda i, j: (i, 0)),
        ],
        out_shape=[
            jax.ShapeDtypeStruct((T, Y_COLS), BF16),
            jax.ShapeDtypeStruct((T, LANES), F32),
        ],
        scratch_shapes=[pltpu.VMEM((N_ATT_GROUPS, TILE, D_MODEL), BF16),
                        pltpu.VMEM((TILE, LANES), F32),
                        pltpu.VMEM((TILE, LANES), F32)],
        compiler_params=_params(("arbitrary", "arbitrary")),
        name="inproj",
    )(*([x2] * n_slab), ln1, w_main, w_in_t)


def _gla_kernel(q_ref, k_ref, v_ref, r_ref, ha_ref, wa2_ref, ba_ref, gn_ref, o_ref, st_ref, la_scr,
                *, nchunk):
    C = GLA_CHUNK

    @pl.when(pl.program_id(1) == 0)
    def _():
        st_ref[...] = jnp.zeros_like(st_ref)

    row = lax.broadcasted_iota(jnp.int32, (C, C), 0)
    col = lax.broadcasted_iota(jnp.int32, (C, C), 1)
    causal = row >= col
    tri = causal.astype(BF16)

    a_hi, a_mid, _ = _split3(ha_ref[0])
    w_hi, w_mid, _ = _split3(wa2_ref[...])
    z = _dot(a_hi, w_hi) + (_dot(a_mid, w_hi) + _dot(a_hi, w_mid)) + ba_ref[...]
    la_scr[...] = (jnp.minimum(z, 0.0) - jnp.log(1.0 + jnp.exp(-jnp.abs(z)))) * (1.0 / GLA_TAU)

    H = range(GLA_HEADS)
    ck = [slice(h * GLA_DK, (h + 1) * GLA_DK) for h in H]
    cv = [slice(h * GLA_DV, (h + 1) * GLA_DV) for h in H]

    def cumsum(c):
        hi, mid, _ = _split3(la_scr[pl.ds(c * C, C), :])
        return _dot(tri, hi) + _dot(tri, mid)

    def products(c, b, st):
        sl = pl.ds(c * C, C)
        decay = jnp.exp(b[C - 1:C, :])
        q_dec = q_ref[0, sl, :] * jnp.exp(b).astype(BF16)
        k_in = k_ref[0, sl, :] * jnp.exp(-b).astype(BF16)
        k_out = k_in * decay.astype(BF16)
        v = [v_ref[0, sl, cv[h]] for h in H]
        attn = [_dot_nt(q_dec[:, ck[h]], k_in[:, ck[h]]) for h in H]
        o_inter = [_dot(q_dec[:, ck[h]], st[h].astype(BF16)) for h in H]
        upd = [_dot_tn(k_out[:, ck[h]], v[h]) for h in H]
        new_st = []
        for h in H:
            dcol = jnp.broadcast_to(decay[:, ck[h]], (GLA_DK, GLA_DK)).T
            new_st.append(st[h] * jnp.concatenate([dcol] * (GLA_DV // GLA_DK), axis=1) + upd[h])
        return new_st, (attn, o_inter, v)

    def output(c, attn, o_inter, v):
        sl = pl.ds(c * C, C)
        for h in H:
            o = _dot(jnp.where(causal, attn[h], 0.0).astype(BF16), v[h]) + o_inter[h]
            r = r_ref[0, sl, cv[h]]
            o_ref[0, sl, cv[h]] = _rms(o, gn_ref[:, cv[h]]).astype(BF16) * (r * jax.nn.sigmoid(r))

    st = [st_ref[h] for h in H]
    b = {0: cumsum(0)}
    if nchunk > 1:
        b[1] = cumsum(1)
    st, pending = products(0, b.pop(0), st)
    for c in range(nchunk):
        if c + 2 < nchunk:
            b[c + 2] = cumsum(c + 2)
        if c + 1 < nchunk:
            st, nxt = products(c + 1, b.pop(c + 1), st)
        output(c, *pending)
        if c + 1 < nchunk:
            pending = nxt
    for h in H:
        st_ref[h] = st[h]


def _gla(y3, ha3, wa2, ba, gn, tb=1024):
    B, S, _ = y3.shape
    return pl.pallas_call(
        functools.partial(_gla_kernel, nchunk=tb // GLA_CHUNK),
        grid=(B, S // tb),
        in_specs=[
            pl.BlockSpec((1, tb, GLA_QK_W), lambda b, t: (b, t, COL_Q // GLA_QK_W)),
            pl.BlockSpec((1, tb, GLA_QK_W), lambda b, t: (b, t, COL_K // GLA_QK_W)),
            pl.BlockSpec((1, tb, GLA_V_W), lambda b, t: (b, t, COL_V // GLA_V_W)),
            pl.BlockSpec((1, tb, GLA_V_W), lambda b, t: (b, t, COL_R // GLA_V_W)),
            pl.BlockSpec((1, tb, LANES), lambda b, t: (b, t, 0)),
            pl.BlockSpec((LANES, GLA_QK_W), lambda b, t: (0, 0)),
            pl.BlockSpec((1, GLA_QK_W), lambda b, t: (0, 0)),
            pl.BlockSpec((1, GLA_V_W), lambda b, t: (0, 0)),
        ],
        out_specs=pl.BlockSpec((1, tb, GLA_V_W), lambda b, t: (b, t, 0)),
        out_shape=jax.ShapeDtypeStruct((B, S, GLA_V_W), BF16),
        scratch_shapes=[pltpu.VMEM((GLA_HEADS, GLA_DK, GLA_DV), F32), pltpu.VMEM((tb, GLA_QK_W), F32)],
        compiler_params=_params(("arbitrary", "arbitrary")),
        name="gla",
    )(y3, y3, y3, y3, ha3, wa2, ba, gn)


def _t5_causal_bucket(n):
    max_exact = REL_BUCKETS // 2
    nf = np.maximum(n, 1).astype(np.float32)
    large = max_exact + (np.log(nf / max_exact) / np.log(REL_MAX_DIST / max_exact)
                         * (REL_BUCKETS - max_exact)).astype(np.int32)
    large = np.minimum(large, REL_BUCKETS - 1)
    return np.where(n < max_exact, n, large).astype(np.int32)


def _bucket_table():
    qi = np.arange(ATT_BLOCK)[:, None]
    kj = np.arange(2 * ATT_BLOCK)[None, :]
    delta = qi + ATT_BLOCK - kj
    out = []
    for win, dil in ATT_GROUPS:
        band = (delta >= 0) & (delta <= win // dil)
        bucket = np.where(band, _t5_causal_bucket(np.maximum(delta, 0) * dil), -1)
        out.append(bucket)
    return np.stack(out).astype(np.int32)


def _att_unit(q, k, v, bias):
    BLK = ATT_BLOCK
    s = _dot_nt(q, k) + bias
    m = jnp.max(jnp.maximum(s[:, :BLK], s[:, BLK:]), axis=-1, keepdims=True)
    p = jnp.exp2((s - m).astype(BF16))
    v_ext = jnp.concatenate([v, jnp.ones_like(v)], axis=1)
    pv = _dot(p, v_ext)
    denom = pv[:, BLK:]
    return pv[:, :BLK] / denom, m + jnp.log2(denom)


def _att_kernel(tab_ref, bucket_ref,
                q0, k0, v0, kp0, vp0, q1, k1, v1, kp1, vp1, q2, k2, v2, kp2, vp2,
                o_ref, bias_scr, o_scr, lse_scr, *, tiles_per_batch, heads_per_step):
    BLK = ATT_BLOCK
    hd = ATT_HEAD_DIM
    ti = pl.program_id(0)
    h0 = pl.program_id(1) * heads_per_step

    @pl.when((ti == 0) & (h0 == 0))
    def _():
        for g in range(N_ATT_GROUPS):
            bucket = bucket_ref[g]
            for hh in range(ATT_HEADS_PER_GROUP):
                bias = jnp.full(bucket.shape, NEG_INF, F32)
                for b in range(REL_BUCKETS):
                    bias = jnp.where(bucket == b, tab_ref[b, g * ATT_HEADS_PER_GROUP + hh] * LOG2_E, bias)
                bias_scr[g, hh] = bias

    lane = lax.broadcasted_iota(jnp.int32, (BLK, 2 * BLK), 1)
    no_prev = jnp.logical_and((ti % tiles_per_batch) == 0, lane < BLK)
    groups = ((q0, k0, v0, kp0, vp0), (q1, k1, v1, kp1, vp1), (q2, k2, v2, kp2, vp2))
    for hh in range(heads_per_step):
        hc = pl.ds(hh * hd, hd)
        for g, (q, k, v, kp, vp) in enumerate(groups):
            dil = ATT_DILS[g]
            n_r = TILE // dil
            kp_rows = kp.shape[0]
            bias = bias_scr[g, h0 + hh]
            bias_first = jnp.where(no_prev, NEG_INF, bias)
            for r in range(dil):
                for j in range(n_r // BLK):
                    cur = pl.ds(r * n_r + j * BLK, BLK)
                    if j > 0:
                        both = pl.ds(r * n_r + (j - 1) * BLK, 2 * BLK)
                        k_all, v_all, b = k[both, hc], v[both, hc], bias
                    else:
                        prev = pl.ds((r * n_r + n_r - BLK) % kp_rows, BLK)
                        k_all = jnp.concatenate([kp[prev, hc], k[cur, hc]], axis=0)
                        v_all = jnp.concatenate([vp[prev, hc], v[cur, hc]], axis=0)
                        b = bias_first
                    if dil == 1:
                        rows = pl.ds(j * BLK, BLK)
                    else:
                        rows = pl.ds(dil * j * BLK + r, BLK, stride=dil)
                    o_scr[g, rows, :], lse_scr[g, rows, :] = _att_unit(q[cur, hc], k_all, v_all, b)

        for c in range(TILE // 256):
            rows = pl.ds(c * 256, 256)
            lse = [lse_scr[g, rows, :] for g in range(N_ATT_GROUPS)]
            top = functools.reduce(jnp.maximum, lse)
            w = [jnp.exp2(x - top) for x in lse]
            num = functools.reduce(lambda a, b: a + b,
                                   [w[g] * o_scr[g, rows, :] for g in range(N_ATT_GROUPS)])
            den = functools.reduce(lambda a, b: a + b, w)
            o_ref[rows, hc] = (num / den).astype(BF16)


def _attention(y2, rel_bias, S, heads_per_step=2):
    T = y2.shape[0]
    tiles_per_batch = S // TILE
    hd = ATT_HEAD_DIM
    bw = heads_per_step * hd

    def col(g, c):
        return (COL_ATT + g * ATT_GROUP_W + c * ATT_W) // bw

    def cur(g, c):
        return pl.BlockSpec((TILE, bw), lambda t, h: (t, col(g, c) + h))

    def prev_tile(g, c):
        return pl.BlockSpec((TILE, bw), lambda t, h: (jnp.maximum(t - 1, 0), col(g, c) + h))

    def prev_block(g, c):
        nb = TILE // ATT_BLOCK
        return pl.BlockSpec((ATT_BLOCK, bw), lambda t, h: (jnp.maximum(t * nb - 1, 0), col(g, c) + h))

    in_specs = [pl.BlockSpec(memory_space=pltpu.SMEM),
                pl.BlockSpec((N_ATT_GROUPS, ATT_BLOCK, 2 * ATT_BLOCK), lambda t, h: (0, 0, 0))]
    for g in range(N_ATT_GROUPS):
        prv = prev_block if ATT_DILS[g] == 1 else prev_tile
        in_specs += [cur(g, 0), cur(g, 1), cur(g, 2), prv(g, 1), prv(g, 2)]
    return pl.pallas_call(
        functools.partial(_att_kernel, tiles_per_batch=tiles_per_batch, heads_per_step=heads_per_step),
        grid=(T // TILE, ATT_HEADS_PER_GROUP // heads_per_step),
        in_specs=in_specs,
        out_specs=pl.BlockSpec((TILE, bw), lambda t, h: (t, h)),
        out_shape=jax.ShapeDtypeStruct((T, ATT_W), BF16),
        scratch_shapes=[
            pltpu.VMEM((N_ATT_GROUPS, ATT_HEADS_PER_GROUP, ATT_BLOCK, 2 * ATT_BLOCK), F32),
            pltpu.VMEM((N_ATT_GROUPS, TILE, hd), F32),
            pltpu.VMEM((N_ATT_GROUPS, TILE, LANES), F32),
        ],
        compiler_params=_params(("arbitrary", "arbitrary")),
        name="attention",
    )(rel_bias, jnp.asarray(_bucket_table()), *([y2] * (5 * N_ATT_GROUPS)))


def _tail_kernel(x_ref, og_ref, oa_ref, ga_ref, gb_ref, p_ref, wog_ref, woa_ref, wout_ref,
                 ln2_ref, w1_ref, w2_ref, ln3_ref, wpg_ref, wpp_ref, lnf_ref, out_ref, *, tf):
    y_gla = _dot(og_ref[...], wog_ref[...])
    y_att = _dot(oa_ref[...], woa_ref[...])
    mix = (jax.nn.sigmoid(ga_ref[...].astype(F32)) * y_gla
           + jax.nn.sigmoid(gb_ref[...].astype(F32)) * y_att)
    x1 = x_ref[...] + _dot(mix.astype(BF16), wout_ref[...])

    h = _rms(x1, ln2_ref[...]).astype(BF16)
    x2 = x1
    for f in range(D_FF // tf):
        cols = pl.ds(f * tf, tf)
        u = jnp.maximum(_dot(h, w1_ref[:, cols]), 0.0)
        x2 = x2 + _dot((u * u).astype(BF16), w2_ref[cols, :])

    h3 = _rms(x2, ln3_ref[...]).astype(BF16)
    gate = jax.nn.sigmoid(_dot(h3, wpg_ref[...]))
    x3 = x2 + gate * _dot(p_ref[...].astype(BF16), wpp_ref[...])
    out_ref[...] = _rms(x3, lnf_ref[...])


def _tail(x2, y2, o_gla, o_att, p2, wog, woa, wout, ln2, w1, w2, ln3, wpg, wpp, lnf, tm=512, tf=1024):
    T = x2.shape[0]
    gcol = COL_GATE // D_MODEL
    row = lambda w: pl.BlockSpec((tm, w), lambda i: (i, 0))
    const = lambda a: pl.BlockSpec(a.shape, lambda i: (0, 0), pipeline_mode=pl.Buffered(1))
    return pl.pallas_call(
        functools.partial(_tail_kernel, tf=tf),
        grid=(T // tm,),
        in_specs=[row(D_MODEL), row(GLA_V_W), row(ATT_W),
                  pl.BlockSpec((tm, D_MODEL), lambda i: (i, gcol)),
                  pl.BlockSpec((tm, D_MODEL), lambda i: (i, gcol + 1)),
                  row(PLE_DIM),
                  const(wog), const(woa), const(wout), const(ln2), const(w1), const(w2),
                  const(ln3), const(wpg), const(wpp), const(lnf)],
        out_specs=row(D_MODEL),
        out_shape=jax.ShapeDtypeStruct((T, D_MODEL), F32),
        compiler_params=_params(("arbitrary",)),
        name="tail",
    )(x2, o_gla, o_att, y2, y2, p2, wog, woa, wout, ln2, w1, w2, ln3, wpg, wpp, lnf)


def _layer(x2, p2, B, S, ln1, w_in, w_a2, b_a, gla_gn, w_o_gla, w_o_attn, w_out,
           ln2, w_mlp1, w_mlp2, ln3, w_pp, w_pg, rel_bias, ln_out):
    wa2 = jnp.pad(w_a2, ((0, LANES - GLA_RANK), (0, 0)))

    w_in_t = w_in.T
    y2, ha = _inproj(x2, ln1[None], _wprep(w_in_t), w_in_t)
    o_gla = _gla(y2.reshape(B, S, Y_COLS), ha.reshape(B, S, LANES), wa2, b_a[None], gla_gn[None])
    o_att = _attention(y2, rel_bias, S)
    return _tail(x2, y2, o_gla.reshape(B * S, GLA_V_W), o_att, p2,
                 w_o_gla.astype(BF16), w_o_attn.astype(BF16), w_out.astype(BF16), ln2[None],
                 w_mlp1.astype(BF16), w_mlp2.astype(BF16), ln3[None],
                 w_pg.astype(BF16), w_pp.astype(BF16), ln_out[None])


def kernel(x, p, ln1, w_in, w_a2, b_a, gla_gn, w_o_gla, w_o_attn, w_out, ln2, w_mlp1, w_mlp2,
           ln3, w_pp, w_pg, rel_bias, ln_f):
    B, S, D = x.shape
    assert p.shape[0] == 1, "the final norm is fused into the single layer's last kernel"
    assert S % TILE == 0
    x2 = x.reshape(B * S, D)
    out = _layer(x2, p[0].reshape(B * S, PLE_DIM), B, S, ln1[0], w_in[0], w_a2[0], b_a[0],
                 gla_gn[0], w_o_gla[0], w_o_attn[0], w_out[0], ln2[0], w_mlp1[0], w_mlp2[0],
                 ln3[0], w_pp[0], w_pg[0], rel_bias, ln_f)
    return out.reshape(B, S, D)
```

```python
import functools

import numpy as np
import jax
import jax.numpy as jnp
from jax import lax
from jax.experimental import pallas as pl
from jax.experimental.pallas import tpu as pltpu

F32 = jnp.float32
BF16 = jnp.bfloat16

D_MODEL = 1024
PLE_DIM = 256
EPS = 1e-6
GLA_HEADS = 4
GLA_DK = 128
GLA_DV = 256
GLA_RANK = 16
GLA_TAU = 16.0
GLA_CHUNK = 64
GLA_QK_W = GLA_HEADS * GLA_DK
GLA_V_W = GLA_HEADS * GLA_DV
ATT_GROUPS = ((128, 1), (512, 4), (2048, 16))
ATT_DILS = tuple(d for _, d in ATT_GROUPS)
ATT_HEADS_PER_GROUP = 4
ATT_HEAD_DIM = 128
N_ATT_GROUPS = len(ATT_GROUPS)
ATT_W = ATT_HEADS_PER_GROUP * ATT_HEAD_DIM
ATT_BLOCK = 128
REL_BUCKETS = 32
REL_MAX_DIST = 2048
D_FF = 4 * D_MODEL
NEG_INF = -1e30
LOG2_E = 1.4426950408889634

LANES = 128

COL_Q = 0
COL_K = COL_Q + GLA_QK_W
COL_V = COL_K + GLA_QK_W
COL_R = COL_V + GLA_V_W
COL_GATE = COL_R + GLA_V_W
COL_ATT = COL_GATE + 2 * D_MODEL
ATT_GROUP_W = 3 * ATT_W
Y_COLS = COL_ATT + N_ATT_GROUPS * ATT_GROUP_W

TILE = max(ATT_DILS) * ATT_BLOCK

VMEM_LIMIT = 56 * 1024 * 1024


def _params(semantics):
    return pltpu.CompilerParams(dimension_semantics=semantics, vmem_limit_bytes=VMEM_LIMIT)


def _rms(x, g):
    return x * lax.rsqrt(jnp.mean(x * x, axis=-1, keepdims=True) + EPS) * g


def _dot(a, b):
    return jnp.dot(a, b, preferred_element_type=F32)


def _dot_nt(a, b):
    return lax.dot_general(a, b, (((1,), (1,)), ((), ())), preferred_element_type=F32)


def _dot_tn(a, b):
    return lax.dot_general(a, b, (((0,), (0,)), ((), ())), preferred_element_type=F32)


def _split3(g):
    hi = g.astype(BF16)
    r1 = g - hi.astype(F32)
    mid = r1.astype(BF16)
    lo = (r1 - mid.astype(F32)).astype(BF16)
    return hi, mid, lo


W_GATE_COL = COL_R + GLA_V_W
W_ATT_COL = W_GATE_COL + GLA_RANK
W_MIXGATE_COL = W_ATT_COL + N_ATT_GROUPS * ATT_GROUP_W


def _wprep_kernel(scale_ref, a_ref, b_ref, o_ref, *, n_aligned, tn):
    t = pl.program_id(0)
    scale = scale_ref[t]

    @pl.when(t < n_aligned)
    def _():
        o_ref[...] = (a_ref[...] * scale).astype(BF16)

    @pl.when(t >= n_aligned)
    def _():
        w = jnp.concatenate([a_ref[...], b_ref[...]], axis=0)[GLA_RANK:GLA_RANK + tn]
        o_ref[...] = (w * scale).astype(BF16)


def _wprep(w_in_t, tn=512):
    n_aligned = W_GATE_COL // tn
    n_att = (W_MIXGATE_COL - W_ATT_COL) // tn
    n_tiles = Y_COLS // tn

    def out_tile(t, where):
        return where(t < n_aligned, t,
                     where(t < n_aligned + n_att, t - n_aligned + COL_ATT // tn,
                           t - n_aligned - n_att + COL_GATE // tn))

    q_sections = [(COL_Q, COL_K, GLA_DK ** -0.5)] + [
        (COL_ATT + g * ATT_GROUP_W, COL_ATT + g * ATT_GROUP_W + ATT_W, ATT_HEAD_DIM ** -0.5 * LOG2_E)
        for g in range(N_ATT_GROUPS)]
    assert all(lo % tn == 0 and hi % tn == 0 for lo, hi, _ in q_sections)
    tile_scale = np.ones((n_tiles,), np.float32)
    for t in range(n_tiles):
        col = out_tile(t, lambda c, a, b: a if c else b) * tn
        for lo, hi, sc in q_sections:
            if lo <= col < hi:
                tile_scale[t] = sc
    out_index = lambda t: (out_tile(t, jnp.where), 0)
    return pl.pallas_call(
        functools.partial(_wprep_kernel, n_aligned=n_aligned, tn=tn),
        grid=(n_tiles,),
        in_specs=[
            pl.BlockSpec(memory_space=pltpu.SMEM),
            pl.BlockSpec((tn, D_MODEL), lambda t: (t, 0)),
            pl.BlockSpec((GLA_RANK, D_MODEL), lambda t: ((t + 1) * (tn // GLA_RANK), 0)),
        ],
        out_specs=pl.BlockSpec((tn, D_MODEL), out_index),
        out_shape=jax.ShapeDtypeStruct((Y_COLS, D_MODEL), BF16),
        compiler_params=_params(("arbitrary",)),
        name="wprep",
    )(jnp.asarray(tile_scale), w_in_t, w_in_t)


def _inproj_kernel(*refs, tile_orders):
    n_slab = D_MODEL // LANES
    x_slabs = refs[:n_slab]
    ln_ref, w_ref, wa1_ref, y_ref, ha_ref, h_scr, inv_scr, tmp_scr = refs[n_slab:]
    j = pl.program_id(1)
    chunk = 256
    assert ATT_DILS == (1, 4, 16) and all(order == (0, 0) for order in tile_orders[:4])

    def project(d_left, d_right):
        half = y_ref.shape[1] // 2
        y_ref[:, :half] = _dot_nt(h_scr[d_left], w_ref[:half, :]).astype(BF16)
        if d_right is not None:
            y_ref[:, half:] = _dot_nt(h_scr[d_right], w_ref[half:, :]).astype(BF16)

    def normalise():
        for c in range(TILE // chunk):
            rows = pl.ds(c * chunk, chunk)
            parts = [xr[rows, :] for xr in x_slabs]
            sq = parts[0] * parts[0]
            for part in parts[1:]:
                sq = sq + part * part
            ms = jnp.sum(sq, axis=-1, keepdims=True) * (1.0 / D_MODEL)
            inv = jnp.broadcast_to(lax.rsqrt(ms + EPS), (chunk, LANES))
            inv_scr[rows, :] = inv
            for s, part in enumerate(parts):
                cols = pl.ds(s * LANES, LANES)
                h_scr[0, rows, cols] = (part * inv * ln_ref[:, cols]).astype(BF16)

    def regroup(slabs):
        n4, n16 = TILE // 4, TILE // 16
        for s in slabs:
            xr = x_slabs[s]
            cols = pl.ds(s * LANES, LANES)
            ln_s = ln_ref[:, cols]
            for a in range(4):
                for c in range(n4 // chunk):
                    src = pl.ds(a + 4 * c * chunk, chunk, stride=4)
                    dst = pl.ds(a * n4 + c * chunk, chunk)
                    hn = xr[src, :] * inv_scr[src, :] * ln_s
                    tmp_scr[dst, :] = hn
                    h_scr[1, dst, cols] = hn.astype(BF16)
            for a in range(4):
                for b in range(4):
                    src = pl.ds(a * n4 + b, n16, stride=4)
                    h_scr[2, pl.ds((a + 4 * b) * n16, n16), cols] = tmp_scr[src, :].astype(BF16)

    def gate_input():
        wa1_t = jnp.concatenate([wa1_ref[...], jnp.zeros((LANES - GLA_RANK, D_MODEL), F32)], axis=0)
        ha_ref[...] = _dot_nt(h_scr[0], wa1_t.astype(BF16))

    prep = {0: normalise, 1: functools.partial(regroup, range(0, n_slab // 2)),
            2: functools.partial(regroup, range(n_slab // 2, n_slab)), 3: gate_input}
    special = [t for t, (left, right) in enumerate(tile_orders) if t in prep or left != right]
    for t in special:
        @pl.when(j == t)
        def _(t=t):
            if t in prep:
                prep[t]()
            project(*tile_orders[t])

    @pl.when(functools.reduce(jnp.logical_and, [j != t for t in special]))
    def _():
        firsts = [min(t for t, o in enumerate(tile_orders) if o[0] == d) for d in (1, 2)]
        d = sum((j >= f).astype(jnp.int32) for f in firsts)
        project(d, d)


def _inproj(x2, ln1, w_main, w_in_t, tn=1024):
    T = x2.shape[0]
    n_slab = D_MODEL // LANES
    half = tn // 2
    bounds = [COL_ATT + ATT_GROUP_W * (g + 1) for g in range(N_ATT_GROUPS)]
    order_of = lambda col: None if col >= Y_COLS else sum(col >= b for b in bounds)
    n_tiles = pl.cdiv(Y_COLS, tn)
    tile_orders = [(order_of(t * tn), order_of(t * tn + half)) for t in range(n_tiles)]
    assert all(b % half == 0 for b in bounds) and Y_COLS % half == 0
    tile_orders[-2:] = tile_orders[:-3:-1]
    tile_orders = tuple(tile_orders)

    def tile_of(j):
        return jnp.where(j == n_tiles - 2, n_tiles - 1, jnp.where(j == n_tiles - 1, n_tiles - 2, j))

    def slab_index(i, j, s):
        moved = (j >= min(3 + s, n_tiles - 1)).astype(jnp.int32)
        return jnp.minimum(i + moved, T // TILE - 1), s

    return pl.pallas_call(
        functools.partial(_inproj_kernel, tile_orders=tile_orders),
        grid=(T // TILE, n_tiles),
        in_specs=[pl.BlockSpec((TILE, LANES), functools.partial(slab_index, s=s))
                  for s in range(n_slab)] + [
            pl.BlockSpec((1, D_MODEL), lambda i, j: (0, 0)),
            pl.BlockSpec((tn, D_MODEL), lambda i, j: (tile_of(j), 0)),
            pl.BlockSpec((GLA_RANK, D_MODEL), lambda i, j: (W_GATE_COL // GLA_RANK, 0)),
        ],
        out_specs=[
            pl.BlockSpec((TILE, tn), lambda i, j: (i, tile_of(j))),
            pl.BlockSpec((TILE, LANES), lambda i, j: (i, 0)),
        ],
        out_shape=[
            jax.ShapeDtypeStruct((T, Y_COLS), BF16),
            jax.ShapeDtypeStruct((T, LANES), F32),
        ],
        scratch_shapes=[pltpu.VMEM((N_ATT_GROUPS, TILE, D_MODEL), BF16),
                        pltpu.VMEM((TILE, LANES), F32),
                        pltpu.VMEM((TILE, LANES), F32)],
        compiler_params=_params(("arbitrary", "arbitrary")),
        name="inproj",
    )(*([x2] * n_slab), ln1, w_main, w_in_t)


def _gla_kernel(q_ref, k_ref, v_ref, r_ref, ha_ref, wa2_ref, ba_ref, gn_ref, o_ref, st_ref, la_scr,
                *, nchunk):
    C = GLA_CHUNK

    @pl.when(pl.program_id(1) == 0)
    def _():
        st_ref[...] = jnp.zeros_like(st_ref)

    row = lax.broadcasted_iota(jnp.int32, (C, C), 0)
    col = lax.broadcasted_iota(jnp.int32, (C, C), 1)
    causal = row >= col
    tri = causal.astype(BF16)

    a_hi, a_mid, _ = _split3(ha_ref[0])
    w_hi, w_mid, _ = _split3(wa2_ref[...])
    z = _dot(a_hi, w_hi) + (_dot(a_mid, w_hi) + _dot(a_hi, w_mid)) + ba_ref[...]
    la_scr[...] = (jnp.minimum(z, 0.0) - jnp.log(1.0 + jnp.exp(-jnp.abs(z)))) * (1.0 / GLA_TAU)

    H = range(GLA_HEADS)
    ck = [slice(h * GLA_DK, (h + 1) * GLA_DK) for h in H]
    cv = [slice(h * GLA_DV, (h + 1) * GLA_DV) for h in H]

    def cumsum(c):
        hi, mid, _ = _split3(la_scr[pl.ds(c * C, C), :])
        return _dot(tri, hi) + _dot(tri, mid)

    def products(c, b, st):
        sl = pl.ds(c * C, C)
        decay = jnp.exp(b[C - 1:C, :])
        q_dec = q_ref[0, sl, :] * jnp.exp(b).astype(BF16)
        k_in = k_ref[0, sl, :] * jnp.exp(-b).astype(BF16)
        k_out = k_in * decay.astype(BF16)
        v = [v_ref[0, sl, cv[h]] for h in H]
        attn = [_dot_nt(q_dec[:, ck[h]], k_in[:, ck[h]]) for h in H]
        o_inter = [_dot(q_dec[:, ck[h]], st[h].astype(BF16)) for h in H]
        upd = [_dot_tn(k_out[:, ck[h]], v[h]) for h in H]
        new_st = []
        for h in H:
            dcol = jnp.broadcast_to(decay[:, ck[h]], (GLA_DK, GLA_DK)).T
            new_st.append(st[h] * jnp.concatenate([dcol] * (GLA_DV // GLA_DK), axis=1) + upd[h])
        return new_st, (attn, o_inter, v)

    def output(c, attn, o_inter, v):
        sl = pl.ds(c * C, C)
        for h in H:
            o = _dot(jnp.where(causal, attn[h], 0.0).astype(BF16), v[h]) + o_inter[h]
            r = r_ref[0, sl, cv[h]]
            o_ref[0, sl, cv[h]] = _rms(o, gn_ref[:, cv[h]]).astype(BF16) * (r * jax.nn.sigmoid(r))

    st = [st_ref[h] for h in H]
    b = {0: cumsum(0)}
    if nchunk > 1:
        b[1] = cumsum(1)
    st, pending = products(0, b.pop(0), st)
    for c in range(nchunk):
        if c + 2 < nchunk:
            b[c + 2] = cumsum(c + 2)
        if c + 1 < nchunk:
            st, nxt = products(c + 1, b.pop(c + 1), st)
        output(c, *pending)
        if c + 1 < nchunk:
            pending = nxt
    for h in H:
        st_ref[h] = st[h]


def _gla(y3, ha3, wa2, ba, gn, tb=1024):
    B, S, _ = y3.shape
    return pl.pallas_call(
        functools.partial(_gla_kernel, nchunk=tb // GLA_CHUNK),
        grid=(B, S // tb),
        in_specs=[
            pl.BlockSpec((1, tb, GLA_QK_W), lambda b, t: (b, t, COL_Q // GLA_QK_W)),
            pl.BlockSpec((1, tb, GLA_QK_W), lambda b, t: (b, t, COL_K // GLA_QK_W)),
            pl.BlockSpec((1, tb, GLA_V_W), lambda b, t: (b, t, COL_V // GLA_V_W)),
            pl.BlockSpec((1, tb, GLA_V_W), lambda b, t: (b, t, COL_R // GLA_V_W)),
            pl.BlockSpec((1, tb, LANES), lambda b, t: (b, t, 0)),
            pl.BlockSpec((LANES, GLA_QK_W), lambda b, t: (0, 0)),
            pl.BlockSpec((1, GLA_QK_W), lambda b, t: (0, 0)),
            pl.BlockSpec((1, GLA_V_W), lambda b, t: (0, 0)),
        ],
        out_specs=pl.BlockSpec((1, tb, GLA_V_W), lambda b, t: (b, t, 0)),
        out_shape=jax.ShapeDtypeStruct((B, S, GLA_V_W), BF16),
        scratch_shapes=[pltpu.VMEM((GLA_HEADS, GLA_DK, GLA_DV), F32), pltpu.VMEM((tb, GLA_QK_W), F32)],
        compiler_params=_params(("arbitrary", "arbitrary")),
        name="gla",
    )(y3, y3, y3, y3, ha3, wa2, ba, gn)


def _t5_causal_bucket(n):
    max_exact = REL_BUCKETS // 2
    nf = np.maximum(n, 1).astype(np.float32)
    large = max_exact + (np.log(nf / max_exact) / np.log(REL_MAX_DIST / max_exact)
                         * (REL_BUCKETS - max_exact)).astype(np.int32)
    large = np.minimum(large, REL_BUCKETS - 1)
    return np.where(n < max_exact, n, large).astype(np.int32)


def _bucket_table():
    qi = np.arange(ATT_BLOCK)[:, None]
    kj = np.arange(2 * ATT_BLOCK)[None, :]
    delta = qi + ATT_BLOCK - kj
    out = []
    for win, dil in ATT_GROUPS:
        band = (delta >= 0) & (delta <= win // dil)
        bucket = np.where(band, _t5_causal_bucket(np.maximum(delta, 0) * dil), -1)
        out.append(bucket)
    return np.stack(out).astype(np.int32)


def _att_unit(q, k, v, bias):
    BLK = ATT_BLOCK
    s = _dot_nt(q, k) + bias
    m = jnp.max(jnp.maximum(s[:, :BLK], s[:, BLK:]), axis=-1, keepdims=True)
    p = jnp.exp2((s - m).astype(BF16))
    v_ext = jnp.concatenate([v, jnp.ones_like(v)], axis=1)
    pv = _dot(p, v_ext)
    denom = pv[:, BLK:]
    return pv[:, :BLK] / denom, m + jnp.log2(denom)


def _att_kernel(tab_ref, bucket_ref, *refs, tiles_per_batch, heads_per_step, buckets_used, n_prev):
    groups, pos = [], 0
    for n in n_prev:
        q, k, v = refs[pos:pos + 3]
        groups.append((q, k, v, refs[pos + 3:pos + 3 + n], refs[pos + 3 + n:pos + 3 + 2 * n]))
        pos += 3 + 2 * n
    o_ref, bias_scr, o_scr, lse_scr = refs[pos:]
    BLK = ATT_BLOCK
    hd = ATT_HEAD_DIM
    ti = pl.program_id(0)
    h0 = pl.program_id(1) * heads_per_step

    @pl.when((ti == 0) & (h0 == 0))
    def _():
        for g in range(N_ATT_GROUPS):
            bucket = bucket_ref[g]
            bias = [jnp.full(bucket.shape, NEG_INF, F32) for _ in range(ATT_HEADS_PER_GROUP)]
            for b in buckets_used[g]:
                hit = bucket == b
                for hh in range(ATT_HEADS_PER_GROUP):
                    bias[hh] = jnp.where(hit, tab_ref[b, g * ATT_HEADS_PER_GROUP + hh] * LOG2_E, bias[hh])
            for hh in range(ATT_HEADS_PER_GROUP):
                bias_scr[g, hh] = bias[hh]

    lane = lax.broadcasted_iota(jnp.int32, (BLK, 2 * BLK), 1)
    no_prev = jnp.logical_and((ti % tiles_per_batch) == 0, lane < BLK)
    for hh in range(heads_per_step):
        hc = pl.ds(hh * hd, hd)
        for g, (q, k, v, kps, vps) in enumerate(groups):
            dil = ATT_DILS[g]
            n_r = TILE // dil
            bias = bias_scr[g, h0 + hh]
            bias_first = jnp.where(no_prev, NEG_INF, bias)
            for r in range(dil):
                for j in range(n_r // BLK):
                    cur = pl.ds(r * n_r + j * BLK, BLK)
                    if j > 0:
                        both = pl.ds(r * n_r + (j - 1) * BLK, 2 * BLK)
                        k_all, v_all, b = k[both, hc], v[both, hc], bias
                    else:
                        if len(kps) == dil:
                            kp, vp, prev = kps[r], vps[r], pl.ds(0, BLK)
                        else:
                            kp, vp, prev = kps[0], vps[0], pl.ds(r * n_r + n_r - BLK, BLK)
                        k_all = jnp.concatenate([kp[prev, hc], k[cur, hc]], axis=0)
                        v_all = jnp.concatenate([vp[prev, hc], v[cur, hc]], axis=0)
                        b = bias_first
                    if dil == 1:
                        rows = pl.ds(j * BLK, BLK)
                    else:
                        rows = pl.ds(dil * j * BLK + r, BLK, stride=dil)
                    o_scr[g, rows, :], lse_scr[g, rows, :] = _att_unit(q[cur, hc], k_all, v_all, b)

        for c in range(TILE // 256):
            rows = pl.ds(c * 256, 256)
            lse = [lse_scr[g, rows, :] for g in range(N_ATT_GROUPS)]
            top = functools.reduce(jnp.maximum, lse)
            w = [jnp.exp2(x - top) for x in lse]
            num = functools.reduce(lambda a, b: a + b,
                                   [w[g] * o_scr[g, rows, :] for g in range(N_ATT_GROUPS)])
            den = functools.reduce(lambda a, b: a + b, w)
            o_ref[rows, hc] = (num / den).astype(BF16)


def _attention(y2, rel_bias, S, heads_per_step=2):
    T = y2.shape[0]
    table = _bucket_table()
    tiles_per_batch = S // TILE
    hd = ATT_HEAD_DIM
    bw = heads_per_step * hd

    def col(g, c):
        return (COL_ATT + g * ATT_GROUP_W + c * ATT_W) // bw

    def cur(g, c):
        return pl.BlockSpec((TILE, bw), lambda t, h: (t, col(g, c) + h))

    def prev_tile(g, c):
        return [pl.BlockSpec((TILE, bw), lambda t, h: (jnp.maximum(t - 1, 0), col(g, c) + h))]

    def prev_blocks(g, c):
        nb, per_class = TILE // ATT_BLOCK, TILE // ATT_DILS[g] // ATT_BLOCK
        return [pl.BlockSpec((ATT_BLOCK, bw),
                             functools.partial(lambda t, h, last: (jnp.maximum((t - 1) * nb + last, 0), col(g, c) + h),
                                               last=(r + 1) * per_class - 1))
                for r in range(ATT_DILS[g])]

    prev_spec = [prev_blocks if d * ATT_BLOCK < TILE else prev_tile for d in ATT_DILS]
    in_specs = [pl.BlockSpec(memory_space=pltpu.SMEM),
                pl.BlockSpec((N_ATT_GROUPS, ATT_BLOCK, 2 * ATT_BLOCK), lambda t, h: (0, 0, 0))]
    for g in range(N_ATT_GROUPS):
        in_specs += [cur(g, 0), cur(g, 1), cur(g, 2)] + prev_spec[g](g, 1) + prev_spec[g](g, 2)
    n_prev = tuple(len(prev_spec[g](g, 1)) for g in range(N_ATT_GROUPS))
    return pl.pallas_call(
        functools.partial(_att_kernel, tiles_per_batch=tiles_per_batch, heads_per_step=heads_per_step,
                          buckets_used=tuple(tuple(int(b) for b in np.unique(t) if b >= 0) for t in table),
                          n_prev=n_prev),
        grid=(T // TILE, ATT_HEADS_PER_GROUP // heads_per_step),
        in_specs=in_specs,
        out_specs=pl.BlockSpec((TILE, bw), lambda t, h: (t, h)),
        out_shape=jax.ShapeDtypeStruct((T, ATT_W), BF16),
        scratch_shapes=[
            pltpu.VMEM((N_ATT_GROUPS, ATT_HEADS_PER_GROUP, ATT_BLOCK, 2 * ATT_BLOCK), F32),
            pltpu.VMEM((N_ATT_GROUPS, TILE, hd), F32),
            pltpu.VMEM((N_ATT_GROUPS, TILE, LANES), F32),
        ],
        compiler_params=_params(("arbitrary", "arbitrary")),
        name="attention",
    )(rel_bias, jnp.asarray(table), *([y2] * (len(in_specs) - 2)))


def _tail_kernel(x_ref, og_ref, oa_ref, ga_ref, gb_ref, p_ref, wog_ref, woa_ref, wout_ref,
                 ln2_ref, w1_ref, w2_ref, ln3_ref, wpg_ref, wpp_ref, lnf_ref, out_ref, *, tf):
    y_gla = _dot(og_ref[...], wog_ref[...])
    y_att = _dot(oa_ref[...], woa_ref[...])
    mix = (jax.nn.sigmoid(ga_ref[...].astype(F32)) * y_gla
           + jax.nn.sigmoid(gb_ref[...].astype(F32)) * y_att)
    x1 = x_ref[...] + _dot(mix.astype(BF16), wout_ref[...])

    h = _rms(x1, ln2_ref[...]).astype(BF16)
    x2 = x1
    for f in range(D_FF // tf):
        cols = pl.ds(f * tf, tf)
        u = jnp.maximum(_dot(h, w1_ref[:, cols]), 0.0)
        x2 = x2 + _dot((u * u).astype(BF16), w2_ref[cols, :])

    h3 = _rms(x2, ln3_ref[...]).astype(BF16)
    gate = jax.nn.sigmoid(_dot(h3, wpg_ref[...]))
    x3 = x2 + gate * _dot(p_ref[...].astype(BF16), wpp_ref[...])
    out_ref[...] = _rms(x3, lnf_ref[...])


def _tail(x2, y2, o_gla, o_att, p2, wog, woa, wout, ln2, w1, w2, ln3, wpg, wpp, lnf, tm=512, tf=1024):
    T = x2.shape[0]
    gcol = COL_GATE // D_MODEL
    row = lambda w: pl.BlockSpec((tm, w), lambda i: (i, 0))
    const = lambda a: pl.BlockSpec(a.shape, lambda i: (0, 0), pipeline_mode=pl.Buffered(1))
    return pl.pallas_call(
        functools.partial(_tail_kernel, tf=tf),
        grid=(T // tm,),
        in_specs=[row(D_MODEL), row(GLA_V_W), row(ATT_W),
                  pl.BlockSpec((tm, D_MODEL), lambda i: (i, gcol)),
                  pl.BlockSpec((tm, D_MODEL), lambda i: (i, gcol + 1)),
                  row(PLE_DIM),
                  const(wog), const(woa), const(wout), const(ln2), const(w1), const(w2),
                  const(ln3), const(wpg), const(wpp), const(lnf)],
        out_specs=row(D_MODEL),
        out_shape=jax.ShapeDtypeStruct((T, D_MODEL), F32),
        compiler_params=_params(("arbitrary",)),
        name="tail",
    )(x2, o_gla, o_att, y2, y2, p2, wog, woa, wout, ln2, w1, w2, ln3, wpg, wpp, lnf)


def _layer(x2, p2, B, S, ln1, w_in, w_a2, b_a, gla_gn, w_o_gla, w_o_attn, w_out,
           ln2, w_mlp1, w_mlp2, ln3, w_pp, w_pg, rel_bias, ln_out):
    wa2 = jnp.pad(w_a2, ((0, LANES - GLA_RANK), (0, 0)))

    w_in_t = w_in.T
    y2, ha = _inproj(x2, ln1[None], _wprep(w_in_t), w_in_t)
    o_gla = _gla(y2.reshape(B, S, Y_COLS), ha.reshape(B, S, LANES), wa2, b_a[None], gla_gn[None])
    o_att = _attention(y2, rel_bias, S)
    return _tail(x2, y2, o_gla.reshape(B * S, GLA_V_W), o_att, p2,
                 w_o_gla.astype(BF16), w_o_attn.astype(BF16), w_out.astype(BF16), ln2[None],
                 w_mlp1.astype(BF16), w_mlp2.astype(BF16), ln3[None],
                 w_pg.astype(BF16), w_pp.astype(BF16), ln_out[None])


def kernel(x, p, ln1, w_in, w_a2, b_a, gla_gn, w_o_gla, w_o_attn, w_out, ln2, w_mlp1, w_mlp2,
           ln3, w_pp, w_pg, rel_bias, ln_f):
    B, S, D = x.shape
    assert p.shape[0] == 1, "the final norm is fused into the single layer's last kernel"
    assert S % TILE == 0
    x2 = x.reshape(B * S, D)
    out = _layer(x2, p[0].reshape(B * S, PLE_DIM), B, S, ln1[0], w_in[0], w_a2[0], b_a[0],
                 gla_gn[0], w_o_gla[0], w_o_attn[0], w_out[0], ln2[0], w_mlp1[0], w_mlp2[0],
                 ln3[0], w_pp[0], w_pg[0], rel_bias, ln_f)
    return out.reshape(B, S, D)
```

```python
import functools

import numpy as np
import jax
import jax.numpy as jnp
from jax import lax
from jax.experimental import pallas as pl
from jax.experimental.pallas import tpu as pltpu

F32 = jnp.float32
BF16 = jnp.bfloat16

D_MODEL = 1024
PLE_DIM = 256
EPS = 1e-6
GLA_HEADS = 4
GLA_DK = 128
GLA_DV = 256
GLA_RANK = 16
GLA_TAU = 16.0
GLA_CHUNK = 64
GLA_QK_W = GLA_HEADS * GLA_DK
GLA_V_W = GLA_HEADS * GLA_DV
ATT_GROUPS = ((128, 1), (512, 4), (2048, 16))
ATT_DILS = tuple(d for _, d in ATT_GROUPS)
ATT_HEADS_PER_GROUP = 4
ATT_HEAD_DIM = 128
N_ATT_GROUPS = len(ATT_GROUPS)
ATT_W = ATT_HEADS_PER_GROUP * ATT_HEAD_DIM
ATT_BLOCK = 128
REL_BUCKETS = 32
REL_MAX_DIST = 2048
D_FF = 4 * D_MODEL
NEG_INF = -1e30
LOG2_E = 1.4426950408889634

LANES = 128

COL_Q = 0
COL_K = COL_Q + GLA_QK_W
COL_V = COL_K + GLA_QK_W
COL_R = COL_V + GLA_V_W
COL_GATE = COL_R + GLA_V_W
COL_ATT = COL_GATE + 2 * D_MODEL
ATT_GROUP_W = 3 * ATT_W
Y_COLS = COL_ATT + N_ATT_GROUPS * ATT_GROUP_W

TILE = max(ATT_DILS) * ATT_BLOCK

VMEM_LIMIT = 56 * 1024 * 1024


def _params(semantics):
    return pltpu.CompilerParams(dimension_semantics=semantics, vmem_limit_bytes=VMEM_LIMIT)


def _rms(x, g):
    return x * lax.rsqrt(jnp.mean(x * x, axis=-1, keepdims=True) + EPS) * g


def _dot(a, b):
    return jnp.dot(a, b, preferred_element_type=F32)


def _dot_nt(a, b):
    return lax.dot_general(a, b, (((1,), (1,)), ((), ())), preferred_element_type=F32)


def _dot_tn(a, b):
    return lax.dot_general(a, b, (((0,), (0,)), ((), ())), preferred_element_type=F32)


def _split3(g):
    hi = g.astype(BF16)
    r1 = g - hi.astype(F32)
    mid = r1.astype(BF16)
    lo = (r1 - mid.astype(F32)).astype(BF16)
    return hi, mid, lo


W_GATE_COL = COL_R + GLA_V_W
W_ATT_COL = W_GATE_COL + GLA_RANK
W_MIXGATE_COL = W_ATT_COL + N_ATT_GROUPS * ATT_GROUP_W


def _wprep_kernel(scale_ref, a_ref, b_ref, o_ref, *, n_aligned, tn):
    t = pl.program_id(0)
    scale = scale_ref[t]

    @pl.when(t < n_aligned)
    def _():
        o_ref[...] = (a_ref[...] * scale).astype(BF16)

    @pl.when(t >= n_aligned)
    def _():
        w = jnp.concatenate([a_ref[...], b_ref[...]], axis=0)[GLA_RANK:GLA_RANK + tn]
        o_ref[...] = (w * scale).astype(BF16)


def _wprep(w_in_t, tn=512):
    n_aligned = W_GATE_COL // tn
    n_att = (W_MIXGATE_COL - W_ATT_COL) // tn
    n_tiles = Y_COLS // tn

    def out_tile(t, where):
        return where(t < n_aligned, t,
                     where(t < n_aligned + n_att, t - n_aligned + COL_ATT // tn,
                           t - n_aligned - n_att + COL_GATE // tn))

    q_sections = [(COL_Q, COL_K, GLA_DK ** -0.5)] + [
        (COL_ATT + g * ATT_GROUP_W, COL_ATT + g * ATT_GROUP_W + ATT_W, ATT_HEAD_DIM ** -0.5 * LOG2_E)
        for g in range(N_ATT_GROUPS)]
    assert all(lo % tn == 0 and hi % tn == 0 for lo, hi, _ in q_sections)
    tile_scale = np.ones((n_tiles,), np.float32)
    for t in range(n_tiles):
        col = out_tile(t, lambda c, a, b: a if c else b) * tn
        for lo, hi, sc in q_sections:
            if lo <= col < hi:
                tile_scale[t] = sc
    out_index = lambda t: (out_tile(t, jnp.where), 0)
    return pl.pallas_call(
        functools.partial(_wprep_kernel, n_aligned=n_aligned, tn=tn),
        grid=(n_tiles,),
        in_specs=[
            pl.BlockSpec(memory_space=pltpu.SMEM),
            pl.BlockSpec((tn, D_MODEL), lambda t: (t, 0)),
            pl.BlockSpec((GLA_RANK, D_MODEL), lambda t: ((t + 1) * (tn // GLA_RANK), 0)),
        ],
        out_specs=pl.BlockSpec((tn, D_MODEL), out_index),
        out_shape=jax.ShapeDtypeStruct((Y_COLS, D_MODEL), BF16),
        compiler_params=_params(("arbitrary",)),
        name="wprep",
    )(jnp.asarray(tile_scale), w_in_t, w_in_t)


def _inproj_kernel(*refs, tile_orders):
    n_slab = D_MODEL // LANES
    x_slabs = refs[:n_slab]
    ln_ref, w_ref, wa1_ref, y_ref, ha_ref, h_scr, inv_scr, tmp_scr = refs[n_slab:]
    j = pl.program_id(1)
    chunk = 256
    assert ATT_DILS == (1, 4, 16) and all(order == (0, 0) for order in tile_orders[:4])

    def project(d_left, d_right):
        half = y_ref.shape[1] // 2
        y_ref[:, :half] = _dot_nt(h_scr[d_left], w_ref[:half, :]).astype(BF16)
        if d_right is not None:
            y_ref[:, half:] = _dot_nt(h_scr[d_right], w_ref[half:, :]).astype(BF16)

    def normalise():
        for c in range(TILE // chunk):
            rows = pl.ds(c * chunk, chunk)
            parts = [xr[rows, :] for xr in x_slabs]
            sq = parts[0] * parts[0]
            for part in parts[1:]:
                sq = sq + part * part
            ms = jnp.sum(sq, axis=-1, keepdims=True) * (1.0 / D_MODEL)
            inv = jnp.broadcast_to(lax.rsqrt(ms + EPS), (chunk, LANES))
            inv_scr[rows, :] = inv
            for s, part in enumerate(parts):
                cols = pl.ds(s * LANES, LANES)
                h_scr[0, rows, cols] = (part * inv * ln_ref[:, cols]).astype(BF16)

    def regroup(slabs):
        n4, n16 = TILE // 4, TILE // 16
        for s in slabs:
            xr = x_slabs[s]
            cols = pl.ds(s * LANES, LANES)
            ln_s = ln_ref[:, cols]
            for a in range(4):
                for c in range(n4 // chunk):
                    src = pl.ds(a + 4 * c * chunk, chunk, stride=4)
                    dst = pl.ds(a * n4 + c * chunk, chunk)
                    hn = xr[src, :] * inv_scr[src, :] * ln_s
                    tmp_scr[dst, :] = hn
                    h_scr[1, dst, cols] = hn.astype(BF16)
            for a in range(4):
                for b in range(4):
                    src = pl.ds(a * n4 + b, n16, stride=4)
                    h_scr[2, pl.ds((a + 4 * b) * n16, n16), cols] = tmp_scr[src, :].astype(BF16)

    def gate_input():
        wa1_t = jnp.concatenate([wa1_ref[...], jnp.zeros((LANES - GLA_RANK, D_MODEL), F32)], axis=0)
        ha_ref[...] = _dot_nt(h_scr[0], wa1_t.astype(BF16))

    prep = {0: normalise, 1: functools.partial(regroup, range(0, n_slab // 2)),
            2: functools.partial(regroup, range(n_slab // 2, n_slab)), 3: gate_input}
    special = [t for t, (left, right) in enumerate(tile_orders) if t in prep or left != right]
    for t in special:
        @pl.when(j == t)
        def _(t=t):
            if t in prep:
                prep[t]()
            project(*tile_orders[t])

    @pl.when(functools.reduce(jnp.logical_and, [j != t for t in special]))
    def _():
        firsts = [min(t for t, o in enumerate(tile_orders) if o[0] == d) for d in (1, 2)]
        d = sum((j >= f).astype(jnp.int32) for f in firsts)
        project(d, d)


def _inproj(x2, ln1, w_main, w_in_t, tn=512):
    T = x2.shape[0]
    n_slab = D_MODEL // LANES
    half = tn // 2
    bounds = [COL_ATT + ATT_GROUP_W * (g + 1) for g in range(N_ATT_GROUPS)]
    order_of = lambda col: None if col >= Y_COLS else sum(col >= b for b in bounds)
    n_tiles = pl.cdiv(Y_COLS, tn)
    tile_orders = [(order_of(t * tn), order_of(t * tn + half)) for t in range(n_tiles)]
    assert all(b % half == 0 for b in bounds) and Y_COLS % half == 0
    tile_orders[-2:] = tile_orders[:-3:-1]
    tile_orders = tuple(tile_orders)

    def tile_of(j):
        return jnp.where(j == n_tiles - 2, n_tiles - 1, jnp.where(j == n_tiles - 1, n_tiles - 2, j))

    def slab_index(i, j, s):
        moved = (j >= min(3 + s, n_tiles - 1)).astype(jnp.int32)
        return jnp.minimum(i + moved, T // TILE - 1), s

    return pl.pallas_call(
        functools.partial(_inproj_kernel, tile_orders=tile_orders),
        grid=(T // TILE, n_tiles),
        in_specs=[pl.BlockSpec((TILE, LANES), functools.partial(slab_index, s=s))
                  for s in range(n_slab)] + [
            pl.BlockSpec((1, D_MODEL), lambda i, j: (0, 0)),
            pl.BlockSpec((tn, D_MODEL), lambda i, j: (tile_of(j), 0)),
            pl.BlockSpec((GLA_RANK, D_MODEL), lambda i, j: (W_GATE_COL // GLA_RANK, 0)),
        ],
        out_specs=[
            pl.BlockSpec((TILE, tn), lambda i, j: (i, tile_of(j))),
            pl.BlockSpec((TILE, LANES), lambda i, j: (i, 0)),
        ],
        out_shape=[
            jax.ShapeDtypeStruct((T, Y_COLS), BF16),
            jax.ShapeDtypeStruct((T, LANES), F32),
        ],
        scratch_shapes=[pltpu.VMEM((N_ATT_GROUPS, TILE, D_MODEL), BF16),
                        pltpu.VMEM((TILE, LANES), F32),
                        pltpu.VMEM((TILE, LANES), F32)],
        compiler_params=_params(("arbitrary", "arbitrary")),
        name="inproj",
    )(*([x2] * n_slab), ln1, w_main, w_in_t)


def _gla_kernel(q_ref, k_ref, v_ref, r_ref, ha_ref, wa2_ref, ba_ref, gn_ref, o_ref, st_ref, la_scr,
                *, nchunk):
    C = GLA_CHUNK

    @pl.when(pl.program_id(1) == 0)
    def _():
        st_ref[...] = jnp.zeros_like(st_ref)

    row = lax.broadcasted_iota(jnp.int32, (C, C), 0)
    col = lax.broadcasted_iota(jnp.int32, (C, C), 1)
    causal = row >= col
    tri = causal.astype(BF16)

    a_hi, a_mid, _ = _split3(ha_ref[0])
    w_hi, w_mid, _ = _split3(wa2_ref[...])
    z = _dot(a_hi, w_hi) + (_dot(a_mid, w_hi) + _dot(a_hi, w_mid)) + ba_ref[...]
    la_scr[...] = (jnp.minimum(z, 0.0) - jnp.log(1.0 + jnp.exp(-jnp.abs(z)))) * (1.0 / GLA_TAU)

    H = range(GLA_HEADS)
    ck = [slice(h * GLA_DK, (h + 1) * GLA_DK) for h in H]
    cv = [slice(h * GLA_DV, (h + 1) * GLA_DV) for h in H]

    def cumsum(c):
        hi, mid, _ = _split3(la_scr[pl.ds(c * C, C), :])
        return _dot(tri, hi) + _dot(tri, mid)

    def products(c, b, st):
        sl = pl.ds(c * C, C)
        decay = jnp.exp(b[C - 1:C, :])
        q_dec = q_ref[0, sl, :] * jnp.exp(b).astype(BF16)
        k_in = k_ref[0, sl, :] * jnp.exp(-b).astype(BF16)
        k_out = k_in * decay.astype(BF16)
        v = [v_ref[0, sl, cv[h]] for h in H]
        attn = [_dot_nt(q_dec[:, ck[h]], k_in[:, ck[h]]) for h in H]
        o_inter = [_dot(q_dec[:, ck[h]], st[h].astype(BF16)) for h in H]
        upd = [_dot_tn(k_out[:, ck[h]], v[h]) for h in H]
        new_st = []
        for h in H:
            dcol = jnp.broadcast_to(decay[:, ck[h]], (GLA_DK, GLA_DK)).T
            new_st.append(st[h] * jnp.concatenate([dcol] * (GLA_DV // GLA_DK), axis=1) + upd[h])
        return new_st, (attn, o_inter, v)

    def output(c, attn, o_inter, v):
        sl = pl.ds(c * C, C)
        for h in H:
            o = _dot(jnp.where(causal, attn[h], 0.0).astype(BF16), v[h]) + o_inter[h]
            r = r_ref[0, sl, cv[h]]
            o_ref[0, sl, cv[h]] = _rms(o, gn_ref[:, cv[h]]).astype(BF16) * (r * jax.nn.sigmoid(r))

    st = [st_ref[h] for h in H]
    b = {0: cumsum(0)}
    if nchunk > 1:
        b[1] = cumsum(1)
    st, pending = products(0, b.pop(0), st)
    for c in range(nchunk):
        if c + 2 < nchunk:
            b[c + 2] = cumsum(c + 2)
        if c + 1 < nchunk:
            st, nxt = products(c + 1, b.pop(c + 1), st)
        output(c, *pending)
        if c + 1 < nchunk:
            pending = nxt
    for h in H:
        st_ref[h] = st[h]


def _gla(y3, ha3, wa2, ba, gn, tb=1024):
    B, S, _ = y3.shape
    return pl.pallas_call(
        functools.partial(_gla_kernel, nchunk=tb // GLA_CHUNK),
        grid=(B, S // tb),
        in_specs=[
            pl.BlockSpec((1, tb, GLA_QK_W), lambda b, t: (b, t, COL_Q // GLA_QK_W)),
            pl.BlockSpec((1, tb, GLA_QK_W), lambda b, t: (b, t, COL_K // GLA_QK_W)),
            pl.BlockSpec((1, tb, GLA_V_W), lambda b, t: (b, t, COL_V // GLA_V_W)),
            pl.BlockSpec((1, tb, GLA_V_W), lambda b, t: (b, t, COL_R // GLA_V_W)),
            pl.BlockSpec((1, tb, LANES), lambda b, t: (b, t, 0)),
            pl.BlockSpec((LANES, GLA_QK_W), lambda b, t: (0, 0)),
            pl.BlockSpec((1, GLA_QK_W), lambda b, t: (0, 0)),
            pl.BlockSpec((1, GLA_V_W), lambda b, t: (0, 0)),
        ],
        out_specs=pl.BlockSpec((1, tb, GLA_V_W), lambda b, t: (b, t, 0)),
        out_shape=jax.ShapeDtypeStruct((B, S, GLA_V_W), BF16),
        scratch_shapes=[pltpu.VMEM((GLA_HEADS, GLA_DK, GLA_DV), F32), pltpu.VMEM((tb, GLA_QK_W), F32)],
        compiler_params=_params(("arbitrary", "arbitrary")),
        name="gla",
    )(y3, y3, y3, y3, ha3, wa2, ba, gn)


def _t5_causal_bucket(n):
    max_exact = REL_BUCKETS // 2
    nf = np.maximum(n, 1).astype(np.float32)
    large = max_exact + (np.log(nf / max_exact) / np.log(REL_MAX_DIST / max_exact)
                         * (REL_BUCKETS - max_exact)).astype(np.int32)
    large = np.minimum(large, REL_BUCKETS - 1)
    return np.where(n < max_exact, n, large).astype(np.int32)


def _bucket_table():
    qi = np.arange(ATT_BLOCK)[:, None]
    kj = np.arange(2 * ATT_BLOCK)[None, :]
    delta = qi + ATT_BLOCK - kj
    out = []
    for win, dil in ATT_GROUPS:
        band = (delta >= 0) & (delta <= win // dil)
        bucket = np.where(band, _t5_causal_bucket(np.maximum(delta, 0) * dil), -1)
        out.append(bucket)
    return np.stack(out).astype(np.int32)


def _att_unit(q, k, v, bias):
    BLK = ATT_BLOCK
    s = _dot_nt(q, k) + bias
    m = jnp.max(jnp.maximum(s[:, :BLK], s[:, BLK:]), axis=-1, keepdims=True)
    p = jnp.exp2((s - m).astype(BF16))
    v_ext = jnp.concatenate([v, jnp.ones_like(v)], axis=1)
    pv = _dot(p, v_ext)
    denom = pv[:, BLK:]
    return pv[:, :BLK] / denom, m + jnp.log2(denom)


def _att_kernel(tab_ref, bucket_ref, *refs, tiles_per_batch, heads_per_step, buckets_used, n_prev):
    groups, pos = [], 0
    for n in n_prev:
        q, k, v = refs[pos:pos + 3]
        groups.append((q, k, v, refs[pos + 3:pos + 3 + n], refs[pos + 3 + n:pos + 3 + 2 * n]))
        pos += 3 + 2 * n
    o_ref, bias_scr, o_scr, lse_scr = refs[pos:]
    BLK = ATT_BLOCK
    hd = ATT_HEAD_DIM
    ti = pl.program_id(0)
    h0 = pl.program_id(1) * heads_per_step

    @pl.when((ti == 0) & (h0 == 0))
    def _():
        for g in range(N_ATT_GROUPS):
            bucket = bucket_ref[g]
            bias = [jnp.full(bucket.shape, NEG_INF, F32) for _ in range(ATT_HEADS_PER_GROUP)]
            for b in buckets_used[g]:
                hit = bucket == b
                for hh in range(ATT_HEADS_PER_GROUP):
                    bias[hh] = jnp.where(hit, tab_ref[b, g * ATT_HEADS_PER_GROUP + hh] * LOG2_E, bias[hh])
            for hh in range(ATT_HEADS_PER_GROUP):
                bias_scr[g, hh] = bias[hh]

    lane = lax.broadcasted_iota(jnp.int32, (BLK, 2 * BLK), 1)
    no_prev = jnp.logical_and((ti % tiles_per_batch) == 0, lane < BLK)
    for hh in range(heads_per_step):
        hc = pl.ds(hh * hd, hd)
        for g, (q, k, v, kps, vps) in enumerate(groups):
            dil = ATT_DILS[g]
            n_r = TILE // dil
            bias = bias_scr[g, h0 + hh]
            bias_first = jnp.where(no_prev, NEG_INF, bias)
            for r in range(dil):
                for j in range(n_r // BLK):
                    cur = pl.ds(r * n_r + j * BLK, BLK)
                    if j > 0:
                        both = pl.ds(r * n_r + (j - 1) * BLK, 2 * BLK)
                        k_all, v_all, b = k[both, hc], v[both, hc], bias
                    else:
                        if len(kps) == dil:
                            kp, vp, prev = kps[r], vps[r], pl.ds(0, BLK)
                        else:
                            kp, vp, prev = kps[0], vps[0], pl.ds(r * n_r + n_r - BLK, BLK)
                        k_all = jnp.concatenate([kp[prev, hc], k[cur, hc]], axis=0)
                        v_all = jnp.concatenate([vp[prev, hc], v[cur, hc]], axis=0)
                        b = bias_first
                    if dil == 1:
                        rows = pl.ds(j * BLK, BLK)
                    else:
                        rows = pl.ds(dil * j * BLK + r, BLK, stride=dil)
                    o_scr[g, rows, :], lse_scr[g, rows, :] = _att_unit(q[cur, hc], k_all, v_all, b)

        for c in range(TILE // 256):
            rows = pl.ds(c * 256, 256)
            lse = [lse_scr[g, rows, :] for g in range(N_ATT_GROUPS)]
            top = functools.reduce(jnp.maximum, lse)
            w = [jnp.exp2(x - top) for x in lse]
            num = functools.reduce(lambda a, b: a + b,
                                   [w[g] * o_scr[g, rows, :] for g in range(N_ATT_GROUPS)])
            den = functools.reduce(lambda a, b: a + b, w)
            o_ref[rows, hc] = (num / den).astype(BF16)


def _attention(y2, rel_bias, S, heads_per_step=2):
    T = y2.shape[0]
    table = _bucket_table()
    tiles_per_batch = S // TILE
    hd = ATT_HEAD_DIM
    bw = heads_per_step * hd

    def col(g, c):
        return (COL_ATT + g * ATT_GROUP_W + c * ATT_W) // bw

    def cur(g, c):
        return pl.BlockSpec((TILE, bw), lambda t, h: (t, col(g, c) + h))

    def prev_tile(g, c):
        return [pl.BlockSpec((TILE, bw), lambda t, h: (jnp.maximum(t - 1, 0), col(g, c) + h))]

    def prev_blocks(g, c):
        nb, per_class = TILE // ATT_BLOCK, TILE // ATT_DILS[g] // ATT_BLOCK
        return [pl.BlockSpec((ATT_BLOCK, bw),
                             functools.partial(lambda t, h, last: (jnp.maximum((t - 1) * nb + last, 0), col(g, c) + h),
                                               last=(r + 1) * per_class - 1))
                for r in range(ATT_DILS[g])]

    prev_spec = [prev_blocks if d * ATT_BLOCK < TILE else prev_tile for d in ATT_DILS]
    in_specs = [pl.BlockSpec(memory_space=pltpu.SMEM),
                pl.BlockSpec((N_ATT_GROUPS, ATT_BLOCK, 2 * ATT_BLOCK), lambda t, h: (0, 0, 0))]
    for g in range(N_ATT_GROUPS):
        in_specs += [cur(g, 0), cur(g, 1), cur(g, 2)] + prev_spec[g](g, 1) + prev_spec[g](g, 2)
    n_prev = tuple(len(prev_spec[g](g, 1)) for g in range(N_ATT_GROUPS))
    return pl.pallas_call(
        functools.partial(_att_kernel, tiles_per_batch=tiles_per_batch, heads_per_step=heads_per_step,
                          buckets_used=tuple(tuple(int(b) for b in np.unique(t) if b >= 0) for t in table),
                          n_prev=n_prev),
        grid=(T // TILE, ATT_HEADS_PER_GROUP // heads_per_step),
        in_specs=in_specs,
        out_specs=pl.BlockSpec((TILE, bw), lambda t, h: (t, h)),
        out_shape=jax.ShapeDtypeStruct((T, ATT_W), BF16),
        scratch_shapes=[
            pltpu.VMEM((N_ATT_GROUPS, ATT_HEADS_PER_GROUP, ATT_BLOCK, 2 * ATT_BLOCK), F32),
            pltpu.VMEM((N_ATT_GROUPS, TILE, hd), F32),
            pltpu.VMEM((N_ATT_GROUPS, TILE, LANES), F32),
        ],
        compiler_params=_params(("arbitrary", "arbitrary")),
        name="attention",
    )(rel_bias, jnp.asarray(table), *([y2] * (len(in_specs) - 2)))


def _tail_kernel(x_ref, og_ref, oa_ref, ga_ref, gb_ref, p_ref, wog_ref, woa_ref, wout_ref,
                 ln2_ref, w1_ref, w2_ref, ln3_ref, wpg_ref, wpp_ref, lnf_ref, out_ref, *, tf):
    y_gla = _dot(og_ref[...], wog_ref[...])
    y_att = _dot(oa_ref[...], woa_ref[...])
    mix = (jax.nn.sigmoid(ga_ref[...].astype(F32)) * y_gla
           + jax.nn.sigmoid(gb_ref[...].astype(F32)) * y_att)
    x1 = x_ref[...] + _dot(mix.astype(BF16), wout_ref[...])

    h = _rms(x1, ln2_ref[...]).astype(BF16)
    x2 = x1
    for f in range(D_FF // tf):
        cols = pl.ds(f * tf, tf)
        u = jnp.maximum(_dot(h, w1_ref[:, cols]), 0.0)
        x2 = x2 + _dot((u * u).astype(BF16), w2_ref[cols, :])

    h3 = _rms(x2, ln3_ref[...]).astype(BF16)
    gate = jax.nn.sigmoid(_dot(h3, wpg_ref[...]))
    x3 = x2 + gate * _dot(p_ref[...].astype(BF16), wpp_ref[...])
    out_ref[...] = _rms(x3, lnf_ref[...])


def _tail(x2, y2, o_gla, o_att, p2, wog, woa, wout, ln2, w1, w2, ln3, wpg, wpp, lnf, tm=512, tf=1024):
    T = x2.shape[0]
    gcol = COL_GATE // D_MODEL
    row = lambda w: pl.BlockSpec((tm, w), lambda i: (i, 0))
    const = lambda a: pl.BlockSpec(a.shape, lambda i: (0, 0), pipeline_mode=pl.Buffered(1))
    return pl.pallas_call(
        functools.partial(_tail_kernel, tf=tf),
        grid=(T // tm,),
        in_specs=[row(D_MODEL), row(GLA_V_W), row(ATT_W),
                  pl.BlockSpec((tm, D_MODEL), lambda i: (i, gcol)),
                  pl.BlockSpec((tm, D_MODEL), lambda i: (i, gcol + 1)),
                  row(PLE_DIM),
                  const(wog), const(woa), const(wout), const(ln2), const(w1), const(w2),
                  const(ln3), const(wpg), const(wpp), const(lnf)],
        out_specs=row(D_MODEL),
        out_shape=jax.ShapeDtypeStruct((T, D_MODEL), F32),
        compiler_params=_params(("arbitrary",)),
        name="tail",
    )(x2, o_gla, o_att, y2, y2, p2, wog, woa, wout, ln2, w1, w2, ln3, wpg, wpp, lnf)


def _layer(x2, p2, B, S, ln1, w_in, w_a2, b_a, gla_gn, w_o_gla, w_o_attn, w_out,
           ln2, w_mlp1, w_mlp2, ln3, w_pp, w_pg, rel_bias, ln_out):
    wa2 = jnp.pad(w_a2, ((0, LANES - GLA_RANK), (0, 0)))

    w_in_t = w_in.T
    y2, ha = _inproj(x2, ln1[None], _wprep(w_in_t), w_in_t)
    o_gla = _gla(y2.reshape(B, S, Y_COLS), ha.reshape(B, S, LANES), wa2, b_a[None], gla_gn[None])
    o_att = _attention(y2, rel_bias, S)
    return _tail(x2, y2, o_gla.reshape(B * S, GLA_V_W), o_att, p2,
                 w_o_gla.astype(BF16), w_o_attn.astype(BF16), w_out.astype(BF16), ln2[None],
                 w_mlp1.astype(BF16), w_mlp2.astype(BF16), ln3[None],
                 w_pg.astype(BF16), w_pp.astype(BF16), ln_out[None])


def kernel(x, p, ln1, w_in, w_a2, b_a, gla_gn, w_o_gla, w_o_attn, w_out, ln2, w_mlp1, w_mlp2,
           ln3, w_pp, w_pg, rel_bias, ln_f):
    B, S, D = x.shape
    assert p.shape[0] == 1, "the final norm is fused into the single layer's last kernel"
    assert S % TILE == 0
    x2 = x.reshape(B * S, D)
    out = _layer(x2, p[0].reshape(B * S, PLE_DIM), B, S, ln1[0], w_in[0], w_a2[0], b_a[0],
                 gla_gn[0], w_o_gla[0], w_o_attn[0], w_out[0], ln2[0], w_mlp1[0], w_mlp2[0],
                 ln3[0], w_pp[0], w_pg[0], rel_bias, ln_f)
    return out.reshape(B, S, D)
```

```python
import functools

import numpy as np
import jax
import jax.numpy as jnp
from jax import lax
from jax.experimental import pallas as pl
from jax.experimental.pallas import tpu as pltpu

F32 = jnp.float32
BF16 = jnp.bfloat16

D_MODEL = 1024
PLE_DIM = 256
EPS = 1e-6
GLA_HEADS = 4
GLA_DK = 128
GLA_DV = 256
GLA_RANK = 16
GLA_TAU = 16.0
GLA_CHUNK = 64
GLA_QK_W = GLA_HEADS * GLA_DK
GLA_V_W = GLA_HEADS * GLA_DV
ATT_GROUPS = ((128, 1), (512, 4), (2048, 16))
ATT_DILS = tuple(d for _, d in ATT_GROUPS)
ATT_HEADS_PER_GROUP = 4
ATT_HEAD_DIM = 128
N_ATT_GROUPS = len(ATT_GROUPS)
ATT_W = ATT_HEADS_PER_GROUP * ATT_HEAD_DIM
ATT_BLOCK = 128
REL_BUCKETS = 32
REL_MAX_DIST = 2048
D_FF = 4 * D_MODEL
NEG_INF = -1e30
LOG2_E = 1.4426950408889634

LANES = 128

COL_Q = 0
COL_K = COL_Q + GLA_QK_W
COL_V = COL_K + GLA_QK_W
COL_R = COL_V + GLA_V_W
COL_GATE = COL_R + GLA_V_W
COL_ATT = COL_GATE + 2 * D_MODEL
ATT_GROUP_W = 3 * ATT_W
Y_COLS = COL_ATT + N_ATT_GROUPS * ATT_GROUP_W

TILE = max(ATT_DILS) * ATT_BLOCK

VMEM_LIMIT = 56 * 1024 * 1024


def _params(semantics):
    return pltpu.CompilerParams(dimension_semantics=semantics, vmem_limit_bytes=VMEM_LIMIT)


def _rms(x, g):
    return x * lax.rsqrt(jnp.mean(x * x, axis=-1, keepdims=True) + EPS) * g


def _dot(a, b):
    return jnp.dot(a, b, preferred_element_type=F32)


def _dot_nt(a, b):
    return lax.dot_general(a, b, (((1,), (1,)), ((), ())), preferred_element_type=F32)


def _dot_tn(a, b):
    return lax.dot_general(a, b, (((0,), (0,)), ((), ())), preferred_element_type=F32)


def _split3(g):
    hi = g.astype(BF16)
    r1 = g - hi.astype(F32)
    mid = r1.astype(BF16)
    lo = (r1 - mid.astype(F32)).astype(BF16)
    return hi, mid, lo


W_GATE_COL = COL_R + GLA_V_W
W_ATT_COL = W_GATE_COL + GLA_RANK
W_MIXGATE_COL = W_ATT_COL + N_ATT_GROUPS * ATT_GROUP_W


def _wprep_kernel(scale_ref, a_ref, b_ref, o_ref, *, n_aligned, tn):
    t = pl.program_id(0)
    scale = scale_ref[t]

    @pl.when(t < n_aligned)
    def _():
        o_ref[...] = (a_ref[...] * scale).astype(BF16)

    @pl.when(t >= n_aligned)
    def _():
        w = jnp.concatenate([a_ref[...], b_ref[...]], axis=0)[GLA_RANK:GLA_RANK + tn]
        o_ref[...] = (w * scale).astype(BF16)


def _wprep(w_in_t, tn=512):
    n_aligned = W_GATE_COL // tn
    n_att = (W_MIXGATE_COL - W_ATT_COL) // tn
    n_tiles = Y_COLS // tn

    def out_tile(t, where):
        return where(t < n_aligned, t,
                     where(t < n_aligned + n_att, t - n_aligned + COL_ATT // tn,
                           t - n_aligned - n_att + COL_GATE // tn))

    q_sections = [(COL_Q, COL_K, GLA_DK ** -0.5)] + [
        (COL_ATT + g * ATT_GROUP_W, COL_ATT + g * ATT_GROUP_W + ATT_W, ATT_HEAD_DIM ** -0.5 * LOG2_E)
        for g in range(N_ATT_GROUPS)]
    assert all(lo % tn == 0 and hi % tn == 0 for lo, hi, _ in q_sections)
    tile_scale = np.ones((n_tiles,), np.float32)
    for t in range(n_tiles):
        col = out_tile(t, lambda c, a, b: a if c else b) * tn
        for lo, hi, sc in q_sections:
            if lo <= col < hi:
                tile_scale[t] = sc
    out_index = lambda t: (out_tile(t, jnp.where), 0)
    return pl.pallas_call(
        functools.partial(_wprep_kernel, n_aligned=n_aligned, tn=tn),
        grid=(n_tiles,),
        in_specs=[
            pl.BlockSpec(memory_space=pltpu.SMEM),
            pl.BlockSpec((tn, D_MODEL), lambda t: (t, 0)),
            pl.BlockSpec((GLA_RANK, D_MODEL), lambda t: ((t + 1) * (tn // GLA_RANK), 0)),
        ],
        out_specs=pl.BlockSpec((tn, D_MODEL), out_index),
        out_shape=jax.ShapeDtypeStruct((Y_COLS, D_MODEL), BF16),
        compiler_params=_params(("arbitrary",)),
        name="wprep",
    )(jnp.asarray(tile_scale), w_in_t, w_in_t)


def _inproj_kernel(*refs, tile_orders):
    n_slab = D_MODEL // LANES
    x_slabs = refs[:n_slab]
    ln_ref, w_ref, wa1_ref, y_ref, ha_ref, h_scr, inv_scr, tmp_scr = refs[n_slab:]
    j = pl.program_id(1)
    chunk = 256
    assert ATT_DILS == (1, 4, 16) and all(order == (0, 0) for order in tile_orders[:4])

    def project(d_left, d_right):
        half = y_ref.shape[1] // 2
        y_ref[:, :half] = _dot_nt(h_scr[d_left], w_ref[:half, :]).astype(BF16)
        if d_right is not None:
            y_ref[:, half:] = _dot_nt(h_scr[d_right], w_ref[half:, :]).astype(BF16)

    def normalise():
        for c in range(TILE // chunk):
            rows = pl.ds(c * chunk, chunk)
            parts = [xr[rows, :] for xr in x_slabs]
            sq = parts[0] * parts[0]
            for part in parts[1:]:
                sq = sq + part * part
            ms = jnp.sum(sq, axis=-1, keepdims=True) * (1.0 / D_MODEL)
            inv = jnp.broadcast_to(lax.rsqrt(ms + EPS), (chunk, LANES))
            inv_scr[rows, :] = inv
            for s, part in enumerate(parts):
                cols = pl.ds(s * LANES, LANES)
                h_scr[0, rows, cols] = (part * inv * ln_ref[:, cols]).astype(BF16)

    def regroup(slabs):
        n4, n16 = TILE // 4, TILE // 16
        for s in slabs:
            xr = x_slabs[s]
            cols = pl.ds(s * LANES, LANES)
            ln_s = ln_ref[:, cols]
            for a in range(4):
                for c in range(n4 // chunk):
                    src = pl.ds(a + 4 * c * chunk, chunk, stride=4)
                    dst = pl.ds(a * n4 + c * chunk, chunk)
                    hn = xr[src, :] * inv_scr[src, :] * ln_s
                    tmp_scr[dst, :] = hn
                    h_scr[1, dst, cols] = hn.astype(BF16)
            for a in range(4):
                for b in range(4):
                    src = pl.ds(a * n4 + b, n16, stride=4)
                    h_scr[2, pl.ds((a + 4 * b) * n16, n16), cols] = tmp_scr[src, :].astype(BF16)

    def gate_input():
        wa1_t = jnp.concatenate([wa1_ref[...], jnp.zeros((LANES - GLA_RANK, D_MODEL), F32)], axis=0)
        ha_ref[...] = _dot_nt(h_scr[0], wa1_t.astype(BF16))

    prep = {0: normalise, 1: functools.partial(regroup, range(0, n_slab // 2)),
            2: functools.partial(regroup, range(n_slab // 2, n_slab)), 3: gate_input}
    for t, order in enumerate(tile_orders):
        @pl.when(j == t)
        def _(t=t, order=order):
            if t in prep:
                prep[t]()
            project(*order)


def _inproj(x2, ln1, w_main, w_in_t, tn=1024):
    T = x2.shape[0]
    n_slab = D_MODEL // LANES
    half = tn // 2
    bounds = [COL_ATT + ATT_GROUP_W * (g + 1) for g in range(N_ATT_GROUPS)]
    order_of = lambda col: None if col >= Y_COLS else sum(col >= b for b in bounds)
    n_tiles = pl.cdiv(Y_COLS, tn)
    tile_orders = [(order_of(t * tn), order_of(t * tn + half)) for t in range(n_tiles)]
    assert all(b % half == 0 for b in bounds) and Y_COLS % half == 0
    tile_orders[-2:] = tile_orders[:-3:-1]
    tile_orders = tuple(tile_orders)

    def tile_of(j):
        return jnp.where(j == n_tiles - 2, n_tiles - 1, jnp.where(j == n_tiles - 1, n_tiles - 2, j))

    def slab_index(i, j, s):
        moved = (j >= min(3 + s, n_tiles - 1)).astype(jnp.int32)
        return jnp.minimum(i + moved, T // TILE - 1), s

    return pl.pallas_call(
        functools.partial(_inproj_kernel, tile_orders=tile_orders),
        grid=(T // TILE, n_tiles),
        in_specs=[pl.BlockSpec((TILE, LANES), functools.partial(slab_index, s=s))
                  for s in range(n_slab)] + [
            pl.BlockSpec((1, D_MODEL), lambda i, j: (0, 0)),
            pl.BlockSpec((tn, D_MODEL), lambda i, j: (tile_of(j), 0)),
            pl.BlockSpec((GLA_RANK, D_MODEL), lambda i, j: (W_GATE_COL // GLA_RANK, 0)),
        ],
        out_specs=[
            pl.BlockSpec((TILE, tn), lambda i, j: (i, tile_of(j))),
            pl.BlockSpec((TILE, LANES), lambda i, j: (i, 0)),
        ],
        out_shape=[
            jax.ShapeDtypeStruct((T, Y_COLS), BF16),
            jax.ShapeDtypeStruct((T, LANES), F32),
        ],
        scratch_shapes=[pltpu.VMEM((N_ATT_GROUPS, TILE, D_MODEL), BF16),
                        pltpu.VMEM((TILE, LANES), F32),
                        pltpu.VMEM((TILE, LANES), F32)],
        compiler_params=_params(("arbitrary", "arbitrary")),
        name="inproj",
    )(*([x2] * n_slab), ln1, w_main, w_in_t)


def _gla_kernel(q_ref, k_ref, v_ref, r_ref, ha_ref, wa2_ref, ba_ref, gn_ref, o_ref, st_ref, la_scr,
                *, nchunk):
    C = GLA_CHUNK

    @pl.when(pl.program_id(1) == 0)
    def _():
        st_ref[...] = jnp.zeros_like(st_ref)

    row = lax.broadcasted_iota(jnp.int32, (C, C), 0)
    col = lax.broadcasted_iota(jnp.int32, (C, C), 1)
    causal = row >= col
    tri = causal.astype(BF16)

    a_hi, a_mid, _ = _split3(ha_ref[0])
    w_hi, w_mid, _ = _split3(wa2_ref[...])
    z = _dot(a_hi, w_hi) + (_dot(a_mid, w_hi) + _dot(a_hi, w_mid)) + ba_ref[...]
    la_scr[...] = (jnp.minimum(z, 0.0) - jnp.log(1.0 + jnp.exp(-jnp.abs(z)))) * (1.0 / GLA_TAU)

    H = range(GLA_HEADS)
    ck = [slice(h * GLA_DK, (h + 1) * GLA_DK) for h in H]
    cv = [slice(h * GLA_DV, (h + 1) * GLA_DV) for h in H]

    def cumsum(c):
        hi, mid, _ = _split3(la_scr[pl.ds(c * C, C), :])
        return _dot(tri, hi) + _dot(tri, mid)

    def products(c, b, st):
        sl = pl.ds(c * C, C)
        decay = jnp.exp(b[C - 1:C, :])
        q_dec = q_ref[0, sl, :] * jnp.exp(b).astype(BF16)
        k_in = k_ref[0, sl, :] * jnp.exp(-b).astype(BF16)
        k_out = k_in * decay.astype(BF16)
        v = [v_ref[0, sl, cv[h]] for h in H]
        attn = [_dot_nt(q_dec[:, ck[h]], k_in[:, ck[h]]) for h in H]
        o_inter = [_dot(q_dec[:, ck[h]], st[h].astype(BF16)) for h in H]
        upd = [_dot_tn(k_out[:, ck[h]], v[h]) for h in H]
        new_st = []
        for h in H:
            dcol = jnp.broadcast_to(decay[:, ck[h]], (GLA_DK, GLA_DK)).T
            new_st.append(st[h] * jnp.concatenate([dcol] * (GLA_DV // GLA_DK), axis=1) + upd[h])
        return new_st, (attn, o_inter, v)

    def output(c, attn, o_inter, v):
        sl = pl.ds(c * C, C)
        for h in H:
            o = _dot(jnp.where(causal, attn[h], 0.0).astype(BF16), v[h]) + o_inter[h]
            r = r_ref[0, sl, cv[h]]
            o_ref[0, sl, cv[h]] = _rms(o, gn_ref[:, cv[h]]).astype(BF16) * (r * jax.nn.sigmoid(r))

    st = [st_ref[h] for h in H]
    b = {0: cumsum(0)}
    if nchunk > 1:
        b[1] = cumsum(1)
    st, pending = products(0, b.pop(0), st)
    for c in range(nchunk):
        if c + 2 < nchunk:
            b[c + 2] = cumsum(c + 2)
        if c + 1 < nchunk:
            st, nxt = products(c + 1, b.pop(c + 1), st)
        output(c, *pending)
        if c + 1 < nchunk:
            pending = nxt
    for h in H:
        st_ref[h] = st[h]


def _gla(y3, ha3, wa2, ba, gn, tb=1024):
    B, S, _ = y3.shape
    return pl.pallas_call(
        functools.partial(_gla_kernel, nchunk=tb // GLA_CHUNK),
        grid=(B, S // tb),
        in_specs=[
            pl.BlockSpec((1, tb, GLA_QK_W), lambda b, t: (b, t, COL_Q // GLA_QK_W)),
            pl.BlockSpec((1, tb, GLA_QK_W), lambda b, t: (b, t, COL_K // GLA_QK_W)),
            pl.BlockSpec((1, tb, GLA_V_W), lambda b, t: (b, t, COL_V // GLA_V_W)),
            pl.BlockSpec((1, tb, GLA_V_W), lambda b, t: (b, t, COL_R // GLA_V_W)),
            pl.BlockSpec((1, tb, LANES), lambda b, t: (b, t, 0)),
            pl.BlockSpec((LANES, GLA_QK_W), lambda b, t: (0, 0)),
            pl.BlockSpec((1, GLA_QK_W), lambda b, t: (0, 0)),
            pl.BlockSpec((1, GLA_V_W), lambda b, t: (0, 0)),
        ],
        out_specs=pl.BlockSpec((1, tb, GLA_V_W), lambda b, t: (b, t, 0)),
        out_shape=jax.ShapeDtypeStruct((B, S, GLA_V_W), BF16),
        scratch_shapes=[pltpu.VMEM((GLA_HEADS, GLA_DK, GLA_DV), F32), pltpu.VMEM((tb, GLA_QK_W), F32)],
        compiler_params=_params(("arbitrary", "arbitrary")),
        name="gla",
    )(y3, y3, y3, y3, ha3, wa2, ba, gn)


def _t5_causal_bucket(n):
    max_exact = REL_BUCKETS // 2
    nf = np.maximum(n, 1).astype(np.float32)
    large = max_exact + (np.log(nf / max_exact) / np.log(REL_MAX_DIST / max_exact)
                         * (REL_BUCKETS - max_exact)).astype(np.int32)
    large = np.minimum(large, REL_BUCKETS - 1)
    return np.where(n < max_exact, n, large).astype(np.int32)


def _bucket_table():
    qi = np.arange(ATT_BLOCK)[:, None]
    kj = np.arange(2 * ATT_BLOCK)[None, :]
    delta = qi + ATT_BLOCK - kj
    out = []
    for win, dil in ATT_GROUPS:
        band = (delta >= 0) & (delta <= win // dil)
        bucket = np.where(band, _t5_causal_bucket(np.maximum(delta, 0) * dil), -1)
        out.append(bucket)
    return np.stack(out).astype(np.int32)


def _att_unit(q, k, v, bias):
    BLK = ATT_BLOCK
    s = _dot_nt(q, k) + bias
    m = jnp.max(jnp.maximum(s[:, :BLK], s[:, BLK:]), axis=-1, keepdims=True)
    p = jnp.exp2((s - m).astype(BF16))
    v_ext = jnp.concatenate([v, jnp.ones_like(v)], axis=1)
    pv = _dot(p, v_ext)
    denom = pv[:, BLK:]
    return pv[:, :BLK] / denom, m + jnp.log2(denom)


def _att_kernel(tab_ref, bucket_ref, *refs, tiles_per_batch, heads_per_step, buckets_used, n_prev):
    groups, pos = [], 0
    for n in n_prev:
        q, k, v = refs[pos:pos + 3]
        groups.append((q, k, v, refs[pos + 3:pos + 3 + n], refs[pos + 3 + n:pos + 3 + 2 * n]))
        pos += 3 + 2 * n
    o_ref, bias_scr, o_scr, lse_scr = refs[pos:]
    BLK = ATT_BLOCK
    hd = ATT_HEAD_DIM
    ti = pl.program_id(0)
    h0 = pl.program_id(1) * heads_per_step

    @pl.when((ti == 0) & (h0 == 0))
    def _():
        for g in range(N_ATT_GROUPS):
            bucket = bucket_ref[g]
            bias = [jnp.full(bucket.shape, NEG_INF, F32) for _ in range(ATT_HEADS_PER_GROUP)]
            for b in buckets_used[g]:
                hit = bucket == b
                for hh in range(ATT_HEADS_PER_GROUP):
                    bias[hh] = jnp.where(hit, tab_ref[b, g * ATT_HEADS_PER_GROUP + hh] * LOG2_E, bias[hh])
            for hh in range(ATT_HEADS_PER_GROUP):
                bias_scr[g, hh] = bias[hh]

    lane = lax.broadcasted_iota(jnp.int32, (BLK, 2 * BLK), 1)
    no_prev = jnp.logical_and((ti % tiles_per_batch) == 0, lane < BLK)
    for hh in range(heads_per_step):
        hc = pl.ds(hh * hd, hd)
        for g, (q, k, v, kps, vps) in enumerate(groups):
            dil = ATT_DILS[g]
            n_r = TILE // dil
            bias = bias_scr[g, h0 + hh]
            bias_first = jnp.where(no_prev, NEG_INF, bias)
            for r in range(dil):
                for j in range(n_r // BLK):
                    cur = pl.ds(r * n_r + j * BLK, BLK)
                    if j > 0:
                        both = pl.ds(r * n_r + (j - 1) * BLK, 2 * BLK)
                        k_all, v_all, b = k[both, hc], v[both, hc], bias
                    else:
                        if len(kps) == dil:
                            kp, vp, prev = kps[r], vps[r], pl.ds(0, BLK)
                        else:
                            kp, vp, prev = kps[0], vps[0], pl.ds(r * n_r + n_r - BLK, BLK)
                        k_all = jnp.concatenate([kp[prev, hc], k[cur, hc]], axis=0)
                        v_all = jnp.concatenate([vp[prev, hc], v[cur, hc]], axis=0)
                        b = bias_first
                    if dil == 1:
                        rows = pl.ds(j * BLK, BLK)
                    else:
                        rows = pl.ds(dil * j * BLK + r, BLK, stride=dil)
                    o_scr[g, rows, :], lse_scr[g, rows, :] = _att_unit(q[cur, hc], k_all, v_all, b)

        for c in range(TILE // 256):
            rows = pl.ds(c * 256, 256)
            lse = [lse_scr[g, rows, :] for g in range(N_ATT_GROUPS)]
            top = functools.reduce(jnp.maximum, lse)
            w = [jnp.exp2(x - top) for x in lse]
            num = functools.reduce(lambda a, b: a + b,
                                   [w[g] * o_scr[g, rows, :] for g in range(N_ATT_GROUPS)])
            den = functools.reduce(lambda a, b: a + b, w)
            o_ref[rows, hc] = (num / den).astype(BF16)


def _attention(y2, rel_bias, S, heads_per_step=2):
    T = y2.shape[0]
    table = _bucket_table()
    tiles_per_batch = S // TILE
    hd = ATT_HEAD_DIM
    bw = heads_per_step * hd

    def col(g, c):
        return (COL_ATT + g * ATT_GROUP_W + c * ATT_W) // bw

    def cur(g, c):
        return pl.BlockSpec((TILE, bw), lambda t, h: (t, col(g, c) + h))

    def prev_tile(g, c):
        return [pl.BlockSpec((TILE, bw), lambda t, h: (jnp.maximum(t - 1, 0), col(g, c) + h))]

    def prev_blocks(g, c):
        nb, per_class = TILE // ATT_BLOCK, TILE // ATT_DILS[g] // ATT_BLOCK
        return [pl.BlockSpec((ATT_BLOCK, bw),
                             functools.partial(lambda t, h, last: (jnp.maximum((t - 1) * nb + last, 0), col(g, c) + h),
                                               last=(r + 1) * per_class - 1))
                for r in range(ATT_DILS[g])]

    prev_spec = [prev_blocks if d * ATT_BLOCK < TILE else prev_tile for d in ATT_DILS]
    in_specs = [pl.BlockSpec(memory_space=pltpu.SMEM),
                pl.BlockSpec((N_ATT_GROUPS, ATT_BLOCK, 2 * ATT_BLOCK), lambda t, h: (0, 0, 0))]
    for g in range(N_ATT_GROUPS):
        in_specs += [cur(g, 0), cur(g, 1), cur(g, 2)] + prev_spec[g](g, 1) + prev_spec[g](g, 2)
    n_prev = tuple(len(prev_spec[g](g, 1)) for g in range(N_ATT_GROUPS))
    return pl.pallas_call(
        functools.partial(_att_kernel, tiles_per_batch=tiles_per_batch, heads_per_step=heads_per_step,
                          buckets_used=tuple(tuple(int(b) for b in np.unique(t) if b >= 0) for t in table),
                          n_prev=n_prev),
        grid=(T // TILE, ATT_HEADS_PER_GROUP // heads_per_step),
        in_specs=in_specs,
        out_specs=pl.BlockSpec((TILE, bw), lambda t, h: (t, h)),
        out_shape=jax.ShapeDtypeStruct((T, ATT_W), BF16),
        scratch_shapes=[
            pltpu.VMEM((N_ATT_GROUPS, ATT_HEADS_PER_GROUP, ATT_BLOCK, 2 * ATT_BLOCK), F32),
            pltpu.VMEM((N_ATT_GROUPS, TILE, hd), F32),
            pltpu.VMEM((N_ATT_GROUPS, TILE, LANES), F32),
        ],
        compiler_params=_params(("arbitrary", "arbitrary")),
        name="attention",
    )(rel_bias, jnp.asarray(table), *([y2] * (len(in_specs) - 2)))


def _tail_kernel(x_ref, og_ref, oa_ref, ga_ref, gb_ref, p_ref, wog_ref, woa_ref, wout_ref,
                 ln2_ref, w1_ref, w2_ref, ln3_ref, wpg_ref, wpp_ref, lnf_ref, out_ref, *, tf):
    y_gla = _dot(og_ref[...], wog_ref[...])
    y_att = _dot(oa_ref[...], woa_ref[...])
    mix = (jax.nn.sigmoid(ga_ref[...].astype(F32)) * y_gla
           + jax.nn.sigmoid(gb_ref[...].astype(F32)) * y_att)
    x1 = x_ref[...] + _dot(mix.astype(BF16), wout_ref[...])

    h = _rms(x1, ln2_ref[...]).astype(BF16)
    x2 = x1
    for f in range(D_FF // tf):
        cols = pl.ds(f * tf, tf)
        u = jnp.maximum(_dot(h, w1_ref[:, cols]), 0.0)
        x2 = x2 + _dot((u * u).astype(BF16), w2_ref[cols, :])

    h3 = _rms(x2, ln3_ref[...]).astype(BF16)
    gate = jax.nn.sigmoid(_dot(h3, wpg_ref[...]))
    x3 = x2 + gate * _dot(p_ref[...].astype(BF16), wpp_ref[...])
    out_ref[...] = _rms(x3, lnf_ref[...])


def _tail(x2, y2, o_gla, o_att, p2, wog, woa, wout, ln2, w1, w2, ln3, wpg, wpp, lnf, tm=512, tf=1024):
    T = x2.shape[0]
    gcol = COL_GATE // D_MODEL
    row = lambda w: pl.BlockSpec((tm, w), lambda i: (i, 0))
    const = lambda a: pl.BlockSpec(a.shape, lambda i: (0, 0), pipeline_mode=pl.Buffered(1))
    return pl.pallas_call(
        functools.partial(_tail_kernel, tf=tf),
        grid=(T // tm,),
        in_specs=[row(D_MODEL), row(GLA_V_W), row(ATT_W),
                  pl.BlockSpec((tm, D_MODEL), lambda i: (i, gcol)),
                  pl.BlockSpec((tm, D_MODEL), lambda i: (i, gcol + 1)),
                  row(PLE_DIM),
                  const(wog), const(woa), const(wout), const(ln2), const(w1), const(w2),
                  const(ln3), const(wpg), const(wpp), const(lnf)],
        out_specs=row(D_MODEL),
        out_shape=jax.ShapeDtypeStruct((T, D_MODEL), F32),
        compiler_params=_params(("arbitrary",)),
        name="tail",
    )(x2, o_gla, o_att, y2, y2, p2, wog, woa, wout, ln2, w1, w2, ln3, wpg, wpp, lnf)


def _layer(x2, p2, B, S, ln1, w_in, w_a2, b_a, gla_gn, w_o_gla, w_o_attn, w_out,
           ln2, w_mlp1, w_mlp2, ln3, w_pp, w_pg, rel_bias, ln_out):
    wa2 = jnp.pad(w_a2, ((0, LANES - GLA_RANK), (0, 0)))

    w_in_t = w_in.T
    y2, ha = _inproj(x2, ln1[None], _wprep(w_in_t), w_in_t)
    o_gla = _gla(y2.reshape(B, S, Y_COLS), ha.reshape(B, S, LANES), wa2, b_a[None], gla_gn[None])
    o_att = _attention(y2, rel_bias, S)
    return _tail(x2, y2, o_gla.reshape(B * S, GLA_V_W), o_att, p2,
                 w_o_gla.astype(BF16), w_o_attn.astype(BF16), w_out.astype(BF16), ln2[None],
                 w_mlp1.astype(BF16), w_mlp2.astype(BF16), ln3[None],
                 w_pg.astype(BF16), w_pp.astype(BF16), ln_out[None])


def kernel(x, p, ln1, w_in, w_a2, b_a, gla_gn, w_o_gla, w_o_attn, w_out, ln2, w_mlp1, w_mlp2,
           ln3, w_pp, w_pg, rel_bias, ln_f):
    B, S, D = x.shape
    assert p.shape[0] == 1, "the final norm is fused into the single layer's last kernel"
    assert S % TILE == 0
    x2 = x.reshape(B * S, D)
    out = _layer(x2, p[0].reshape(B * S, PLE_DIM), B, S, ln1[0], w_in[0], w_a2[0], b_a[0],
                 gla_gn[0], w_o_gla[0], w_o_attn[0], w_out[0], ln2[0], w_mlp1[0], w_mlp2[0],
                 ln3[0], w_pp[0], w_pg[0], rel_bias, ln_f)
    return out.reshape(B, S, D)
```

```python
import functools

import numpy as np
import jax
import jax.numpy as jnp
from jax import lax
from jax.experimental import pallas as pl
from jax.experimental.pallas import tpu as pltpu

F32 = jnp.float32
BF16 = jnp.bfloat16

D_MODEL = 1024
PLE_DIM = 256
EPS = 1e-6
GLA_HEADS = 4
GLA_DK = 128
GLA_DV = 256
GLA_RANK = 16
GLA_TAU = 16.0
GLA_CHUNK = 64
GLA_QK_W = GLA_HEADS * GLA_DK
GLA_V_W = GLA_HEADS * GLA_DV
ATT_GROUPS = ((128, 1), (512, 4), (2048, 16))
ATT_DILS = tuple(d for _, d in ATT_GROUPS)
ATT_HEADS_PER_GROUP = 4
ATT_HEAD_DIM = 128
N_ATT_GROUPS = len(ATT_GROUPS)
ATT_W = ATT_HEADS_PER_GROUP * ATT_HEAD_DIM
ATT_BLOCK = 128
REL_BUCKETS = 32
REL_MAX_DIST = 2048
D_FF = 4 * D_MODEL
NEG_INF = -1e30
LOG2_E = 1.4426950408889634

LANES = 128

COL_Q = 0
COL_K = COL_Q + GLA_QK_W
COL_V = COL_K + GLA_QK_W
COL_R = COL_V + GLA_V_W
COL_GATE = COL_R + GLA_V_W
COL_ATT = COL_GATE + 2 * D_MODEL
ATT_GROUP_W = 3 * ATT_W
Y_COLS = COL_ATT + N_ATT_GROUPS * ATT_GROUP_W

TILE = max(ATT_DILS) * ATT_BLOCK

VMEM_LIMIT = 56 * 1024 * 1024


def _params(semantics):
    return pltpu.CompilerParams(dimension_semantics=semantics, vmem_limit_bytes=VMEM_LIMIT)


def _rms(x, g):
    return x * lax.rsqrt(jnp.mean(x * x, axis=-1, keepdims=True) + EPS) * g


def _dot(a, b):
    return jnp.dot(a, b, preferred_element_type=F32)


def _dot_nt(a, b):
    return lax.dot_general(a, b, (((1,), (1,)), ((), ())), preferred_element_type=F32)


def _dot_tn(a, b):
    return lax.dot_general(a, b, (((0,), (0,)), ((), ())), preferred_element_type=F32)


def _split3(g):
    hi = g.astype(BF16)
    r1 = g - hi.astype(F32)
    mid = r1.astype(BF16)
    lo = (r1 - mid.astype(F32)).astype(BF16)
    return hi, mid, lo


W_GATE_COL = COL_R + GLA_V_W
W_ATT_COL = W_GATE_COL + GLA_RANK
W_MIXGATE_COL = W_ATT_COL + N_ATT_GROUPS * ATT_GROUP_W


def _wprep_kernel(scale_ref, a_ref, b_ref, o_ref, *, n_aligned, tn):
    t = pl.program_id(0)
    scale = scale_ref[t]

    @pl.when(t < n_aligned)
    def _():
        o_ref[...] = (a_ref[...] * scale).astype(BF16)

    @pl.when(t >= n_aligned)
    def _():
        w = jnp.concatenate([a_ref[...], b_ref[...]], axis=0)[GLA_RANK:GLA_RANK + tn]
        o_ref[...] = (w * scale).astype(BF16)


def _wprep(w_in_t, tn=512):
    n_aligned = W_GATE_COL // tn
    n_att = (W_MIXGATE_COL - W_ATT_COL) // tn
    n_tiles = Y_COLS // tn

    def out_tile(t, where):
        return where(t < n_aligned, t,
                     where(t < n_aligned + n_att, t - n_aligned + COL_ATT // tn,
                           t - n_aligned - n_att + COL_GATE // tn))

    q_sections = [(COL_Q, COL_K, GLA_DK ** -0.5)] + [
        (COL_ATT + g * ATT_GROUP_W, COL_ATT + g * ATT_GROUP_W + ATT_W, ATT_HEAD_DIM ** -0.5 * LOG2_E)
        for g in range(N_ATT_GROUPS)]
    assert all(lo % tn == 0 and hi % tn == 0 for lo, hi, _ in q_sections)
    tile_scale = np.ones((n_tiles,), np.float32)
    for t in range(n_tiles):
        col = out_tile(t, lambda c, a, b: a if c else b) * tn
        for lo, hi, sc in q_sections:
            if lo <= col < hi:
                tile_scale[t] = sc
    out_index = lambda t: (out_tile(t, jnp.where), 0)
    return pl.pallas_call(
        functools.partial(_wprep_kernel, n_aligned=n_aligned, tn=tn),
        grid=(n_tiles,),
        in_specs=[
            pl.BlockSpec(memory_space=pltpu.SMEM),
            pl.BlockSpec((tn, D_MODEL), lambda t: (t, 0)),
            pl.BlockSpec((GLA_RANK, D_MODEL), lambda t: ((t + 1) * (tn // GLA_RANK), 0)),
        ],
        out_specs=pl.BlockSpec((tn, D_MODEL), out_index),
        out_shape=jax.ShapeDtypeStruct((Y_COLS, D_MODEL), BF16),
        compiler_params=_params(("arbitrary",)),
        name="wprep",
    )(jnp.asarray(tile_scale), w_in_t, w_in_t)


def _inproj_kernel(*refs, tile_orders):
    n_slab = D_MODEL // LANES
    x_slabs = refs[:n_slab]
    ln_ref, w_ref, wa1_ref, y_ref, ha_ref, h_scr, inv_scr, tmp_scr = refs[n_slab:]
    j = pl.program_id(1)
    chunk = 256
    assert ATT_DILS == (1, 4, 16) and all(order == (0, 0) for order in tile_orders[:2])

    def project(d_left, d_right):
        half = y_ref.shape[1] // 2
        y_ref[:, :half] = _dot_nt(h_scr[d_left], w_ref[:half, :]).astype(BF16)
        if d_right is not None:
            y_ref[:, half:] = _dot_nt(h_scr[d_right], w_ref[half:, :]).astype(BF16)

    def normalise():
        for c in range(TILE // chunk):
            rows = pl.ds(c * chunk, chunk)
            parts = [xr[rows, :] for xr in x_slabs]
            sq = parts[0] * parts[0]
            for part in parts[1:]:
                sq = sq + part * part
            ms = jnp.sum(sq, axis=-1, keepdims=True) * (1.0 / D_MODEL)
            inv = jnp.broadcast_to(lax.rsqrt(ms + EPS), (chunk, LANES))
            inv_scr[rows, :] = inv
            for s, part in enumerate(parts):
                cols = pl.ds(s * LANES, LANES)
                h_scr[0, rows, cols] = (part * inv * ln_ref[:, cols]).astype(BF16)

    def regroup(slabs):
        n4, n16 = TILE // 4, TILE // 16
        for s in slabs:
            xr = x_slabs[s]
            cols = pl.ds(s * LANES, LANES)
            ln_s = ln_ref[:, cols]
            for a in range(4):
                for c in range(n4 // chunk):
                    src = pl.ds(a + 4 * c * chunk, chunk, stride=4)
                    dst = pl.ds(a * n4 + c * chunk, chunk)
                    hn = xr[src, :] * inv_scr[src, :] * ln_s
                    tmp_scr[dst, :] = hn
                    h_scr[1, dst, cols] = hn.astype(BF16)
            for a in range(4):
                for b in range(4):
                    src = pl.ds(a * n4 + b, n16, stride=4)
                    h_scr[2, pl.ds((a + 4 * b) * n16, n16), cols] = tmp_scr[src, :].astype(BF16)

    def gate_input():
        wa1_t = jnp.concatenate([wa1_ref[...], jnp.zeros((LANES - GLA_RANK, D_MODEL), F32)], axis=0)
        ha_ref[...] = _dot_nt(h_scr[0], wa1_t.astype(BF16))

    def normalise_and_gate():
        normalise()
        gate_input()

    prep = {0: normalise_and_gate, 1: functools.partial(regroup, range(n_slab))}
    special = [t for t, (left, right) in enumerate(tile_orders) if t in prep or left != right]
    for t in special:
        @pl.when(j == t)
        def _(t=t):
            if t in prep:
                prep[t]()
            project(*tile_orders[t])

    @pl.when(functools.reduce(jnp.logical_and, [j != t for t in special]))
    def _():
        firsts = [min(t for t, o in enumerate(tile_orders) if o[0] == d) for d in (1, 2)]
        d = sum((j >= f).astype(jnp.int32) for f in firsts)
        project(d, d)


def _inproj(x2, ln1, w_main, w_in_t, tn=1024):
    T = x2.shape[0]
    n_slab = D_MODEL // LANES
    half = tn // 2
    bounds = [COL_ATT + ATT_GROUP_W * (g + 1) for g in range(N_ATT_GROUPS)]
    order_of = lambda col: None if col >= Y_COLS else sum(col >= b for b in bounds)
    n_tiles = pl.cdiv(Y_COLS, tn)
    tile_orders = [(order_of(t * tn), order_of(t * tn + half)) for t in range(n_tiles)]
    assert all(b % half == 0 for b in bounds) and Y_COLS % half == 0
    tile_orders[-2:] = tile_orders[:-3:-1]
    tile_orders = tuple(tile_orders)

    def tile_of(j):
        return jnp.where(j == n_tiles - 2, n_tiles - 1, jnp.where(j == n_tiles - 1, n_tiles - 2, j))

    def slab_index(i, j, s):
        moved = (j >= min(2 + s, n_tiles - 1)).astype(jnp.int32)
        return jnp.minimum(i + moved, T // TILE - 1), s

    return pl.pallas_call(
        functools.partial(_inproj_kernel, tile_orders=tile_orders),
        grid=(T // TILE, n_tiles),
        in_specs=[pl.BlockSpec((TILE, LANES), functools.partial(slab_index, s=s))
                  for s in range(n_slab)] + [
            pl.BlockSpec((1, D_MODEL), lambda i, j: (0, 0)),
            pl.BlockSpec((tn, D_MODEL), lambda i, j: (tile_of(j), 0)),
            pl.BlockSpec((GLA_RANK, D_MODEL), lambda i, j: (W_GATE_COL // GLA_RANK, 0)),
        ],
        out_specs=[
            pl.BlockSpec((TILE, tn), lambda i, j: (i, tile_of(j))),
            pl.BlockSpec((TILE, LANES), lambda i, j: (i, 0)),
        ],
        out_shape=[
            jax.ShapeDtypeStruct((T, Y_COLS), BF16),
            jax.ShapeDtypeStruct((T, LANES), F32),
        ],
        scratch_shapes=[pltpu.VMEM((N_ATT_GROUPS, TILE, D_MODEL), BF16),
                        pltpu.VMEM((TILE, LANES), F32),
                        pltpu.VMEM((TILE, LANES), F32)],
        compiler_params=_params(("arbitrary", "arbitrary")),
        name="inproj",
    )(*([x2] * n_slab), ln1, w_main, w_in_t)


def _gla_kernel(q_ref, k_ref, v_ref, r_ref, ha_ref, wa2_ref, ba_ref, gn_ref, o_ref, st_ref, la_scr,
                *, nchunk):
    C = GLA_CHUNK

    @pl.when(pl.program_id(1) == 0)
    def _():
        st_ref[...] = jnp.zeros_like(st_ref)

    row = lax.broadcasted_iota(jnp.int32, (C, C), 0)
    col = lax.broadcasted_iota(jnp.int32, (C, C), 1)
    causal = row >= col
    tri = causal.astype(BF16)

    a_hi, a_mid, _ = _split3(ha_ref[0])
    w_hi, w_mid, _ = _split3(wa2_ref[...])
    z = _dot(a_hi, w_hi) + (_dot(a_mid, w_hi) + _dot(a_hi, w_mid)) + ba_ref[...]
    la_scr[...] = (jnp.minimum(z, 0.0) - jnp.log(1.0 + jnp.exp(-jnp.abs(z)))) * (1.0 / GLA_TAU)

    H = range(GLA_HEADS)
    ck = [slice(h * GLA_DK, (h + 1) * GLA_DK) for h in H]
    cv = [slice(h * GLA_DV, (h + 1) * GLA_DV) for h in H]

    def cumsum(c):
        hi, mid, _ = _split3(la_scr[pl.ds(c * C, C), :])
        return _dot(tri, hi) + _dot(tri, mid)

    def products(c, b, st):
        sl = pl.ds(c * C, C)
        decay = jnp.exp(b[C - 1:C, :])
        q_dec = q_ref[0, sl, :] * jnp.exp(b).astype(BF16)
        k_in = k_ref[0, sl, :] * jnp.exp(-b).astype(BF16)
        k_out = k_in * decay.astype(BF16)
        v = [v_ref[0, sl, cv[h]] for h in H]
        attn = [_dot_nt(q_dec[:, ck[h]], k_in[:, ck[h]]) for h in H]
        o_inter = [_dot(q_dec[:, ck[h]], st[h].astype(BF16)) for h in H]
        upd = [_dot_tn(k_out[:, ck[h]], v[h]) for h in H]
        new_st = []
        for h in H:
            dcol = jnp.broadcast_to(decay[:, ck[h]], (GLA_DK, GLA_DK)).T
            new_st.append(st[h] * jnp.concatenate([dcol] * (GLA_DV // GLA_DK), axis=1) + upd[h])
        return new_st, (attn, o_inter, v)

    def output(c, attn, o_inter, v):
        sl = pl.ds(c * C, C)
        for h in H:
            o = _dot(jnp.where(causal, attn[h], 0.0).astype(BF16), v[h]) + o_inter[h]
            r = r_ref[0, sl, cv[h]]
            o_ref[0, sl, cv[h]] = _rms(o, gn_ref[:, cv[h]]).astype(BF16) * (r * jax.nn.sigmoid(r))

    st = [st_ref[h] for h in H]
    b = {0: cumsum(0)}
    if nchunk > 1:
        b[1] = cumsum(1)
    st, pending = products(0, b.pop(0), st)
    for c in range(nchunk):
        if c + 2 < nchunk:
            b[c + 2] = cumsum(c + 2)
        if c + 1 < nchunk:
            st, nxt = products(c + 1, b.pop(c + 1), st)
        output(c, *pending)
        if c + 1 < nchunk:
            pending = nxt
    for h in H:
        st_ref[h] = st[h]


def _gla(y3, ha3, wa2, ba, gn, tb=1024):
    B, S, _ = y3.shape
    return pl.pallas_call(
        functools.partial(_gla_kernel, nchunk=tb // GLA_CHUNK),
        grid=(B, S // tb),
        in_specs=[
            pl.BlockSpec((1, tb, GLA_QK_W), lambda b, t: (b, t, COL_Q // GLA_QK_W)),
            pl.BlockSpec((1, tb, GLA_QK_W), lambda b, t: (b, t, COL_K // GLA_QK_W)),
            pl.BlockSpec((1, tb, GLA_V_W), lambda b, t: (b, t, COL_V // GLA_V_W)),
            pl.BlockSpec((1, tb, GLA_V_W), lambda b, t: (b, t, COL_R // GLA_V_W)),
            pl.BlockSpec((1, tb, LANES), lambda b, t: (b, t, 0)),
            pl.BlockSpec((LANES, GLA_QK_W), lambda b, t: (0, 0)),
            pl.BlockSpec((1, GLA_QK_W), lambda b, t: (0, 0)),
            pl.BlockSpec((1, GLA_V_W), lambda b, t: (0, 0)),
        ],
        out_specs=pl.BlockSpec((1, tb, GLA_V_W), lambda b, t: (b, t, 0)),
        out_shape=jax.ShapeDtypeStruct((B, S, GLA_V_W), BF16),
        scratch_shapes=[pltpu.VMEM((GLA_HEADS, GLA_DK, GLA_DV), F32), pltpu.VMEM((tb, GLA_QK_W), F32)],
        compiler_params=_params(("arbitrary", "arbitrary")),
        name="gla",
    )(y3, y3, y3, y3, ha3, wa2, ba, gn)


def _t5_causal_bucket(n):
    max_exact = REL_BUCKETS // 2
    nf = np.maximum(n, 1).astype(np.float32)
    large = max_exact + (np.log(nf / max_exact) / np.log(REL_MAX_DIST / max_exact)
                         * (REL_BUCKETS - max_exact)).astype(np.int32)
    large = np.minimum(large, REL_BUCKETS - 1)
    return np.where(n < max_exact, n, large).astype(np.int32)


def _bucket_table():
    qi = np.arange(ATT_BLOCK)[:, None]
    kj = np.arange(2 * ATT_BLOCK)[None, :]
    delta = qi + ATT_BLOCK - kj
    out = []
    for win, dil in ATT_GROUPS:
        band = (delta >= 0) & (delta <= win // dil)
        bucket = np.where(band, _t5_causal_bucket(np.maximum(delta, 0) * dil), -1)
        out.append(bucket)
    return np.stack(out).astype(np.int32)


def _att_unit(q, k, v, bias):
    BLK = ATT_BLOCK
    s = _dot_nt(q, k) + bias
    m = jnp.max(jnp.maximum(s[:, :BLK], s[:, BLK:]), axis=-1, keepdims=True)
    p = jnp.exp2((s - m).astype(BF16))
    v_ext = jnp.concatenate([v, jnp.ones_like(v)], axis=1)
    pv = _dot(p, v_ext)
    denom = pv[:, BLK:]
    return pv[:, :BLK] / denom, m + jnp.log2(denom)


def _att_kernel(tab_ref, bucket_ref, *refs, tiles_per_batch, heads_per_step, buckets_used, n_prev):
    groups, pos = [], 0
    for n in n_prev:
        q, k, v = refs[pos:pos + 3]
        groups.append((q, k, v, refs[pos + 3:pos + 3 + n], refs[pos + 3 + n:pos + 3 + 2 * n]))
        pos += 3 + 2 * n
    o_ref, bias_scr, o_scr, lse_scr = refs[pos:]
    BLK = ATT_BLOCK
    hd = ATT_HEAD_DIM
    ti = pl.program_id(0)
    h0 = pl.program_id(1) * heads_per_step

    @pl.when((ti == 0) & (h0 == 0))
    def _():
        for g in range(N_ATT_GROUPS):
            bucket = bucket_ref[g]
            bias = [jnp.full(bucket.shape, NEG_INF, F32) for _ in range(ATT_HEADS_PER_GROUP)]
            for b in buckets_used[g]:
                hit = bucket == b
                for hh in range(ATT_HEADS_PER_GROUP):
                    bias[hh] = jnp.where(hit, tab_ref[b, g * ATT_HEADS_PER_GROUP + hh] * LOG2_E, bias[hh])
            for hh in range(ATT_HEADS_PER_GROUP):
                bias_scr[g, hh] = bias[hh]

    lane = lax.broadcasted_iota(jnp.int32, (BLK, 2 * BLK), 1)
    no_prev = jnp.logical_and((ti % tiles_per_batch) == 0, lane < BLK)
    for hh in range(heads_per_step):
        hc = pl.ds(hh * hd, hd)
        for g, (q, k, v, kps, vps) in enumerate(groups):
            dil = ATT_DILS[g]
            n_r = TILE // dil
            bias = bias_scr[g, h0 + hh]
            bias_first = jnp.where(no_prev, NEG_INF, bias)
            for r in range(dil):
                for j in range(n_r // BLK):
                    cur = pl.ds(r * n_r + j * BLK, BLK)
                    if j > 0:
                        both = pl.ds(r * n_r + (j - 1) * BLK, 2 * BLK)
                        k_all, v_all, b = k[both, hc], v[both, hc], bias
                    else:
                        if len(kps) == dil:
                            kp, vp, prev = kps[r], vps[r], pl.ds(0, BLK)
                        else:
                            kp, vp, prev = kps[0], vps[0], pl.ds(r * n_r + n_r - BLK, BLK)
                        k_all = jnp.concatenate([kp[prev, hc], k[cur, hc]], axis=0)
                        v_all = jnp.concatenate([vp[prev, hc], v[cur, hc]], axis=0)
                        b = bias_first
                    if dil == 1:
                        rows = pl.ds(j * BLK, BLK)
                    else:
                        rows = pl.ds(dil * j * BLK + r, BLK, stride=dil)
                    o_scr[g, rows, :], lse_scr[g, rows, :] = _att_unit(q[cur, hc], k_all, v_all, b)

        for c in range(TILE // 256):
            rows = pl.ds(c * 256, 256)
            lse = [lse_scr[g, rows, :] for g in range(N_ATT_GROUPS)]
            top = functools.reduce(jnp.maximum, lse)
            w = [jnp.exp2(x - top) for x in lse]
            num = functools.reduce(lambda a, b: a + b,
                                   [w[g] * o_scr[g, rows, :] for g in range(N_ATT_GROUPS)])
            den = functools.reduce(lambda a, b: a + b, w)
            o_ref[rows, hc] = (num / den).astype(BF16)


def _attention(y2, rel_bias, S, heads_per_step=2):
    T = y2.shape[0]
    table = _bucket_table()
    tiles_per_batch = S // TILE
    hd = ATT_HEAD_DIM
    bw = heads_per_step * hd

    def col(g, c):
        return (COL_ATT + g * ATT_GROUP_W + c * ATT_W) // bw

    def cur(g, c):
        return pl.BlockSpec((TILE, bw), lambda t, h: (t, col(g, c) + h))

    def prev_tile(g, c):
        return [pl.BlockSpec((TILE, bw), lambda t, h: (jnp.maximum(t - 1, 0), col(g, c) + h))]

    def prev_blocks(g, c):
        nb, per_class = TILE // ATT_BLOCK, TILE // ATT_DILS[g] // ATT_BLOCK
        return [pl.BlockSpec((ATT_BLOCK, bw),
                             functools.partial(lambda t, h, last: (jnp.maximum((t - 1) * nb + last, 0), col(g, c) + h),
                                               last=(r + 1) * per_class - 1))
                for r in range(ATT_DILS[g])]

    prev_spec = [prev_blocks if d * ATT_BLOCK < TILE else prev_tile for d in ATT_DILS]
    in_specs = [pl.BlockSpec(memory_space=pltpu.SMEM),
                pl.BlockSpec((N_ATT_GROUPS, ATT_BLOCK, 2 * ATT_BLOCK), lambda t, h: (0, 0, 0))]
    for g in range(N_ATT_GROUPS):
        in_specs += [cur(g, 0), cur(g, 1), cur(g, 2)] + prev_spec[g](g, 1) + prev_spec[g](g, 2)
    n_prev = tuple(len(prev_spec[g](g, 1)) for g in range(N_ATT_GROUPS))
    return pl.pallas_call(
        functools.partial(_att_kernel, tiles_per_batch=tiles_per_batch, heads_per_step=heads_per_step,
                          buckets_used=tuple(tuple(int(b) for b in np.unique(t) if b >= 0) for t in table),
                          n_prev=n_prev),
        grid=(T // TILE, ATT_HEADS_PER_GROUP // heads_per_step),
        in_specs=in_specs,
        out_specs=pl.BlockSpec((TILE, bw), lambda t, h: (t, h)),
        out_shape=jax.ShapeDtypeStruct((T, ATT_W), BF16),
        scratch_shapes=[
            pltpu.VMEM((N_ATT_GROUPS, ATT_HEADS_PER_GROUP, ATT_BLOCK, 2 * ATT_BLOCK), F32),
            pltpu.VMEM((N_ATT_GROUPS, TILE, hd), F32),
            pltpu.VMEM((N_ATT_GROUPS, TILE, LANES), F32),
        ],
        compiler_params=_params(("arbitrary", "arbitrary")),
        name="attention",
    )(rel_bias, jnp.asarray(table), *([y2] * (len(in_specs) - 2)))


def _tail_kernel(x_ref, og_ref, oa_ref, ga_ref, gb_ref, p_ref, wog_ref, woa_ref, wout_ref,
                 ln2_ref, w1_ref, w2_ref, ln3_ref, wpg_ref, wpp_ref, lnf_ref, out_ref, *, tf):
    y_gla = _dot(og_ref[...], wog_ref[...])
    y_att = _dot(oa_ref[...], woa_ref[...])
    mix = (jax.nn.sigmoid(ga_ref[...].astype(F32)) * y_gla
           + jax.nn.sigmoid(gb_ref[...].astype(F32)) * y_att)
    x1 = x_ref[...] + _dot(mix.astype(BF16), wout_ref[...])

    h = _rms(x1, ln2_ref[...]).astype(BF16)
    x2 = x1
    for f in range(D_FF // tf):
        cols = pl.ds(f * tf, tf)
        u = jnp.maximum(_dot(h, w1_ref[:, cols]), 0.0)
        x2 = x2 + _dot((u * u).astype(BF16), w2_ref[cols, :])

    h3 = _rms(x2, ln3_ref[...]).astype(BF16)
    gate = jax.nn.sigmoid(_dot(h3, wpg_ref[...]))
    x3 = x2 + gate * _dot(p_ref[...].astype(BF16), wpp_ref[...])
    out_ref[...] = _rms(x3, lnf_ref[...])


def _tail(x2, y2, o_gla, o_att, p2, wog, woa, wout, ln2, w1, w2, ln3, wpg, wpp, lnf, tm=512, tf=1024):
    T = x2.shape[0]
    gcol = COL_GATE // D_MODEL
    row = lambda w: pl.BlockSpec((tm, w), lambda i: (i, 0))
    const = lambda a: pl.BlockSpec(a.shape, lambda i: (0, 0), pipeline_mode=pl.Buffered(1))
    return pl.pallas_call(
        functools.partial(_tail_kernel, tf=tf),
        grid=(T // tm,),
        in_specs=[row(D_MODEL), row(GLA_V_W), row(ATT_W),
                  pl.BlockSpec((tm, D_MODEL), lambda i: (i, gcol)),
                  pl.BlockSpec((tm, D_MODEL), lambda i: (i, gcol + 1)),
                  row(PLE_DIM),
                  const(wog), const(woa), const(wout), const(ln2), const(w1), const(w2),
                  const(ln3), const(wpg), const(wpp), const(lnf)],
        out_specs=row(D_MODEL),
        out_shape=jax.ShapeDtypeStruct((T, D_MODEL), F32),
        compiler_params=_params(("arbitrary",)),
        name="tail",
    )(x2, o_gla, o_att, y2, y2, p2, wog, woa, wout, ln2, w1, w2, ln3, wpg, wpp, lnf)


def _layer(x2, p2, B, S, ln1, w_in, w_a2, b_a, gla_gn, w_o_gla, w_o_attn, w_out,
           ln2, w_mlp1, w_mlp2, ln3, w_pp, w_pg, rel_bias, ln_out):
    wa2 = jnp.pad(w_a2, ((0, LANES - GLA_RANK), (0, 0)))

    w_in_t = w_in.T
    y2, ha = _inproj(x2, ln1[None], _wprep(w_in_t), w_in_t)
    o_gla = _gla(y2.reshape(B, S, Y_COLS), ha.reshape(B, S, LANES), wa2, b_a[None], gla_gn[None])
    o_att = _attention(y2, rel_bias, S)
    return _tail(x2, y2, o_gla.reshape(B * S, GLA_V_W), o_att, p2,
                 w_o_gla.astype(BF16), w_o_attn.astype(BF16), w_out.astype(BF16), ln2[None],
                 w_mlp1.astype(BF16), w_mlp2.astype(BF16), ln3[None],
                 w_pg.astype(BF16), w_pp.astype(BF16), ln_out[None])


def kernel(x, p, ln1, w_in, w_a2, b_a, gla_gn, w_o_gla, w_o_attn, w_out, ln2, w_mlp1, w_mlp2,
           ln3, w_pp, w_pg, rel_bias, ln_f):
    B, S, D = x.shape
    assert p.shape[0] == 1, "the final norm is fused into the single layer's last kernel"
    assert S % TILE == 0
    x2 = x.reshape(B * S, D)
    out = _layer(x2, p[0].reshape(B * S, PLE_DIM), B, S, ln1[0], w_in[0], w_a2[0], b_a[0],
                 gla_gn[0], w_o_gla[0], w_o_attn[0], w_out[0], ln2[0], w_mlp1[0], w_mlp2[0],
                 ln3[0], w_pp[0], w_pg[0], rel_bias, ln_f)
    return out.reshape(B, S, D)
```

```python
import functools

import numpy as np
import jax
import jax.numpy as jnp
from jax import lax
from jax.experimental import pallas as pl
from jax.experimental.pallas import tpu as pltpu

F32 = jnp.float32
BF16 = jnp.bfloat16

D_MODEL = 1024
PLE_DIM = 256
EPS = 1e-6
GLA_HEADS = 4
GLA_DK = 128
GLA_DV = 256
GLA_RANK = 16
GLA_TAU = 16.0
GLA_CHUNK = 64
GLA_QK_W = GLA_HEADS * GLA_DK
GLA_V_W = GLA_HEADS * GLA_DV
ATT_GROUPS = ((128, 1), (512, 4), (2048, 16))
ATT_DILS = tuple(d for _, d in ATT_GROUPS)
ATT_HEADS_PER_GROUP = 4
ATT_HEAD_DIM = 128
N_ATT_GROUPS = len(ATT_GROUPS)
ATT_W = ATT_HEADS_PER_GROUP * ATT_HEAD_DIM
ATT_BLOCK = 128
REL_BUCKETS = 32
REL_MAX_DIST = 2048
D_FF = 4 * D_MODEL
NEG_INF = -1e30
LOG2_E = 1.4426950408889634

LANES = 128

COL_Q = 0
COL_K = COL_Q + GLA_QK_W
COL_V = COL_K + GLA_QK_W
COL_R = COL_V + GLA_V_W
COL_GATE = COL_R + GLA_V_W
COL_ATT = COL_GATE + 2 * D_MODEL
ATT_GROUP_W = 3 * ATT_W
Y_COLS = COL_ATT + N_ATT_GROUPS * ATT_GROUP_W

TILE = max(ATT_DILS) * ATT_BLOCK

VMEM_LIMIT = 56 * 1024 * 1024


def _params(semantics):
    return pltpu.CompilerParams(dimension_semantics=semantics, vmem_limit_bytes=VMEM_LIMIT)


def _rms(x, g):
    return x * lax.rsqrt(jnp.mean(x * x, axis=-1, keepdims=True) + EPS) * g


def _dot(a, b):
    return jnp.dot(a, b, preferred_element_type=F32)


def _dot_nt(a, b):
    return lax.dot_general(a, b, (((1,), (1,)), ((), ())), preferred_element_type=F32)


def _dot_tn(a, b):
    return lax.dot_general(a, b, (((0,), (0,)), ((), ())), preferred_element_type=F32)


def _split3(g):
    hi = g.astype(BF16)
    r1 = g - hi.astype(F32)
    mid = r1.astype(BF16)
    lo = (r1 - mid.astype(F32)).astype(BF16)
    return hi, mid, lo


W_GATE_COL = COL_R + GLA_V_W
W_ATT_COL = W_GATE_COL + GLA_RANK
W_MIXGATE_COL = W_ATT_COL + N_ATT_GROUPS * ATT_GROUP_W


def _wprep_kernel(scale_ref, a_ref, b_ref, o_ref, *, n_aligned, tn):
    t = pl.program_id(0)
    scale = scale_ref[t]

    @pl.when(t < n_aligned)
    def _():
        o_ref[...] = (a_ref[...] * scale).astype(BF16)

    @pl.when(t >= n_aligned)
    def _():
        w = jnp.concatenate([a_ref[...], b_ref[...]], axis=0)[GLA_RANK:GLA_RANK + tn]
        o_ref[...] = (w * scale).astype(BF16)


def _wprep(w_in_t, tn=512):
    n_aligned = W_GATE_COL // tn
    n_att = (W_MIXGATE_COL - W_ATT_COL) // tn
    n_tiles = Y_COLS // tn

    def out_tile(t, where):
        return where(t < n_aligned, t,
                     where(t < n_aligned + n_att, t - n_aligned + COL_ATT // tn,
                           t - n_aligned - n_att + COL_GATE // tn))

    q_sections = [(COL_Q, COL_K, GLA_DK ** -0.5)] + [
        (COL_ATT + g * ATT_GROUP_W, COL_ATT + g * ATT_GROUP_W + ATT_W, ATT_HEAD_DIM ** -0.5 * LOG2_E)
        for g in range(N_ATT_GROUPS)]
    assert all(lo % tn == 0 and hi % tn == 0 for lo, hi, _ in q_sections)
    tile_scale = np.ones((n_tiles,), np.float32)
    for t in range(n_tiles):
        col = out_tile(t, lambda c, a, b: a if c else b) * tn
        for lo, hi, sc in q_sections:
            if lo <= col < hi:
                tile_scale[t] = sc
    out_index = lambda t: (out_tile(t, jnp.where), 0)
    return pl.pallas_call(
        functools.partial(_wprep_kernel, n_aligned=n_aligned, tn=tn),
        grid=(n_tiles,),
        in_specs=[
            pl.BlockSpec(memory_space=pltpu.SMEM),
            pl.BlockSpec((tn, D_MODEL), lambda t: (t, 0)),
            pl.BlockSpec((GLA_RANK, D_MODEL), lambda t: ((t + 1) * (tn // GLA_RANK), 0)),
        ],
        out_specs=pl.BlockSpec((tn, D_MODEL), out_index),
        out_shape=jax.ShapeDtypeStruct((Y_COLS, D_MODEL), BF16),
        compiler_params=_params(("arbitrary",)),
        name="wprep",
    )(jnp.asarray(tile_scale), w_in_t, w_in_t)


def _inproj_kernel(*refs, tile_orders):
    n_slab = D_MODEL // LANES
    x_slabs = refs[:n_slab]
    ln_ref, w_ref, wa1_ref, y_ref, ha_ref, h_scr, inv_scr, tmp_scr = refs[n_slab:]
    j = pl.program_id(1)
    chunk = 256
    assert ATT_DILS == (1, 4, 16) and all(order == (0, 0) for order in tile_orders[:2])

    def project(d_left, d_right):
        half = y_ref.shape[1] // 2
        y_ref[:, :half] = _dot_nt(h_scr[d_left], w_ref[:half, :]).astype(BF16)
        if d_right is not None:
            y_ref[:, half:] = _dot_nt(h_scr[d_right], w_ref[half:, :]).astype(BF16)

    def normalise():
        for c in range(TILE // chunk):
            rows = pl.ds(c * chunk, chunk)
            parts = [xr[rows, :] for xr in x_slabs]
            sq = parts[0] * parts[0]
            for part in parts[1:]:
                sq = sq + part * part
            ms = jnp.sum(sq, axis=-1, keepdims=True) * (1.0 / D_MODEL)
            inv = jnp.broadcast_to(lax.rsqrt(ms + EPS), (chunk, LANES))
            inv_scr[rows, :] = inv
            for s, part in enumerate(parts):
                cols = pl.ds(s * LANES, LANES)
                h_scr[0, rows, cols] = (part * inv * ln_ref[:, cols]).astype(BF16)

    def regroup(slabs):
        n4, n16 = TILE // 4, TILE // 16
        for s in slabs:
            xr = x_slabs[s]
            cols = pl.ds(s * LANES, LANES)
            ln_s = ln_ref[:, cols]
            for a in range(4):
                for c in range(n4 // chunk):
                    src = pl.ds(a + 4 * c * chunk, chunk, stride=4)
                    dst = pl.ds(a * n4 + c * chunk, chunk)
                    hn = xr[src, :] * inv_scr[src, :] * ln_s
                    tmp_scr[dst, :] = hn
                    h_scr[1, dst, cols] = hn.astype(BF16)
            for a in range(4):
                for b in range(4):
                    src = pl.ds(a * n4 + b, n16, stride=4)
                    h_scr[2, pl.ds((a + 4 * b) * n16, n16), cols] = tmp_scr[src, :].astype(BF16)

    def gate_input():
        wa1_t = jnp.concatenate([wa1_ref[...], jnp.zeros((LANES - GLA_RANK, D_MODEL), F32)], axis=0)
        ha_ref[...] = _dot_nt(h_scr[0], wa1_t.astype(BF16))

    def normalise_and_gate():
        normalise()
        gate_input()

    prep = {0: normalise_and_gate, 1: functools.partial(regroup, range(n_slab))}
    special = [t for t, (left, right) in enumerate(tile_orders) if t in prep or left != right]
    for t in special:
        @pl.when(j == t)
        def _(t=t):
            if t in prep:
                prep[t]()
            project(*tile_orders[t])

    @pl.when(functools.reduce(jnp.logical_and, [j != t for t in special]))
    def _():
        firsts = [min(t for t, o in enumerate(tile_orders) if o[0] == d) for d in (1, 2)]
        d = sum((j >= f).astype(jnp.int32) for f in firsts)
        project(d, d)


def _inproj(x2, ln1, w_main, w_in_t, tn=1024):
    T = x2.shape[0]
    n_slab = D_MODEL // LANES
    half = tn // 2
    bounds = [COL_ATT + ATT_GROUP_W * (g + 1) for g in range(N_ATT_GROUPS)]
    order_of = lambda col: None if col >= Y_COLS else sum(col >= b for b in bounds)
    n_tiles = pl.cdiv(Y_COLS, tn)
    tile_orders = [(order_of(t * tn), order_of(t * tn + half)) for t in range(n_tiles)]
    assert all(b % half == 0 for b in bounds) and Y_COLS % half == 0
    tile_orders[-2:] = tile_orders[:-3:-1]
    tile_orders = tuple(tile_orders)

    def tile_of(j):
        return jnp.where(j == n_tiles - 2, n_tiles - 1, jnp.where(j == n_tiles - 1, n_tiles - 2, j))

    def slab_index(i, j, s):
        moved = (j >= min(2 + s, n_tiles - 1)).astype(jnp.int32)
        return jnp.minimum(i + moved, T // TILE - 1), s

    return pl.pallas_call(
        functools.partial(_inproj_kernel, tile_orders=tile_orders),
        grid=(T // TILE, n_tiles),
        in_specs=[pl.BlockSpec((TILE, LANES), functools.partial(slab_index, s=s))
                  for s in range(n_slab)] + [
            pl.BlockSpec((1, D_MODEL), lambda i, j: (0, 0)),
            pl.BlockSpec((tn, D_MODEL), lambda i, j: (tile_of(j), 0)),
            pl.BlockSpec((GLA_RANK, D_MODEL), lambda i, j: (W_GATE_COL // GLA_RANK, 0)),
        ],
        out_specs=[
            pl.BlockSpec((TILE, tn), lambda i, j: (i, tile_of(j))),
            pl.BlockSpec((TILE, LANES), lambda i, j: (i, 0)),
        ],
        out_shape=[
            jax.ShapeDtypeStruct((T, Y_COLS), BF16),
            jax.ShapeDtypeStruct((T, LANES), F32),
        ],
        scratch_shapes=[pltpu.VMEM((N_ATT_GROUPS, TILE, D_MODEL), BF16),
                        pltpu.VMEM((TILE, LANES), F32),
                        pltpu.VMEM((TILE, LANES), F32)],
        compiler_params=_params(("arbitrary", "arbitrary")),
        name="inproj",
    )(*([x2] * n_slab), ln1, w_main, w_in_t)


def _gla_kernel(q_ref, k_ref, v_ref, r_ref, ha_ref, wa2_ref, ba_ref, gn_ref, o_ref, st_ref, la_scr,
                *, nchunk):
    C = GLA_CHUNK

    @pl.when(pl.program_id(1) == 0)
    def _():
        st_ref[...] = jnp.zeros_like(st_ref)

    row = lax.broadcasted_iota(jnp.int32, (C, C), 0)
    col = lax.broadcasted_iota(jnp.int32, (C, C), 1)
    causal = row >= col
    tri = causal.astype(BF16)

    a_hi, a_mid, _ = _split3(ha_ref[0])
    w_hi, w_mid, _ = _split3(wa2_ref[...])
    z = _dot(a_hi, w_hi) + (_dot(a_mid, w_hi) + _dot(a_hi, w_mid)) + ba_ref[...]
    la_scr[...] = (jnp.minimum(z, 0.0) - jnp.log(1.0 + jnp.exp(-jnp.abs(z)))) * (1.0 / GLA_TAU)

    H = range(GLA_HEADS)
    ck = [slice(h * GLA_DK, (h + 1) * GLA_DK) for h in H]
    cv = [slice(h * GLA_DV, (h + 1) * GLA_DV) for h in H]

    def cumsum(c):
        hi, mid, _ = _split3(la_scr[pl.ds(c * C, C), :])
        return _dot(tri, hi) + _dot(tri, mid)

    def products(c, b, st):
        sl = pl.ds(c * C, C)
        decay = jnp.exp(b[C - 1:C, :])
        q_dec = q_ref[0, sl, :] * jnp.exp(b).astype(BF16)
        k_in = k_ref[0, sl, :] * jnp.exp(-b).astype(BF16)
        k_out = k_in * decay.astype(BF16)
        v = [v_ref[0, sl, cv[h]] for h in H]
        attn = [_dot_nt(q_dec[:, ck[h]], k_in[:, ck[h]]) for h in H]
        o_inter = [_dot(q_dec[:, ck[h]], st[h].astype(BF16)) for h in H]
        upd = [_dot_tn(k_out[:, ck[h]], v[h]) for h in H]
        new_st = []
        for h in H:
            dcol = jnp.broadcast_to(decay[:, ck[h]], (GLA_DK, GLA_DK)).T
            new_st.append(st[h] * jnp.concatenate([dcol] * (GLA_DV // GLA_DK), axis=1) + upd[h])
        return new_st, (attn, o_inter, v)

    def output(c, attn, o_inter, v):
        sl = pl.ds(c * C, C)
        for h in H:
            o = _dot(jnp.where(causal, attn[h], 0.0).astype(BF16), v[h]) + o_inter[h]
            r = r_ref[0, sl, cv[h]]
            o_ref[0, sl, cv[h]] = _rms(o, gn_ref[:, cv[h]]).astype(BF16) * (r * jax.nn.sigmoid(r))

    st = [st_ref[h] for h in H]
    b = {0: cumsum(0)}
    if nchunk > 1:
        b[1] = cumsum(1)
    st, pending = products(0, b.pop(0), st)
    for c in range(nchunk):
        if c + 2 < nchunk:
            b[c + 2] = cumsum(c + 2)
        if c + 1 < nchunk:
            st, nxt = products(c + 1, b.pop(c + 1), st)
        output(c, *pending)
        if c + 1 < nchunk:
            pending = nxt
    for h in H:
        st_ref[h] = st[h]


def _gla(y3, ha3, wa2, ba, gn, tb=2048):
    B, S, _ = y3.shape
    return pl.pallas_call(
        functools.partial(_gla_kernel, nchunk=tb // GLA_CHUNK),
        grid=(B, S // tb),
        in_specs=[
            pl.BlockSpec((1, tb, GLA_QK_W), lambda b, t: (b, t, COL_Q // GLA_QK_W)),
            pl.BlockSpec((1, tb, GLA_QK_W), lambda b, t: (b, t, COL_K // GLA_QK_W)),
            pl.BlockSpec((1, tb, GLA_V_W), lambda b, t: (b, t, COL_V // GLA_V_W)),
            pl.BlockSpec((1, tb, GLA_V_W), lambda b, t: (b, t, COL_R // GLA_V_W)),
            pl.BlockSpec((1, tb, LANES), lambda b, t: (b, t, 0)),
            pl.BlockSpec((LANES, GLA_QK_W), lambda b, t: (0, 0)),
            pl.BlockSpec((1, GLA_QK_W), lambda b, t: (0, 0)),
            pl.BlockSpec((1, GLA_V_W), lambda b, t: (0, 0)),
        ],
        out_specs=pl.BlockSpec((1, tb, GLA_V_W), lambda b, t: (b, t, 0)),
        out_shape=jax.ShapeDtypeStruct((B, S, GLA_V_W), BF16),
        scratch_shapes=[pltpu.VMEM((GLA_HEADS, GLA_DK, GLA_DV), F32), pltpu.VMEM((tb, GLA_QK_W), F32)],
        compiler_params=_params(("arbitrary", "arbitrary")),
        name="gla",
    )(y3, y3, y3, y3, ha3, wa2, ba, gn)


def _t5_causal_bucket(n):
    max_exact = REL_BUCKETS // 2
    nf = np.maximum(n, 1).astype(np.float32)
    large = max_exact + (np.log(nf / max_exact) / np.log(REL_MAX_DIST / max_exact)
                         * (REL_BUCKETS - max_exact)).astype(np.int32)
    large = np.minimum(large, REL_BUCKETS - 1)
    return np.where(n < max_exact, n, large).astype(np.int32)


def _bucket_table():
    qi = np.arange(ATT_BLOCK)[:, None]
    kj = np.arange(2 * ATT_BLOCK)[None, :]
    delta = qi + ATT_BLOCK - kj
    out = []
    for win, dil in ATT_GROUPS:
        band = (delta >= 0) & (delta <= win // dil)
        bucket = np.where(band, _t5_causal_bucket(np.maximum(delta, 0) * dil), -1)
        out.append(bucket)
    return np.stack(out).astype(np.int32)


def _att_unit(q, k, v, bias):
    BLK = ATT_BLOCK
    s = _dot_nt(q, k) + bias
    m = jnp.max(jnp.maximum(s[:, :BLK], s[:, BLK:]), axis=-1, keepdims=True)
    p = jnp.exp2((s - m).astype(BF16))
    v_ext = jnp.concatenate([v, jnp.ones_like(v)], axis=1)
    pv = _dot(p, v_ext)
    denom = pv[:, BLK:]
    return pv[:, :BLK] / denom, m + jnp.log2(denom)


def _att_kernel(tab_ref, bucket_ref, *refs, tiles_per_batch, heads_per_step, buckets_used, n_prev):
    groups, pos = [], 0
    for n in n_prev:
        q, k, v = refs[pos:pos + 3]
        groups.append((q, k, v, refs[pos + 3:pos + 3 + n], refs[pos + 3 + n:pos + 3 + 2 * n]))
        pos += 3 + 2 * n
    o_ref, bias_scr, o_scr, lse_scr = refs[pos:]
    BLK = ATT_BLOCK
    hd = ATT_HEAD_DIM
    ti = pl.program_id(0)
    h0 = pl.program_id(1) * heads_per_step

    @pl.when((ti == 0) & (h0 == 0))
    def _():
        for g in range(N_ATT_GROUPS):
            bucket = bucket_ref[g]
            bias = [jnp.full(bucket.shape, NEG_INF, F32) for _ in range(ATT_HEADS_PER_GROUP)]
            for b in buckets_used[g]:
                hit = bucket == b
                for hh in range(ATT_HEADS_PER_GROUP):
                    bias[hh] = jnp.where(hit, tab_ref[b, g * ATT_HEADS_PER_GROUP + hh] * LOG2_E, bias[hh])
            for hh in range(ATT_HEADS_PER_GROUP):
                bias_scr[g, hh] = bias[hh]

    lane = lax.broadcasted_iota(jnp.int32, (BLK, 2 * BLK), 1)
    no_prev = jnp.logical_and((ti % tiles_per_batch) == 0, lane < BLK)
    for hh in range(heads_per_step):
        hc = pl.ds(hh * hd, hd)
        for g, (q, k, v, kps, vps) in enumerate(groups):
            dil = ATT_DILS[g]
            n_r = TILE // dil
            bias = bias_scr[g, h0 + hh]
            bias_first = jnp.where(no_prev, NEG_INF, bias)
            for r in range(dil):
                for j in range(n_r // BLK):
                    cur = pl.ds(r * n_r + j * BLK, BLK)
                    if j > 0:
                        both = pl.ds(r * n_r + (j - 1) * BLK, 2 * BLK)
                        k_all, v_all, b = k[both, hc], v[both, hc], bias
                    else:
                        if len(kps) == dil:
                            kp, vp, prev = kps[r], vps[r], pl.ds(0, BLK)
                        else:
                            kp, vp, prev = kps[0], vps[0], pl.ds(r * n_r + n_r - BLK, BLK)
                        k_all = jnp.concatenate([kp[prev, hc], k[cur, hc]], axis=0)
                        v_all = jnp.concatenate([vp[prev, hc], v[cur, hc]], axis=0)
                        b = bias_first
                    if dil == 1:
                        rows = pl.ds(j * BLK, BLK)
                    else:
                        rows = pl.ds(dil * j * BLK + r, BLK, stride=dil)
                    o_scr[g, rows, :], lse_scr[g, rows, :] = _att_unit(q[cur, hc], k_all, v_all, b)

        for c in range(TILE // 256):
            rows = pl.ds(c * 256, 256)
            lse = [lse_scr[g, rows, :] for g in range(N_ATT_GROUPS)]
            top = functools.reduce(jnp.maximum, lse)
            w = [jnp.exp2(x - top) for x in lse]
            num = functools.reduce(lambda a, b: a + b,
                                   [w[g] * o_scr[g, rows, :] for g in range(N_ATT_GROUPS)])
            den = functools.reduce(lambda a, b: a + b, w)
            o_ref[rows, hc] = (num / den).astype(BF16)


def _attention(y2, rel_bias, S, heads_per_step=2):
    T = y2.shape[0]
    table = _bucket_table()
    tiles_per_batch = S // TILE
    hd = ATT_HEAD_DIM
    bw = heads_per_step * hd

    def col(g, c):
        return (COL_ATT + g * ATT_GROUP_W + c * ATT_W) // bw

    def cur(g, c):
        return pl.BlockSpec((TILE, bw), lambda t, h: (t, col(g, c) + h))

    def prev_tile(g, c):
        return [pl.BlockSpec((TILE, bw), lambda t, h: (jnp.maximum(t - 1, 0), col(g, c) + h))]

    def prev_blocks(g, c):
        nb, per_class = TILE // ATT_BLOCK, TILE // ATT_DILS[g] // ATT_BLOCK
        return [pl.BlockSpec((ATT_BLOCK, bw),
                             functools.partial(lambda t, h, last: (jnp.maximum((t - 1) * nb + last, 0), col(g, c) + h),
                                               last=(r + 1) * per_class - 1))
                for r in range(ATT_DILS[g])]

    prev_spec = [prev_blocks if d * ATT_BLOCK < TILE else prev_tile for d in ATT_DILS]
    in_specs = [pl.BlockSpec(memory_space=pltpu.SMEM),
                pl.BlockSpec((N_ATT_GROUPS, ATT_BLOCK, 2 * ATT_BLOCK), lambda t, h: (0, 0, 0))]
    for g in range(N_ATT_GROUPS):
        in_specs += [cur(g, 0), cur(g, 1), cur(g, 2)] + prev_spec[g](g, 1) + prev_spec[g](g, 2)
    n_prev = tuple(len(prev_spec[g](g, 1)) for g in range(N_ATT_GROUPS))
    return pl.pallas_call(
        functools.partial(_att_kernel, tiles_per_batch=tiles_per_batch, heads_per_step=heads_per_step,
                          buckets_used=tuple(tuple(int(b) for b in np.unique(t) if b >= 0) for t in table),
                          n_prev=n_prev),
        grid=(T // TILE, ATT_HEADS_PER_GROUP // heads_per_step),
        in_specs=in_specs,
        out_specs=pl.BlockSpec((TILE, bw), lambda t, h: (t, h)),
        out_shape=jax.ShapeDtypeStruct((T, ATT_W), BF16),
        scratch_shapes=[
            pltpu.VMEM((N_ATT_GROUPS, ATT_HEADS_PER_GROUP, ATT_BLOCK, 2 * ATT_BLOCK), F32),
            pltpu.VMEM((N_ATT_GROUPS, TILE, hd), F32),
            pltpu.VMEM((N_ATT_GROUPS, TILE, LANES), F32),
        ],
        compiler_params=_params(("arbitrary", "arbitrary")),
        name="attention",
    )(rel_bias, jnp.asarray(table), *([y2] * (len(in_specs) - 2)))


def _tail_kernel(x_ref, og_ref, oa_ref, ga_ref, gb_ref, p_ref, wog_ref, woa_ref, wout_ref,
                 ln2_ref, w1_ref, w2_ref, ln3_ref, wpg_ref, wpp_ref, lnf_ref, out_ref, *, tf):
    y_gla = _dot(og_ref[...], wog_ref[...])
    y_att = _dot(oa_ref[...], woa_ref[...])
    mix = (jax.nn.sigmoid(ga_ref[...].astype(F32)) * y_gla
           + jax.nn.sigmoid(gb_ref[...].astype(F32)) * y_att)
    x1 = x_ref[...] + _dot(mix.astype(BF16), wout_ref[...])

    h = _rms(x1, ln2_ref[...]).astype(BF16)
    x2 = x1
    for f in range(D_FF // tf):
        cols = pl.ds(f * tf, tf)
        u = jnp.maximum(_dot(h, w1_ref[:, cols]), 0.0)
        x2 = x2 + _dot((u * u).astype(BF16), w2_ref[cols, :])

    h3 = _rms(x2, ln3_ref[...]).astype(BF16)
    gate = jax.nn.sigmoid(_dot(h3, wpg_ref[...]))
    x3 = x2 + gate * _dot(p_ref[...].astype(BF16), wpp_ref[...])
    out_ref[...] = _rms(x3, lnf_ref[...])


def _tail(x2, y2, o_gla, o_att, p2, wog, woa, wout, ln2, w1, w2, ln3, wpg, wpp, lnf, tm=512, tf=1024):
    T = x2.shape[0]
    gcol = COL_GATE // D_MODEL
    row = lambda w: pl.BlockSpec((tm, w), lambda i: (i, 0))
    const = lambda a: pl.BlockSpec(a.shape, lambda i: (0, 0), pipeline_mode=pl.Buffered(1))
    return pl.pallas_call(
        functools.partial(_tail_kernel, tf=tf),
        grid=(T // tm,),
        in_specs=[row(D_MODEL), row(GLA_V_W), row(ATT_W),
                  pl.BlockSpec((tm, D_MODEL), lambda i: (i, gcol)),
                  pl.BlockSpec((tm, D_MODEL), lambda i: (i, gcol + 1)),
                  row(PLE_DIM),
                  const(wog), const(woa), const(wout), const(ln2), const(w1), const(w2),
                  const(ln3), const(wpg), const(wpp), const(lnf)],
        out_specs=row(D_MODEL),
        out_shape=jax.ShapeDtypeStruct((T, D_MODEL), F32),
        compiler_params=_params(("arbitrary",)),
        name="tail",
    )(x2, o_gla, o_att, y2, y2, p2, wog, woa, wout, ln2, w1, w2, ln3, wpg, wpp, lnf)


def _layer(x2, p2, B, S, ln1, w_in, w_a2, b_a, gla_gn, w_o_gla, w_o_attn, w_out,
           ln2, w_mlp1, w_mlp2, ln3, w_pp, w_pg, rel_bias, ln_out):
    wa2 = jnp.pad(w_a2, ((0, LANES - GLA_RANK), (0, 0)))

    w_in_t = w_in.T
    y2, ha = _inproj(x2, ln1[None], _wprep(w_in_t), w_in_t)
    o_gla = _gla(y2.reshape(B, S, Y_COLS), ha.reshape(B, S, LANES), wa2, b_a[None], gla_gn[None])
    o_att = _attention(y2, rel_bias, S)
    return _tail(x2, y2, o_gla.reshape(B * S, GLA_V_W), o_att, p2,
                 w_o_gla.astype(BF16), w_o_attn.astype(BF16), w_out.astype(BF16), ln2[None],
                 w_mlp1.astype(BF16), w_mlp2.astype(BF16), ln3[None],
                 w_pg.astype(BF16), w_pp.astype(BF16), ln_out[None])


def kernel(x, p, ln1, w_in, w_a2, b_a, gla_gn, w_o_gla, w_o_attn, w_out, ln2, w_mlp1, w_mlp2,
           ln3, w_pp, w_pg, rel_bias, ln_f):
    B, S, D = x.shape
    assert p.shape[0] == 1, "the final norm is fused into the single layer's last kernel"
    assert S % TILE == 0
    x2 = x.reshape(B * S, D)
    out = _layer(x2, p[0].reshape(B * S, PLE_DIM), B, S, ln1[0], w_in[0], w_a2[0], b_a[0],
                 gla_gn[0], w_o_gla[0], w_o_attn[0], w_out[0], ln2[0], w_mlp1[0], w_mlp2[0],
                 ln3[0], w_pp[0], w_pg[0], rel_bias, ln_f)
    return out.reshape(B, S, D)
```

```python
import functools

import numpy as np
import jax
import jax.numpy as jnp
from jax import lax
from jax.experimental import pallas as pl
from jax.experimental.pallas import tpu as pltpu

F32 = jnp.float32
BF16 = jnp.bfloat16

D_MODEL = 1024
PLE_DIM = 256
EPS = 1e-6
GLA_HEADS = 4
GLA_DK = 128
GLA_DV = 256
GLA_RANK = 16
GLA_TAU = 16.0
GLA_CHUNK = 64
GLA_QK_W = GLA_HEADS * GLA_DK
GLA_V_W = GLA_HEADS * GLA_DV
ATT_GROUPS = ((128, 1), (512, 4), (2048, 16))
ATT_DILS = tuple(d for _, d in ATT_GROUPS)
ATT_HEADS_PER_GROUP = 4
ATT_HEAD_DIM = 128
N_ATT_GROUPS = len(ATT_GROUPS)
ATT_W = ATT_HEADS_PER_GROUP * ATT_HEAD_DIM
ATT_BLOCK = 128
REL_BUCKETS = 32
REL_MAX_DIST = 2048
D_FF = 4 * D_MODEL
NEG_INF = -1e30
LOG2_E = 1.4426950408889634

LANES = 128

COL_Q = 0
COL_K = COL_Q + GLA_QK_W
COL_V = COL_K + GLA_QK_W
COL_R = COL_V + GLA_V_W
COL_GATE = COL_R + GLA_V_W
COL_ATT = COL_GATE + 2 * D_MODEL
ATT_GROUP_W = 3 * ATT_W
Y_COLS = COL_ATT + N_ATT_GROUPS * ATT_GROUP_W

TILE = max(ATT_DILS) * ATT_BLOCK

VMEM_LIMIT = 56 * 1024 * 1024


def _params(semantics):
    return pltpu.CompilerParams(dimension_semantics=semantics, vmem_limit_bytes=VMEM_LIMIT)


def _rms(x, g):
    return x * lax.rsqrt(jnp.mean(x * x, axis=-1, keepdims=True) + EPS) * g


def _dot(a, b):
    return jnp.dot(a, b, preferred_element_type=F32)


def _dot_nt(a, b):
    return lax.dot_general(a, b, (((1,), (1,)), ((), ())), preferred_element_type=F32)


def _dot_tn(a, b):
    return lax.dot_general(a, b, (((0,), (0,)), ((), ())), preferred_element_type=F32)


def _split3(g):
    hi = g.astype(BF16)
    r1 = g - hi.astype(F32)
    mid = r1.astype(BF16)
    lo = (r1 - mid.astype(F32)).astype(BF16)
    return hi, mid, lo


W_GATE_COL = COL_R + GLA_V_W
W_ATT_COL = W_GATE_COL + GLA_RANK
W_MIXGATE_COL = W_ATT_COL + N_ATT_GROUPS * ATT_GROUP_W


def _wprep_kernel(scale_ref, a_ref, b_ref, o_ref, *, n_aligned, tn):
    t = pl.program_id(0)
    scale = scale_ref[t]

    @pl.when(t < n_aligned)
    def _():
        o_ref[...] = (a_ref[...] * scale).astype(BF16)

    @pl.when(t >= n_aligned)
    def _():
        w = jnp.concatenate([a_ref[...], b_ref[...]], axis=0)[GLA_RANK:GLA_RANK + tn]
        o_ref[...] = (w * scale).astype(BF16)


def _wprep(w_in_t, tn=512):
    n_aligned = W_GATE_COL // tn
    n_att = (W_MIXGATE_COL - W_ATT_COL) // tn
    n_tiles = Y_COLS // tn

    def out_tile(t, where):
        return where(t < n_aligned, t,
                     where(t < n_aligned + n_att, t - n_aligned + COL_ATT // tn,
                           t - n_aligned - n_att + COL_GATE // tn))

    q_sections = [(COL_Q, COL_K, GLA_DK ** -0.5)] + [
        (COL_ATT + g * ATT_GROUP_W, COL_ATT + g * ATT_GROUP_W + ATT_W, ATT_HEAD_DIM ** -0.5 * LOG2_E)
        for g in range(N_ATT_GROUPS)]
    assert all(lo % tn == 0 and hi % tn == 0 for lo, hi, _ in q_sections)
    tile_scale = np.ones((n_tiles,), np.float32)
    for t in range(n_tiles):
        col = out_tile(t, lambda c, a, b: a if c else b) * tn
        for lo, hi, sc in q_sections:
            if lo <= col < hi:
                tile_scale[t] = sc
    out_index = lambda t: (out_tile(t, jnp.where), 0)
    return pl.pallas_call(
        functools.partial(_wprep_kernel, n_aligned=n_aligned, tn=tn),
        grid=(n_tiles,),
        in_specs=[
            pl.BlockSpec(memory_space=pltpu.SMEM),
            pl.BlockSpec((tn, D_MODEL), lambda t: (t, 0)),
            pl.BlockSpec((GLA_RANK, D_MODEL), lambda t: ((t + 1) * (tn // GLA_RANK), 0)),
        ],
        out_specs=pl.BlockSpec((tn, D_MODEL), out_index),
        out_shape=jax.ShapeDtypeStruct((Y_COLS, D_MODEL), BF16),
        compiler_params=_params(("arbitrary",)),
        name="wprep",
    )(jnp.asarray(tile_scale), w_in_t, w_in_t)


def _inproj_kernel(*refs, tile_orders):
    n_slab = D_MODEL // LANES
    x_slabs = refs[:n_slab]
    ln_ref, w_ref, wa1_ref, y_ref, ha_ref, h_scr, inv_scr, tmp_scr = refs[n_slab:]
    j = pl.program_id(1)
    chunk = 256
    assert ATT_DILS == (1, 4, 16) and all(order == (0, 0) for order in tile_orders[:2])

    def project(d_left, d_right):
        half = y_ref.shape[1] // 2
        y_ref[:, :half] = _dot_nt(h_scr[d_left], w_ref[:half, :]).astype(BF16)
        if d_right is not None:
            y_ref[:, half:] = _dot_nt(h_scr[d_right], w_ref[half:, :]).astype(BF16)

    def normalise():
        for c in range(TILE // chunk):
            rows = pl.ds(c * chunk, chunk)
            parts = [xr[rows, :] for xr in x_slabs]
            sq = parts[0] * parts[0]
            for part in parts[1:]:
                sq = sq + part * part
            ms = jnp.sum(sq, axis=-1, keepdims=True) * (1.0 / D_MODEL)
            inv = jnp.broadcast_to(lax.rsqrt(ms + EPS), (chunk, LANES))
            inv_scr[rows, :] = inv
            for s, part in enumerate(parts):
                cols = pl.ds(s * LANES, LANES)
                h_scr[0, rows, cols] = (part * inv * ln_ref[:, cols]).astype(BF16)

    def regroup(slabs):
        n4, n16 = TILE // 4, TILE // 16
        for s in slabs:
            xr = x_slabs[s]
            cols = pl.ds(s * LANES, LANES)
            ln_s = ln_ref[:, cols]
            for a in range(4):
                for c in range(n4 // chunk):
                    src = pl.ds(a + 4 * c * chunk, chunk, stride=4)
                    dst = pl.ds(a * n4 + c * chunk, chunk)
                    hn = xr[src, :] * inv_scr[src, :] * ln_s
                    tmp_scr[dst, :] = hn
                    h_scr[1, dst, cols] = hn.astype(BF16)
            for a in range(4):
                for b in range(4):
                    src = pl.ds(a * n4 + b, n16, stride=4)
                    h_scr[2, pl.ds((a + 4 * b) * n16, n16), cols] = tmp_scr[src, :].astype(BF16)

    def gate_input():
        wa1_t = jnp.concatenate([wa1_ref[...], jnp.zeros((LANES - GLA_RANK, D_MODEL), F32)], axis=0)
        ha_ref[...] = _dot_nt(h_scr[0], wa1_t.astype(BF16))

    def normalise_and_gate():
        normalise()
        gate_input()

    prep = {0: normalise_and_gate, 1: functools.partial(regroup, range(n_slab))}
    special = [t for t, (left, right) in enumerate(tile_orders) if t in prep or left != right]
    for t in special:
        @pl.when(j == t)
        def _(t=t):
            if t in prep:
                prep[t]()
            project(*tile_orders[t])

    @pl.when(functools.reduce(jnp.logical_and, [j != t for t in special]))
    def _():
        firsts = [min(t for t, o in enumerate(tile_orders) if o[0] == d) for d in (1, 2)]
        d = sum((j >= f).astype(jnp.int32) for f in firsts)
        project(d, d)


def _inproj(x2, ln1, w_main, w_in_t, tn=1024):
    T = x2.shape[0]
    n_slab = D_MODEL // LANES
    half = tn // 2
    bounds = [COL_ATT + ATT_GROUP_W * (g + 1) for g in range(N_ATT_GROUPS)]
    order_of = lambda col: None if col >= Y_COLS else sum(col >= b for b in bounds)
    n_tiles = pl.cdiv(Y_COLS, tn)
    tile_orders = [(order_of(t * tn), order_of(t * tn + half)) for t in range(n_tiles)]
    assert all(b % half == 0 for b in bounds) and Y_COLS % half == 0
    tile_orders[-2:] = tile_orders[:-3:-1]
    tile_orders = tuple(tile_orders)

    def tile_of(j):
        return jnp.where(j == n_tiles - 2, n_tiles - 1, jnp.where(j == n_tiles - 1, n_tiles - 2, j))

    def slab_index(i, j, s):
        moved = (j >= min(2 + s, n_tiles - 1)).astype(jnp.int32)
        return jnp.minimum(i + moved, T // TILE - 1), s

    return pl.pallas_call(
        functools.partial(_inproj_kernel, tile_orders=tile_orders),
        grid=(T // TILE, n_tiles),
        in_specs=[pl.BlockSpec((TILE, LANES), functools.partial(slab_index, s=s))
                  for s in range(n_slab)] + [
            pl.BlockSpec((1, D_MODEL), lambda i, j: (0, 0)),
            pl.BlockSpec((tn, D_MODEL), lambda i, j: (tile_of(j), 0)),
            pl.BlockSpec((GLA_RANK, D_MODEL), lambda i, j: (W_GATE_COL // GLA_RANK, 0)),
        ],
        out_specs=[
            pl.BlockSpec((TILE, tn), lambda i, j: (i, tile_of(j))),
            pl.BlockSpec((TILE, LANES), lambda i, j: (i, 0)),
        ],
        out_shape=[
            jax.ShapeDtypeStruct((T, Y_COLS), BF16),
            jax.ShapeDtypeStruct((T, LANES), F32),
        ],
        scratch_shapes=[pltpu.VMEM((N_ATT_GROUPS, TILE, D_MODEL), BF16),
                        pltpu.VMEM((TILE, LANES), F32),
                        pltpu.VMEM((TILE, LANES), F32)],
        compiler_params=_params(("arbitrary", "arbitrary")),
        name="inproj",
    )(*([x2] * n_slab), ln1, w_main, w_in_t)


def _gla_kernel(q_ref, k_ref, v_ref, r_ref, ha_ref, wa2_ref, ba_ref, gn_ref, o_ref, st_ref, *, nchunk):
    C = GLA_CHUNK

    @pl.when(pl.program_id(1) == 0)
    def _():
        st_ref[...] = jnp.zeros_like(st_ref)

    row = lax.broadcasted_iota(jnp.int32, (C, C), 0)
    col = lax.broadcasted_iota(jnp.int32, (C, C), 1)
    causal = row >= col
    tri = causal.astype(BF16)

    w_hi, w_mid, _ = _split3(wa2_ref[...])

    def log_decay(c):
        a_hi, a_mid, _ = _split3(ha_ref[0, pl.ds(c * C, C), :])
        z = _dot(a_hi, w_hi) + (_dot(a_mid, w_hi) + _dot(a_hi, w_mid)) + ba_ref[...]
        return (jnp.minimum(z, 0.0) - jnp.log(1.0 + jnp.exp(-jnp.abs(z)))) * (1.0 / GLA_TAU)

    H = range(GLA_HEADS)
    ck = [slice(h * GLA_DK, (h + 1) * GLA_DK) for h in H]
    cv = [slice(h * GLA_DV, (h + 1) * GLA_DV) for h in H]

    def cumsum(la):
        hi, mid, _ = _split3(la)
        return _dot(tri, hi) + _dot(tri, mid)

    def products(c, b, st):
        sl = pl.ds(c * C, C)
        decay = jnp.exp(b[C - 1:C, :])
        q_dec = q_ref[0, sl, :] * jnp.exp(b).astype(BF16)
        k_in = k_ref[0, sl, :] * jnp.exp(-b).astype(BF16)
        k_out = k_in * decay.astype(BF16)
        v = [v_ref[0, sl, cv[h]] for h in H]
        attn = [_dot_nt(q_dec[:, ck[h]], k_in[:, ck[h]]) for h in H]
        o_inter = [_dot(q_dec[:, ck[h]], st[h].astype(BF16)) for h in H]
        upd = [_dot_tn(k_out[:, ck[h]], v[h]) for h in H]
        new_st = []
        for h in H:
            dcol = jnp.broadcast_to(decay[:, ck[h]], (GLA_DK, GLA_DK)).T
            new_st.append(st[h] * jnp.concatenate([dcol] * (GLA_DV // GLA_DK), axis=1) + upd[h])
        return new_st, (attn, o_inter, v)

    def output(c, attn, o_inter, v):
        sl = pl.ds(c * C, C)
        for h in H:
            o = _dot(jnp.where(causal, attn[h], 0.0).astype(BF16), v[h]) + o_inter[h]
            r = r_ref[0, sl, cv[h]]
            o_ref[0, sl, cv[h]] = _rms(o, gn_ref[:, cv[h]]).astype(BF16) * (r * jax.nn.sigmoid(r))

    st = [st_ref[h] for h in H]
    la = {c: log_decay(c) for c in range(min(3, nchunk))}
    b = {c: cumsum(la.pop(c)) for c in range(min(2, nchunk))}
    st, pending = products(0, b.pop(0), st)
    for c in range(nchunk):
        if c + 3 < nchunk:
            la[c + 3] = log_decay(c + 3)
        if c + 2 < nchunk:
            b[c + 2] = cumsum(la.pop(c + 2))
        if c + 1 < nchunk:
            st, nxt = products(c + 1, b.pop(c + 1), st)
        output(c, *pending)
        if c + 1 < nchunk:
            pending = nxt
    for h in H:
        st_ref[h] = st[h]


def _gla(y3, ha3, wa2, ba, gn, tb=1024):
    B, S, _ = y3.shape
    return pl.pallas_call(
        functools.partial(_gla_kernel, nchunk=tb // GLA_CHUNK),
        grid=(B, S // tb),
        in_specs=[
            pl.BlockSpec((1, tb, GLA_QK_W), lambda b, t: (b, t, COL_Q // GLA_QK_W)),
            pl.BlockSpec((1, tb, GLA_QK_W), lambda b, t: (b, t, COL_K // GLA_QK_W)),
            pl.BlockSpec((1, tb, GLA_V_W), lambda b, t: (b, t, COL_V // GLA_V_W)),
            pl.BlockSpec((1, tb, GLA_V_W), lambda b, t: (b, t, COL_R // GLA_V_W)),
            pl.BlockSpec((1, tb, LANES), lambda b, t: (b, t, 0)),
            pl.BlockSpec((LANES, GLA_QK_W), lambda b, t: (0, 0)),
            pl.BlockSpec((1, GLA_QK_W), lambda b, t: (0, 0)),
            pl.BlockSpec((1, GLA_V_W), lambda b, t: (0, 0)),
        ],
        out_specs=pl.BlockSpec((1, tb, GLA_V_W), lambda b, t: (b, t, 0)),
        out_shape=jax.ShapeDtypeStruct((B, S, GLA_V_W), BF16),
        scratch_shapes=[pltpu.VMEM((GLA_HEADS, GLA_DK, GLA_DV), F32)],
        compiler_params=_params(("arbitrary", "arbitrary")),
        name="gla",
    )(y3, y3, y3, y3, ha3, wa2, ba, gn)


def _t5_causal_bucket(n):
    max_exact = REL_BUCKETS // 2
    nf = np.maximum(n, 1).astype(np.float32)
    large = max_exact + (np.log(nf / max_exact) / np.log(REL_MAX_DIST / max_exact)
                         * (REL_BUCKETS - max_exact)).astype(np.int32)
    large = np.minimum(large, REL_BUCKETS - 1)
    return np.where(n < max_exact, n, large).astype(np.int32)


def _bucket_table():
    qi = np.arange(ATT_BLOCK)[:, None]
    kj = np.arange(2 * ATT_BLOCK)[None, :]
    delta = qi + ATT_BLOCK - kj
    out = []
    for win, dil in ATT_GROUPS:
        band = (delta >= 0) & (delta <= win // dil)
        bucket = np.where(band, _t5_causal_bucket(np.maximum(delta, 0) * dil), -1)
        out.append(bucket)
    return np.stack(out).astype(np.int32)


def _att_unit(q, k, v, bias):
    BLK = ATT_BLOCK
    s = _dot_nt(q, k) + bias
    m = jnp.max(jnp.maximum(s[:, :BLK], s[:, BLK:]), axis=-1, keepdims=True)
    p = jnp.exp2((s - m).astype(BF16))
    v_ext = jnp.concatenate([v, jnp.ones_like(v)], axis=1)
    pv = _dot(p, v_ext)
    denom = pv[:, BLK:]
    return pv[:, :BLK] / denom, m + jnp.log2(denom)


def _att_kernel(tab_ref, bucket_ref, *refs, tiles_per_batch, heads_per_step, buckets_used, n_prev):
    groups, pos = [], 0
    for n in n_prev:
        q, k, v = refs[pos:pos + 3]
        groups.append((q, k, v, refs[pos + 3:pos + 3 + n], refs[pos + 3 + n:pos + 3 + 2 * n]))
        pos += 3 + 2 * n
    o_ref, bias_scr, o_scr, lse_scr = refs[pos:]
    BLK = ATT_BLOCK
    hd = ATT_HEAD_DIM
    ti = pl.program_id(0)
    h0 = pl.program_id(1) * heads_per_step

    @pl.when((ti == 0) & (h0 == 0))
    def _():
        for g in range(N_ATT_GROUPS):
            bucket = bucket_ref[g]
            bias = [jnp.full(bucket.shape, NEG_INF, F32) for _ in range(ATT_HEADS_PER_GROUP)]
            for b in buckets_used[g]:
                hit = bucket == b
                for hh in range(ATT_HEADS_PER_GROUP):
                    bias[hh] = jnp.where(hit, tab_ref[b, g * ATT_HEADS_PER_GROUP + hh] * LOG2_E, bias[hh])
            for hh in range(ATT_HEADS_PER_GROUP):
                bias_scr[g, hh] = bias[hh]

    lane = lax.broadcasted_iota(jnp.int32, (BLK, 2 * BLK), 1)
    no_prev = jnp.logical_and((ti % tiles_per_batch) == 0, lane < BLK)
    for hh in range(heads_per_step):
        hc = pl.ds(hh * hd, hd)
        for g, (q, k, v, kps, vps) in enumerate(groups):
            dil = ATT_DILS[g]
            n_r = TILE // dil
            bias = bias_scr[g, h0 + hh]
            bias_first = jnp.where(no_prev, NEG_INF, bias)
            for r in range(dil):
                for j in range(n_r // BLK):
                    cur = pl.ds(r * n_r + j * BLK, BLK)
                    if j > 0:
                        both = pl.ds(r * n_r + (j - 1) * BLK, 2 * BLK)
                        k_all, v_all, b = k[both, hc], v[both, hc], bias
                    else:
                        if len(kps) == dil:
                            kp, vp, prev = kps[r], vps[r], pl.ds(0, BLK)
                        else:
                            kp, vp, prev = kps[0], vps[0], pl.ds(r * n_r + n_r - BLK, BLK)
                        k_all = jnp.concatenate([kp[prev, hc], k[cur, hc]], axis=0)
                        v_all = jnp.concatenate([vp[prev, hc], v[cur, hc]], axis=0)
                        b = bias_first
                    if dil == 1:
                        rows = pl.ds(j * BLK, BLK)
                    else:
                        rows = pl.ds(dil * j * BLK + r, BLK, stride=dil)
                    o_scr[g, rows, :], lse_scr[g, rows, :] = _att_unit(q[cur, hc], k_all, v_all, b)

        for c in range(TILE // 256):
            rows = pl.ds(c * 256, 256)
            lse = [lse_scr[g, rows, :] for g in range(N_ATT_GROUPS)]
            top = functools.reduce(jnp.maximum, lse)
            w = [jnp.exp2(x - top) for x in lse]
            num = functools.reduce(lambda a, b: a + b,
                                   [w[g] * o_scr[g, rows, :] for g in range(N_ATT_GROUPS)])
            den = functools.reduce(lambda a, b: a + b, w)
            o_ref[rows, hc] = (num / den).astype(BF16)


def _attention(y2, rel_bias, S, heads_per_step=2):
    T = y2.shape[0]
    table = _bucket_table()
    tiles_per_batch = S // TILE
    hd = ATT_HEAD_DIM
    bw = heads_per_step * hd

    def col(g, c):
        return (COL_ATT + g * ATT_GROUP_W + c * ATT_W) // bw

    def cur(g, c):
        return pl.BlockSpec((TILE, bw), lambda t, h: (t, col(g, c) + h))

    def prev_tile(g, c):
        return [pl.BlockSpec((TILE, bw), lambda t, h: (jnp.maximum(t - 1, 0), col(g, c) + h))]

    def prev_blocks(g, c):
        nb, per_class = TILE // ATT_BLOCK, TILE // ATT_DILS[g] // ATT_BLOCK
        return [pl.BlockSpec((ATT_BLOCK, bw),
                             functools.partial(lambda t, h, last: (jnp.maximum((t - 1) * nb + last, 0), col(g, c) + h),
                                               last=(r + 1) * per_class - 1))
                for r in range(ATT_DILS[g])]

    prev_spec = [prev_blocks if d * ATT_BLOCK < TILE else prev_tile for d in ATT_DILS]
    in_specs = [pl.BlockSpec(memory_space=pltpu.SMEM),
                pl.BlockSpec((N_ATT_GROUPS, ATT_BLOCK, 2 * ATT_BLOCK), lambda t, h: (0, 0, 0))]
    for g in range(N_ATT_GROUPS):
        in_specs += [cur(g, 0), cur(g, 1), cur(g, 2)] + prev_spec[g](g, 1) + prev_spec[g](g, 2)
    n_prev = tuple(len(prev_spec[g](g, 1)) for g in range(N_ATT_GROUPS))
    return pl.pallas_call(
        functools.partial(_att_kernel, tiles_per_batch=tiles_per_batch, heads_per_step=heads_per_step,
                          buckets_used=tuple(tuple(int(b) for b in np.unique(t) if b >= 0) for t in table),
                          n_prev=n_prev),
        grid=(T // TILE, ATT_HEADS_PER_GROUP // heads_per_step),
        in_specs=in_specs,
        out_specs=pl.BlockSpec((TILE, bw), lambda t, h: (t, h)),
        out_shape=jax.ShapeDtypeStruct((T, ATT_W), BF16),
        scratch_shapes=[
            pltpu.VMEM((N_ATT_GROUPS, ATT_HEADS_PER_GROUP, ATT_BLOCK, 2 * ATT_BLOCK), F32),
            pltpu.VMEM((N_ATT_GROUPS, TILE, hd), F32),
            pltpu.VMEM((N_ATT_GROUPS, TILE, LANES), F32),
        ],
        compiler_params=_params(("arbitrary", "arbitrary")),
        name="attention",
    )(rel_bias, jnp.asarray(table), *([y2] * (len(in_specs) - 2)))


def _tail_kernel(x_ref, og_ref, oa_ref, ga_ref, gb_ref, p_ref, wog_ref, woa_ref, wout_ref,
                 ln2_ref, w1_ref, w2_ref, ln3_ref, wpg_ref, wpp_ref, lnf_ref, out_ref, *, tf):
    y_gla = _dot(og_ref[...], wog_ref[...])
    y_att = _dot(oa_ref[...], woa_ref[...])
    mix = (jax.nn.sigmoid(ga_ref[...].astype(F32)) * y_gla
           + jax.nn.sigmoid(gb_ref[...].astype(F32)) * y_att)
    x1 = x_ref[...] + _dot(mix.astype(BF16), wout_ref[...])

    h = _rms(x1, ln2_ref[...]).astype(BF16)
    x2 = x1
    for f in range(D_FF // tf):
        cols = pl.ds(f * tf, tf)
        u = jnp.maximum(_dot(h, w1_ref[:, cols]), 0.0)
        x2 = x2 + _dot((u * u).astype(BF16), w2_ref[cols, :])

    h3 = _rms(x2, ln3_ref[...]).astype(BF16)
    gate = jax.nn.sigmoid(_dot(h3, wpg_ref[...]))
    x3 = x2 + gate * _dot(p_ref[...].astype(BF16), wpp_ref[...])
    out_ref[...] = _rms(x3, lnf_ref[...])


def _tail(x2, y2, o_gla, o_att, p2, wog, woa, wout, ln2, w1, w2, ln3, wpg, wpp, lnf, tm=512, tf=1024):
    T = x2.shape[0]
    gcol = COL_GATE // D_MODEL
    row = lambda w: pl.BlockSpec((tm, w), lambda i: (i, 0))
    const = lambda a: pl.BlockSpec(a.shape, lambda i: (0, 0), pipeline_mode=pl.Buffered(1))
    return pl.pallas_call(
        functools.partial(_tail_kernel, tf=tf),
        grid=(T // tm,),
        in_specs=[row(D_MODEL), row(GLA_V_W), row(ATT_W),
                  pl.BlockSpec((tm, D_MODEL), lambda i: (i, gcol)),
                  pl.BlockSpec((tm, D_MODEL), lambda i: (i, gcol + 1)),
                  row(PLE_DIM),
                  const(wog), const(woa), const(wout), const(ln2), const(w1), const(w2),
                  const(ln3), const(wpg), const(wpp), const(lnf)],
        out_specs=row(D_MODEL),
        out_shape=jax.ShapeDtypeStruct((T, D_MODEL), F32),
        compiler_params=_params(("arbitrary",)),
        name="tail",
    )(x2, o_gla, o_att, y2, y2, p2, wog, woa, wout, ln2, w1, w2, ln3, wpg, wpp, lnf)


def _layer(x2, p2, B, S, ln1, w_in, w_a2, b_a, gla_gn, w_o_gla, w_o_attn, w_out,
           ln2, w_mlp1, w_mlp2, ln3, w_pp, w_pg, rel_bias, ln_out):
    wa2 = jnp.pad(w_a2, ((0, LANES - GLA_RANK), (0, 0)))

    w_in_t = w_in.T
    y2, ha = _inproj(x2, ln1[None], _wprep(w_in_t), w_in_t)
    o_gla = _gla(y2.reshape(B, S, Y_COLS), ha.reshape(B, S, LANES), wa2, b_a[None], gla_gn[None])
    o_att = _attention(y2, rel_bias, S)
    return _tail(x2, y2, o_gla.reshape(B * S, GLA_V_W), o_att, p2,
                 w_o_gla.astype(BF16), w_o_attn.astype(BF16), w_out.astype(BF16), ln2[None],
                 w_mlp1.astype(BF16), w_mlp2.astype(BF16), ln3[None],
                 w_pg.astype(BF16), w_pp.astype(BF16), ln_out[None])


def kernel(x, p, ln1, w_in, w_a2, b_a, gla_gn, w_o_gla, w_o_attn, w_out, ln2, w_mlp1, w_mlp2,
           ln3, w_pp, w_pg, rel_bias, ln_f):
    B, S, D = x.shape
    assert p.shape[0] == 1, "the final norm is fused into the single layer's last kernel"
    assert S % TILE == 0
    x2 = x.reshape(B * S, D)
    out = _layer(x2, p[0].reshape(B * S, PLE_DIM), B, S, ln1[0], w_in[0], w_a2[0], b_a[0],
                 gla_gn[0], w_o_gla[0], w_o_attn[0], w_out[0], ln2[0], w_mlp1[0], w_mlp2[0],
                 ln3[0], w_pp[0], w_pg[0], rel_bias, ln_f)
    return out.reshape(B, S, D)
```
